```python
import math
import jax
import jax.numpy as jnp
from jax import lax
import numpy as np

D_MODEL = 2048
BATCH = 2
SEQ = 16384
DEPTH = 2

DEEPNORM_ALPHA = (2 * DEPTH) ** 0.25
DEEPNORM_BETA = (8 * DEPTH) ** -0.25
LN_EPS = 1e-5
RMS_EPS = 1e-6

N_BRANCHES = 3
BRANCH_WIDTH = D_MODEL
GATE_COLS = N_BRANCHES * BRANCH_WIDTH

SSD_HEAD_DIM = 64
SSD_HEADS = BRANCH_WIDTH // SSD_HEAD_DIM
SSD_GROUPS = 4
SSD_HPG = SSD_HEADS // SSD_GROUPS
SSD_STATE = 128
SSD_CONV = 4
SSD_CHUNK = 128
SSD_WIDTH = SSD_HEADS * SSD_HEAD_DIM
SSD_BC = SSD_GROUPS * SSD_STATE
SSD_CONV_CH = SSD_WIDTH + 2 * SSD_BC
SSD_COLS = SSD_WIDTH + SSD_CONV_CH + SSD_HEADS

RWKV_HEAD = 64
RWKV_WIDTH = BRANCH_WIDTH
RWKV_HEADS = RWKV_WIDTH // RWKV_HEAD
RWKV_DECAY_LORA = 96
RWKV_A_LORA = 96
RWKV_V_LORA = 64
RWKV_G_LORA = 256
RWKV_COLS = 3 * RWKV_WIDTH + RWKV_DECAY_LORA + RWKV_A_LORA + RWKV_G_LORA
RWKV_GN_EPS = 64e-5

MLA_NOPE = 128
MLA_ROPE = 64
MLA_V = 128
MLA_QK = MLA_NOPE + MLA_ROPE
MLA_HEADS = BRANCH_WIDTH // MLA_V
MLA_Q_RANK = 512
MLA_KV_RANK = 512
MLA_COLS = MLA_Q_RANK + MLA_KV_RANK + MLA_ROPE
ROPE_THETA = 10000.0
ATTN_BLOCK = 128

D_FF = 5632
N_EXPERTS = 8
TOP_K = 2
MOE_BLOCK = 256
N_DENSE = (DEPTH + 1) // 2
N_MOE = DEPTH // 2

IN_COLS = GATE_COLS + SSD_COLS + MLA_COLS + RWKV_COLS

kernel_name = 'hybrid_ssd_rwkv7_mla_moe_deepnorm'


def _split(x, sizes):
    return jnp.split(x, [int(s) for s in np.cumsum(sizes)[:-1]], axis=-1)


def _layernorm(x, w, b):
    xf = x.astype(jnp.float32)
    mu = jnp.mean(xf, axis=-1, keepdims=True)
    var = jnp.mean(jnp.square(xf - mu), axis=-1, keepdims=True)
    return ((xf - mu) * lax.rsqrt(var + LN_EPS) * w + b).astype(x.dtype)


def _rmsnorm(x, w):
    xf = x.astype(jnp.float32)
    y = xf * lax.rsqrt(jnp.mean(xf * xf, axis=-1, keepdims=True) + RMS_EPS)
    return (y * w).astype(x.dtype)


def _rope_tables(positions):
    inv_freq = ROPE_THETA ** (-jnp.arange(0, MLA_ROPE, 2, dtype=jnp.float32) / MLA_ROPE)
    ang = positions.astype(jnp.float32)[..., None] * inv_freq
    return jnp.cos(ang), jnp.sin(ang)


def _rope(x, cos, sin):
    x1, x2 = jnp.split(x, 2, axis=-1)
    return jnp.concatenate([x1 * cos - x2 * sin, x2 * cos + x1 * sin], axis=-1).astype(x.dtype)


def _causal_depthwise_conv(x, w, b):
    k_width, ch = w.shape
    y = lax.conv_general_dilated(x, w[:, None, :].astype(x.dtype), window_strides=(1,),
                                 padding=[(k_width - 1, 0)],
                                 dimension_numbers=('NWC', 'WIO', 'NWC'),
                                 feature_group_count=ch)
    return y + b


def _token_shift(p, mu):
    prev = jnp.pad(p, ((0, 0), (1, 0), (0, 0)))[:, :-1]
    return p + (prev - p) * mu


def _swiglu(x, w1, w3, w2):
    return (jax.nn.silu(x @ w1) * (x @ w3)) @ w2


def _ssd_mixer(p, conv_w, conv_b, dt_bias, a_log, d_skip, norm_w):
    bn, sn, _ = p.shape
    nc = sn // SSD_CHUNK
    z, xbc, dt = _split(p, [SSD_WIDTH, SSD_CONV_CH, SSD_HEADS])
    xbc = jax.nn.silu(_causal_depthwise_conv(xbc, conv_w, conv_b))
    xs, b_in, c_in = _split(xbc, [SSD_WIDTH, SSD_BC, SSD_BC])
    xs = xs.astype(jnp.float32).reshape(bn, nc, SSD_CHUNK, SSD_GROUPS, SSD_HPG, SSD_HEAD_DIM)
    b_in = b_in.astype(jnp.float32).reshape(bn, nc, SSD_CHUNK, SSD_GROUPS, SSD_STATE)
    c_in = c_in.astype(jnp.float32).reshape(bn, nc, SSD_CHUNK, SSD_GROUPS, SSD_STATE)
    dt = jax.nn.softplus(dt.astype(jnp.float32) + dt_bias).reshape(bn, nc, SSD_CHUNK, SSD_GROUPS, SSD_HPG)
    a = -jnp.exp(a_log.astype(jnp.float32)).reshape(SSD_GROUPS, SSD_HPG)
    causal = jnp.tril(jnp.ones((SSD_CHUNK, SSD_CHUNK), dtype=bool))[None, :, :, None, None]

    def chunk_step(h, inp):
        xc, dtc, bc, cc = inp
        cum = jnp.cumsum(dtc * a, axis=1)
        seg = cum[:, :, None] - cum[:, None, :]
        decay = jnp.exp(jnp.where(causal, seg, -jnp.inf))
        xdt = xc * dtc[..., None]
        cb = jnp.einsum('bign,bjgn->bijg', cc, bc)
        y = jnp.einsum('bijg,bijge,bjgep->bigep', cb, decay, xdt)
        y = y + jnp.einsum('bign,bgepn->bigep', cc, h) * jnp.exp(cum)[..., None]
        to_end = jnp.exp(cum[:, -1:] - cum)
        h = h * jnp.exp(cum[:, -1])[..., None, None] + jnp.einsum('bjgn,bjge,bjgep->bgepn', bc, to_end, xdt)
        return h, y

    mv = lambda t: jnp.moveaxis(t, 1, 0)
    h0 = jnp.zeros((bn, SSD_GROUPS, SSD_HPG, SSD_HEAD_DIM, SSD_STATE), jnp.float32)
    _, y = lax.scan(chunk_step, h0, (mv(xs), mv(dt), mv(b_in), mv(c_in)))
    y = jnp.moveaxis(y, 0, 1)
    y = y + d_skip.astype(jnp.float32).reshape(SSD_GROUPS, SSD_HPG)[..., None] * xs
    y = y.reshape(bn, sn, SSD_WIDTH) * jax.nn.silu(z.astype(jnp.float32))
    yg = y.reshape(bn, sn, SSD_GROUPS, SSD_WIDTH // SSD_GROUPS)
    yg = yg * lax.rsqrt(jnp.mean(yg * yg, axis=-1, keepdims=True) + RMS_EPS)
    return (yg.reshape(bn, sn, SSD_WIDTH) * norm_w).astype(p.dtype)


def _rwkv7_mixer(p, v_first, w0, w2, a0, a2, g2, k_k, k_a, r_k, ln_w, ln_b, v0, v2):
    bn, sn, _ = p.shape
    sizes = [RWKV_WIDTH, RWKV_WIDTH, RWKV_WIDTH, RWKV_DECAY_LORA, RWKV_A_LORA, RWKV_G_LORA]
    if v2 is not None:
        sizes.append(RWKV_V_LORA)
    parts = _split(p, sizes)
    r, k, v, w_lo, a_lo, g_lo = parts[:6]
    log_w = -jax.nn.softplus(-(w0 + jnp.tanh(w_lo) @ w2).astype(jnp.float32)) - 0.5
    decay = jnp.exp(-jnp.exp(log_w))
    a = jax.nn.sigmoid((a0 + a_lo @ a2).astype(jnp.float32))
    g = jax.nn.sigmoid(g_lo) @ g2
    if v2 is None:
        v_first = v
    else:
        v = v + (v_first - v) * jax.nn.sigmoid(v0 + parts[6] @ v2)
    heads = lambda t: t.astype(jnp.float32).reshape(bn, sn, RWKV_HEADS, RWKV_HEAD)
    kk = heads(k * k_k)
    kk = kk / jnp.maximum(jnp.sqrt(jnp.sum(kk * kk, axis=-1, keepdims=True)), 1e-12)
    k = k * (1.0 + (a - 1.0) * k_a)
    rh, kh, vh, ah, wh = heads(r), heads(k), heads(v), heads(a), heads(decay)

    def step(state, inp):
        r_t, w_t, k_t, v_t, kk_t, a_t = inp
        s_kk = jnp.einsum('bhvk,bhk->bhv', state, kk_t)
        state = (state * w_t[:, :, None, :] - s_kk[..., None] * (kk_t * a_t)[:, :, None, :]
                 + v_t[..., None] * k_t[:, :, None, :])
        return state, jnp.einsum('bhvk,bhk->bhv', state, r_t)

    tm = lambda t: jnp.moveaxis(t, 1, 0)
    s0 = jnp.zeros((bn, RWKV_HEADS, RWKV_HEAD, RWKV_HEAD), jnp.float32)
    _, y = lax.scan(step, s0, (tm(rh), tm(wh), tm(kh), tm(vh), tm(kk), tm(ah)))
    y = jnp.moveaxis(y, 0, 1)
    mu = jnp.mean(y, axis=-1, keepdims=True)
    var = jnp.mean(jnp.square(y - mu), axis=-1, keepdims=True)
    y = ((y - mu) * lax.rsqrt(var + RWKV_GN_EPS)).reshape(bn, sn, RWKV_WIDTH) * ln_w + ln_b
    bonus = jnp.sum(rh * kh * r_k.reshape(RWKV_HEADS, RWKV_HEAD), axis=-1, keepdims=True) * vh
    y = (y + bonus.reshape(bn, sn, RWKV_WIDTH)) * g
    return y.astype(p.dtype), v_first


def _causal_mla_attention(q_nope, q_pe, k_nope, k_pe, v):
    bn, sn = q_nope.shape[:2]
    nb = sn // ATTN_BLOCK
    scale = MLA_QK ** -0.5
    key_idx = jnp.arange(sn)

    def block(args):
        i, qn, qp = args
        s = jnp.einsum('bqhd,bkhd->bhqk', qn, k_nope) + jnp.einsum('bqhr,bkr->bhqk', qp, k_pe)
        s = s.astype(jnp.float32) * scale
        q_idx = i * ATTN_BLOCK + jnp.arange(ATTN_BLOCK)
        s = jnp.where(key_idx[None, :] <= q_idx[:, None], s, -jnp.inf)
        prob = jax.nn.softmax(s, axis=-1).astype(v.dtype)
        return jnp.einsum('bhqk,bkhv->bqhv', prob, v)

    blocks = lambda t: jnp.moveaxis(t.reshape(bn, nb, ATTN_BLOCK, *t.shape[2:]), 1, 0)
    out = lax.map(block, (jnp.arange(nb), blocks(q_nope), blocks(q_pe)))
    return jnp.moveaxis(out, 0, 1).reshape(bn, sn, MLA_HEADS, MLA_V)


def _mla_mixer(p, cos, sin, q_norm_w, w_q_b, kv_norm_w, w_kv_b):
    bn, sn, _ = p.shape
    q_lat, kv_lat, k_pe = _split(p, [MLA_Q_RANK, MLA_KV_RANK, MLA_ROPE])
    q = (_rmsnorm(q_lat, q_norm_w) @ w_q_b).reshape(bn, sn, MLA_HEADS, MLA_QK)
    q_nope, q_pe = _split(q, [MLA_NOPE, MLA_ROPE])
    kv = (_rmsnorm(kv_lat, kv_norm_w) @ w_kv_b).reshape(bn, sn, MLA_HEADS, MLA_NOPE + MLA_V)
    k_nope, v = _split(kv, [MLA_NOPE, MLA_V])
    q_pe = _rope(q_pe, cos[:, :, None, :], sin[:, :, None, :])
    k_pe = _rope(k_pe, cos, sin)
    out = _causal_mla_attention(q_nope, q_pe, k_nope, k_pe, v)
    return out.reshape(bn, sn, MLA_HEADS * MLA_V)


def _moe_swiglu(x, router, w1, w3, w2):
    bn, sn, d = x.shape
    t = bn * sn
    xt = x.reshape(t, d)
    logits = (xt @ router).astype(jnp.float32)
    top_logits, top_idx = lax.top_k(logits, TOP_K)
    top_w = jax.nn.softmax(top_logits, axis=-1)
    flat_e = top_idx.reshape(-1)
    n_assign = t * TOP_K
    order = jnp.argsort(flat_e)
    sorted_e = flat_e[order]
    counts = jnp.bincount(flat_e, length=N_EXPERTS)
    start = jnp.cumsum(counts) - counts
    padded = (counts + MOE_BLOCK - 1) // MOE_BLOCK * MOE_BLOCK
    pad_end = jnp.cumsum(padded)
    pad_start = pad_end - padded
    dest = pad_start[sorted_e] + jnp.arange(n_assign) - start[sorted_e]
    n_blocks = -(-(n_assign + N_EXPERTS * (MOE_BLOCK - 1)) // MOE_BLOCK)
    token_of_row = jnp.full((n_blocks * MOE_BLOCK,), t, dtype=jnp.int32).at[dest].set((order // TOP_K).astype(jnp.int32))
    x_pad = jnp.concatenate([xt, jnp.zeros((1, d), xt.dtype)], axis=0)
    xb = x_pad[token_of_row].reshape(n_blocks, MOE_BLOCK, d)
    block_expert = jnp.minimum(jnp.searchsorted(pad_end, jnp.arange(n_blocks) * MOE_BLOCK, side='right'), N_EXPERTS - 1)

    def expert_block(args):
        xblk, e = args
        return _swiglu(xblk, w1[e], w3[e], w2[e])

    yb = lax.map(expert_block, (xb, block_expert)).reshape(n_blocks * MOE_BLOCK, d)
    contrib = yb[dest].astype(jnp.float32) * top_w.reshape(-1)[order][:, None]
    out = jnp.zeros((t, d), jnp.float32).at[order // TOP_K].add(contrib)
    return out.astype(x.dtype).reshape(bn, sn, d)


def setup_inputs(seed: int = 0) -> dict:
    key = jax.random.key(seed)
    keys = iter(jax.random.split(key, 48))
    f32 = jnp.float32

    def normal(shape, scale):
        return jax.random.normal(next(keys), shape, f32) * scale

    def uniform(shape, lo, hi):
        return jax.random.uniform(next(keys), shape, f32, lo, hi)

    nl, nv = DEPTH, DEPTH - 1
    x = jax.random.normal(next(keys), (BATCH, SEQ, D_MODEL), f32)
    steps = jax.random.randint(next(keys), (BATCH, SEQ), 1, 3)
    offset = jax.random.randint(next(keys), (BATCH, 1), 0, 4096)
    positions = (offset + jnp.cumsum(steps, axis=1) - 1).astype(jnp.int32)
    dt = jnp.exp(uniform((nl, SSD_HEADS), math.log(1e-3), math.log(1e-1)))
    return {
        'x': x,
        'positions': positions,
        'w_in': normal((nl, D_MODEL, IN_COLS), D_MODEL ** -0.5),
        'w_in_vres': normal((nv, D_MODEL, RWKV_V_LORA), D_MODEL ** -0.5),
        'w_out': normal((nl, BRANCH_WIDTH, D_MODEL), DEEPNORM_BETA * BRANCH_WIDTH ** -0.5),
        'ssd_conv_w': normal((nl, SSD_CONV, SSD_CONV_CH), SSD_CONV ** -0.5),
        'ssd_conv_b': normal((nl, SSD_CONV_CH), 0.02),
        'ssd_dt_bias': dt + jnp.log(-jnp.expm1(-dt)),
        'ssd_a_log': jnp.log(uniform((nl, SSD_HEADS), 1.0, 16.0)),
        'ssd_d': 1.0 + normal((nl, SSD_HEADS), 0.1),
        'ssd_norm_w': 1.0 + normal((nl, SSD_WIDTH), 0.02),
        'rwkv_mu': uniform((nl, RWKV_COLS), 0.0, 1.0),
        'rwkv_mu_vres': uniform((nv, RWKV_V_LORA), 0.0, 1.0),
        'rwkv_w0': uniform((nl, RWKV_WIDTH), -6.5, -1.5),
        'rwkv_w2': normal((nl, RWKV_DECAY_LORA, RWKV_WIDTH), 0.5 * RWKV_DECAY_LORA ** -0.5),
        'rwkv_a0': normal((nl, RWKV_WIDTH), 0.1),
        'rwkv_a2': normal((nl, RWKV_A_LORA, RWKV_WIDTH), 0.5 * RWKV_A_LORA ** -0.5),
        'rwkv_g2': normal((nl, RWKV_G_LORA, RWKV_WIDTH), RWKV_G_LORA ** -0.5),
        'rwkv_v0': normal((nv, RWKV_WIDTH), 0.1),
        'rwkv_v2': normal((nv, RWKV_V_LORA, RWKV_WIDTH), 0.5 * RWKV_V_LORA ** -0.5),
        'rwkv_k_k': 0.85 + normal((nl, RWKV_WIDTH), 0.02),
        'rwkv_k_a': 1.0 + normal((nl, RWKV_WIDTH), 0.02),
        'rwkv_r_k': normal((nl, RWKV_WIDTH), 0.1),
        'rwkv_ln_w': 1.0 + normal((nl, RWKV_WIDTH), 0.02),
        'rwkv_ln_b': normal((nl, RWKV_WIDTH), 0.02),
        'mla_q_norm_w': 1.0 + normal((nl, MLA_Q_RANK), 0.02),
        'mla_w_q_b': normal((nl, MLA_Q_RANK, MLA_HEADS * MLA_QK), MLA_Q_RANK ** -0.5),
        'mla_kv_norm_w': 1.0 + normal((nl, MLA_KV_RANK), 0.02),
        'mla_w_kv_b': normal((nl, MLA_KV_RANK, MLA_HEADS * (MLA_NOPE + MLA_V)), MLA_KV_RANK ** -0.5),
        'ln1_w': 1.0 + normal((nl, D_MODEL), 0.02),
        'ln1_b': normal((nl, D_MODEL), 0.02),
        'ln2_w': 1.0 + normal((nl, D_MODEL), 0.02),
        'ln2_b': normal((nl, D_MODEL), 0.02),
        'ffn_w1': normal((N_DENSE, D_MODEL, D_FF), D_MODEL ** -0.5),
        'ffn_w3': normal((N_DENSE, D_MODEL, D_FF), D_MODEL ** -0.5),
        'ffn_w2': normal((N_DENSE, D_FF, D_MODEL), DEEPNORM_BETA * D_FF ** -0.5),
        'moe_router': normal((N_MOE, D_MODEL, N_EXPERTS), D_MODEL ** -0.5),
        'moe_w1': normal((N_MOE, N_EXPERTS, D_MODEL, D_FF), D_MODEL ** -0.5),
        'moe_w3': normal((N_MOE, N_EXPERTS, D_MODEL, D_FF), D_MODEL ** -0.5),
        'moe_w2': normal((N_MOE, N_EXPERTS, D_FF, D_MODEL), DEEPNORM_BETA * D_FF ** -0.5),
    }


def reference(x, positions, w_in, w_in_vres, w_out,
              ssd_conv_w, ssd_conv_b, ssd_dt_bias, ssd_a_log, ssd_d, ssd_norm_w,
              rwkv_mu, rwkv_mu_vres, rwkv_w0, rwkv_w2, rwkv_a0, rwkv_a2, rwkv_g2,
              rwkv_v0, rwkv_v2, rwkv_k_k, rwkv_k_a, rwkv_r_k, rwkv_ln_w, rwkv_ln_b,
              mla_q_norm_w, mla_w_q_b, mla_kv_norm_w, mla_w_kv_b,
              ln1_w, ln1_b, ln2_w, ln2_b,
              ffn_w1, ffn_w3, ffn_w2,
              moe_router, moe_w1, moe_w3, moe_w2):
    cos, sin = _rope_tables(positions)
    v_first = None
    for l in range(DEPTH):
        if l == 0:
            w_cat, mu, v0, v2 = w_in[l], rwkv_mu[l], None, None
        else:
            w_cat = jnp.concatenate([w_in[l], w_in_vres[l - 1]], axis=1)
            mu = jnp.concatenate([rwkv_mu[l], rwkv_mu_vres[l - 1]], axis=0)
            v0, v2 = rwkv_v0[l - 1], rwkv_v2[l - 1]
        proj = x @ w_cat
        gates, p_ssd, p_mla, p_rwkv = jnp.split(
            proj, [GATE_COLS, GATE_COLS + SSD_COLS, GATE_COLS + SSD_COLS + MLA_COLS], axis=-1)
        y_ssd = _ssd_mixer(p_ssd, ssd_conv_w[l], ssd_conv_b[l], ssd_dt_bias[l], ssd_a_log[l],
                           ssd_d[l], ssd_norm_w[l])
        y_rwkv, v_first = _rwkv7_mixer(_token_shift(p_rwkv, mu), v_first, rwkv_w0[l], rwkv_w2[l],
                                       rwkv_a0[l], rwkv_a2[l], rwkv_g2[l], rwkv_k_k[l], rwkv_k_a[l],
                                       rwkv_r_k[l], rwkv_ln_w[l], rwkv_ln_b[l], v0, v2)
        y_mla = _mla_mixer(p_mla, cos, sin, mla_q_norm_w[l], mla_w_q_b[l], mla_kv_norm_w[l], mla_w_kv_b[l])
        g_ssd, g_rwkv, g_mla = jnp.split(jax.nn.sigmoid(gates), N_BRANCHES, axis=-1)
        merged = g_ssd * y_ssd + g_rwkv * y_rwkv + g_mla * y_mla
        x = _layernorm(DEEPNORM_ALPHA * x + merged @ w_out[l], ln1_w[l], ln1_b[l])
        if l % 2 == 0:
            f = _swiglu(x, ffn_w1[l // 2], ffn_w3[l // 2], ffn_w2[l // 2])
        else:
            f = _moe_swiglu(x, moe_router[l // 2], moe_w1[l // 2], moe_w3[l // 2], moe_w2[l // 2])
        x = _layernorm(DEEPNORM_ALPHA * x + f, ln2_w[l], ln2_b[l])
    return x
```

```python
import functools
import math

import jax
import jax.numpy as jnp
import numpy as np
from jax import lax
from jax.experimental import pallas as pl
from jax.experimental.pallas import tpu as pltpu

F32 = jnp.float32
BF16 = jnp.bfloat16

D_MODEL = 2048
DEPTH = 2
ALPHA = (2 * DEPTH) ** 0.25
LN_EPS = 1e-5
RMS_EPS = 1e-6
SSD_HEADS, SSD_HEAD_DIM, SSD_GROUPS, SSD_STATE, SSD_CONV = 32, 64, 4, 128, 4
SSD_BC = SSD_GROUPS * SSD_STATE
RWKV_HEADS, RWKV_HEAD = 32, 64
RWKV_GN_EPS = 64e-5
LORA_W, LORA_A, LORA_G, LORA_V = 96, 96, 256, 64
MLA_HEADS, MLA_NOPE, MLA_ROPE, MLA_V, MLA_RANK = 16, 128, 64, 128, 512
MLA_QK = MLA_NOPE + MLA_ROPE
ROPE_THETA = 10000.0
D_FF = 5632
N_EXPERTS = 8
TOP_K = 2

LANES = 128
SUBLANES = 8
VMEM_LIMIT = 56 * 1024 * 1024

W2K = 2048
COL_GATE = 0
COL_Z = 3 * W2K
COL_R = 4 * W2K
COL_XS = 7 * W2K
COL_BC = 8 * W2K
COL_QLAT = COL_BC + 1024
COL_KVLAT = COL_QLAT + 512
COL_MISC = COL_KVLAT + 512
MISC_W, MISC_A, MISC_G, MISC_V, MISC_DT, MISC_KPE = 0, 128, 256, 512, 640, 768
N_PROJ = COL_MISC + 1024

SSD_Q = 128
RW_T = 64
RW_BLK = 128


def _cparams(sem, vmem=VMEM_LIMIT):
    return pltpu.CompilerParams(dimension_semantics=sem, vmem_limit_bytes=vmem)


def _dot(a, b):
    return jnp.dot(a, b, preferred_element_type=F32)


def _dot_nt(a, b):
    return lax.dot_general(a, b, (((1,), (1,)), ((), ())), preferred_element_type=F32)


def _split3(a):
    hi = a.astype(BF16)
    r1 = a - hi.astype(F32)
    mid = r1.astype(BF16)
    lo = (r1 - mid.astype(F32)).astype(BF16)
    return hi, mid, lo


def _dot_exact_rhs(a_bf, b):
    hi, mid, lo = _split3(b)
    return _dot(a_bf, hi) + _dot(a_bf, mid) + _dot(a_bf, lo)


def _dot_exact_lhs(a, b_bf):
    hi, mid, lo = _split3(a)
    return _dot(hi, b_bf) + _dot(mid, b_bf) + _dot(lo, b_bf)


def _sigmoid(x):
    return 1.0 / (1.0 + jnp.exp(-x))


def _silu(x):
    return x * _sigmoid(x)


def _softplus(x):
    return jnp.maximum(x, 0.0) + jnp.log(1.0 + jnp.exp(-jnp.abs(x)))


def _layernorm(x, w, b):
    mu = jnp.mean(x, axis=-1, keepdims=True)
    xc = x - mu
    var = jnp.mean(xc * xc, axis=-1, keepdims=True)
    return xc * lax.rsqrt(var + LN_EPS) * w + b


def _mm_kernel(x_ref, w_ref, o_ref):
    o_ref[...] = _dot(x_ref[...], w_ref[...]).astype(o_ref.dtype)


def _matmul(x, w, out_dtype, tm, tn):
    m, k = x.shape
    n = w.shape[1]
    return pl.pallas_call(
        _mm_kernel,
        grid=(m // tm, n // tn),
        in_specs=[pl.BlockSpec((tm, k), lambda i, j: (i, 0)),
                  pl.BlockSpec((k, tn), lambda i, j: (0, j))],
        out_specs=pl.BlockSpec((tm, tn), lambda i, j: (i, j)),
        out_shape=jax.ShapeDtypeStruct((m, n), out_dtype),
        compiler_params=_cparams(("parallel", "arbitrary")),
        name="in_proj",
    )(x, w)


def _rope_kernel(pos_ref, freq_ref, cos_ref, sin_ref):
    ang = pos_ref[...] * freq_ref[...]
    valid = lax.broadcasted_iota(jnp.int32, ang.shape, 1) < MLA_ROPE
    cos_ref[...] = jnp.where(valid, jnp.cos(ang), 0.0)
    sin_ref[...] = jnp.where(valid, jnp.sin(ang), 0.0)


def _rope_tables(positions):
    t = positions.size
    tm = min(t, 1024)
    pos = positions.reshape(t, 1).astype(F32)
    inv_freq = ROPE_THETA ** (-jnp.arange(0, MLA_ROPE, 2, dtype=F32) / MLA_ROPE)
    freq = jnp.concatenate([inv_freq, inv_freq, jnp.zeros((LANES - MLA_ROPE,), F32)]).reshape(1, LANES)
    return pl.pallas_call(
        _rope_kernel,
        grid=(t // tm,),
        in_specs=[pl.BlockSpec((tm, 1), lambda i: (i, 0)),
                  pl.BlockSpec((1, LANES), lambda i: (0, 0))],
        out_specs=[pl.BlockSpec((tm, LANES), lambda i: (i, 0))] * 2,
        out_shape=[jax.ShapeDtypeStruct((t, LANES), F32)] * 2,
        compiler_params=_cparams(("parallel",)),
        name="rope_tables",
    )(pos, freq)


def _ssd_kernel(z_ref, xs_ref, bc_ref, misc_ref, cwx_ref, cwb_ref, cbx_ref, cbb_ref, dtb_ref, alog_ref,
                d_ref, nw_ref, e_ref, y_ref, state_ref, bufx_ref, bufb_ref):
    q = SSD_Q
    hp = SSD_HEADS // SSD_GROUPS * SSD_HEAD_DIM

    @pl.when(pl.program_id(1) == 0)
    def _():
        state_ref[...] = jnp.zeros_like(state_ref)
        bufx_ref[0:SUBLANES, :] = jnp.zeros((SUBLANES, bufx_ref.shape[1]), F32)
        bufb_ref[0:SUBLANES, :] = jnp.zeros((SUBLANES, bufb_ref.shape[1]), F32)

    bufx_ref[SUBLANES:SUBLANES + q, :] = xs_ref[...]
    bufb_ref[SUBLANES:SUBLANES + q, :] = bc_ref[...]

    def conv(buf_ref, w_ref, b_ref):
        acc = b_ref[...] + w_ref[SSD_CONV - 1:SSD_CONV, :] * buf_ref[SUBLANES:SUBLANES + q, :]
        for k in range(SSD_CONV - 1):
            off = SUBLANES - (SSD_CONV - 1) + k
            acc = acc + w_ref[k:k + 1, :] * buf_ref[off:off + q, :]
        return _silu(acc)

    xs = conv(bufx_ref, cwx_ref, cbx_ref)
    bc = conv(bufb_ref, cwb_ref, cbb_ref)
    bufx_ref[0:SUBLANES, :] = bufx_ref[q:q + SUBLANES, :]
    bufb_ref[0:SUBLANES, :] = bufb_ref[q:q + SUBLANES, :]

    dt = _softplus(misc_ref[:, MISC_DT:MISC_DT + LANES] + dtb_ref[...])
    da = dt * (-jnp.exp(alog_ref[...]))
    row = lax.broadcasted_iota(jnp.int32, (q, q), 0)
    col = lax.broadcasted_iota(jnp.int32, (q, q), 1)
    causal = row >= col
    tri = jnp.where(causal, 1.0, 0.0).astype(BF16)
    cum = _dot_exact_rhs(tri, da)
    cum_t = cum.T
    ecum = jnp.exp(cum)
    toend = jnp.exp(cum[q - 1:q, :] - cum)
    e_mat = e_ref[...]
    dt_e = _dot_exact_lhs(dt, e_mat)
    ecum_e = _dot_exact_lhs(ecum, e_mat)
    toend_e = _dot_exact_lhs(toend, e_mat)

    xdt = xs * dt_e
    xdt_b = xdt.astype(BF16)
    xw_b = (xdt * toend_e).astype(BF16)
    lane = lax.broadcasted_iota(jnp.int32, (q, LANES), 1)
    lo_half = lane < SSD_HEAD_DIM

    y_groups = []
    for g in range(SSD_GROUPS):
        b_g = bc[:, g * SSD_STATE:(g + 1) * SSD_STATE]
        c_g = bc[:, SSD_BC + g * SSD_STATE:SSD_BC + (g + 1) * SSD_STATE]
        b_gb = b_g.astype(BF16)
        c_gb = c_g.astype(BF16)
        cb = _dot_nt(c_gb, b_gb)
        st = state_ref[:, g * hp:(g + 1) * hp]
        y_inter = _dot(c_gb, st.astype(BF16)) * ecum_e[:, g * hp:(g + 1) * hp]
        parts = []
        for pr in range(hp // LANES):
            ms = []
            for e in range(2):
                h = g * (SSD_HEADS // SSD_GROUPS) + pr * 2 + e
                ci = jnp.broadcast_to(cum[:, h:h + 1], (q, q))
                cj = jnp.broadcast_to(cum_t[h:h + 1, :], (q, q))
                dec = jnp.exp(jnp.where(causal, ci - cj, -jnp.inf))
                ms.append((cb * dec).astype(BF16))
            lo = g * hp + pr * LANES
            xp = xdt_b[:, lo:lo + LANES]
            zero = jnp.zeros_like(xp)
            rhs = jnp.concatenate([jnp.where(lo_half, xp, zero), jnp.where(lo_half, zero, xp)], axis=0)
            parts.append(_dot(jnp.concatenate(ms, axis=1), rhs))
        y_groups.append(jnp.concatenate(parts, axis=1) + y_inter)
        upd = _dot(b_g.T.astype(BF16), xw_b[:, g * hp:(g + 1) * hp])
        state_ref[:, g * hp:(g + 1) * hp] = st * ecum_e[q - 1:q, g * hp:(g + 1) * hp] + upd

    y = jnp.concatenate(y_groups, axis=1) + d_ref[...] * xs
    y = y * _silu(z_ref[...])
    outs = []
    for g in range(SSD_GROUPS):
        yg = y[:, g * hp:(g + 1) * hp]
        outs.append(yg * lax.rsqrt(jnp.mean(yg * yg, axis=-1, keepdims=True) + RMS_EPS))
    y_ref[...] = jnp.concatenate(outs, axis=1) * nw_ref[...]


def _ssd_mixer(proj, conv_w, conv_b, dt_bias, a_log, d_skip, norm_w):
    bn, sn, _ = proj.shape
    w = SSD_HEADS * SSD_HEAD_DIM
    pad = lambda v: jnp.pad(v, (0, LANES - v.shape[0])).reshape(1, LANES)
    a_log_p = jnp.pad(a_log, (0, LANES - SSD_HEADS), constant_values=-jnp.inf).reshape(1, LANES)
    expand = jnp.pad(jnp.repeat(jnp.eye(SSD_HEADS, dtype=BF16), SSD_HEAD_DIM, axis=1),
                     ((0, LANES - SSD_HEADS), (0, 0)))
    row = lambda v: v.reshape(1, -1)
    const = lambda shape: pl.BlockSpec(shape, lambda b, c: (0, 0))
    blk = lambda width, idx: pl.BlockSpec((None, SSD_Q, width), lambda b, c: (b, c, idx))
    return pl.pallas_call(
        _ssd_kernel,
        grid=(bn, sn // SSD_Q),
        in_specs=[blk(w, COL_Z // w), blk(w, COL_XS // w), blk(1024, COL_BC // 1024), blk(1024, COL_MISC // 1024),
                  const((SSD_CONV, w)), const((SSD_CONV, 2 * SSD_BC)), const((1, w)), const((1, 2 * SSD_BC)),
                  const((1, LANES)), const((1, LANES)), const((1, w)), const((1, w)), const((LANES, w))],
        out_specs=pl.BlockSpec((None, SSD_Q, w), lambda b, c: (b, c, 0)),
        out_shape=jax.ShapeDtypeStruct((bn, sn, w), F32),
        scratch_shapes=[pltpu.VMEM((SSD_STATE, w), F32),
                        pltpu.VMEM((SSD_Q + SUBLANES, w), F32),
                        pltpu.VMEM((SSD_Q + SUBLANES, 2 * SSD_BC), F32)],
        compiler_params=_cparams(("parallel", "arbitrary")),
        name="ssd_mixer",
    )(proj, proj, proj, proj, conv_w[:, :w], conv_w[:, w:], row(conv_b[:w]), row(conv_b[w:]),
      pad(dt_bias), a_log_p, row(jnp.repeat(d_skip, SSD_HEAD_DIM)), row(norm_w), expand)


def _dot_hi(a, b):
    return jnp.dot(a, b, precision=lax.Precision.HIGHEST, preferred_element_type=F32)


def _split2(a):
    hi = a.astype(BF16)
    return hi, (a - hi.astype(F32)).astype(BF16)


def _head_sums(x, ones_blk):
    outs = []
    for s in range(x.shape[1] // LANES):
        hi, lo = _split2(x[:, s * LANES:(s + 1) * LANES])
        outs.append(_dot(hi, ones_blk) + _dot(lo, ones_blk))
    return jnp.concatenate(outs, axis=1)


def _stack_heads(x, lo_half):
    zero = jnp.zeros_like(x)
    return jnp.concatenate([jnp.where(lo_half, x, zero), jnp.where(lo_half, zero, x)], axis=0)


RW_PAIRS = RWKV_HEADS // 2
RW_UNROLL = 4


def _rwkv_kernel(has_vres, *refs):
    if has_vres:
        (r_ref, k_ref, v_ref, misc_ref, vfirst_ref, mur_ref, muk_ref, muv_ref, mum_ref, w0_ref, w2_ref, a0_ref,
         a2_ref, g2_ref, kk_ref, ka_ref, rk_ref, lnw_ref, lnb_ref, v0_ref, v2_ref,
         y_ref, state_ref, carry_ref, carrym_ref, st_ref, yp_ref, pt_ref) = refs
    else:
        (r_ref, k_ref, v_ref, misc_ref, mur_ref, muk_ref, muv_ref, mum_ref, w0_ref, w2_ref, a0_ref,
         a2_ref, g2_ref, kk_ref, ka_ref, rk_ref, lnw_ref, lnb_ref,
         y_ref, vout_ref, state_ref, carry_ref, carrym_ref, st_ref, yp_ref, pt_ref) = refs
    tb, t = RW_BLK, RW_T
    w = RWKV_HEADS * RWKV_HEAD

    @pl.when(pl.program_id(1) == 0)
    def _():
        state_ref[...] = jnp.zeros_like(state_ref)
        carry_ref[...] = jnp.zeros_like(carry_ref)
        carrym_ref[...] = jnp.zeros_like(carrym_ref)

    first_row = lax.broadcasted_iota(jnp.int32, (tb, 1), 0) == 0

    def shift(p, carry_row, mu):
        prev = jnp.where(first_row, carry_row, pltpu.roll(p, 1, 0))
        return p + (prev - p) * mu

    rp, kp, vp, mp = r_ref[...], k_ref[...], v_ref[...], misc_ref[...]
    r = shift(rp, carry_ref[0:1, :], mur_ref[...])
    k = shift(kp, carry_ref[1:2, :], muk_ref[...])
    v = shift(vp, carry_ref[2:3, :], muv_ref[...])
    m = shift(mp, carrym_ref[0:1, :], mum_ref[...])
    carry_ref[0:1, :] = rp[tb - 1:tb, :]
    carry_ref[1:2, :] = kp[tb - 1:tb, :]
    carry_ref[2:3, :] = vp[tb - 1:tb, :]
    carrym_ref[0:1, :] = mp[tb - 1:tb, :]

    w_lo = m[:, MISC_W:MISC_W + LANES]
    a_lo = m[:, MISC_A:MISC_A + LANES]
    g_lo = m[:, MISC_G:MISC_G + LORA_G]
    log_w = -_softplus(-(w0_ref[...] + _dot_hi(jnp.tanh(w_lo), w2_ref[...]))) - 0.5
    lw = -jnp.exp(log_w)
    a = _sigmoid(a0_ref[...] + _dot_hi(a_lo, a2_ref[...]))
    g = _dot(_sigmoid(g_lo).astype(BF16), g2_ref[...])
    if has_vres:
        v_lo = m[:, MISC_V:MISC_V + LANES]
        v = v + (vfirst_ref[...] - v) * _sigmoid(v0_ref[...] + _dot_hi(v_lo, v2_ref[...]))
    else:
        vout_ref[...] = v

    lane = lax.broadcasted_iota(jnp.int32, (LANES, LANES), 1)
    rowi = lax.broadcasted_iota(jnp.int32, (LANES, LANES), 0)
    ones_blk = jnp.where((lane // RWKV_HEAD) == (rowi // RWKV_HEAD), 1.0, 0.0).astype(BF16)

    kk = k * kk_ref[...]
    kk = kk / jnp.maximum(jnp.sqrt(_head_sums(kk * kk, ones_blk)), 1e-12)
    k = k * (1.0 + (a - 1.0) * ka_ref[...])
    b = kk * a

    ti = lax.broadcasted_iota(jnp.int32, (t, t), 0)
    tj = lax.broadcasted_iota(jnp.int32, (t, t), 1)
    tri = jnp.where(ti >= tj, 1.0, 0.0).astype(BF16)

    for c in range(tb // t):
        sl = slice(c * t, (c + 1) * t)
        lw_c = lw[sl]
        cl = _dot_exact_rhs(tri, lw_c)
        cl_end = cl[t - 1:t, :]
        e_neg = jnp.exp(-cl)
        e_end = jnp.exp(cl_end - cl)
        ops = (kk[sl] * jnp.exp(cl - lw_c), r[sl] * jnp.exp(cl), k[sl] * e_neg, b[sl] * e_neg,
               v[sl], k[sl] * e_end, b[sl] * e_end)
        for pi in range(RW_PAIRS):
            ls = slice(pi * LANES, (pi + 1) * LANES)
            for oi, op in enumerate(ops):
                st_ref[c, oi, pi] = op[:, ls]
            pt_ref[c, pi] = jnp.broadcast_to(jnp.exp(cl_end[:, ls]), (SUBLANES, LANES))

    lo_half = lax.broadcasted_iota(jnp.int32, (t, LANES), 1) < RWKV_HEAD
    bi = lax.broadcasted_iota(jnp.int32, (2 * t, 2 * t), 0) % t
    bj = lax.broadcasted_iota(jnp.int32, (2 * t, 2 * t), 1) % t
    strict = bi > bj
    incl = bi >= bj

    def pair_step(pi):
        ht = state_ref[pi]
        for c in range(tb // t):
            kks, rs, ks, bs, vs, kes, bes = [_stack_heads(st_ref[c, oi, pi], lo_half) for oi in range(7)]
            lhs2 = jnp.concatenate([kks, rs], axis=0).astype(BF16)
            rhs2 = jnp.concatenate([ks, bs], axis=0).astype(BF16)
            amat = _dot_nt(lhs2, rhs2)
            a_kk = jnp.where(strict, amat[0:2 * t, 0:2 * t], 0.0)
            lmat = jnp.where(strict, amat[0:2 * t, 2 * t:], 0.0)
            a_rk = jnp.where(incl, amat[2 * t:, 0:2 * t], 0.0)
            a_rb = jnp.where(incl, amat[2 * t:, 2 * t:], 0.0)
            sh = _dot_nt(lhs2, ht.astype(BF16))
            vsb = vs.astype(BF16)
            x = sh[0:2 * t] + _dot(a_kk.astype(BF16), vsb)
            lb = lmat.astype(BF16)
            x = x - _dot(lb, x.astype(BF16))
            pw = lmat
            n = 2
            while n < t:
                pwb = pw.astype(BF16)
                pw = _dot(pwb, pwb)
                x = x + _dot(pw.astype(BF16), x.astype(BF16))
                n *= 2
            ub = x.astype(BF16)
            ys = sh[2 * t:] + _dot(a_rk.astype(BF16), vsb) - _dot(a_rb.astype(BF16), ub)
            yp_ref[pi, c * t:(c + 1) * t, :] = ys[0:t] + ys[t:]
            lhs3 = jnp.concatenate([vs.T, -(x.T)], axis=1).astype(BF16)
            rhs3 = jnp.concatenate([kes, bes], axis=0).astype(BF16)
            ht = ht * pt_ref[c, pi][0:1, :] + _dot(lhs3, rhs3)
        state_ref[pi] = ht

    def loop_body(it, carry):
        for u in range(RW_UNROLL):
            pair_step(it * RW_UNROLL + u)
        return carry

    lax.fori_loop(0, RW_PAIRS // RW_UNROLL, loop_body, 0)

    y = jnp.concatenate([yp_ref[pi] for pi in range(RW_PAIRS)], axis=1)
    inv_n = 1.0 / RWKV_HEAD
    mu = _head_sums(y, ones_blk) * inv_n
    yc = y - mu
    var = _head_sums(yc * yc, ones_blk) * inv_n
    y = yc * lax.rsqrt(var + RWKV_GN_EPS) * lnw_ref[...] + lnb_ref[...]
    bonus = _head_sums(r * k * rk_ref[...], ones_blk) * v
    y_ref[...] = (y + bonus) * g


def _rwkv_mixer(proj, v_first, mu, w0, w2, a0, a2, g2, k_k, k_a, r_k, ln_w, ln_b, v0, v2):
    bn, sn, _ = proj.shape
    w = RWKV_HEADS * RWKV_HEAD
    has_vres = v_first is not None
    row = lambda x: x.reshape(1, -1)
    padrows = lambda x: jnp.pad(x, ((0, LANES - x.shape[0]), (0, 0)))
    padl = lambda x, n: jnp.pad(x, (0, n - x.shape[0]))
    mu_misc = [padl(mu[3 * w:3 * w + LORA_W], LANES), padl(mu[3 * w + LORA_W:3 * w + LORA_W + LORA_A], LANES),
               mu[3 * w + LORA_W + LORA_A:3 * w + LORA_W + LORA_A + LORA_G]]
    if has_vres:
        mu_misc.append(padl(mu[3 * w + LORA_W + LORA_A + LORA_G:], LANES))
    mu_m = padl(jnp.concatenate(mu_misc), 1024)
    const = lambda shape: pl.BlockSpec(shape, lambda b, c: (0,) * len(shape))
    blk = lambda width, idx: pl.BlockSpec((None, RW_BLK, width), lambda b, c: (b, c, idx))
    seq = pl.BlockSpec((None, RW_BLK, w), lambda b, c: (b, c, 0))
    in_specs = [blk(w, COL_R // w), blk(w, COL_R // w + 1), blk(w, COL_R // w + 2), blk(1024, COL_MISC // 1024)]
    args = [proj, proj, proj, proj]
    if has_vres:
        in_specs.append(seq)
        args.append(v_first)
    in_specs += [const((1, w))] * 3 + [const((1, 1024)), const((1, w)), const((LANES, w)), const((1, w)),
                                       const((LANES, w)), const((LORA_G, w))] + [const((1, w))] * 5
    args += [row(mu[:w]), row(mu[w:2 * w]), row(mu[2 * w:3 * w]), row(mu_m), row(w0), padrows(w2), row(a0),
             padrows(a2), g2.astype(BF16), row(k_k), row(k_a), row(r_k), row(ln_w), row(ln_b)]
    if has_vres:
        in_specs += [const((1, w)), const((LANES, w))]
        args += [row(v0), padrows(v2)]
    out_shape = [jax.ShapeDtypeStruct((bn, sn, w), F32)]
    out_specs = [seq]
    if not has_vres:
        out_shape.append(jax.ShapeDtypeStruct((bn, sn, w), F32))
        out_specs.append(seq)
    nc = RW_BLK // RW_T
    outs = pl.pallas_call(
        functools.partial(_rwkv_kernel, has_vres),
        grid=(bn, sn // RW_BLK),
        in_specs=in_specs,
        out_specs=out_specs,
        out_shape=out_shape,
        scratch_shapes=[pltpu.VMEM((RW_PAIRS, LANES, LANES), F32),
                        pltpu.VMEM((SUBLANES, w), F32),
                        pltpu.VMEM((SUBLANES, 1024), F32),
                        pltpu.VMEM((nc, 7, RW_PAIRS, RW_T, LANES), F32),
                        pltpu.VMEM((RW_PAIRS, RW_BLK, LANES), F32),
                        pltpu.VMEM((nc, RW_PAIRS, SUBLANES, LANES), F32)],
        compiler_params=_cparams(("parallel", "arbitrary")),
        name="rwkv7_mixer",
    )(*args)
    if has_vres:
        return outs[0], v_first
    return outs[0], outs[1]


MLA_TM = 256
ATT_TQ = 1024
ATT_TK = 512
HEAD_Q = 2 * LANES


def _rope_half(x2, cos, sin):
    return x2 * cos + pltpu.roll(x2, MLA_ROPE, 1) * sin


def _mla_prep_kernel(qlat_ref, kvlat_ref, misc_ref, cos_ref, sin_ref, qnw_ref, kvnw_ref, wq_ref, wkv_ref,
                     q_ref, kv_ref, kpe_ref):
    def rms(x, w):
        return (x * lax.rsqrt(jnp.mean(x * x, axis=-1, keepdims=True) + RMS_EPS) * w).astype(BF16)

    cos, sin = cos_ref[...], sin_ref[...]
    scale = MLA_QK ** -0.5
    q = _dot(rms(qlat_ref[...], qnw_ref[...]), wq_ref[...])
    for h in range(MLA_HEADS):
        lo = h * HEAD_Q
        q_ref[:, lo:lo + LANES] = (q[:, lo:lo + LANES] * scale).astype(BF16)
        q_ref[:, lo + LANES:lo + HEAD_Q] = (_rope_half(q[:, lo + LANES:lo + HEAD_Q], cos, sin) * scale).astype(BF16)
    kv_ref[...] = _dot(rms(kvlat_ref[...], kvnw_ref[...]), wkv_ref[...]).astype(BF16)
    kpe_ref[...] = _rope_half(misc_ref[:, MISC_KPE:MISC_KPE + LANES], cos, sin).astype(BF16)


def _flash_kernel(qi_ref, ki_ref, q_ref, k_ref, kpe_ref, v_ref, o_ref, m_ref, l_ref, acc_ref):
    step = pl.program_id(2)
    qi, ki = qi_ref[step], ki_ref[step]
    ratio = ATT_TQ // ATT_TK

    @pl.when(ki == 0)
    def _():
        m_ref[...] = jnp.full_like(m_ref, -jnp.inf)
        l_ref[...] = jnp.zeros_like(l_ref)
        acc_ref[...] = jnp.zeros_like(acc_ref)

    def update(masked):
        kcat = jnp.concatenate([k_ref[...], kpe_ref[...]], axis=1)
        s = _dot_nt(q_ref[...], kcat)
        if masked:
            rows = qi * ATT_TQ + lax.broadcasted_iota(jnp.int32, s.shape, 0)
            cols = ki * ATT_TK + lax.broadcasted_iota(jnp.int32, s.shape, 1)
            s = jnp.where(cols <= rows, s, -jnp.inf)
        m_prev = m_ref[...]
        m_new = jnp.maximum(m_prev, jnp.max(s, axis=-1, keepdims=True))
        p = jnp.exp(s - m_new)
        alpha = jnp.exp(m_prev - m_new)
        l_ref[...] = alpha * l_ref[...] + jnp.sum(p, axis=-1, keepdims=True)
        acc_ref[...] = alpha * acc_ref[...] + _dot(p.astype(BF16), v_ref[...])
        m_ref[...] = m_new

    @pl.when(ki < qi * ratio)
    def _():
        update(False)

    @pl.when(ki >= qi * ratio)
    def _():
        update(True)

    @pl.when(ki == (qi + 1) * ratio - 1)
    def _():
        o_ref[...] = acc_ref[...] / l_ref[...]


def _mla_mixer(proj, cos, sin, q_norm_w, w_q_b, kv_norm_w, w_kv_b):
    bn, sn, _ = proj.shape
    t = bn * sn
    proj2 = proj.reshape(t, N_PROJ)
    wq = w_q_b.reshape(MLA_RANK, MLA_HEADS, MLA_QK)
    pe = wq[..., MLA_NOPE:]
    rot = jnp.concatenate([-pe[..., MLA_ROPE // 2:], pe[..., :MLA_ROPE // 2]], axis=-1)
    wq = jnp.concatenate([wq, rot], axis=-1).reshape(MLA_RANK, MLA_HEADS * HEAD_Q).astype(BF16)
    wkv = w_kv_b.reshape(MLA_RANK, MLA_HEADS, MLA_NOPE + MLA_V)
    wkv = jnp.concatenate([wkv[..., :MLA_NOPE].reshape(MLA_RANK, -1), wkv[..., MLA_NOPE:].reshape(MLA_RANK, -1)],
                          axis=1).astype(BF16)
    nq, nkv = MLA_HEADS * HEAD_Q, MLA_HEADS * (MLA_NOPE + MLA_V)
    tm = min(MLA_TM, t)
    rowblk = lambda width, idx: pl.BlockSpec((tm, width), lambda i: (i, idx))
    const = lambda shape: pl.BlockSpec(shape, lambda i: (0, 0))
    q, kv, kpe = pl.pallas_call(
        _mla_prep_kernel,
        grid=(t // tm,),
        in_specs=[rowblk(MLA_RANK, COL_QLAT // MLA_RANK), rowblk(MLA_RANK, COL_KVLAT // MLA_RANK),
                  rowblk(1024, COL_MISC // 1024), rowblk(LANES, 0), rowblk(LANES, 0),
                  const((1, MLA_RANK)), const((1, MLA_RANK)), const((MLA_RANK, nq)), const((MLA_RANK, nkv))],
        out_specs=[rowblk(nq, 0), rowblk(nkv, 0), rowblk(LANES, 0)],
        out_shape=[jax.ShapeDtypeStruct((t, nq), BF16), jax.ShapeDtypeStruct((t, nkv), BF16),
                   jax.ShapeDtypeStruct((t, LANES), BF16)],
        compiler_params=_cparams(("parallel",)),
        name="mla_prep",
    )(proj2, proj2, proj2, cos, sin, q_norm_w.reshape(1, -1), kv_norm_w.reshape(1, -1), wq, wkv)

    tq, tk = min(ATT_TQ, sn), min(ATT_TK, sn)
    assert (tq, tk) == (ATT_TQ, ATT_TK) or tq == tk, "attention tiles must divide the sequence"
    ratio = tq // tk
    pairs = [(a, b) for a in range(sn // tq) for b in range((a + 1) * ratio)]
    qi_arr = jnp.asarray([p[0] for p in pairs], jnp.int32)
    ki_arr = jnp.asarray([p[1] for p in pairs], jnp.int32)
    kernel = _flash_kernel if (tq, tk) == (ATT_TQ, ATT_TK) else None
    assert kernel is not None, "sequence shorter than one attention tile"
    out = pl.pallas_call(
        kernel,
        grid_spec=pltpu.PrefetchScalarGridSpec(
            num_scalar_prefetch=2,
            grid=(bn, MLA_HEADS, len(pairs)),
            in_specs=[pl.BlockSpec((None, tq, HEAD_Q), lambda b, h, s, qi, ki: (b, qi[s], h)),
                      pl.BlockSpec((None, tk, MLA_NOPE), lambda b, h, s, qi, ki: (b, ki[s], h)),
                      pl.BlockSpec((None, tk, LANES), lambda b, h, s, qi, ki: (b, ki[s], 0)),
                      pl.BlockSpec((None, tk, MLA_V), lambda b, h, s, qi, ki: (b, ki[s], MLA_HEADS + h))],
            out_specs=pl.BlockSpec((None, tq, MLA_V), lambda b, h, s, qi, ki: (b, qi[s], h)),
            scratch_shapes=[pltpu.VMEM((tq, 1), F32), pltpu.VMEM((tq, 1), F32), pltpu.VMEM((tq, MLA_V), F32)]),
        out_shape=jax.ShapeDtypeStruct((bn, sn, MLA_HEADS * MLA_V), F32),
        compiler_params=_cparams(("parallel", "parallel", "arbitrary")),
        name="mla_attention",
    )(qi_arr, ki_arr, q.reshape(bn, sn, nq), kv.reshape(bn, sn, nkv), kpe.reshape(bn, sn, LANES),
      kv.reshape(bn, sn, nkv))
    return out


MERGE_TM = 256


def _merge_kernel(g0_ref, g1_ref, g2_ref, ys_ref, yr_ref, ym_ref, x_ref, wo_ref, lnw_ref, lnb_ref, xo_ref, xb_ref):
    merged = (_sigmoid(g0_ref[...]) * ys_ref[...] + _sigmoid(g1_ref[...]) * yr_ref[...]
              + _sigmoid(g2_ref[...]) * ym_ref[...])
    h = ALPHA * x_ref[...] + _dot(merged.astype(BF16), wo_ref[...])
    y = _layernorm(h, lnw_ref[...], lnb_ref[...])
    xo_ref[...] = y
    xb_ref[...] = y.astype(BF16)


def _merge_out(proj2, y_ssd, y_rwkv, y_mla, x, w_out, ln_w, ln_b):
    t, d = x.shape
    tm = min(MERGE_TM, t)
    rowblk = lambda idx: pl.BlockSpec((tm, d), lambda i: (i, idx))
    const = lambda shape: pl.BlockSpec(shape, lambda i: (0, 0))
    return pl.pallas_call(
        _merge_kernel,
        grid=(t // tm,),
        in_specs=[rowblk(0), rowblk(1), rowblk(2), rowblk(0), rowblk(0), rowblk(0), rowblk(0),
                  const((d, d)), const((1, d)), const((1, d))],
        out_specs=[rowblk(0), rowblk(0)],
        out_shape=[jax.ShapeDtypeStruct((t, d), F32), jax.ShapeDtypeStruct((t, d), BF16)],
        compiler_params=_cparams(("parallel",)),
        name="merge_out_ln1",
    )(proj2, proj2, proj2, y_ssd, y_rwkv, y_mla, x, w_out.astype(BF16), ln_w.reshape(1, d), ln_b.reshape(1, d))


FFN_TM = 512
FFN_TF = 512


def _ffn_kernel(xb_ref, x_ref, w1_ref, w3_ref, w2_ref, lnw_ref, lnb_ref, o_ref, ob_ref, acc_ref):
    f = pl.program_id(1)

    @pl.when(f == 0)
    def _():
        acc_ref[...] = jnp.zeros_like(acc_ref)

    xb = xb_ref[...]
    h = _silu(_dot(xb, w1_ref[...])) * _dot(xb, w3_ref[...])
    acc_ref[...] += _dot(h.astype(BF16), w2_ref[...])

    @pl.when(f == pl.num_programs(1) - 1)
    def _():
        y = _layernorm(ALPHA * x_ref[...] + acc_ref[...], lnw_ref[...], lnb_ref[...])
        o_ref[...] = y
        ob_ref[...] = y.astype(BF16)


def _ffn_dense(xb, x, w1, w3, w2, ln_w, ln_b):
    t, d = x.shape
    ff = w1.shape[1]
    tm, tf = min(FFN_TM, t), FFN_TF
    return pl.pallas_call(
        _ffn_kernel,
        grid=(t // tm, ff // tf),
        in_specs=[pl.BlockSpec((tm, d), lambda i, f: (i, 0)), pl.BlockSpec((tm, d), lambda i, f: (i, 0)),
                  pl.BlockSpec((d, tf), lambda i, f: (0, f)), pl.BlockSpec((d, tf), lambda i, f: (0, f)),
                  pl.BlockSpec((tf, d), lambda i, f: (f, 0)),
                  pl.BlockSpec((1, d), lambda i, f: (0, 0)), pl.BlockSpec((1, d), lambda i, f: (0, 0))],
        out_specs=[pl.BlockSpec((tm, d), lambda i, f: (i, 0)), pl.BlockSpec((tm, d), lambda i, f: (i, 0))],
        out_shape=[jax.ShapeDtypeStruct((t, d), F32), jax.ShapeDtypeStruct((t, d), BF16)],
        scratch_shapes=[pltpu.VMEM((tm, d), F32)],
        compiler_params=_cparams(("parallel", "arbitrary")),
        name="ffn_dense_ln2",
    )(xb, x, w1.astype(BF16), w3.astype(BF16), w2.astype(BF16), ln_w.reshape(1, d), ln_b.reshape(1, d))


ROUTER_TM = 512
MOE_TM = 512
MOE_TF = 512
COMBINE_TM = 256


def _router_kernel(x_ref, wr_ref, idx_ref, wgt_ref):
    logits = _dot_hi(x_ref[...], wr_ref[...])
    lane = lax.broadcasted_iota(jnp.int32, logits.shape, 1)
    lg = jnp.where(lane < N_EXPERTS, logits, -jnp.inf)
    m1 = jnp.max(lg, axis=-1, keepdims=True)
    i1 = jnp.min(jnp.where(lg == m1, lane, LANES), axis=-1, keepdims=True)
    lg2 = jnp.where(lane == i1, -jnp.inf, lg)
    m2 = jnp.max(lg2, axis=-1, keepdims=True)
    i2 = jnp.min(jnp.where(lg2 == m2, lane, LANES), axis=-1, keepdims=True)
    e = jnp.exp(m2 - m1)
    idx_ref[...] = jnp.where(lane == 0, i1, jnp.where(lane == 1, i2, 0))
    wgt_ref[...] = jnp.where(lane == 0, 1.0 / (1.0 + e), jnp.where(lane == 1, e / (1.0 + e), 0.0))


def _moe_ffn_kernel(be_ref, tok_ref, x_hbm, w1_ref, w3_ref, w2_ref, o_ref, xg_ref, xb_ref, acc_ref, sem):
    f = pl.program_id(1)
    rows = xg_ref.shape[0]

    def row_copy(r):
        return pltpu.make_async_copy(x_hbm.at[pl.ds(tok_ref[0, 0, r], 1), :], xg_ref.at[pl.ds(r, 1), :], sem)

    @pl.when(f == 0)
    def _():
        def start(r, c):
            row_copy(r).start()
            return c

        def wait(r, c):
            row_copy(r).wait()
            return c

        lax.fori_loop(0, rows, start, 0)
        lax.fori_loop(0, rows, wait, 0)
        xb_ref[...] = xg_ref[...].astype(BF16)
        acc_ref[...] = jnp.zeros_like(acc_ref)

    xb = xb_ref[...]
    h = _silu(_dot(xb, w1_ref[...])) * _dot(xb, w3_ref[...])
    acc_ref[...] += _dot(h.astype(BF16), w2_ref[...])

    @pl.when(f == pl.num_programs(1) - 1)
    def _():
        o_ref[...] = acc_ref[...]


def _combine_kernel(dst_ref, y_hbm, wgt_ref, x_ref, lnw_ref, lnb_ref, o_ref, yg_ref, sem):
    rows = x_ref.shape[0]

    def row_copy(r, k):
        return pltpu.make_async_copy(y_hbm.at[pl.ds(dst_ref[0, 0, TOP_K * r + k], 1), :],
                                     yg_ref.at[k, pl.ds(r, 1), :], sem)

    def start(r, c):
        for k in range(TOP_K):
            row_copy(r, k).start()
        return c

    def wait(r, c):
        for k in range(TOP_K):
            row_copy(r, k).wait()
        return c

    lax.fori_loop(0, rows, start, 0)
    lax.fori_loop(0, rows, wait, 0)
    wgt = wgt_ref[...]
    f = wgt[:, 0:1] * yg_ref[0] + wgt[:, 1:2] * yg_ref[1]
    o_ref[...] = _layernorm(ALPHA * x_ref[...] + f, lnw_ref[...], lnb_ref[...])


def _ffn_moe(x, router, w1, w3, w2, ln_w, ln_b):
    t, d = x.shape
    ne, _, ff = w1.shape
    tm = min(ROUTER_TM, t)
    rw = jnp.pad(router, ((0, 0), (0, LANES - ne)))
    idx, wgt = pl.pallas_call(
        _router_kernel,
        grid=(t // tm,),
        in_specs=[pl.BlockSpec((tm, d), lambda i: (i, 0)), pl.BlockSpec((d, LANES), lambda i: (0, 0))],
        out_specs=[pl.BlockSpec((tm, LANES), lambda i: (i, 0))] * 2,
        out_shape=[jax.ShapeDtypeStruct((t, LANES), jnp.int32), jax.ShapeDtypeStruct((t, LANES), F32)],
        compiler_params=_cparams(("parallel",)),
        name="moe_router",
    )(x, rw)

    blk = MOE_TM
    n_assign = t * TOP_K
    flat_e = idx[:, :TOP_K].reshape(-1)
    onehot = (flat_e[:, None] == jnp.arange(ne, dtype=jnp.int32)[None, :]).astype(jnp.int32)
    csum = jnp.cumsum(onehot, axis=0)
    rank = jnp.take_along_axis(csum, flat_e[:, None], axis=1)[:, 0] - 1
    counts = csum[-1]
    padded = (counts + blk - 1) // blk * blk
    pad_end = jnp.cumsum(padded)
    dest = (pad_end - padded)[flat_e] + rank
    n_blocks = -(-(n_assign + ne * (blk - 1)) // blk)
    n_rows = n_blocks * blk
    token_of_row = jnp.zeros((n_rows,), jnp.int32).at[dest].set(jnp.arange(n_assign, dtype=jnp.int32) // TOP_K)
    block_expert = jnp.minimum(jnp.searchsorted(pad_end, jnp.arange(n_blocks, dtype=jnp.int32) * blk, side='right'),
                               ne - 1).astype(jnp.int32)

    tf = MOE_TF
    yb = pl.pallas_call(
        _moe_ffn_kernel,
        grid_spec=pltpu.PrefetchScalarGridSpec(
            num_scalar_prefetch=1,
            grid=(n_blocks, ff // tf),
            in_specs=[pl.BlockSpec((1, 1, blk), lambda i, f, be: (i, 0, 0), memory_space=pltpu.SMEM),
                      pl.BlockSpec(memory_space=pl.ANY),
                      pl.BlockSpec((None, d, tf), lambda i, f, be: (be[i], 0, f)),
                      pl.BlockSpec((None, d, tf), lambda i, f, be: (be[i], 0, f)),
                      pl.BlockSpec((None, tf, d), lambda i, f, be: (be[i], f, 0))],
            out_specs=pl.BlockSpec((blk, d), lambda i, f, be: (i, 0)),
            scratch_shapes=[pltpu.VMEM((blk, d), F32), pltpu.VMEM((blk, d), BF16), pltpu.VMEM((blk, d), F32),
                            pltpu.SemaphoreType.DMA(())]),
        out_shape=jax.ShapeDtypeStruct((n_rows, d), F32),
        compiler_params=_cparams(("arbitrary", "arbitrary")),
        name="moe_expert_ffn",
    )(block_expert, token_of_row.reshape(n_blocks, 1, blk), x, w1.astype(BF16), w3.astype(BF16), w2.astype(BF16))

    tc = min(COMBINE_TM, t)
    return pl.pallas_call(
        _combine_kernel,
        grid=(t // tc,),
        in_specs=[pl.BlockSpec((1, 1, TOP_K * tc), lambda i: (i, 0, 0), memory_space=pltpu.SMEM),
                  pl.BlockSpec(memory_space=pl.ANY),
                  pl.BlockSpec((tc, LANES), lambda i: (i, 0)), pl.BlockSpec((tc, d), lambda i: (i, 0)),
                  pl.BlockSpec((1, d), lambda i: (0, 0)), pl.BlockSpec((1, d), lambda i: (0, 0))],
        out_specs=pl.BlockSpec((tc, d), lambda i: (i, 0)),
        out_shape=jax.ShapeDtypeStruct((t, d), F32),
        scratch_shapes=[pltpu.VMEM((TOP_K, tc, d), F32), pltpu.SemaphoreType.DMA(())],
        compiler_params=_cparams(("arbitrary",)),
        name="moe_combine_ln2",
    )(dest.reshape(t // tc, 1, TOP_K * tc), yb, wgt, x, ln_w.reshape(1, d), ln_b.reshape(1, d))


def _pack_w_in(w, w_vres):
    d = w.shape[0]
    o = 0

    def take(n):
        nonlocal o
        s = w[:, o:o + n]
        o += n
        return s

    padc = lambda s, n: jnp.pad(s, ((0, 0), (0, n - s.shape[1])))
    gates = take(3 * W2K)
    z = take(W2K)
    xbc = take(W2K + 2 * SSD_BC)
    dt = take(SSD_HEADS)
    qlat = take(MLA_RANK)
    kvlat = take(MLA_RANK)
    kpe = take(MLA_ROPE)
    rkv = take(3 * W2K)
    w_lo, a_lo, g_lo = take(LORA_W), take(LORA_A), take(LORA_G)
    v_lo = jnp.zeros((d, LANES), w.dtype) if w_vres is None else padc(w_vres, LANES)
    kpe_rot = jnp.concatenate([-kpe[:, MLA_ROPE // 2:], kpe[:, :MLA_ROPE // 2]], axis=1)
    misc = jnp.concatenate([padc(w_lo, LANES), padc(a_lo, LANES), g_lo, v_lo, padc(dt, LANES), kpe, kpe_rot], axis=1)
    return jnp.concatenate([gates, z, rkv, xbc, qlat, kvlat, padc(misc, 1024)], axis=1).astype(BF16)


PROJ_TM = 1024
PROJ_TN = 1024


def kernel(x, positions, w_in, w_in_vres, w_out, ssd_conv_w, ssd_conv_b, ssd_dt_bias, ssd_a_log, ssd_d, ssd_norm_w, rwkv_mu, rwkv_mu_vres, rwkv_w0, rwkv_w2, rwkv_a0, rwkv_a2, rwkv_g2, rwkv_v0, rwkv_v2, rwkv_k_k, rwkv_k_a, rwkv_r_k, rwkv_ln_w, rwkv_ln_b, mla_q_norm_w, mla_w_q_b, mla_kv_norm_w, mla_w_kv_b, ln1_w, ln1_b, ln2_w, ln2_b, ffn_w1, ffn_w3, ffn_w2, moe_router, moe_w1, moe_w3, moe_w2):
    bn, sn, d = x.shape
    t = bn * sn
    cos, sin = _rope_tables(positions)
    xf = x.reshape(t, d)
    xb = xf.astype(BF16)
    v_first = None
    for l in range(DEPTH):
        if l == 0:
            wp, mu, v0, v2 = _pack_w_in(w_in[l], None), rwkv_mu[l], None, None
        else:
            wp = _pack_w_in(w_in[l], w_in_vres[l - 1])
            mu = jnp.concatenate([rwkv_mu[l], rwkv_mu_vres[l - 1]], axis=0)
            v0, v2 = rwkv_v0[l - 1], rwkv_v2[l - 1]
        proj2 = _matmul(xb, wp, F32, min(PROJ_TM, t), PROJ_TN)
        proj = proj2.reshape(bn, sn, N_PROJ)
        y_ssd = _ssd_mixer(proj, ssd_conv_w[l], ssd_conv_b[l], ssd_dt_bias[l], ssd_a_log[l], ssd_d[l], ssd_norm_w[l])
        y_rwkv, v_first = _rwkv_mixer(proj, v_first, mu, rwkv_w0[l], rwkv_w2[l], rwkv_a0[l], rwkv_a2[l], rwkv_g2[l],
                                      rwkv_k_k[l], rwkv_k_a[l], rwkv_r_k[l], rwkv_ln_w[l], rwkv_ln_b[l], v0, v2)
        y_mla = _mla_mixer(proj, cos, sin, mla_q_norm_w[l], mla_w_q_b[l], mla_kv_norm_w[l], mla_w_kv_b[l])
        x1, x1b = _merge_out(proj2, y_ssd.reshape(t, d), y_rwkv.reshape(t, d), y_mla.reshape(t, d), xf, w_out[l],
                             ln1_w[l], ln1_b[l])
        if l % 2 == 0:
            xf, xb = _ffn_dense(x1b, x1, ffn_w1[l // 2], ffn_w3[l // 2], ffn_w2[l // 2], ln2_w[l], ln2_b[l])
        else:
            xf = _ffn_moe(x1, moe_router[l // 2], moe_w1[l // 2], moe_w3[l // 2], moe_w2[l // 2], ln2_w[l], ln2_b[l])
            xb = xf.astype(BF16)
    return xf.reshape(bn, sn, d)
```

```python
import functools
import math

import jax
import jax.numpy as jnp
import numpy as np
from jax import lax
from jax.experimental import pallas as pl
from jax.experimental.pallas import tpu as pltpu

F32 = jnp.float32
BF16 = jnp.bfloat16

D_MODEL = 2048
DEPTH = 2
ALPHA = (2 * DEPTH) ** 0.25
LN_EPS = 1e-5
RMS_EPS = 1e-6
SSD_HEADS, SSD_HEAD_DIM, SSD_GROUPS, SSD_STATE, SSD_CONV = 32, 64, 4, 128, 4
SSD_BC = SSD_GROUPS * SSD_STATE
RWKV_HEADS, RWKV_HEAD = 32, 64
RWKV_GN_EPS = 64e-5
LORA_W, LORA_A, LORA_G, LORA_V = 96, 96, 256, 64
MLA_HEADS, MLA_NOPE, MLA_ROPE, MLA_V, MLA_RANK = 16, 128, 64, 128, 512
MLA_QK = MLA_NOPE + MLA_ROPE
ROPE_THETA = 10000.0
D_FF = 5632
N_EXPERTS = 8
TOP_K = 2

LANES = 128
SUBLANES = 8
VMEM_LIMIT = 56 * 1024 * 1024

W2K = 2048
COL_GATE = 0
COL_Z = 3 * W2K
COL_R = 4 * W2K
COL_XS = 7 * W2K
COL_BC = 8 * W2K
COL_QLAT = COL_BC + 1024
COL_KVLAT = COL_QLAT + 512
COL_MISC = COL_KVLAT + 512
MISC_W, MISC_A, MISC_G, MISC_V, MISC_DT, MISC_KPE = 0, 128, 256, 512, 640, 768
N_PROJ = COL_MISC + 1024

SSD_Q = 128
RW_T = 64
RW_BLK = 128


def _cparams(sem, vmem=VMEM_LIMIT):
    return pltpu.CompilerParams(dimension_semantics=sem, vmem_limit_bytes=vmem)


def _dot(a, b):
    return jnp.dot(a, b, preferred_element_type=F32)


def _dot_nt(a, b):
    return lax.dot_general(a, b, (((1,), (1,)), ((), ())), preferred_element_type=F32)


def _split3(a):
    hi = a.astype(BF16)
    r1 = a - hi.astype(F32)
    mid = r1.astype(BF16)
    lo = (r1 - mid.astype(F32)).astype(BF16)
    return hi, mid, lo


def _dot_exact_rhs(a_bf, b):
    hi, mid, lo = _split3(b)
    return _dot(a_bf, hi) + _dot(a_bf, mid) + _dot(a_bf, lo)


def _dot_exact_lhs(a, b_bf):
    hi, mid, lo = _split3(a)
    return _dot(hi, b_bf) + _dot(mid, b_bf) + _dot(lo, b_bf)


def _sigmoid(x):
    return 1.0 / (1.0 + jnp.exp(-x))


def _silu(x):
    return x * _sigmoid(x)


def _softplus(x):
    return jnp.maximum(x, 0.0) + jnp.log(1.0 + jnp.exp(-jnp.abs(x)))


def _layernorm(x, w, b):
    mu = jnp.mean(x, axis=-1, keepdims=True)
    xc = x - mu
    var = jnp.mean(xc * xc, axis=-1, keepdims=True)
    return xc * lax.rsqrt(var + LN_EPS) * w + b


def _mm_kernel(x_ref, w_ref, o_ref):
    o_ref[...] = _dot(x_ref[...], w_ref[...]).astype(o_ref.dtype)


def _matmul(x, w, out_dtype, tm, tn):
    m, k = x.shape
    n = w.shape[1]
    return pl.pallas_call(
        _mm_kernel,
        grid=(m // tm, n // tn),
        in_specs=[pl.BlockSpec((tm, k), lambda i, j: (i, 0)),
                  pl.BlockSpec((k, tn), lambda i, j: (0, j))],
        out_specs=pl.BlockSpec((tm, tn), lambda i, j: (i, j)),
        out_shape=jax.ShapeDtypeStruct((m, n), out_dtype),
        compiler_params=_cparams(("parallel", "arbitrary")),
        name="in_proj",
    )(x, w)


def _rope_kernel(pos_ref, freq_ref, cos_ref, sin_ref):
    ang = pos_ref[...] * freq_ref[...]
    valid = lax.broadcasted_iota(jnp.int32, ang.shape, 1) < MLA_ROPE
    cos_ref[...] = jnp.where(valid, jnp.cos(ang), 0.0)
    sin_ref[...] = jnp.where(valid, jnp.sin(ang), 0.0)


def _rope_tables(positions):
    t = positions.size
    tm = min(t, 1024)
    pos = positions.reshape(t, 1).astype(F32)
    inv_freq = ROPE_THETA ** (-jnp.arange(0, MLA_ROPE, 2, dtype=F32) / MLA_ROPE)
    freq = jnp.concatenate([inv_freq, inv_freq, jnp.zeros((LANES - MLA_ROPE,), F32)]).reshape(1, LANES)
    return pl.pallas_call(
        _rope_kernel,
        grid=(t // tm,),
        in_specs=[pl.BlockSpec((tm, 1), lambda i: (i, 0)),
                  pl.BlockSpec((1, LANES), lambda i: (0, 0))],
        out_specs=[pl.BlockSpec((tm, LANES), lambda i: (i, 0))] * 2,
        out_shape=[jax.ShapeDtypeStruct((t, LANES), F32)] * 2,
        compiler_params=_cparams(("parallel",)),
        name="rope_tables",
    )(pos, freq)


def _ssd_kernel(z_ref, xs_ref, bc_ref, misc_ref, cwx_ref, cwb_ref, cbx_ref, cbb_ref, dtb_ref, alog_ref,
                d_ref, nw_ref, e_ref, y_ref, state_ref, bufx_ref, bufb_ref):
    q = SSD_Q
    hp = SSD_HEADS // SSD_GROUPS * SSD_HEAD_DIM

    @pl.when(pl.program_id(1) == 0)
    def _():
        state_ref[...] = jnp.zeros_like(state_ref)
        bufx_ref[0:SUBLANES, :] = jnp.zeros((SUBLANES, bufx_ref.shape[1]), F32)
        bufb_ref[0:SUBLANES, :] = jnp.zeros((SUBLANES, bufb_ref.shape[1]), F32)

    bufx_ref[SUBLANES:SUBLANES + q, :] = xs_ref[...]
    bufb_ref[SUBLANES:SUBLANES + q, :] = bc_ref[...]

    def conv(buf_ref, w_ref, b_ref):
        acc = b_ref[...] + w_ref[SSD_CONV - 1:SSD_CONV, :] * buf_ref[SUBLANES:SUBLANES + q, :]
        for k in range(SSD_CONV - 1):
            off = SUBLANES - (SSD_CONV - 1) + k
            acc = acc + w_ref[k:k + 1, :] * buf_ref[off:off + q, :]
        return _silu(acc)

    xs = conv(bufx_ref, cwx_ref, cbx_ref)
    bc = conv(bufb_ref, cwb_ref, cbb_ref)
    bufx_ref[0:SUBLANES, :] = bufx_ref[q:q + SUBLANES, :]
    bufb_ref[0:SUBLANES, :] = bufb_ref[q:q + SUBLANES, :]

    dt = _softplus(misc_ref[:, MISC_DT:MISC_DT + LANES] + dtb_ref[...])
    da = dt * (-jnp.exp(alog_ref[...]))
    row = lax.broadcasted_iota(jnp.int32, (q, q), 0)
    col = lax.broadcasted_iota(jnp.int32, (q, q), 1)
    causal = row >= col
    tri = jnp.where(causal, 1.0, 0.0).astype(BF16)
    cum = _dot_exact_rhs(tri, da)
    cum_t = cum.T
    ecum = jnp.exp(cum)
    toend = jnp.exp(cum[q - 1:q, :] - cum)
    e_mat = e_ref[...]
    dt_e = _dot_exact_lhs(dt, e_mat)
    ecum_e = _dot_exact_lhs(ecum, e_mat)
    toend_e = _dot_exact_lhs(toend, e_mat)

    xdt = xs * dt_e
    xdt_b = xdt.astype(BF16)
    xw_b = (xdt * toend_e).astype(BF16)
    lane = lax.broadcasted_iota(jnp.int32, (q, LANES), 1)
    lo_half = lane < SSD_HEAD_DIM

    y_groups = []
    for g in range(SSD_GROUPS):
        b_g = bc[:, g * SSD_STATE:(g + 1) * SSD_STATE]
        c_g = bc[:, SSD_BC + g * SSD_STATE:SSD_BC + (g + 1) * SSD_STATE]
        b_gb = b_g.astype(BF16)
        c_gb = c_g.astype(BF16)
        cb = _dot_nt(c_gb, b_gb)
        st = state_ref[:, g * hp:(g + 1) * hp]
        y_inter = _dot(c_gb, st.astype(BF16)) * ecum_e[:, g * hp:(g + 1) * hp]
        parts = []
        for pr in range(hp // LANES):
            ms = []
            for e in range(2):
                h = g * (SSD_HEADS // SSD_GROUPS) + pr * 2 + e
                ci = jnp.broadcast_to(cum[:, h:h + 1], (q, q))
                cj = jnp.broadcast_to(cum_t[h:h + 1, :], (q, q))
                dec = jnp.exp(jnp.where(causal, ci - cj, -jnp.inf))
                ms.append((cb * dec).astype(BF16))
            lo = g * hp + pr * LANES
            xp = xdt_b[:, lo:lo + LANES]
            zero = jnp.zeros_like(xp)
            rhs = jnp.concatenate([jnp.where(lo_half, xp, zero), jnp.where(lo_half, zero, xp)], axis=0)
            parts.append(_dot(jnp.concatenate(ms, axis=1), rhs))
        y_groups.append(jnp.concatenate(parts, axis=1) + y_inter)
        upd = _dot(b_g.T.astype(BF16), xw_b[:, g * hp:(g + 1) * hp])
        state_ref[:, g * hp:(g + 1) * hp] = st * ecum_e[q - 1:q, g * hp:(g + 1) * hp] + upd

    y = jnp.concatenate(y_groups, axis=1) + d_ref[...] * xs
    y = y * _silu(z_ref[...])
    outs = []
    for g in range(SSD_GROUPS):
        yg = y[:, g * hp:(g + 1) * hp]
        outs.append(yg * lax.rsqrt(jnp.mean(yg * yg, axis=-1, keepdims=True) + RMS_EPS))
    y_ref[...] = jnp.concatenate(outs, axis=1) * nw_ref[...]


def _ssd_mixer(proj, conv_w, conv_b, dt_bias, a_log, d_skip, norm_w):
    bn, sn, _ = proj.shape
    w = SSD_HEADS * SSD_HEAD_DIM
    pad = lambda v: jnp.pad(v, (0, LANES - v.shape[0])).reshape(1, LANES)
    a_log_p = jnp.pad(a_log, (0, LANES - SSD_HEADS), constant_values=-jnp.inf).reshape(1, LANES)
    expand = jnp.pad(jnp.repeat(jnp.eye(SSD_HEADS, dtype=BF16), SSD_HEAD_DIM, axis=1),
                     ((0, LANES - SSD_HEADS), (0, 0)))
    row = lambda v: v.reshape(1, -1)
    const = lambda shape: pl.BlockSpec(shape, lambda b, c: (0, 0))
    blk = lambda width, idx: pl.BlockSpec((None, SSD_Q, width), lambda b, c: (b, c, idx))
    return pl.pallas_call(
        _ssd_kernel,
        grid=(bn, sn // SSD_Q),
        in_specs=[blk(w, COL_Z // w), blk(w, COL_XS // w), blk(1024, COL_BC // 1024), blk(1024, COL_MISC // 1024),
                  const((SSD_CONV, w)), const((SSD_CONV, 2 * SSD_BC)), const((1, w)), const((1, 2 * SSD_BC)),
                  const((1, LANES)), const((1, LANES)), const((1, w)), const((1, w)), const((LANES, w))],
        out_specs=pl.BlockSpec((None, SSD_Q, w), lambda b, c: (b, c, 0)),
        out_shape=jax.ShapeDtypeStruct((bn, sn, w), F32),
        scratch_shapes=[pltpu.VMEM((SSD_STATE, w), F32),
                        pltpu.VMEM((SSD_Q + SUBLANES, w), F32),
                        pltpu.VMEM((SSD_Q + SUBLANES, 2 * SSD_BC), F32)],
        compiler_params=_cparams(("parallel", "arbitrary")),
        name="ssd_mixer",
    )(proj, proj, proj, proj, conv_w[:, :w], conv_w[:, w:], row(conv_b[:w]), row(conv_b[w:]),
      pad(dt_bias), a_log_p, row(jnp.repeat(d_skip, SSD_HEAD_DIM)), row(norm_w), expand)


def _dot_hi(a, b):
    return jnp.dot(a, b, precision=lax.Precision.HIGHEST, preferred_element_type=F32)


def _split2(a):
    hi = a.astype(BF16)
    return hi, (a - hi.astype(F32)).astype(BF16)


def _head_sums(x, ones_blk):
    outs = []
    for s in range(x.shape[1] // LANES):
        hi, lo = _split2(x[:, s * LANES:(s + 1) * LANES])
        outs.append(_dot(hi, ones_blk) + _dot(lo, ones_blk))
    return jnp.concatenate(outs, axis=1)


def _stack_heads(x, lo_half):
    zero = jnp.zeros_like(x)
    return jnp.concatenate([jnp.where(lo_half, x, zero), jnp.where(lo_half, zero, x)], axis=0)


RW_PAIRS = RWKV_HEADS // 2
RW_GROUP = 8


def _rwkv_kernel(has_vres, *refs):
    if has_vres:
        (r_ref, k_ref, v_ref, misc_ref, vfirst_ref, mur_ref, muk_ref, muv_ref, mum_ref, w0_ref, w2_ref, a0_ref,
         a2_ref, g2_ref, kk_ref, ka_ref, rk_ref, lnw_ref, lnb_ref, v0_ref, v2_ref,
         y_ref, state_ref, carry_ref, carrym_ref, st_ref, yp_ref, pt_ref) = refs
    else:
        (r_ref, k_ref, v_ref, misc_ref, mur_ref, muk_ref, muv_ref, mum_ref, w0_ref, w2_ref, a0_ref,
         a2_ref, g2_ref, kk_ref, ka_ref, rk_ref, lnw_ref, lnb_ref,
         y_ref, vout_ref, state_ref, carry_ref, carrym_ref, st_ref, yp_ref, pt_ref) = refs
    tb, t = RW_BLK, RW_T
    w = RWKV_HEADS * RWKV_HEAD

    @pl.when(pl.program_id(1) == 0)
    def _():
        state_ref[...] = jnp.zeros_like(state_ref)
        carry_ref[...] = jnp.zeros_like(carry_ref)
        carrym_ref[...] = jnp.zeros_like(carrym_ref)

    first_row = lax.broadcasted_iota(jnp.int32, (tb, 1), 0) == 0

    def shift(p, carry_row, mu):
        prev = jnp.where(first_row, carry_row, pltpu.roll(p, 1, 0))
        return p + (prev - p) * mu

    rp, kp, vp, mp = r_ref[...], k_ref[...], v_ref[...], misc_ref[...]
    r = shift(rp, carry_ref[0:1, :], mur_ref[...])
    k = shift(kp, carry_ref[1:2, :], muk_ref[...])
    v = shift(vp, carry_ref[2:3, :], muv_ref[...])
    m = shift(mp, carrym_ref[0:1, :], mum_ref[...])
    carry_ref[0:1, :] = rp[tb - 1:tb, :]
    carry_ref[1:2, :] = kp[tb - 1:tb, :]
    carry_ref[2:3, :] = vp[tb - 1:tb, :]
    carrym_ref[0:1, :] = mp[tb - 1:tb, :]

    w_lo = m[:, MISC_W:MISC_W + LANES]
    a_lo = m[:, MISC_A:MISC_A + LANES]
    g_lo = m[:, MISC_G:MISC_G + LORA_G]
    log_w = -_softplus(-(w0_ref[...] + _dot_hi(jnp.tanh(w_lo), w2_ref[...]))) - 0.5
    lw = -jnp.exp(log_w)
    a = _sigmoid(a0_ref[...] + _dot_hi(a_lo, a2_ref[...]))
    g = _dot(_sigmoid(g_lo).astype(BF16), g2_ref[...])
    if has_vres:
        v_lo = m[:, MISC_V:MISC_V + LANES]
        v = v + (vfirst_ref[...] - v) * _sigmoid(v0_ref[...] + _dot_hi(v_lo, v2_ref[...]))
    else:
        vout_ref[...] = v

    lane = lax.broadcasted_iota(jnp.int32, (LANES, LANES), 1)
    rowi = lax.broadcasted_iota(jnp.int32, (LANES, LANES), 0)
    ones_blk = jnp.where((lane // RWKV_HEAD) == (rowi // RWKV_HEAD), 1.0, 0.0).astype(BF16)

    kk = k * kk_ref[...]
    kk = kk / jnp.maximum(jnp.sqrt(_head_sums(kk * kk, ones_blk)), 1e-12)
    k = k * (1.0 + (a - 1.0) * ka_ref[...])
    b = kk * a

    ti = lax.broadcasted_iota(jnp.int32, (t, t), 0)
    tj = lax.broadcasted_iota(jnp.int32, (t, t), 1)
    tri = jnp.where(ti >= tj, 1.0, 0.0).astype(BF16)

    for c in range(tb // t):
        sl = slice(c * t, (c + 1) * t)
        lw_c = lw[sl]
        cl = _dot_exact_rhs(tri, lw_c)
        cl_end = cl[t - 1:t, :]
        e_neg = jnp.exp(-cl)
        e_end = jnp.exp(cl_end - cl)
        ops = (kk[sl] * jnp.exp(cl - lw_c), r[sl] * jnp.exp(cl), k[sl] * e_neg, b[sl] * e_neg,
               v[sl], k[sl] * e_end, b[sl] * e_end)
        for pi in range(RW_PAIRS):
            ls = slice(pi * LANES, (pi + 1) * LANES)
            for oi, op in enumerate(ops):
                st_ref[c, oi, pi] = op[:, ls]
            pt_ref[c, pi] = jnp.broadcast_to(jnp.exp(cl_end[:, ls]), (SUBLANES, LANES))

    lo_half = lax.broadcasted_iota(jnp.int32, (t, LANES), 1) < RWKV_HEAD
    bi = lax.broadcasted_iota(jnp.int32, (2 * t, 2 * t), 0) % t
    bj = lax.broadcasted_iota(jnp.int32, (2 * t, 2 * t), 1) % t
    strict = bi > bj
    incl = bi >= bj

    def chunk_group(c, pis, hts):
        h2 = 2 * t
        stk = [[_stack_heads(st_ref[c, oi, pi], lo_half) for oi in range(7)] for pi in pis]
        lhs2 = [jnp.concatenate([s[0], s[1]], axis=0).astype(BF16) for s in stk]
        rhs2 = [jnp.concatenate([s[2], s[3]], axis=0).astype(BF16) for s in stk]
        amat = [_dot_nt(a, b) for a, b in zip(lhs2, rhs2)]
        sh = [_dot_nt(a, h.astype(BF16)) for a, h in zip(lhs2, hts)]
        vsb = [s[4].astype(BF16) for s in stk]
        x = [s_[0:h2] + _dot(jnp.where(strict, am[0:h2, 0:h2], 0.0).astype(BF16), v_)
             for s_, am, v_ in zip(sh, amat, vsb)]
        pw = [jnp.where(strict, am[0:h2, h2:], 0.0) for am in amat]
        n, sign = 1, -1.0
        while n < t:
            if 2 * n < t:
                res = [_dot(p_.astype(BF16), jnp.concatenate([p_, x_], axis=1).astype(BF16)) for p_, x_ in zip(pw, x)]
                x = [x_ + sign * r_[:, h2:] for x_, r_ in zip(x, res)]
                pw = [r_[:, 0:h2] for r_ in res]
            else:
                x = [x_ + sign * _dot(p_.astype(BF16), x_.astype(BF16)) for p_, x_ in zip(pw, x)]
            n, sign = 2 * n, 1.0
        new_hts = []
        for i, pi in enumerate(pis):
            am, s = amat[i], stk[i]
            a_r = jnp.concatenate([jnp.where(incl, am[h2:, 0:h2], 0.0), jnp.where(incl, -am[h2:, h2:], 0.0)], axis=1)
            ys = sh[i][h2:] + _dot(a_r.astype(BF16), jnp.concatenate([s[4], x[i]], axis=0).astype(BF16))
            yp_ref[pi, c * t:(c + 1) * t, :] = ys[0:t] + ys[t:]
            lhs3 = jnp.concatenate([s[4].T, -(x[i].T)], axis=1).astype(BF16)
            rhs3 = jnp.concatenate([s[5], s[6]], axis=0).astype(BF16)
            new_hts.append(hts[i] * pt_ref[c, pi][0:1, :] + _dot(lhs3, rhs3))
        return new_hts

    for g0 in range(0, RW_PAIRS, RW_GROUP):
        pis = list(range(g0, g0 + RW_GROUP))
        hts = [state_ref[pi] for pi in pis]
        for c in range(tb // t):
            hts = chunk_group(c, pis, hts)
        for pi, ht in zip(pis, hts):
            state_ref[pi] = ht

    y = jnp.concatenate([yp_ref[pi] for pi in range(RW_PAIRS)], axis=1)
    inv_n = 1.0 / RWKV_HEAD
    mu = _head_sums(y, ones_blk) * inv_n
    yc = y - mu
    var = _head_sums(yc * yc, ones_blk) * inv_n
    y = yc * lax.rsqrt(var + RWKV_GN_EPS) * lnw_ref[...] + lnb_ref[...]
    bonus = _head_sums(r * k * rk_ref[...], ones_blk) * v
    y_ref[...] = (y + bonus) * g


def _rwkv_mixer(proj, v_first, mu, w0, w2, a0, a2, g2, k_k, k_a, r_k, ln_w, ln_b, v0, v2):
    bn, sn, _ = proj.shape
    w = RWKV_HEADS * RWKV_HEAD
    has_vres = v_first is not None
    row = lambda x: x.reshape(1, -1)
    padrows = lambda x: jnp.pad(x, ((0, LANES - x.shape[0]), (0, 0)))
    padl = lambda x, n: jnp.pad(x, (0, n - x.shape[0]))
    mu_misc = [padl(mu[3 * w:3 * w + LORA_W], LANES), padl(mu[3 * w + LORA_W:3 * w + LORA_W + LORA_A], LANES),
               mu[3 * w + LORA_W + LORA_A:3 * w + LORA_W + LORA_A + LORA_G]]
    if has_vres:
        mu_misc.append(padl(mu[3 * w + LORA_W + LORA_A + LORA_G:], LANES))
    mu_m = padl(jnp.concatenate(mu_misc), 1024)
    const = lambda shape: pl.BlockSpec(shape, lambda b, c: (0,) * len(shape))
    blk = lambda width, idx: pl.BlockSpec((None, RW_BLK, width), lambda b, c: (b, c, idx))
    seq = pl.BlockSpec((None, RW_BLK, w), lambda b, c: (b, c, 0))
    in_specs = [blk(w, COL_R // w), blk(w, COL_R // w + 1), blk(w, COL_R // w + 2), blk(1024, COL_MISC // 1024)]
    args = [proj, proj, proj, proj]
    if has_vres:
        in_specs.append(seq)
        args.append(v_first)
    in_specs += [const((1, w))] * 3 + [const((1, 1024)), const((1, w)), const((LANES, w)), const((1, w)),
                                       const((LANES, w)), const((LORA_G, w))] + [const((1, w))] * 5
    args += [row(mu[:w]), row(mu[w:2 * w]), row(mu[2 * w:3 * w]), row(mu_m), row(w0), padrows(w2), row(a0),
             padrows(a2), g2.astype(BF16), row(k_k), row(k_a), row(r_k), row(ln_w), row(ln_b)]
    if has_vres:
        in_specs += [const((1, w)), const((LANES, w))]
        args += [row(v0), padrows(v2)]
    out_shape = [jax.ShapeDtypeStruct((bn, sn, w), F32)]
    out_specs = [seq]
    if not has_vres:
        out_shape.append(jax.ShapeDtypeStruct((bn, sn, w), F32))
        out_specs.append(seq)
    nc = RW_BLK // RW_T
    outs = pl.pallas_call(
        functools.partial(_rwkv_kernel, has_vres),
        grid=(bn, sn // RW_BLK),
        in_specs=in_specs,
        out_specs=out_specs,
        out_shape=out_shape,
        scratch_shapes=[pltpu.VMEM((RW_PAIRS, LANES, LANES), F32),
                        pltpu.VMEM((SUBLANES, w), F32),
                        pltpu.VMEM((SUBLANES, 1024), F32),
                        pltpu.VMEM((nc, 7, RW_PAIRS, RW_T, LANES), F32),
                        pltpu.VMEM((RW_PAIRS, RW_BLK, LANES), F32),
                        pltpu.VMEM((nc, RW_PAIRS, SUBLANES, LANES), F32)],
        compiler_params=_cparams(("parallel", "arbitrary")),
        name="rwkv7_mixer",
    )(*args)
    if has_vres:
        return outs[0], v_first
    return outs[0], outs[1]


MLA_TM = 256
ATT_TQ = 2048
ATT_TK = 1024
ATT_SUB = 256
HEAD_Q = 2 * LANES


def _rope_half(x2, cos, sin):
    return x2 * cos + pltpu.roll(x2, MLA_ROPE, 1) * sin


def _mla_prep_kernel(qlat_ref, kvlat_ref, misc_ref, cos_ref, sin_ref, qnw_ref, kvnw_ref, wq_ref, wkv_ref,
                     q_ref, kv_ref, kpe_ref):
    def rms(x, w):
        return (x * lax.rsqrt(jnp.mean(x * x, axis=-1, keepdims=True) + RMS_EPS) * w).astype(BF16)

    cos, sin = cos_ref[...], sin_ref[...]
    scale = MLA_QK ** -0.5
    q = _dot(rms(qlat_ref[...], qnw_ref[...]), wq_ref[...])
    for h in range(MLA_HEADS):
        lo = h * HEAD_Q
        q_ref[:, lo:lo + LANES] = (q[:, lo:lo + LANES] * scale).astype(BF16)
        q_ref[:, lo + LANES:lo + HEAD_Q] = (_rope_half(q[:, lo + LANES:lo + HEAD_Q], cos, sin) * scale).astype(BF16)
    kv_ref[...] = _dot(rms(kvlat_ref[...], kvnw_ref[...]), wkv_ref[...]).astype(BF16)
    kpe_ref[...] = _rope_half(misc_ref[:, MISC_KPE:MISC_KPE + LANES], cos, sin).astype(BF16)


def _flash_kernel(tq, tk, sub, qi_ref, ki_ref, q_ref, k_ref, kpe_ref, v_ref, o_ref, m_ref, acc_ref):
    step = pl.program_id(2)
    qi, ki = qi_ref[step], ki_ref[step]
    ratio = tq // tk

    @pl.when(ki == 0)
    def _():
        m_ref[...] = jnp.full_like(m_ref, -jnp.inf)
        acc_ref[...] = jnp.zeros_like(acc_ref)

    def update(diag):
        kcat = jnp.concatenate([k_ref[...], kpe_ref[...]], axis=1)
        vcat = jnp.concatenate([v_ref[...], jnp.ones((tk, LANES), BF16)], axis=1)
        for r in range(tq // sub):
            row_lo, row_hi = r * sub, (r + 1) * sub - 1
            ncols = tk
            masked = False
            if diag is not None:
                col_lo = diag * tk
                if row_hi < col_lo:
                    continue
                ncols = min(tk, -(-(row_hi - col_lo + 1) // HEAD_Q) * HEAD_Q)
                masked = row_lo < col_lo + ncols - 1
            rows = slice(row_lo, row_lo + sub)
            s = _dot_nt(q_ref[rows, :], kcat[:ncols])
            if masked:
                ri = row_lo + lax.broadcasted_iota(jnp.int32, s.shape, 0)
                ci = diag * tk + lax.broadcasted_iota(jnp.int32, s.shape, 1)
                s = jnp.where(ci <= ri, s, -jnp.inf)
            tiles = [s[:, j * LANES:(j + 1) * LANES] for j in range(ncols // LANES)]
            fold = tiles[0]
            for tl in tiles[1:]:
                fold = jnp.maximum(fold, tl)
            m_prev = m_ref[rows, :]
            m_new = jnp.maximum(m_prev, jnp.max(fold, axis=-1, keepdims=True))
            p = jnp.concatenate([jnp.exp(tl - m_new) for tl in tiles], axis=1).astype(BF16)
            alpha = jnp.exp(m_prev - m_new)
            acc_ref[rows, :] = (acc_ref[rows, :] * jnp.concatenate([alpha, alpha], axis=1)
                                + _dot(p, vcat[:ncols]))
            m_ref[rows, :] = m_new

    @pl.when(ki < qi * ratio)
    def _():
        update(None)

    for d in range(ratio):
        @pl.when(ki == qi * ratio + d)
        def _(d=d):
            update(d)

    @pl.when(ki == (qi + 1) * ratio - 1)
    def _():
        o_ref[...] = acc_ref[:, 0:MLA_V] / acc_ref[:, MLA_V:]


def _mla_mixer(proj, cos, sin, q_norm_w, w_q_b, kv_norm_w, w_kv_b):
    bn, sn, _ = proj.shape
    t = bn * sn
    proj2 = proj.reshape(t, N_PROJ)
    wq = w_q_b.reshape(MLA_RANK, MLA_HEADS, MLA_QK)
    pe = wq[..., MLA_NOPE:]
    rot = jnp.concatenate([-pe[..., MLA_ROPE // 2:], pe[..., :MLA_ROPE // 2]], axis=-1)
    wq = jnp.concatenate([wq, rot], axis=-1).reshape(MLA_RANK, MLA_HEADS * HEAD_Q).astype(BF16)
    wkv = w_kv_b.reshape(MLA_RANK, MLA_HEADS, MLA_NOPE + MLA_V)
    wkv = jnp.concatenate([wkv[..., :MLA_NOPE].reshape(MLA_RANK, -1), wkv[..., MLA_NOPE:].reshape(MLA_RANK, -1)],
                          axis=1).astype(BF16)
    nq, nkv = MLA_HEADS * HEAD_Q, MLA_HEADS * (MLA_NOPE + MLA_V)
    tm = min(MLA_TM, t)
    rowblk = lambda width, idx: pl.BlockSpec((tm, width), lambda i: (i, idx))
    const = lambda shape: pl.BlockSpec(shape, lambda i: (0, 0))
    q, kv, kpe = pl.pallas_call(
        _mla_prep_kernel,
        grid=(t // tm,),
        in_specs=[rowblk(MLA_RANK, COL_QLAT // MLA_RANK), rowblk(MLA_RANK, COL_KVLAT // MLA_RANK),
                  rowblk(1024, COL_MISC // 1024), rowblk(LANES, 0), rowblk(LANES, 0),
                  const((1, MLA_RANK)), const((1, MLA_RANK)), const((MLA_RANK, nq)), const((MLA_RANK, nkv))],
        out_specs=[rowblk(nq, 0), rowblk(nkv, 0), rowblk(LANES, 0)],
        out_shape=[jax.ShapeDtypeStruct((t, nq), BF16), jax.ShapeDtypeStruct((t, nkv), BF16),
                   jax.ShapeDtypeStruct((t, LANES), BF16)],
        compiler_params=_cparams(("parallel",)),
        name="mla_prep",
    )(proj2, proj2, proj2, cos, sin, q_norm_w.reshape(1, -1), kv_norm_w.reshape(1, -1), wq, wkv)

    tq, tk = min(ATT_TQ, sn), min(ATT_TK, sn)
    sub = min(ATT_SUB, tq)
    ratio = tq // tk
    pairs = [(a, b) for a in range(sn // tq) for b in range((a + 1) * ratio)]
    qi_arr = jnp.asarray([p[0] for p in pairs], jnp.int32)
    ki_arr = jnp.asarray([p[1] for p in pairs], jnp.int32)
    out = pl.pallas_call(
        functools.partial(_flash_kernel, tq, tk, sub),
        grid_spec=pltpu.PrefetchScalarGridSpec(
            num_scalar_prefetch=2,
            grid=(bn, MLA_HEADS, len(pairs)),
            in_specs=[pl.BlockSpec((None, tq, HEAD_Q), lambda b, h, s, qi, ki: (b, qi[s], h)),
                      pl.BlockSpec((None, tk, MLA_NOPE), lambda b, h, s, qi, ki: (b, ki[s], h)),
                      pl.BlockSpec((None, tk, LANES), lambda b, h, s, qi, ki: (b, ki[s], 0)),
                      pl.BlockSpec((None, tk, MLA_V), lambda b, h, s, qi, ki: (b, ki[s], MLA_HEADS + h))],
            out_specs=pl.BlockSpec((None, tq, MLA_V), lambda b, h, s, qi, ki: (b, qi[s], h)),
            scratch_shapes=[pltpu.VMEM((tq, LANES), F32), pltpu.VMEM((tq, 2 * MLA_V), F32)]),
        out_shape=jax.ShapeDtypeStruct((bn, sn, MLA_HEADS * MLA_V), F32),
        compiler_params=_cparams(("parallel", "parallel", "arbitrary")),
        name="mla_attention",
    )(qi_arr, ki_arr, q.reshape(bn, sn, nq), kv.reshape(bn, sn, nkv), kpe.reshape(bn, sn, LANES),
      kv.reshape(bn, sn, nkv))
    return out


MERGE_TM = 256


def _merge_kernel(g0_ref, g1_ref, g2_ref, ys_ref, yr_ref, ym_ref, x_ref, wo_ref, lnw_ref, lnb_ref, xo_ref, xb_ref):
    merged = (_sigmoid(g0_ref[...]) * ys_ref[...] + _sigmoid(g1_ref[...]) * yr_ref[...]
              + _sigmoid(g2_ref[...]) * ym_ref[...])
    h = ALPHA * x_ref[...] + _dot(merged.astype(BF16), wo_ref[...])
    y = _layernorm(h, lnw_ref[...], lnb_ref[...])
    xo_ref[...] = y
    xb_ref[...] = y.astype(BF16)


def _merge_out(proj2, y_ssd, y_rwkv, y_mla, x, w_out, ln_w, ln_b):
    t, d = x.shape
    tm = min(MERGE_TM, t)
    rowblk = lambda idx: pl.BlockSpec((tm, d), lambda i: (i, idx))
    const = lambda shape: pl.BlockSpec(shape, lambda i: (0, 0))
    return pl.pallas_call(
        _merge_kernel,
        grid=(t // tm,),
        in_specs=[rowblk(0), rowblk(1), rowblk(2), rowblk(0), rowblk(0), rowblk(0), rowblk(0),
                  const((d, d)), const((1, d)), const((1, d))],
        out_specs=[rowblk(0), rowblk(0)],
        out_shape=[jax.ShapeDtypeStruct((t, d), F32), jax.ShapeDtypeStruct((t, d), BF16)],
        compiler_params=_cparams(("parallel",)),
        name="merge_out_ln1",
    )(proj2, proj2, proj2, y_ssd, y_rwkv, y_mla, x, w_out.astype(BF16), ln_w.reshape(1, d), ln_b.reshape(1, d))


FFN_TM = 512
FFN_TF = 512


def _ffn_kernel(xb_ref, x_ref, w1_ref, w3_ref, w2_ref, lnw_ref, lnb_ref, o_ref, ob_ref, acc_ref):
    f = pl.program_id(1)

    @pl.when(f == 0)
    def _():
        acc_ref[...] = jnp.zeros_like(acc_ref)

    xb = xb_ref[...]
    h = _silu(_dot(xb, w1_ref[...])) * _dot(xb, w3_ref[...])
    acc_ref[...] += _dot(h.astype(BF16), w2_ref[...])

    @pl.when(f == pl.num_programs(1) - 1)
    def _():
        y = _layernorm(ALPHA * x_ref[...] + acc_ref[...], lnw_ref[...], lnb_ref[...])
        o_ref[...] = y
        ob_ref[...] = y.astype(BF16)


def _ffn_dense(xb, x, w1, w3, w2, ln_w, ln_b):
    t, d = x.shape
    ff = w1.shape[1]
    tm, tf = min(FFN_TM, t), FFN_TF
    return pl.pallas_call(
        _ffn_kernel,
        grid=(t // tm, ff // tf),
        in_specs=[pl.BlockSpec((tm, d), lambda i, f: (i, 0)), pl.BlockSpec((tm, d), lambda i, f: (i, 0)),
                  pl.BlockSpec((d, tf), lambda i, f: (0, f)), pl.BlockSpec((d, tf), lambda i, f: (0, f)),
                  pl.BlockSpec((tf, d), lambda i, f: (f, 0)),
                  pl.BlockSpec((1, d), lambda i, f: (0, 0)), pl.BlockSpec((1, d), lambda i, f: (0, 0))],
        out_specs=[pl.BlockSpec((tm, d), lambda i, f: (i, 0)), pl.BlockSpec((tm, d), lambda i, f: (i, 0))],
        out_shape=[jax.ShapeDtypeStruct((t, d), F32), jax.ShapeDtypeStruct((t, d), BF16)],
        scratch_shapes=[pltpu.VMEM((tm, d), F32)],
        compiler_params=_cparams(("parallel", "arbitrary")),
        name="ffn_dense_ln2",
    )(xb, x, w1.astype(BF16), w3.astype(BF16), w2.astype(BF16), ln_w.reshape(1, d), ln_b.reshape(1, d))


ROUTER_TM = 512
MOE_TM = 512
MOE_TF = 512
COMBINE_TM = 256


def _router_kernel(x_ref, wr_ref, idx_ref, wgt_ref):
    logits = _dot_hi(x_ref[...], wr_ref[...])
    lane = lax.broadcasted_iota(jnp.int32, logits.shape, 1)
    lg = jnp.where(lane < N_EXPERTS, logits, -jnp.inf)
    m1 = jnp.max(lg, axis=-1, keepdims=True)
    i1 = jnp.min(jnp.where(lg == m1, lane, LANES), axis=-1, keepdims=True)
    lg2 = jnp.where(lane == i1, -jnp.inf, lg)
    m2 = jnp.max(lg2, axis=-1, keepdims=True)
    i2 = jnp.min(jnp.where(lg2 == m2, lane, LANES), axis=-1, keepdims=True)
    e = jnp.exp(m2 - m1)
    idx_ref[...] = jnp.where(lane == 0, i1, jnp.where(lane == 1, i2, 0))
    wgt_ref[...] = jnp.where(lane == 0, 1.0 / (1.0 + e), jnp.where(lane == 1, e / (1.0 + e), 0.0))


def _moe_ffn_kernel(be_ref, tok_ref, x_hbm, w1_ref, w3_ref, w2_ref, o_ref, xg_ref, xb_ref, acc_ref, sem):
    f = pl.program_id(1)
    rows = xg_ref.shape[0]

    def row_copy(r):
        return pltpu.make_async_copy(x_hbm.at[pl.ds(tok_ref[0, 0, r], 1), :], xg_ref.at[pl.ds(r, 1), :], sem)

    @pl.when(f == 0)
    def _():
        def start(r, c):
            row_copy(r).start()
            return c

        def wait(r, c):
            row_copy(r).wait()
            return c

        lax.fori_loop(0, rows, start, 0)
        lax.fori_loop(0, rows, wait, 0)
        xb_ref[...] = xg_ref[...].astype(BF16)
        acc_ref[...] = jnp.zeros_like(acc_ref)

    xb = xb_ref[...]
    h = _silu(_dot(xb, w1_ref[...])) * _dot(xb, w3_ref[...])
    acc_ref[...] += _dot(h.astype(BF16), w2_ref[...])

    @pl.when(f == pl.num_programs(1) - 1)
    def _():
        o_ref[...] = acc_ref[...]


def _combine_kernel(dst_ref, y_hbm, wgt_ref, x_ref, lnw_ref, lnb_ref, o_ref, yg_ref, sem):
    rows = x_ref.shape[0]

    def row_copy(r, k):
        return pltpu.make_async_copy(y_hbm.at[pl.ds(dst_ref[0, 0, TOP_K * r + k], 1), :],
                                     yg_ref.at[k, pl.ds(r, 1), :], sem)

    def start(r, c):
        for k in range(TOP_K):
            row_copy(r, k).start()
        return c

    def wait(r, c):
        for k in range(TOP_K):
            row_copy(r, k).wait()
        return c

    lax.fori_loop(0, rows, start, 0)
    lax.fori_loop(0, rows, wait, 0)
    wgt = wgt_ref[...]
    f = wgt[:, 0:1] * yg_ref[0] + wgt[:, 1:2] * yg_ref[1]
    o_ref[...] = _layernorm(ALPHA * x_ref[...] + f, lnw_ref[...], lnb_ref[...])


def _ffn_moe(x, router, w1, w3, w2, ln_w, ln_b):
    t, d = x.shape
    ne, _, ff = w1.shape
    tm = min(ROUTER_TM, t)
    rw = jnp.pad(router, ((0, 0), (0, LANES - ne)))
    idx, wgt = pl.pallas_call(
        _router_kernel,
        grid=(t // tm,),
        in_specs=[pl.BlockSpec((tm, d), lambda i: (i, 0)), pl.BlockSpec((d, LANES), lambda i: (0, 0))],
        out_specs=[pl.BlockSpec((tm, LANES), lambda i: (i, 0))] * 2,
        out_shape=[jax.ShapeDtypeStruct((t, LANES), jnp.int32), jax.ShapeDtypeStruct((t, LANES), F32)],
        compiler_params=_cparams(("parallel",)),
        name="moe_router",
    )(x, rw)

    blk = MOE_TM
    n_assign = t * TOP_K
    flat_e = idx[:, :TOP_K].reshape(-1)
    onehot = (flat_e[:, None] == jnp.arange(ne, dtype=jnp.int32)[None, :]).astype(jnp.int32)
    csum = jnp.cumsum(onehot, axis=0)
    rank = jnp.take_along_axis(csum, flat_e[:, None], axis=1)[:, 0] - 1
    counts = csum[-1]
    padded = (counts + blk - 1) // blk * blk
    pad_end = jnp.cumsum(padded)
    dest = (pad_end - padded)[flat_e] + rank
    n_blocks = -(-(n_assign + ne * (blk - 1)) // blk)
    n_rows = n_blocks * blk
    token_of_row = jnp.zeros((n_rows,), jnp.int32).at[dest].set(jnp.arange(n_assign, dtype=jnp.int32) // TOP_K)
    block_expert = jnp.minimum(jnp.searchsorted(pad_end, jnp.arange(n_blocks, dtype=jnp.int32) * blk, side='right'),
                               ne - 1).astype(jnp.int32)

    tf = MOE_TF
    yb = pl.pallas_call(
        _moe_ffn_kernel,
        grid_spec=pltpu.PrefetchScalarGridSpec(
            num_scalar_prefetch=1,
            grid=(n_blocks, ff // tf),
            in_specs=[pl.BlockSpec((1, 1, blk), lambda i, f, be: (i, 0, 0), memory_space=pltpu.SMEM),
                      pl.BlockSpec(memory_space=pl.ANY),
                      pl.BlockSpec((None, d, tf), lambda i, f, be: (be[i], 0, f)),
                      pl.BlockSpec((None, d, tf), lambda i, f, be: (be[i], 0, f)),
                      pl.BlockSpec((None, tf, d), lambda i, f, be: (be[i], f, 0))],
            out_specs=pl.BlockSpec((blk, d), lambda i, f, be: (i, 0)),
            scratch_shapes=[pltpu.VMEM((blk, d), F32), pltpu.VMEM((blk, d), BF16), pltpu.VMEM((blk, d), F32),
                            pltpu.SemaphoreType.DMA(())]),
        out_shape=jax.ShapeDtypeStruct((n_rows, d), F32),
        compiler_params=_cparams(("arbitrary", "arbitrary")),
        name="moe_expert_ffn",
    )(block_expert, token_of_row.reshape(n_blocks, 1, blk), x, w1.astype(BF16), w3.astype(BF16), w2.astype(BF16))

    tc = min(COMBINE_TM, t)
    return pl.pallas_call(
        _combine_kernel,
        grid=(t // tc,),
        in_specs=[pl.BlockSpec((1, 1, TOP_K * tc), lambda i: (i, 0, 0), memory_space=pltpu.SMEM),
                  pl.BlockSpec(memory_space=pl.ANY),
                  pl.BlockSpec((tc, LANES), lambda i: (i, 0)), pl.BlockSpec((tc, d), lambda i: (i, 0)),
                  pl.BlockSpec((1, d), lambda i: (0, 0)), pl.BlockSpec((1, d), lambda i: (0, 0))],
        out_specs=pl.BlockSpec((tc, d), lambda i: (i, 0)),
        out_shape=jax.ShapeDtypeStruct((t, d), F32),
        scratch_shapes=[pltpu.VMEM((TOP_K, tc, d), F32), pltpu.SemaphoreType.DMA(())],
        compiler_params=_cparams(("arbitrary",)),
        name="moe_combine_ln2",
    )(dest.reshape(t // tc, 1, TOP_K * tc), yb, wgt, x, ln_w.reshape(1, d), ln_b.reshape(1, d))


def _pack_w_in(w, w_vres):
    d = w.shape[0]
    o = 0

    def take(n):
        nonlocal o
        s = w[:, o:o + n]
        o += n
        return s

    padc = lambda s, n: jnp.pad(s, ((0, 0), (0, n - s.shape[1])))
    gates = take(3 * W2K)
    z = take(W2K)
    xbc = take(W2K + 2 * SSD_BC)
    dt = take(SSD_HEADS)
    qlat = take(MLA_RANK)
    kvlat = take(MLA_RANK)
    kpe = take(MLA_ROPE)
    rkv = take(3 * W2K)
    w_lo, a_lo, g_lo = take(LORA_W), take(LORA_A), take(LORA_G)
    v_lo = jnp.zeros((d, LANES), w.dtype) if w_vres is None else padc(w_vres, LANES)
    kpe_rot = jnp.concatenate([-kpe[:, MLA_ROPE // 2:], kpe[:, :MLA_ROPE // 2]], axis=1)
    misc = jnp.concatenate([padc(w_lo, LANES), padc(a_lo, LANES), g_lo, v_lo, padc(dt, LANES), kpe, kpe_rot], axis=1)
    return jnp.concatenate([gates, z, rkv, xbc, qlat, kvlat, padc(misc, 1024)], axis=1).astype(BF16)


PROJ_TM = 1024
PROJ_TN = 1024


def kernel(x, positions, w_in, w_in_vres, w_out, ssd_conv_w, ssd_conv_b, ssd_dt_bias, ssd_a_log, ssd_d, ssd_norm_w, rwkv_mu, rwkv_mu_vres, rwkv_w0, rwkv_w2, rwkv_a0, rwkv_a2, rwkv_g2, rwkv_v0, rwkv_v2, rwkv_k_k, rwkv_k_a, rwkv_r_k, rwkv_ln_w, rwkv_ln_b, mla_q_norm_w, mla_w_q_b, mla_kv_norm_w, mla_w_kv_b, ln1_w, ln1_b, ln2_w, ln2_b, ffn_w1, ffn_w3, ffn_w2, moe_router, moe_w1, moe_w3, moe_w2):
    bn, sn, d = x.shape
    t = bn * sn
    cos, sin = _rope_tables(positions)
    xf = x.reshape(t, d)
    xb = xf.astype(BF16)
    v_first = None
    for l in range(DEPTH):
        if l == 0:
            wp, mu, v0, v2 = _pack_w_in(w_in[l], None), rwkv_mu[l], None, None
        else:
            wp = _pack_w_in(w_in[l], w_in_vres[l - 1])
            mu = jnp.concatenate([rwkv_mu[l], rwkv_mu_vres[l - 1]], axis=0)
            v0, v2 = rwkv_v0[l - 1], rwkv_v2[l - 1]
        proj2 = _matmul(xb, wp, F32, min(PROJ_TM, t), PROJ_TN)
        proj = proj2.reshape(bn, sn, N_PROJ)
        y_ssd = _ssd_mixer(proj, ssd_conv_w[l], ssd_conv_b[l], ssd_dt_bias[l], ssd_a_log[l], ssd_d[l], ssd_norm_w[l])
        y_rwkv, v_first = _rwkv_mixer(proj, v_first, mu, rwkv_w0[l], rwkv_w2[l], rwkv_a0[l], rwkv_a2[l], rwkv_g2[l],
                                      rwkv_k_k[l], rwkv_k_a[l], rwkv_r_k[l], rwkv_ln_w[l], rwkv_ln_b[l], v0, v2)
        y_mla = _mla_mixer(proj, cos, sin, mla_q_norm_w[l], mla_w_q_b[l], mla_kv_norm_w[l], mla_w_kv_b[l])
        x1, x1b = _merge_out(proj2, y_ssd.reshape(t, d), y_rwkv.reshape(t, d), y_mla.reshape(t, d), xf, w_out[l],
                             ln1_w[l], ln1_b[l])
        if l % 2 == 0:
            xf, xb = _ffn_dense(x1b, x1, ffn_w1[l // 2], ffn_w3[l // 2], ffn_w2[l // 2], ln2_w[l], ln2_b[l])
        else:
            xf = _ffn_moe(x1, moe_router[l // 2], moe_w1[l // 2], moe_w3[l // 2], moe_w2[l // 2], ln2_w[l], ln2_b[l])
            xb = xf.astype(BF16)
    return xf.reshape(bn, sn, d)
```

```python
import functools
import math

import jax
import jax.numpy as jnp
import numpy as np
from jax import lax
from jax.experimental import pallas as pl
from jax.experimental.pallas import tpu as pltpu

F32 = jnp.float32
BF16 = jnp.bfloat16

D_MODEL = 2048
DEPTH = 2
ALPHA = (2 * DEPTH) ** 0.25
LN_EPS = 1e-5
RMS_EPS = 1e-6
SSD_HEADS, SSD_HEAD_DIM, SSD_GROUPS, SSD_STATE, SSD_CONV = 32, 64, 4, 128, 4
SSD_BC = SSD_GROUPS * SSD_STATE
RWKV_HEADS, RWKV_HEAD = 32, 64
RWKV_GN_EPS = 64e-5
LORA_W, LORA_A, LORA_G, LORA_V = 96, 96, 256, 64
MLA_HEADS, MLA_NOPE, MLA_ROPE, MLA_V, MLA_RANK = 16, 128, 64, 128, 512
MLA_QK = MLA_NOPE + MLA_ROPE
ROPE_THETA = 10000.0
D_FF = 5632
N_EXPERTS = 8
TOP_K = 2

LANES = 128
SUBLANES = 8
VMEM_LIMIT = 56 * 1024 * 1024

W2K = 2048
COL_GATE = 0
COL_Z = 3 * W2K
COL_R = 4 * W2K
COL_XS = 7 * W2K
COL_BC = 8 * W2K
COL_QLAT = COL_BC + 1024
COL_KVLAT = COL_QLAT + 512
COL_MISC = COL_KVLAT + 512
MISC_W, MISC_A, MISC_G, MISC_V, MISC_DT, MISC_KPE = 0, 128, 256, 512, 640, 768
N_PROJ = COL_MISC + 1024

SSD_Q = 128
RW_T = 64
RW_BLK = 128


def _cparams(sem, vmem=VMEM_LIMIT):
    return pltpu.CompilerParams(dimension_semantics=sem, vmem_limit_bytes=vmem)


def _dot(a, b):
    return jnp.dot(a, b, preferred_element_type=F32)


def _dot_nt(a, b):
    return lax.dot_general(a, b, (((1,), (1,)), ((), ())), preferred_element_type=F32)


def _split3(a):
    hi = a.astype(BF16)
    r1 = a - hi.astype(F32)
    mid = r1.astype(BF16)
    lo = (r1 - mid.astype(F32)).astype(BF16)
    return hi, mid, lo


def _dot_exact_rhs(a_bf, b):
    hi, mid, lo = _split3(b)
    return _dot(a_bf, hi) + _dot(a_bf, mid) + _dot(a_bf, lo)


def _dot_exact_lhs(a, b_bf):
    hi, mid, lo = _split3(a)
    return _dot(hi, b_bf) + _dot(mid, b_bf) + _dot(lo, b_bf)


def _sigmoid(x):
    return 1.0 / (1.0 + jnp.exp(-x))


def _silu(x):
    return x * _sigmoid(x)


def _softplus(x):
    return jnp.maximum(x, 0.0) + jnp.log(1.0 + jnp.exp(-jnp.abs(x)))


def _layernorm(x, w, b):
    mu = jnp.mean(x, axis=-1, keepdims=True)
    xc = x - mu
    var = jnp.mean(xc * xc, axis=-1, keepdims=True)
    return xc * lax.rsqrt(var + LN_EPS) * w + b


def _mm_kernel(x_ref, w_ref, o_ref):
    o_ref[...] = _dot(x_ref[...], w_ref[...]).astype(o_ref.dtype)


def _matmul(x, w, out_dtype, tm, tn):
    m, k = x.shape
    n = w.shape[1]
    return pl.pallas_call(
        _mm_kernel,
        grid=(m // tm, n // tn),
        in_specs=[pl.BlockSpec((tm, k), lambda i, j: (i, 0)),
                  pl.BlockSpec((k, tn), lambda i, j: (0, j))],
        out_specs=pl.BlockSpec((tm, tn), lambda i, j: (i, j)),
        out_shape=jax.ShapeDtypeStruct((m, n), out_dtype),
        compiler_params=_cparams(("parallel", "arbitrary")),
        name="in_proj",
    )(x, w)


def _rope_kernel(pos_ref, freq_ref, cos_ref, sin_ref):
    ang = pos_ref[...] * freq_ref[...]
    valid = lax.broadcasted_iota(jnp.int32, ang.shape, 1) < MLA_ROPE
    cos_ref[...] = jnp.where(valid, jnp.cos(ang), 0.0)
    sin_ref[...] = jnp.where(valid, jnp.sin(ang), 0.0)


def _rope_tables(positions):
    t = positions.size
    tm = min(t, 1024)
    pos = positions.reshape(t, 1).astype(F32)
    inv_freq = ROPE_THETA ** (-jnp.arange(0, MLA_ROPE, 2, dtype=F32) / MLA_ROPE)
    freq = jnp.concatenate([inv_freq, inv_freq, jnp.zeros((LANES - MLA_ROPE,), F32)]).reshape(1, LANES)
    return pl.pallas_call(
        _rope_kernel,
        grid=(t // tm,),
        in_specs=[pl.BlockSpec((tm, 1), lambda i: (i, 0)),
                  pl.BlockSpec((1, LANES), lambda i: (0, 0))],
        out_specs=[pl.BlockSpec((tm, LANES), lambda i: (i, 0))] * 2,
        out_shape=[jax.ShapeDtypeStruct((t, LANES), F32)] * 2,
        compiler_params=_cparams(("parallel",)),
        name="rope_tables",
    )(pos, freq)


def _ssd_kernel(z_ref, xs_ref, bc_ref, misc_ref, cwx_ref, cwb_ref, cbx_ref, cbb_ref, dtb_ref, alog_ref,
                d_ref, nw_ref, e_ref, y_ref, state_ref, bufx_ref, bufb_ref):
    q = SSD_Q
    hp = SSD_HEADS // SSD_GROUPS * SSD_HEAD_DIM

    @pl.when(pl.program_id(1) == 0)
    def _():
        state_ref[...] = jnp.zeros_like(state_ref)
        bufx_ref[0:SUBLANES, :] = jnp.zeros((SUBLANES, bufx_ref.shape[1]), F32)
        bufb_ref[0:SUBLANES, :] = jnp.zeros((SUBLANES, bufb_ref.shape[1]), F32)

    bufx_ref[SUBLANES:SUBLANES + q, :] = xs_ref[...]
    bufb_ref[SUBLANES:SUBLANES + q, :] = bc_ref[...]

    def conv(buf_ref, w_ref, b_ref):
        acc = b_ref[...] + w_ref[SSD_CONV - 1:SSD_CONV, :] * buf_ref[SUBLANES:SUBLANES + q, :]
        for k in range(SSD_CONV - 1):
            off = SUBLANES - (SSD_CONV - 1) + k
            acc = acc + w_ref[k:k + 1, :] * buf_ref[off:off + q, :]
        return _silu(acc)

    xs = conv(bufx_ref, cwx_ref, cbx_ref)
    bc = conv(bufb_ref, cwb_ref, cbb_ref)
    bufx_ref[0:SUBLANES, :] = bufx_ref[q:q + SUBLANES, :]
    bufb_ref[0:SUBLANES, :] = bufb_ref[q:q + SUBLANES, :]

    dt = _softplus(misc_ref[:, MISC_DT:MISC_DT + LANES] + dtb_ref[...])
    da = dt * (-jnp.exp(alog_ref[...]))
    row = lax.broadcasted_iota(jnp.int32, (q, q), 0)
    col = lax.broadcasted_iota(jnp.int32, (q, q), 1)
    causal = row >= col
    tri = jnp.where(causal, 1.0, 0.0).astype(BF16)
    cum = _dot_exact_rhs(tri, da)
    cum_t = cum.T
    ecum = jnp.exp(cum)
    toend = jnp.exp(cum[q - 1:q, :] - cum)
    e_mat = e_ref[...]
    dt_e = _dot_exact_lhs(dt, e_mat)
    ecum_e = _dot_exact_lhs(ecum, e_mat)
    toend_e = _dot_exact_lhs(toend, e_mat)

    xdt = xs * dt_e
    xdt_b = xdt.astype(BF16)
    xw_b = (xdt * toend_e).astype(BF16)
    lane = lax.broadcasted_iota(jnp.int32, (q, LANES), 1)
    lo_half = lane < SSD_HEAD_DIM

    y_groups = []
    for g in range(SSD_GROUPS):
        b_g = bc[:, g * SSD_STATE:(g + 1) * SSD_STATE]
        c_g = bc[:, SSD_BC + g * SSD_STATE:SSD_BC + (g + 1) * SSD_STATE]
        b_gb = b_g.astype(BF16)
        c_gb = c_g.astype(BF16)
        cb = _dot_nt(c_gb, b_gb)
        st = state_ref[:, g * hp:(g + 1) * hp]
        y_inter = _dot(c_gb, st.astype(BF16)) * ecum_e[:, g * hp:(g + 1) * hp]
        parts = []
        for pr in range(hp // LANES):
            ms = []
            for e in range(2):
                h = g * (SSD_HEADS // SSD_GROUPS) + pr * 2 + e
                ci = jnp.broadcast_to(cum[:, h:h + 1], (q, q))
                cj = jnp.broadcast_to(cum_t[h:h + 1, :], (q, q))
                dec = jnp.exp(jnp.where(causal, ci - cj, -jnp.inf))
                ms.append((cb * dec).astype(BF16))
            lo = g * hp + pr * LANES
            xp = xdt_b[:, lo:lo + LANES]
            zero = jnp.zeros_like(xp)
            rhs = jnp.concatenate([jnp.where(lo_half, xp, zero), jnp.where(lo_half, zero, xp)], axis=0)
            parts.append(_dot(jnp.concatenate(ms, axis=1), rhs))
        y_groups.append(jnp.concatenate(parts, axis=1) + y_inter)
        upd = _dot(b_g.T.astype(BF16), xw_b[:, g * hp:(g + 1) * hp])
        state_ref[:, g * hp:(g + 1) * hp] = st * ecum_e[q - 1:q, g * hp:(g + 1) * hp] + upd

    y = jnp.concatenate(y_groups, axis=1) + d_ref[...] * xs
    y = y * _silu(z_ref[...])
    outs = []
    for g in range(SSD_GROUPS):
        yg = y[:, g * hp:(g + 1) * hp]
        outs.append(yg * lax.rsqrt(jnp.mean(yg * yg, axis=-1, keepdims=True) + RMS_EPS))
    y_ref[...] = jnp.concatenate(outs, axis=1) * nw_ref[...]


def _ssd_mixer(proj, conv_w, conv_b, dt_bias, a_log, d_skip, norm_w):
    bn, sn, _ = proj.shape
    w = SSD_HEADS * SSD_HEAD_DIM
    pad = lambda v: jnp.pad(v, (0, LANES - v.shape[0])).reshape(1, LANES)
    a_log_p = jnp.pad(a_log, (0, LANES - SSD_HEADS), constant_values=-jnp.inf).reshape(1, LANES)
    expand = jnp.pad(jnp.repeat(jnp.eye(SSD_HEADS, dtype=BF16), SSD_HEAD_DIM, axis=1),
                     ((0, LANES - SSD_HEADS), (0, 0)))
    row = lambda v: v.reshape(1, -1)
    const = lambda shape: pl.BlockSpec(shape, lambda b, c: (0, 0))
    blk = lambda width, idx: pl.BlockSpec((None, SSD_Q, width), lambda b, c: (b, c, idx))
    return pl.pallas_call(
        _ssd_kernel,
        grid=(bn, sn // SSD_Q),
        in_specs=[blk(w, COL_Z // w), blk(w, COL_XS // w), blk(1024, COL_BC // 1024), blk(1024, COL_MISC // 1024),
                  const((SSD_CONV, w)), const((SSD_CONV, 2 * SSD_BC)), const((1, w)), const((1, 2 * SSD_BC)),
                  const((1, LANES)), const((1, LANES)), const((1, w)), const((1, w)), const((LANES, w))],
        out_specs=pl.BlockSpec((None, SSD_Q, w), lambda b, c: (b, c, 0)),
        out_shape=jax.ShapeDtypeStruct((bn, sn, w), F32),
        scratch_shapes=[pltpu.VMEM((SSD_STATE, w), F32),
                        pltpu.VMEM((SSD_Q + SUBLANES, w), F32),
                        pltpu.VMEM((SSD_Q + SUBLANES, 2 * SSD_BC), F32)],
        compiler_params=_cparams(("parallel", "arbitrary")),
        name="ssd_mixer",
    )(proj, proj, proj, proj, conv_w[:, :w], conv_w[:, w:], row(conv_b[:w]), row(conv_b[w:]),
      pad(dt_bias), a_log_p, row(jnp.repeat(d_skip, SSD_HEAD_DIM)), row(norm_w), expand)


def _dot_hi(a, b):
    return jnp.dot(a, b, precision=lax.Precision.HIGHEST, preferred_element_type=F32)


def _split2(a):
    hi = a.astype(BF16)
    return hi, (a - hi.astype(F32)).astype(BF16)


def _head_sums(x, ones_blk):
    outs = []
    for s in range(x.shape[1] // LANES):
        hi, lo = _split2(x[:, s * LANES:(s + 1) * LANES])
        outs.append(_dot(hi, ones_blk) + _dot(lo, ones_blk))
    return jnp.concatenate(outs, axis=1)


def _stack_heads(x, lo_half):
    zero = jnp.zeros_like(x)
    return jnp.concatenate([jnp.where(lo_half, x, zero), jnp.where(lo_half, zero, x)], axis=0)


RW_PAIRS = RWKV_HEADS // 2
RW_GROUP = 8


def _rwkv_kernel(has_vres, *refs):
    if has_vres:
        (r_ref, k_ref, v_ref, misc_ref, vfirst_ref, mur_ref, muk_ref, muv_ref, mum_ref, w0_ref, w2_ref, a0_ref,
         a2_ref, g2_ref, kk_ref, ka_ref, rk_ref, lnw_ref, lnb_ref, v0_ref, v2_ref,
         y_ref, state_ref, carry_ref, carrym_ref, st_ref, yp_ref, pt_ref) = refs
    else:
        (r_ref, k_ref, v_ref, misc_ref, mur_ref, muk_ref, muv_ref, mum_ref, w0_ref, w2_ref, a0_ref,
         a2_ref, g2_ref, kk_ref, ka_ref, rk_ref, lnw_ref, lnb_ref,
         y_ref, vout_ref, state_ref, carry_ref, carrym_ref, st_ref, yp_ref, pt_ref) = refs
    tb, t = RW_BLK, RW_T
    w = RWKV_HEADS * RWKV_HEAD

    @pl.when(pl.program_id(1) == 0)
    def _():
        state_ref[...] = jnp.zeros_like(state_ref)
        carry_ref[...] = jnp.zeros_like(carry_ref)
        carrym_ref[...] = jnp.zeros_like(carrym_ref)

    first_row = lax.broadcasted_iota(jnp.int32, (tb, 1), 0) == 0

    def shift(p, carry_row, mu):
        prev = jnp.where(first_row, carry_row, pltpu.roll(p, 1, 0))
        return p + (prev - p) * mu

    rp, kp, vp, mp = r_ref[...], k_ref[...], v_ref[...], misc_ref[...]
    r = shift(rp, carry_ref[0:1, :], mur_ref[...])
    k = shift(kp, carry_ref[1:2, :], muk_ref[...])
    v = shift(vp, carry_ref[2:3, :], muv_ref[...])
    m = shift(mp, carrym_ref[0:1, :], mum_ref[...])
    carry_ref[0:1, :] = rp[tb - 1:tb, :]
    carry_ref[1:2, :] = kp[tb - 1:tb, :]
    carry_ref[2:3, :] = vp[tb - 1:tb, :]
    carrym_ref[0:1, :] = mp[tb - 1:tb, :]

    w_lo = m[:, MISC_W:MISC_W + LANES]
    a_lo = m[:, MISC_A:MISC_A + LANES]
    g_lo = m[:, MISC_G:MISC_G + LORA_G]
    log_w = -_softplus(-(w0_ref[...] + _dot_hi(jnp.tanh(w_lo), w2_ref[...]))) - 0.5
    lw = -jnp.exp(log_w)
    a = _sigmoid(a0_ref[...] + _dot_hi(a_lo, a2_ref[...]))
    g = _dot(_sigmoid(g_lo).astype(BF16), g2_ref[...])
    if has_vres:
        v_lo = m[:, MISC_V:MISC_V + LANES]
        v = v + (vfirst_ref[...] - v) * _sigmoid(v0_ref[...] + _dot_hi(v_lo, v2_ref[...]))
    else:
        vout_ref[...] = v

    lane = lax.broadcasted_iota(jnp.int32, (LANES, LANES), 1)
    rowi = lax.broadcasted_iota(jnp.int32, (LANES, LANES), 0)
    ones_blk = jnp.where((lane // RWKV_HEAD) == (rowi // RWKV_HEAD), 1.0, 0.0).astype(BF16)

    kk = k * kk_ref[...]
    kk = kk / jnp.maximum(jnp.sqrt(_head_sums(kk * kk, ones_blk)), 1e-12)
    k = k * (1.0 + (a - 1.0) * ka_ref[...])
    b = kk * a

    ti = lax.broadcasted_iota(jnp.int32, (t, t), 0)
    tj = lax.broadcasted_iota(jnp.int32, (t, t), 1)
    tri = jnp.where(ti >= tj, 1.0, 0.0).astype(BF16)

    for c in range(tb // t):
        sl = slice(c * t, (c + 1) * t)
        lw_c = lw[sl]
        cl = _dot_exact_rhs(tri, lw_c)
        cl_end = cl[t - 1:t, :]
        e_neg = jnp.exp(-cl)
        e_end = jnp.exp(cl_end - cl)
        ops = (kk[sl] * jnp.exp(cl - lw_c), r[sl] * jnp.exp(cl), k[sl] * e_neg, b[sl] * e_neg,
               v[sl], k[sl] * e_end, b[sl] * e_end)
        for pi in range(RW_PAIRS):
            ls = slice(pi * LANES, (pi + 1) * LANES)
            for oi, op in enumerate(ops):
                st_ref[c, oi, pi] = op[:, ls]
            pt_ref[c, pi] = jnp.broadcast_to(jnp.exp(cl_end[:, ls]), (SUBLANES, LANES))

    lo_half = lax.broadcasted_iota(jnp.int32, (t, LANES), 1) < RWKV_HEAD
    bi = lax.broadcasted_iota(jnp.int32, (2 * t, 2 * t), 0) % t
    bj = lax.broadcasted_iota(jnp.int32, (2 * t, 2 * t), 1) % t
    strict = bi > bj
    incl = bi >= bj

    def chunk_group(c, pis, hts):
        h2 = 2 * t
        stk = [[_stack_heads(st_ref[c, oi, pi], lo_half) for oi in range(7)] for pi in pis]
        lhs2 = [jnp.concatenate([s[0], s[1]], axis=0).astype(BF16) for s in stk]
        rhs2 = [jnp.concatenate([s[2], s[3]], axis=0).astype(BF16) for s in stk]
        amat = [_dot_nt(a, b) for a, b in zip(lhs2, rhs2)]
        sh = [_dot_nt(a, h.astype(BF16)) for a, h in zip(lhs2, hts)]
        vsb = [s[4].astype(BF16) for s in stk]
        x = [s_[0:h2] + _dot(jnp.where(strict, am[0:h2, 0:h2], 0.0).astype(BF16), v_)
             for s_, am, v_ in zip(sh, amat, vsb)]
        pw = [jnp.where(strict, am[0:h2, h2:], 0.0) for am in amat]
        n, sign = 1, -1.0
        while n < t:
            if 2 * n < t:
                res = [_dot(p_.astype(BF16), jnp.concatenate([p_, x_], axis=1).astype(BF16)) for p_, x_ in zip(pw, x)]
                x = [x_ + sign * r_[:, h2:] for x_, r_ in zip(x, res)]
                pw = [r_[:, 0:h2] for r_ in res]
            else:
                x = [x_ + sign * _dot(p_.astype(BF16), x_.astype(BF16)) for p_, x_ in zip(pw, x)]
            n, sign = 2 * n, 1.0
        new_hts = []
        for i, pi in enumerate(pis):
            am, s = amat[i], stk[i]
            a_r = jnp.concatenate([jnp.where(incl, am[h2:, 0:h2], 0.0), jnp.where(incl, -am[h2:, h2:], 0.0)], axis=1)
            ys = sh[i][h2:] + _dot(a_r.astype(BF16), jnp.concatenate([s[4], x[i]], axis=0).astype(BF16))
            yp_ref[pi, c * t:(c + 1) * t, :] = ys[0:t] + ys[t:]
            lhs3 = jnp.concatenate([s[4].T, -(x[i].T)], axis=1).astype(BF16)
            rhs3 = jnp.concatenate([s[5], s[6]], axis=0).astype(BF16)
            new_hts.append(hts[i] * pt_ref[c, pi][0:1, :] + _dot(lhs3, rhs3))
        return new_hts

    for g0 in range(0, RW_PAIRS, RW_GROUP):
        pis = list(range(g0, g0 + RW_GROUP))
        hts = [state_ref[pi] for pi in pis]
        for c in range(tb // t):
            hts = chunk_group(c, pis, hts)
        for pi, ht in zip(pis, hts):
            state_ref[pi] = ht

    y = jnp.concatenate([yp_ref[pi] for pi in range(RW_PAIRS)], axis=1)
    inv_n = 1.0 / RWKV_HEAD
    mu = _head_sums(y, ones_blk) * inv_n
    yc = y - mu
    var = _head_sums(yc * yc, ones_blk) * inv_n
    y = yc * lax.rsqrt(var + RWKV_GN_EPS) * lnw_ref[...] + lnb_ref[...]
    bonus = _head_sums(r * k * rk_ref[...], ones_blk) * v
    y_ref[...] = (y + bonus) * g


def _rwkv_mixer(proj, v_first, mu, w0, w2, a0, a2, g2, k_k, k_a, r_k, ln_w, ln_b, v0, v2):
    bn, sn, _ = proj.shape
    w = RWKV_HEADS * RWKV_HEAD
    has_vres = v_first is not None
    row = lambda x: x.reshape(1, -1)
    padrows = lambda x: jnp.pad(x, ((0, LANES - x.shape[0]), (0, 0)))
    padl = lambda x, n: jnp.pad(x, (0, n - x.shape[0]))
    mu_misc = [padl(mu[3 * w:3 * w + LORA_W], LANES), padl(mu[3 * w + LORA_W:3 * w + LORA_W + LORA_A], LANES),
               mu[3 * w + LORA_W + LORA_A:3 * w + LORA_W + LORA_A + LORA_G]]
    if has_vres:
        mu_misc.append(padl(mu[3 * w + LORA_W + LORA_A + LORA_G:], LANES))
    mu_m = padl(jnp.concatenate(mu_misc), 1024)
    const = lambda shape: pl.BlockSpec(shape, lambda b, c: (0,) * len(shape))
    blk = lambda width, idx: pl.BlockSpec((None, RW_BLK, width), lambda b, c: (b, c, idx))
    seq = pl.BlockSpec((None, RW_BLK, w), lambda b, c: (b, c, 0))
    in_specs = [blk(w, COL_R // w), blk(w, COL_R // w + 1), blk(w, COL_R // w + 2), blk(1024, COL_MISC // 1024)]
    args = [proj, proj, proj, proj]
    if has_vres:
        in_specs.append(seq)
        args.append(v_first)
    in_specs += [const((1, w))] * 3 + [const((1, 1024)), const((1, w)), const((LANES, w)), const((1, w)),
                                       const((LANES, w)), const((LORA_G, w))] + [const((1, w))] * 5
    args += [row(mu[:w]), row(mu[w:2 * w]), row(mu[2 * w:3 * w]), row(mu_m), row(w0), padrows(w2), row(a0),
             padrows(a2), g2.astype(BF16), row(k_k), row(k_a), row(r_k), row(ln_w), row(ln_b)]
    if has_vres:
        in_specs += [const((1, w)), const((LANES, w))]
        args += [row(v0), padrows(v2)]
    out_shape = [jax.ShapeDtypeStruct((bn, sn, w), F32)]
    out_specs = [seq]
    if not has_vres:
        out_shape.append(jax.ShapeDtypeStruct((bn, sn, w), F32))
        out_specs.append(seq)
    nc = RW_BLK // RW_T
    outs = pl.pallas_call(
        functools.partial(_rwkv_kernel, has_vres),
        grid=(bn, sn // RW_BLK),
        in_specs=in_specs,
        out_specs=out_specs,
        out_shape=out_shape,
        scratch_shapes=[pltpu.VMEM((RW_PAIRS, LANES, LANES), F32),
                        pltpu.VMEM((SUBLANES, w), F32),
                        pltpu.VMEM((SUBLANES, 1024), F32),
                        pltpu.VMEM((nc, 7, RW_PAIRS, RW_T, LANES), F32),
                        pltpu.VMEM((RW_PAIRS, RW_BLK, LANES), F32),
                        pltpu.VMEM((nc, RW_PAIRS, SUBLANES, LANES), F32)],
        compiler_params=_cparams(("parallel", "arbitrary")),
        name="rwkv7_mixer",
    )(*args)
    if has_vres:
        return outs[0], v_first
    return outs[0], outs[1]


MLA_TM = 256
ATT_TQ = 2048
ATT_TK = 1024
ATT_SUB = 512
HEAD_Q = 2 * LANES


def _rope_half(x2, cos, sin):
    return x2 * cos + pltpu.roll(x2, MLA_ROPE, 1) * sin


def _mla_prep_kernel(qlat_ref, kvlat_ref, misc_ref, cos_ref, sin_ref, qnw_ref, kvnw_ref, wq_ref, wkt_ref, wv_ref,
                     q_ref, kt_ref, v_ref, kpet_ref):
    def rms(x, w):
        return (x * lax.rsqrt(jnp.mean(x * x, axis=-1, keepdims=True) + RMS_EPS) * w).astype(BF16)

    cos, sin = cos_ref[...], sin_ref[...]
    scale = MLA_QK ** -0.5
    q = _dot(rms(qlat_ref[...], qnw_ref[...]), wq_ref[...])
    for h in range(MLA_HEADS):
        lo = h * HEAD_Q
        q_ref[:, lo:lo + LANES] = (q[:, lo:lo + LANES] * scale).astype(BF16)
        q_ref[:, lo + LANES:lo + HEAD_Q] = (_rope_half(q[:, lo + LANES:lo + HEAD_Q], cos, sin) * scale).astype(BF16)
    kvn = rms(kvlat_ref[...], kvnw_ref[...])
    kt_ref[...] = _dot_nt(wkt_ref[...], kvn).astype(BF16)
    v_ref[...] = _dot(kvn, wv_ref[...]).astype(BF16)
    kpet_ref[...] = _rope_half(misc_ref[:, MISC_KPE:MISC_KPE + LANES], cos, sin).T.astype(BF16)


def _flash_kernel(tq, tk, sub, qi_ref, ki_ref, q_ref, kt_ref, kpet_ref, v_ref, o_ref, m_ref, acc_ref):
    step = pl.program_id(2)
    qi, ki = qi_ref[step], ki_ref[step]
    ratio = tq // tk

    @pl.when(ki == 0)
    def _():
        m_ref[...] = jnp.full_like(m_ref, -jnp.inf)
        acc_ref[...] = jnp.zeros_like(acc_ref)

    def update(diag):
        kcat_t = jnp.concatenate([kt_ref[...], kpet_ref[...]], axis=0)
        vcat = jnp.concatenate([v_ref[...], jnp.ones((tk, LANES), BF16)], axis=1)
        plan = []
        for r in range(tq // sub):
            row_lo, row_hi = r * sub, (r + 1) * sub - 1
            ncols = tk
            masked = False
            if diag is not None:
                col_lo = diag * tk
                if row_hi < col_lo:
                    continue
                ncols = min(tk, -(-(row_hi - col_lo + 1) // HEAD_Q) * HEAD_Q)
                masked = row_lo < col_lo + ncols - 1
            plan.append((row_lo, ncols, masked))

        def scores(row_lo, ncols, masked):
            s = _dot(q_ref[row_lo:row_lo + sub, :], kcat_t[:, :ncols])
            if masked:
                ri = row_lo + lax.broadcasted_iota(jnp.int32, s.shape, 0)
                ci = diag * tk + lax.broadcasted_iota(jnp.int32, s.shape, 1)
                s = jnp.where(ci <= ri, s, -jnp.inf)
            return s

        def absorb(row_lo, ncols, s):
            rows = slice(row_lo, row_lo + sub)
            tiles = [s[:, j * LANES:(j + 1) * LANES] for j in range(ncols // LANES)]
            fold = tiles[0]
            for tl in tiles[1:]:
                fold = jnp.maximum(fold, tl)
            m_prev = m_ref[rows, :]
            m_new = jnp.maximum(m_prev, jnp.max(fold, axis=-1, keepdims=True))
            p = jnp.concatenate([jnp.exp(tl - m_new) for tl in tiles], axis=1).astype(BF16)
            alpha = jnp.exp(m_prev - m_new)
            acc_ref[rows, :] = (acc_ref[rows, :] * jnp.concatenate([alpha, alpha], axis=1)
                                + _dot(p, vcat[:ncols]))
            m_ref[rows, :] = m_new

        s_next = scores(*plan[0])
        for idx, (row_lo, ncols, _) in enumerate(plan):
            s_cur = s_next
            if idx + 1 < len(plan):
                s_next = scores(*plan[idx + 1])
            absorb(row_lo, ncols, s_cur)

    @pl.when(ki < qi * ratio)
    def _():
        update(None)

    for d in range(ratio):
        @pl.when(ki == qi * ratio + d)
        def _(d=d):
            update(d)

    @pl.when(ki == (qi + 1) * ratio - 1)
    def _():
        o_ref[...] = acc_ref[:, 0:MLA_V] / acc_ref[:, MLA_V:]


def _mla_mixer(proj, cos, sin, q_norm_w, w_q_b, kv_norm_w, w_kv_b):
    bn, sn, _ = proj.shape
    t = bn * sn
    proj2 = proj.reshape(t, N_PROJ)
    wq = w_q_b.reshape(MLA_RANK, MLA_HEADS, MLA_QK)
    pe = wq[..., MLA_NOPE:]
    rot = jnp.concatenate([-pe[..., MLA_ROPE // 2:], pe[..., :MLA_ROPE // 2]], axis=-1)
    wq = jnp.concatenate([wq, rot], axis=-1).reshape(MLA_RANK, MLA_HEADS * HEAD_Q).astype(BF16)
    wkv = w_kv_b.reshape(MLA_RANK, MLA_HEADS, MLA_NOPE + MLA_V)
    wkt = wkv[..., :MLA_NOPE].reshape(MLA_RANK, -1).T.astype(BF16)
    wv = wkv[..., MLA_NOPE:].reshape(MLA_RANK, -1).astype(BF16)
    nq, nk, nv = MLA_HEADS * HEAD_Q, MLA_HEADS * MLA_NOPE, MLA_HEADS * MLA_V
    tm = min(MLA_TM, sn)
    nsb = sn // tm
    rowblk = lambda width, idx: pl.BlockSpec((tm, width), lambda i: (i, idx))
    const = lambda shape: pl.BlockSpec(shape, lambda i: (0, 0))
    colblk = lambda rows: pl.BlockSpec((None, rows, tm), lambda i: (i // nsb, 0, i % nsb))
    q, kt, v, kpet = pl.pallas_call(
        _mla_prep_kernel,
        grid=(t // tm,),
        in_specs=[rowblk(MLA_RANK, COL_QLAT // MLA_RANK), rowblk(MLA_RANK, COL_KVLAT // MLA_RANK),
                  rowblk(1024, COL_MISC // 1024), rowblk(LANES, 0), rowblk(LANES, 0),
                  const((1, MLA_RANK)), const((1, MLA_RANK)), const((MLA_RANK, nq)), const((nk, MLA_RANK)),
                  const((MLA_RANK, nv))],
        out_specs=[rowblk(nq, 0), colblk(nk), rowblk(nv, 0), colblk(LANES)],
        out_shape=[jax.ShapeDtypeStruct((t, nq), BF16), jax.ShapeDtypeStruct((bn, nk, sn), BF16),
                   jax.ShapeDtypeStruct((t, nv), BF16), jax.ShapeDtypeStruct((bn, LANES, sn), BF16)],
        compiler_params=_cparams(("parallel",)),
        name="mla_prep",
    )(proj2, proj2, proj2, cos, sin, q_norm_w.reshape(1, -1), kv_norm_w.reshape(1, -1), wq, wkt, wv)

    tq, tk = min(ATT_TQ, sn), min(ATT_TK, sn)
    sub = min(ATT_SUB, tq)
    ratio = tq // tk
    pairs = [(a, b) for a in range(sn // tq) for b in range((a + 1) * ratio)]
    qi_arr = jnp.asarray([p[0] for p in pairs], jnp.int32)
    ki_arr = jnp.asarray([p[1] for p in pairs], jnp.int32)
    out = pl.pallas_call(
        functools.partial(_flash_kernel, tq, tk, sub),
        grid_spec=pltpu.PrefetchScalarGridSpec(
            num_scalar_prefetch=2,
            grid=(bn, MLA_HEADS, len(pairs)),
            in_specs=[pl.BlockSpec((None, tq, HEAD_Q), lambda b, h, s, qi, ki: (b, qi[s], h)),
                      pl.BlockSpec((None, MLA_NOPE, tk), lambda b, h, s, qi, ki: (b, h, ki[s])),
                      pl.BlockSpec((None, LANES, tk), lambda b, h, s, qi, ki: (b, 0, ki[s])),
                      pl.BlockSpec((None, tk, MLA_V), lambda b, h, s, qi, ki: (b, ki[s], h))],
            out_specs=pl.BlockSpec((None, tq, MLA_V), lambda b, h, s, qi, ki: (b, qi[s], h)),
            scratch_shapes=[pltpu.VMEM((tq, LANES), F32), pltpu.VMEM((tq, 2 * MLA_V), F32)]),
        out_shape=jax.ShapeDtypeStruct((bn, sn, MLA_HEADS * MLA_V), F32),
        compiler_params=_cparams(("parallel", "parallel", "arbitrary")),
        name="mla_attention",
    )(qi_arr, ki_arr, q.reshape(bn, sn, nq), kt, kpet, v.reshape(bn, sn, nv))
    return out


MERGE_TM = 256


def _merge_kernel(g0_ref, g1_ref, g2_ref, ys_ref, yr_ref, ym_ref, x_ref, wo_ref, lnw_ref, lnb_ref, xo_ref, xb_ref):
    merged = (_sigmoid(g0_ref[...]) * ys_ref[...] + _sigmoid(g1_ref[...]) * yr_ref[...]
              + _sigmoid(g2_ref[...]) * ym_ref[...])
    h = ALPHA * x_ref[...] + _dot(merged.astype(BF16), wo_ref[...])
    y = _layernorm(h, lnw_ref[...], lnb_ref[...])
    xo_ref[...] = y
    xb_ref[...] = y.astype(BF16)


def _merge_out(proj2, y_ssd, y_rwkv, y_mla, x, w_out, ln_w, ln_b):
    t, d = x.shape
    tm = min(MERGE_TM, t)
    rowblk = lambda idx: pl.BlockSpec((tm, d), lambda i: (i, idx))
    const = lambda shape: pl.BlockSpec(shape, lambda i: (0, 0))
    return pl.pallas_call(
        _merge_kernel,
        grid=(t // tm,),
        in_specs=[rowblk(0), rowblk(1), rowblk(2), rowblk(0), rowblk(0), rowblk(0), rowblk(0),
                  const((d, d)), const((1, d)), const((1, d))],
        out_specs=[rowblk(0), rowblk(0)],
        out_shape=[jax.ShapeDtypeStruct((t, d), F32), jax.ShapeDtypeStruct((t, d), BF16)],
        compiler_params=_cparams(("parallel",)),
        name="merge_out_ln1",
    )(proj2, proj2, proj2, y_ssd, y_rwkv, y_mla, x, w_out.astype(BF16), ln_w.reshape(1, d), ln_b.reshape(1, d))


FFN_TM = 512
FFN_TF = 512


def _ffn_kernel(xb_ref, x_ref, w1_ref, w3_ref, w2_ref, lnw_ref, lnb_ref, o_ref, ob_ref, acc_ref):
    f = pl.program_id(1)

    @pl.when(f == 0)
    def _():
        acc_ref[...] = jnp.zeros_like(acc_ref)

    xb = xb_ref[...]
    h = _silu(_dot(xb, w1_ref[...])) * _dot(xb, w3_ref[...])
    acc_ref[...] += _dot(h.astype(BF16), w2_ref[...])

    @pl.when(f == pl.num_programs(1) - 1)
    def _():
        y = _layernorm(ALPHA * x_ref[...] + acc_ref[...], lnw_ref[...], lnb_ref[...])
        o_ref[...] = y
        ob_ref[...] = y.astype(BF16)


def _ffn_dense(xb, x, w1, w3, w2, ln_w, ln_b):
    t, d = x.shape
    ff = w1.shape[1]
    tm, tf = min(FFN_TM, t), FFN_TF
    return pl.pallas_call(
        _ffn_kernel,
        grid=(t // tm, ff // tf),
        in_specs=[pl.BlockSpec((tm, d), lambda i, f: (i, 0)), pl.BlockSpec((tm, d), lambda i, f: (i, 0)),
                  pl.BlockSpec((d, tf), lambda i, f: (0, f)), pl.BlockSpec((d, tf), lambda i, f: (0, f)),
                  pl.BlockSpec((tf, d), lambda i, f: (f, 0)),
                  pl.BlockSpec((1, d), lambda i, f: (0, 0)), pl.BlockSpec((1, d), lambda i, f: (0, 0))],
        out_specs=[pl.BlockSpec((tm, d), lambda i, f: (i, 0)), pl.BlockSpec((tm, d), lambda i, f: (i, 0))],
        out_shape=[jax.ShapeDtypeStruct((t, d), F32), jax.ShapeDtypeStruct((t, d), BF16)],
        scratch_shapes=[pltpu.VMEM((tm, d), F32)],
        compiler_params=_cparams(("parallel", "arbitrary")),
        name="ffn_dense_ln2",
    )(xb, x, w1.astype(BF16), w3.astype(BF16), w2.astype(BF16), ln_w.reshape(1, d), ln_b.reshape(1, d))


ROUTER_TM = 512
MOE_TM = 512
MOE_TF = 512
COMBINE_TM = 256


def _router_kernel(x_ref, wr_ref, idx_ref, wgt_ref):
    logits = _dot_hi(x_ref[...], wr_ref[...])
    lane = lax.broadcasted_iota(jnp.int32, logits.shape, 1)
    lg = jnp.where(lane < N_EXPERTS, logits, -jnp.inf)
    m1 = jnp.max(lg, axis=-1, keepdims=True)
    i1 = jnp.min(jnp.where(lg == m1, lane, LANES), axis=-1, keepdims=True)
    lg2 = jnp.where(lane == i1, -jnp.inf, lg)
    m2 = jnp.max(lg2, axis=-1, keepdims=True)
    i2 = jnp.min(jnp.where(lg2 == m2, lane, LANES), axis=-1, keepdims=True)
    e = jnp.exp(m2 - m1)
    idx_ref[...] = jnp.where(lane == 0, i1, jnp.where(lane == 1, i2, 0))
    wgt_ref[...] = jnp.where(lane == 0, 1.0 / (1.0 + e), jnp.where(lane == 1, e / (1.0 + e), 0.0))


def _moe_ffn_kernel(be_ref, tok_ref, tok_next_ref, x_hbm, w1_ref, w3_ref, w2_ref, o_ref, xg_ref, xb_ref, acc_ref,
                    sem):
    i, f = pl.program_id(0), pl.program_id(1)
    rows = xg_ref.shape[1]

    def row_copy(idx_ref, slot, r):
        return pltpu.make_async_copy(x_hbm.at[pl.ds(idx_ref[0, 0, r], 1), :], xg_ref.at[slot, pl.ds(r, 1), :],
                                     sem.at[slot])

    def start_gather(idx_ref, slot):
        def body(r, c):
            row_copy(idx_ref, slot, r).start()
            return c
        lax.fori_loop(0, rows, body, 0)

    @pl.when(f == 0)
    def _():
        slot = i % 2

        @pl.when(i == 0)
        def _():
            start_gather(tok_ref, slot)

        def wait(r, c):
            row_copy(tok_ref, slot, r).wait()
            return c
        lax.fori_loop(0, rows, wait, 0)

        @pl.when(i + 1 < pl.num_programs(0))
        def _():
            start_gather(tok_next_ref, 1 - slot)

        xb_ref[...] = xg_ref[slot].astype(BF16)
        acc_ref[...] = jnp.zeros_like(acc_ref)

    xb = xb_ref[...]
    h = _silu(_dot(xb, w1_ref[...])) * _dot(xb, w3_ref[...])
    acc_ref[...] += _dot(h.astype(BF16), w2_ref[...])

    @pl.when(f == pl.num_programs(1) - 1)
    def _():
        o_ref[...] = acc_ref[...]


def _combine_kernel(dst_ref, y_hbm, wgt_ref, x_ref, lnw_ref, lnb_ref, o_ref, yg_ref, sem):
    rows = x_ref.shape[0]

    def row_copy(r, k):
        return pltpu.make_async_copy(y_hbm.at[pl.ds(dst_ref[0, 0, TOP_K * r + k], 1), :],
                                     yg_ref.at[k, pl.ds(r, 1), :], sem)

    def start(r, c):
        for k in range(TOP_K):
            row_copy(r, k).start()
        return c

    def wait(r, c):
        for k in range(TOP_K):
            row_copy(r, k).wait()
        return c

    lax.fori_loop(0, rows, start, 0)
    lax.fori_loop(0, rows, wait, 0)
    wgt = wgt_ref[...]
    f = wgt[:, 0:1] * yg_ref[0] + wgt[:, 1:2] * yg_ref[1]
    o_ref[...] = _layernorm(ALPHA * x_ref[...] + f, lnw_ref[...], lnb_ref[...])


def _ffn_moe(x, router, w1, w3, w2, ln_w, ln_b):
    t, d = x.shape
    ne, _, ff = w1.shape
    tm = min(ROUTER_TM, t)
    rw = jnp.pad(router, ((0, 0), (0, LANES - ne)))
    idx, wgt = pl.pallas_call(
        _router_kernel,
        grid=(t // tm,),
        in_specs=[pl.BlockSpec((tm, d), lambda i: (i, 0)), pl.BlockSpec((d, LANES), lambda i: (0, 0))],
        out_specs=[pl.BlockSpec((tm, LANES), lambda i: (i, 0))] * 2,
        out_shape=[jax.ShapeDtypeStruct((t, LANES), jnp.int32), jax.ShapeDtypeStruct((t, LANES), F32)],
        compiler_params=_cparams(("parallel",)),
        name="moe_router",
    )(x, rw)

    blk = MOE_TM
    n_assign = t * TOP_K
    flat_e = idx[:, :TOP_K].reshape(-1)
    onehot = (flat_e[:, None] == jnp.arange(ne, dtype=jnp.int32)[None, :]).astype(jnp.int32)
    csum = jnp.cumsum(onehot, axis=0)
    rank = jnp.take_along_axis(csum, flat_e[:, None], axis=1)[:, 0] - 1
    counts = csum[-1]
    padded = (counts + blk - 1) // blk * blk
    pad_end = jnp.cumsum(padded)
    dest = (pad_end - padded)[flat_e] + rank
    n_blocks = -(-(n_assign + ne * (blk - 1)) // blk)
    n_rows = n_blocks * blk
    token_of_row = jnp.zeros((n_rows,), jnp.int32).at[dest].set(jnp.arange(n_assign, dtype=jnp.int32) // TOP_K)
    block_expert = jnp.minimum(jnp.searchsorted(pad_end, jnp.arange(n_blocks, dtype=jnp.int32) * blk, side='right'),
                               ne - 1).astype(jnp.int32)

    tf = MOE_TF
    yb = pl.pallas_call(
        _moe_ffn_kernel,
        grid_spec=pltpu.PrefetchScalarGridSpec(
            num_scalar_prefetch=1,
            grid=(n_blocks, ff // tf),
            in_specs=[pl.BlockSpec((1, 1, blk), lambda i, f, be: (i, 0, 0), memory_space=pltpu.SMEM),
                      pl.BlockSpec((1, 1, blk), lambda i, f, be: (jnp.minimum(i + 1, n_blocks - 1), 0, 0),
                                   memory_space=pltpu.SMEM),
                      pl.BlockSpec(memory_space=pl.ANY),
                      pl.BlockSpec((None, d, tf), lambda i, f, be: (be[i], 0, f)),
                      pl.BlockSpec((None, d, tf), lambda i, f, be: (be[i], 0, f)),
                      pl.BlockSpec((None, tf, d), lambda i, f, be: (be[i], f, 0))],
            out_specs=pl.BlockSpec((blk, d), lambda i, f, be: (i, 0)),
            scratch_shapes=[pltpu.VMEM((2, blk, d), F32), pltpu.VMEM((blk, d), BF16), pltpu.VMEM((blk, d), F32),
                            pltpu.SemaphoreType.DMA((2,))]),
        out_shape=jax.ShapeDtypeStruct((n_rows, d), F32),
        compiler_params=_cparams(("arbitrary", "arbitrary")),
        name="moe_expert_ffn",
    )(block_expert, token_of_row.reshape(n_blocks, 1, blk), token_of_row.reshape(n_blocks, 1, blk), x,
      w1.astype(BF16), w3.astype(BF16), w2.astype(BF16))

    tc = min(COMBINE_TM, t)
    return pl.pallas_call(
        _combine_kernel,
        grid=(t // tc,),
        in_specs=[pl.BlockSpec((1, 1, TOP_K * tc), lambda i: (i, 0, 0), memory_space=pltpu.SMEM),
                  pl.BlockSpec(memory_space=pl.ANY),
                  pl.BlockSpec((tc, LANES), lambda i: (i, 0)), pl.BlockSpec((tc, d), lambda i: (i, 0)),
                  pl.BlockSpec((1, d), lambda i: (0, 0)), pl.BlockSpec((1, d), lambda i: (0, 0))],
        out_specs=pl.BlockSpec((tc, d), lambda i: (i, 0)),
        out_shape=jax.ShapeDtypeStruct((t, d), F32),
        scratch_shapes=[pltpu.VMEM((TOP_K, tc, d), F32), pltpu.SemaphoreType.DMA(())],
        compiler_params=_cparams(("arbitrary",)),
        name="moe_combine_ln2",
    )(dest.reshape(t // tc, 1, TOP_K * tc), yb, wgt, x, ln_w.reshape(1, d), ln_b.reshape(1, d))


def _pack_w_in(w, w_vres):
    d = w.shape[0]
    o = 0

    def take(n):
        nonlocal o
        s = w[:, o:o + n]
        o += n
        return s

    padc = lambda s, n: jnp.pad(s, ((0, 0), (0, n - s.shape[1])))
    gates = take(3 * W2K)
    z = take(W2K)
    xbc = take(W2K + 2 * SSD_BC)
    dt = take(SSD_HEADS)
    qlat = take(MLA_RANK)
    kvlat = take(MLA_RANK)
    kpe = take(MLA_ROPE)
    rkv = take(3 * W2K)
    w_lo, a_lo, g_lo = take(LORA_W), take(LORA_A), take(LORA_G)
    v_lo = jnp.zeros((d, LANES), w.dtype) if w_vres is None else padc(w_vres, LANES)
    kpe_rot = jnp.concatenate([-kpe[:, MLA_ROPE // 2:], kpe[:, :MLA_ROPE // 2]], axis=1)
    misc = jnp.concatenate([padc(w_lo, LANES), padc(a_lo, LANES), g_lo, v_lo, padc(dt, LANES), kpe, kpe_rot], axis=1)
    return jnp.concatenate([gates, z, rkv, xbc, qlat, kvlat, padc(misc, 1024)], axis=1).astype(BF16)


PROJ_TM = 1024
PROJ_TN = 1024


def kernel(x, positions, w_in, w_in_vres, w_out, ssd_conv_w, ssd_conv_b, ssd_dt_bias, ssd_a_log, ssd_d, ssd_norm_w, rwkv_mu, rwkv_mu_vres, rwkv_w0, rwkv_w2, rwkv_a0, rwkv_a2, rwkv_g2, rwkv_v0, rwkv_v2, rwkv_k_k, rwkv_k_a, rwkv_r_k, rwkv_ln_w, rwkv_ln_b, mla_q_norm_w, mla_w_q_b, mla_kv_norm_w, mla_w_kv_b, ln1_w, ln1_b, ln2_w, ln2_b, ffn_w1, ffn_w3, ffn_w2, moe_router, moe_w1, moe_w3, moe_w2):
    bn, sn, d = x.shape
    t = bn * sn
    cos, sin = _rope_tables(positions)
    xf = x.reshape(t, d)
    xb = xf.astype(BF16)
    v_first = None
    for l in range(DEPTH):
        if l == 0:
            wp, mu, v0, v2 = _pack_w_in(w_in[l], None), rwkv_mu[l], None, None
        else:
            wp = _pack_w_in(w_in[l], w_in_vres[l - 1])
            mu = jnp.concatenate([rwkv_mu[l], rwkv_mu_vres[l - 1]], axis=0)
            v0, v2 = rwkv_v0[l - 1], rwkv_v2[l - 1]
        proj2 = _matmul(xb, wp, F32, min(PROJ_TM, t), PROJ_TN)
        proj = proj2.reshape(bn, sn, N_PROJ)
        y_ssd = _ssd_mixer(proj, ssd_conv_w[l], ssd_conv_b[l], ssd_dt_bias[l], ssd_a_log[l], ssd_d[l], ssd_norm_w[l])
        y_rwkv, v_first = _rwkv_mixer(proj, v_first, mu, rwkv_w0[l], rwkv_w2[l], rwkv_a0[l], rwkv_a2[l], rwkv_g2[l],
                                      rwkv_k_k[l], rwkv_k_a[l], rwkv_r_k[l], rwkv_ln_w[l], rwkv_ln_b[l], v0, v2)
        y_mla = _mla_mixer(proj, cos, sin, mla_q_norm_w[l], mla_w_q_b[l], mla_kv_norm_w[l], mla_w_kv_b[l])
        x1, x1b = _merge_out(proj2, y_ssd.reshape(t, d), y_rwkv.reshape(t, d), y_mla.reshape(t, d), xf, w_out[l],
                             ln1_w[l], ln1_b[l])
        if l % 2 == 0:
            xf, xb = _ffn_dense(x1b, x1, ffn_w1[l // 2], ffn_w3[l // 2], ffn_w2[l // 2], ln2_w[l], ln2_b[l])
        else:
            xf = _ffn_moe(x1, moe_router[l // 2], moe_w1[l // 2], moe_w3[l // 2], moe_w2[l // 2], ln2_w[l], ln2_b[l])
            xb = xf.astype(BF16)
    return xf.reshape(bn, sn, d)
```

```python
import functools
import math

import jax
import jax.numpy as jnp
import numpy as np
from jax import lax
from jax.experimental import pallas as pl
from jax.experimental.pallas import tpu as pltpu

F32 = jnp.float32
BF16 = jnp.bfloat16

D_MODEL = 2048
DEPTH = 2
ALPHA = (2 * DEPTH) ** 0.25
LN_EPS = 1e-5
RMS_EPS = 1e-6
SSD_HEADS, SSD_HEAD_DIM, SSD_GROUPS, SSD_STATE, SSD_CONV = 32, 64, 4, 128, 4
SSD_BC = SSD_GROUPS * SSD_STATE
RWKV_HEADS, RWKV_HEAD = 32, 64
RWKV_GN_EPS = 64e-5
LORA_W, LORA_A, LORA_G, LORA_V = 96, 96, 256, 64
MLA_HEADS, MLA_NOPE, MLA_ROPE, MLA_V, MLA_RANK = 16, 128, 64, 128, 512
MLA_QK = MLA_NOPE + MLA_ROPE
ROPE_THETA = 10000.0
D_FF = 5632
N_EXPERTS = 8
TOP_K = 2

LANES = 128
SUBLANES = 8
VMEM_LIMIT = 56 * 1024 * 1024

W2K = 2048
COL_GATE = 0
COL_Z = 3 * W2K
COL_R = 4 * W2K
COL_XS = 7 * W2K
COL_BC = 8 * W2K
COL_QLAT = COL_BC + 1024
COL_KVLAT = COL_QLAT + 512
COL_MISC = COL_KVLAT + 512
MISC_W, MISC_A, MISC_G, MISC_V, MISC_DT, MISC_KPE = 0, 128, 256, 512, 640, 768
N_PROJ = COL_MISC + 1024

SSD_Q = 128
RW_T = 64
RW_BLK = 128


def _cparams(sem, vmem=VMEM_LIMIT):
    return pltpu.CompilerParams(dimension_semantics=sem, vmem_limit_bytes=vmem)


def _dot(a, b):
    return jnp.dot(a, b, preferred_element_type=F32)


def _dot_nt(a, b):
    return lax.dot_general(a, b, (((1,), (1,)), ((), ())), preferred_element_type=F32)


def _split3(a):
    hi = a.astype(BF16)
    r1 = a - hi.astype(F32)
    mid = r1.astype(BF16)
    lo = (r1 - mid.astype(F32)).astype(BF16)
    return hi, mid, lo


def _dot_exact_rhs(a_bf, b):
    hi, mid, lo = _split3(b)
    return _dot(a_bf, hi) + _dot(a_bf, mid) + _dot(a_bf, lo)


def _dot_exact_lhs(a, b_bf):
    hi, mid, lo = _split3(a)
    return _dot(hi, b_bf) + _dot(mid, b_bf) + _dot(lo, b_bf)


def _sigmoid(x):
    return 1.0 / (1.0 + jnp.exp(-x))


def _silu(x):
    return x * _sigmoid(x)


def _softplus(x):
    return jnp.maximum(x, 0.0) + jnp.log(1.0 + jnp.exp(-jnp.abs(x)))


def _layernorm(x, w, b):
    mu = jnp.mean(x, axis=-1, keepdims=True)
    xc = x - mu
    var = jnp.mean(xc * xc, axis=-1, keepdims=True)
    return xc * lax.rsqrt(var + LN_EPS) * w + b


def _mm_kernel(x_ref, w_ref, o_ref):
    o_ref[...] = _dot(x_ref[...], w_ref[...]).astype(o_ref.dtype)


def _matmul(x, w, out_dtype, tm, tn):
    m, k = x.shape
    n = w.shape[1]
    w_tiles = w.reshape(k, n // tn, tn).transpose(1, 0, 2)
    return pl.pallas_call(
        _mm_kernel,
        grid=(m // tm, n // tn),
        in_specs=[pl.BlockSpec((tm, k), lambda i, j: (i, 0)),
                  pl.BlockSpec((None, k, tn), lambda i, j: (j, 0, 0))],
        out_specs=pl.BlockSpec((tm, tn), lambda i, j: (i, j)),
        out_shape=jax.ShapeDtypeStruct((m, n), out_dtype),
        compiler_params=_cparams(("parallel", "arbitrary")),
        name="in_proj",
    )(x, w_tiles)


def _rope_kernel(pos_ref, freq_ref, cos_ref, sin_ref):
    ang = pos_ref[...] * freq_ref[...]
    valid = lax.broadcasted_iota(jnp.int32, ang.shape, 1) < MLA_ROPE
    cos_ref[...] = jnp.where(valid, jnp.cos(ang), 0.0)
    sin_ref[...] = jnp.where(valid, jnp.sin(ang), 0.0)


def _rope_tables(positions):
    t = positions.size
    tm = min(t, 1024)
    pos = positions.reshape(t, 1).astype(F32)
    inv_freq = ROPE_THETA ** (-jnp.arange(0, MLA_ROPE, 2, dtype=F32) / MLA_ROPE)
    freq = jnp.concatenate([inv_freq, inv_freq, jnp.zeros((LANES - MLA_ROPE,), F32)]).reshape(1, LANES)
    return pl.pallas_call(
        _rope_kernel,
        grid=(t // tm,),
        in_specs=[pl.BlockSpec((tm, 1), lambda i: (i, 0)),
                  pl.BlockSpec((1, LANES), lambda i: (0, 0))],
        out_specs=[pl.BlockSpec((tm, LANES), lambda i: (i, 0))] * 2,
        out_shape=[jax.ShapeDtypeStruct((t, LANES), F32)] * 2,
        compiler_params=_cparams(("parallel",)),
        name="rope_tables",
    )(pos, freq)


def _ssd_kernel(z_ref, xs_ref, bc_ref, misc_ref, cwx_ref, cwb_ref, cbx_ref, cbb_ref, dtb_ref, alog_ref,
                d_ref, nw_ref, e_ref, y_ref, state_ref, bufx_ref, bufb_ref):
    q = SSD_Q
    hp = SSD_HEADS // SSD_GROUPS * SSD_HEAD_DIM

    @pl.when(pl.program_id(1) == 0)
    def _():
        state_ref[...] = jnp.zeros_like(state_ref)
        bufx_ref[0:SUBLANES, :] = jnp.zeros((SUBLANES, bufx_ref.shape[1]), F32)
        bufb_ref[0:SUBLANES, :] = jnp.zeros((SUBLANES, bufb_ref.shape[1]), F32)

    bufx_ref[SUBLANES:SUBLANES + q, :] = xs_ref[...]
    bufb_ref[SUBLANES:SUBLANES + q, :] = bc_ref[...]

    def conv(buf_ref, w_ref, b_ref):
        acc = b_ref[...] + w_ref[SSD_CONV - 1:SSD_CONV, :] * buf_ref[SUBLANES:SUBLANES + q, :]
        for k in range(SSD_CONV - 1):
            off = SUBLANES - (SSD_CONV - 1) + k
            acc = acc + w_ref[k:k + 1, :] * buf_ref[off:off + q, :]
        return _silu(acc)

    xs = conv(bufx_ref, cwx_ref, cbx_ref)
    bc = conv(bufb_ref, cwb_ref, cbb_ref)
    bufx_ref[0:SUBLANES, :] = bufx_ref[q:q + SUBLANES, :]
    bufb_ref[0:SUBLANES, :] = bufb_ref[q:q + SUBLANES, :]

    dt = _softplus(misc_ref[:, MISC_DT:MISC_DT + LANES] + dtb_ref[...])
    da = dt * (-jnp.exp(alog_ref[...]))
    row = lax.broadcasted_iota(jnp.int32, (q, q), 0)
    col = lax.broadcasted_iota(jnp.int32, (q, q), 1)
    causal = row >= col
    tri = jnp.where(causal, 1.0, 0.0).astype(BF16)
    cum = _dot_exact_rhs(tri, da)
    cum_t = cum.T
    ecum = jnp.exp(cum)
    toend = jnp.exp(cum[q - 1:q, :] - cum)
    e_mat = e_ref[...]
    dt_e = _dot_exact_lhs(dt, e_mat)
    ecum_e = _dot_exact_lhs(ecum, e_mat)
    toend_e = _dot_exact_lhs(toend, e_mat)

    xdt = xs * dt_e
    xdt_b = xdt.astype(BF16)
    xw_b = (xdt * toend_e).astype(BF16)
    lane = lax.broadcasted_iota(jnp.int32, (q, LANES), 1)
    lo_half = lane < SSD_HEAD_DIM

    y_groups = []
    for g in range(SSD_GROUPS):
        b_g = bc[:, g * SSD_STATE:(g + 1) * SSD_STATE]
        c_g = bc[:, SSD_BC + g * SSD_STATE:SSD_BC + (g + 1) * SSD_STATE]
        b_gb = b_g.astype(BF16)
        c_gb = c_g.astype(BF16)
        cb = _dot_nt(c_gb, b_gb)
        st = state_ref[:, g * hp:(g + 1) * hp]
        y_inter = _dot(c_gb, st.astype(BF16)) * ecum_e[:, g * hp:(g + 1) * hp]
        parts = []
        for pr in range(hp // LANES):
            ms = []
            for e in range(2):
                h = g * (SSD_HEADS // SSD_GROUPS) + pr * 2 + e
                ci = jnp.broadcast_to(cum[:, h:h + 1], (q, q))
                cj = jnp.broadcast_to(cum_t[h:h + 1, :], (q, q))
                dec = jnp.exp(jnp.where(causal, ci - cj, -jnp.inf))
                ms.append((cb * dec).astype(BF16))
            lo = g * hp + pr * LANES
            xp = xdt_b[:, lo:lo + LANES]
            zero = jnp.zeros_like(xp)
            rhs = jnp.concatenate([jnp.where(lo_half, xp, zero), jnp.where(lo_half, zero, xp)], axis=0)
            parts.append(_dot(jnp.concatenate(ms, axis=1), rhs))
        y_groups.append(jnp.concatenate(parts, axis=1) + y_inter)
        upd = _dot(b_g.T.astype(BF16), xw_b[:, g * hp:(g + 1) * hp])
        state_ref[:, g * hp:(g + 1) * hp] = st * ecum_e[q - 1:q, g * hp:(g + 1) * hp] + upd

    y = jnp.concatenate(y_groups, axis=1) + d_ref[...] * xs
    y = y * _silu(z_ref[...])
    outs = []
    for g in range(SSD_GROUPS):
        yg = y[:, g * hp:(g + 1) * hp]
        outs.append(yg * lax.rsqrt(jnp.mean(yg * yg, axis=-1, keepdims=True) + RMS_EPS))
    y_ref[...] = jnp.concatenate(outs, axis=1) * nw_ref[...]


def _ssd_mixer(proj, conv_w, conv_b, dt_bias, a_log, d_skip, norm_w):
    bn, sn, _ = proj.shape
    w = SSD_HEADS * SSD_HEAD_DIM
    pad = lambda v: jnp.pad(v, (0, LANES - v.shape[0])).reshape(1, LANES)
    a_log_p = jnp.pad(a_log, (0, LANES - SSD_HEADS), constant_values=-jnp.inf).reshape(1, LANES)
    expand = jnp.pad(jnp.repeat(jnp.eye(SSD_HEADS, dtype=BF16), SSD_HEAD_DIM, axis=1),
                     ((0, LANES - SSD_HEADS), (0, 0)))
    row = lambda v: v.reshape(1, -1)
    const = lambda shape: pl.BlockSpec(shape, lambda b, c: (0, 0))
    blk = lambda width, idx: pl.BlockSpec((None, SSD_Q, width), lambda b, c: (b, c, idx))
    return pl.pallas_call(
        _ssd_kernel,
        grid=(bn, sn // SSD_Q),
        in_specs=[blk(w, COL_Z // w), blk(w, COL_XS // w), blk(1024, COL_BC // 1024), blk(1024, COL_MISC // 1024),
                  const((SSD_CONV, w)), const((SSD_CONV, 2 * SSD_BC)), const((1, w)), const((1, 2 * SSD_BC)),
                  const((1, LANES)), const((1, LANES)), const((1, w)), const((1, w)), const((LANES, w))],
        out_specs=pl.BlockSpec((None, SSD_Q, w), lambda b, c: (b, c, 0)),
        out_shape=jax.ShapeDtypeStruct((bn, sn, w), F32),
        scratch_shapes=[pltpu.VMEM((SSD_STATE, w), F32),
                        pltpu.VMEM((SSD_Q + SUBLANES, w), F32),
                        pltpu.VMEM((SSD_Q + SUBLANES, 2 * SSD_BC), F32)],
        compiler_params=_cparams(("parallel", "arbitrary")),
        name="ssd_mixer",
    )(proj, proj, proj, proj, conv_w[:, :w], conv_w[:, w:], row(conv_b[:w]), row(conv_b[w:]),
      pad(dt_bias), a_log_p, row(jnp.repeat(d_skip, SSD_HEAD_DIM)), row(norm_w), expand)


def _dot_hi(a, b):
    return jnp.dot(a, b, precision=lax.Precision.HIGHEST, preferred_element_type=F32)


def _split2(a):
    hi = a.astype(BF16)
    return hi, (a - hi.astype(F32)).astype(BF16)


def _head_sums(x, ones_blk):
    outs = []
    for s in range(x.shape[1] // LANES):
        hi, lo = _split2(x[:, s * LANES:(s + 1) * LANES])
        outs.append(_dot(hi, ones_blk) + _dot(lo, ones_blk))
    return jnp.concatenate(outs, axis=1)


def _stack_heads(x, lo_half):
    zero = jnp.zeros_like(x)
    return jnp.concatenate([jnp.where(lo_half, x, zero), jnp.where(lo_half, zero, x)], axis=0)


RW_PAIRS = RWKV_HEADS // 2
RW_GROUP = 8


def _rwkv_kernel(has_vres, *refs):
    if has_vres:
        (r_ref, k_ref, v_ref, misc_ref, vfirst_ref, mur_ref, muk_ref, muv_ref, mum_ref, w0_ref, w2_ref, a0_ref,
         a2_ref, g2_ref, kk_ref, ka_ref, rk_ref, lnw_ref, lnb_ref, v0_ref, v2_ref,
         y_ref, state_ref, carry_ref, carrym_ref, st_ref, yp_ref, pt_ref) = refs
    else:
        (r_ref, k_ref, v_ref, misc_ref, mur_ref, muk_ref, muv_ref, mum_ref, w0_ref, w2_ref, a0_ref,
         a2_ref, g2_ref, kk_ref, ka_ref, rk_ref, lnw_ref, lnb_ref,
         y_ref, vout_ref, state_ref, carry_ref, carrym_ref, st_ref, yp_ref, pt_ref) = refs
    tb, t = RW_BLK, RW_T
    w = RWKV_HEADS * RWKV_HEAD

    @pl.when(pl.program_id(1) == 0)
    def _():
        state_ref[...] = jnp.zeros_like(state_ref)
        carry_ref[...] = jnp.zeros_like(carry_ref)
        carrym_ref[...] = jnp.zeros_like(carrym_ref)

    first_row = lax.broadcasted_iota(jnp.int32, (tb, 1), 0) == 0

    def shift(p, carry_row, mu):
        prev = jnp.where(first_row, carry_row, pltpu.roll(p, 1, 0))
        return p + (prev - p) * mu

    rp, kp, vp, mp = r_ref[...], k_ref[...], v_ref[...], misc_ref[...]
    r = shift(rp, carry_ref[0:1, :], mur_ref[...])
    k = shift(kp, carry_ref[1:2, :], muk_ref[...])
    v = shift(vp, carry_ref[2:3, :], muv_ref[...])
    m = shift(mp, carrym_ref[0:1, :], mum_ref[...])
    carry_ref[0:1, :] = rp[tb - 1:tb, :]
    carry_ref[1:2, :] = kp[tb - 1:tb, :]
    carry_ref[2:3, :] = vp[tb - 1:tb, :]
    carrym_ref[0:1, :] = mp[tb - 1:tb, :]

    w_lo = m[:, MISC_W:MISC_W + LANES]
    a_lo = m[:, MISC_A:MISC_A + LANES]
    g_lo = m[:, MISC_G:MISC_G + LORA_G]
    log_w = -_softplus(-(w0_ref[...] + _dot_hi(jnp.tanh(w_lo), w2_ref[...]))) - 0.5
    lw = -jnp.exp(log_w)
    a = _sigmoid(a0_ref[...] + _dot_hi(a_lo, a2_ref[...]))
    g = _dot(_sigmoid(g_lo).astype(BF16), g2_ref[...])
    if has_vres:
        v_lo = m[:, MISC_V:MISC_V + LANES]
        v = v + (vfirst_ref[...] - v) * _sigmoid(v0_ref[...] + _dot_hi(v_lo, v2_ref[...]))
    else:
        vout_ref[...] = v

    lane = lax.broadcasted_iota(jnp.int32, (LANES, LANES), 1)
    rowi = lax.broadcasted_iota(jnp.int32, (LANES, LANES), 0)
    ones_blk = jnp.where((lane // RWKV_HEAD) == (rowi // RWKV_HEAD), 1.0, 0.0).astype(BF16)

    kk = k * kk_ref[...]
    kk = kk / jnp.maximum(jnp.sqrt(_head_sums(kk * kk, ones_blk)), 1e-12)
    k = k * (1.0 + (a - 1.0) * ka_ref[...])
    b = kk * a

    ti = lax.broadcasted_iota(jnp.int32, (t, t), 0)
    tj = lax.broadcasted_iota(jnp.int32, (t, t), 1)
    tri = jnp.where(ti >= tj, 1.0, 0.0).astype(BF16)

    for c in range(tb // t):
        sl = slice(c * t, (c + 1) * t)
        lw_c = lw[sl]
        cl = _dot_exact_rhs(tri, lw_c)
        cl_end = cl[t - 1:t, :]
        e_neg = jnp.exp(-cl)
        e_end = jnp.exp(cl_end - cl)
        ops = (kk[sl] * jnp.exp(cl - lw_c), r[sl] * jnp.exp(cl), k[sl] * e_neg, b[sl] * e_neg,
               v[sl], k[sl] * e_end, b[sl] * e_end)
        for pi in range(RW_PAIRS):
            ls = slice(pi * LANES, (pi + 1) * LANES)
            for oi, op in enumerate(ops):
                st_ref[c, oi, pi] = op[:, ls]
            pt_ref[c, pi] = jnp.broadcast_to(jnp.exp(cl_end[:, ls]), (SUBLANES, LANES))

    lo_half = lax.broadcasted_iota(jnp.int32, (t, LANES), 1) < RWKV_HEAD
    bi = lax.broadcasted_iota(jnp.int32, (2 * t, 2 * t), 0) % t
    bj = lax.broadcasted_iota(jnp.int32, (2 * t, 2 * t), 1) % t
    strict = bi > bj
    incl = bi >= bj

    def chunk_group(c, pis, hts):
        h2 = 2 * t
        stk = [[_stack_heads(st_ref[c, oi, pi], lo_half) for oi in range(7)] for pi in pis]
        lhs2 = [jnp.concatenate([s[0], s[1]], axis=0).astype(BF16) for s in stk]
        rhs2 = [jnp.concatenate([s[2], s[3]], axis=0).astype(BF16) for s in stk]
        amat = [_dot_nt(a, b) for a, b in zip(lhs2, rhs2)]
        sh = [_dot_nt(a, h.astype(BF16)) for a, h in zip(lhs2, hts)]
        vsb = [s[4].astype(BF16) for s in stk]
        x = [s_[0:h2] + _dot(jnp.where(strict, am[0:h2, 0:h2], 0.0).astype(BF16), v_)
             for s_, am, v_ in zip(sh, amat, vsb)]
        pw = [jnp.where(strict, am[0:h2, h2:], 0.0) for am in amat]
        n, sign = 1, -1.0
        while n < t:
            if 2 * n < t:
                res = [_dot(p_.astype(BF16), jnp.concatenate([p_, x_], axis=1).astype(BF16)) for p_, x_ in zip(pw, x)]
                x = [x_ + sign * r_[:, h2:] for x_, r_ in zip(x, res)]
                pw = [r_[:, 0:h2] for r_ in res]
            else:
                x = [x_ + sign * _dot(p_.astype(BF16), x_.astype(BF16)) for p_, x_ in zip(pw, x)]
            n, sign = 2 * n, 1.0
        new_hts = []
        for i, pi in enumerate(pis):
            am, s = amat[i], stk[i]
            a_r = jnp.concatenate([jnp.where(incl, am[h2:, 0:h2], 0.0), jnp.where(incl, -am[h2:, h2:], 0.0)], axis=1)
            ys = sh[i][h2:] + _dot(a_r.astype(BF16), jnp.concatenate([s[4], x[i]], axis=0).astype(BF16))
            yp_ref[pi, c * t:(c + 1) * t, :] = ys[0:t] + ys[t:]
            lhs3 = jnp.concatenate([s[4].T, -(x[i].T)], axis=1).astype(BF16)
            rhs3 = jnp.concatenate([s[5], s[6]], axis=0).astype(BF16)
            new_hts.append(hts[i] * pt_ref[c, pi][0:1, :] + _dot(lhs3, rhs3))
        return new_hts

    for g0 in range(0, RW_PAIRS, RW_GROUP):
        pis = list(range(g0, g0 + RW_GROUP))
        hts = [state_ref[pi] for pi in pis]
        for c in range(tb // t):
            hts = chunk_group(c, pis, hts)
        for pi, ht in zip(pis, hts):
            state_ref[pi] = ht

    y = jnp.concatenate([yp_ref[pi] for pi in range(RW_PAIRS)], axis=1)
    inv_n = 1.0 / RWKV_HEAD
    mu = _head_sums(y, ones_blk) * inv_n
    yc = y - mu
    var = _head_sums(yc * yc, ones_blk) * inv_n
    y = yc * lax.rsqrt(var + RWKV_GN_EPS) * lnw_ref[...] + lnb_ref[...]
    bonus = _head_sums(r * k * rk_ref[...], ones_blk) * v
    y_ref[...] = (y + bonus) * g


def _rwkv_mixer(proj, v_first, mu, w0, w2, a0, a2, g2, k_k, k_a, r_k, ln_w, ln_b, v0, v2):
    bn, sn, _ = proj.shape
    w = RWKV_HEADS * RWKV_HEAD
    has_vres = v_first is not None
    row = lambda x: x.reshape(1, -1)
    padrows = lambda x: jnp.pad(x, ((0, LANES - x.shape[0]), (0, 0)))
    padl = lambda x, n: jnp.pad(x, (0, n - x.shape[0]))
    mu_misc = [padl(mu[3 * w:3 * w + LORA_W], LANES), padl(mu[3 * w + LORA_W:3 * w + LORA_W + LORA_A], LANES),
               mu[3 * w + LORA_W + LORA_A:3 * w + LORA_W + LORA_A + LORA_G]]
    if has_vres:
        mu_misc.append(padl(mu[3 * w + LORA_W + LORA_A + LORA_G:], LANES))
    mu_m = padl(jnp.concatenate(mu_misc), 1024)
    const = lambda shape: pl.BlockSpec(shape, lambda b, c: (0,) * len(shape))
    blk = lambda width, idx: pl.BlockSpec((None, RW_BLK, width), lambda b, c: (b, c, idx))
    seq = pl.BlockSpec((None, RW_BLK, w), lambda b, c: (b, c, 0))
    in_specs = [blk(w, COL_R // w), blk(w, COL_R // w + 1), blk(w, COL_R // w + 2), blk(1024, COL_MISC // 1024)]
    args = [proj, proj, proj, proj]
    if has_vres:
        in_specs.append(seq)
        args.append(v_first)
    in_specs += [const((1, w))] * 3 + [const((1, 1024)), const((1, w)), const((LANES, w)), const((1, w)),
                                       const((LANES, w)), const((LORA_G, w))] + [const((1, w))] * 5
    args += [row(mu[:w]), row(mu[w:2 * w]), row(mu[2 * w:3 * w]), row(mu_m), row(w0), padrows(w2), row(a0),
             padrows(a2), g2.astype(BF16), row(k_k), row(k_a), row(r_k), row(ln_w), row(ln_b)]
    if has_vres:
        in_specs += [const((1, w)), const((LANES, w))]
        args += [row(v0), padrows(v2)]
    out_shape = [jax.ShapeDtypeStruct((bn, sn, w), F32)]
    out_specs = [seq]
    if not has_vres:
        out_shape.append(jax.ShapeDtypeStruct((bn, sn, w), F32))
        out_specs.append(seq)
    nc = RW_BLK // RW_T
    outs = pl.pallas_call(
        functools.partial(_rwkv_kernel, has_vres),
        grid=(bn, sn // RW_BLK),
        in_specs=in_specs,
        out_specs=out_specs,
        out_shape=out_shape,
        scratch_shapes=[pltpu.VMEM((RW_PAIRS, LANES, LANES), F32),
                        pltpu.VMEM((SUBLANES, w), F32),
                        pltpu.VMEM((SUBLANES, 1024), F32),
                        pltpu.VMEM((nc, 7, RW_PAIRS, RW_T, LANES), F32),
                        pltpu.VMEM((RW_PAIRS, RW_BLK, LANES), F32),
                        pltpu.VMEM((nc, RW_PAIRS, SUBLANES, LANES), F32)],
        compiler_params=_cparams(("parallel", "arbitrary")),
        name="rwkv7_mixer",
    )(*args)
    if has_vres:
        return outs[0], v_first
    return outs[0], outs[1]


MLA_TM = 256
ATT_TQ = 2048
ATT_TK = 1024
ATT_SUB = 512
HEAD_Q = 2 * LANES


def _rope_half(x2, cos, sin):
    return x2 * cos + pltpu.roll(x2, MLA_ROPE, 1) * sin


def _mla_prep_kernel(qlat_ref, kvlat_ref, misc_ref, cos_ref, sin_ref, qnw_ref, kvnw_ref, wq_ref, wkt_ref, wv_ref,
                     q_ref, kt_ref, v_ref, kpet_ref):
    def rms(x, w):
        return (x * lax.rsqrt(jnp.mean(x * x, axis=-1, keepdims=True) + RMS_EPS) * w).astype(BF16)

    cos, sin = cos_ref[...], sin_ref[...]
    scale = MLA_QK ** -0.5
    q = _dot(rms(qlat_ref[...], qnw_ref[...]), wq_ref[...])
    for h in range(MLA_HEADS):
        lo = h * HEAD_Q
        q_ref[:, lo:lo + LANES] = (q[:, lo:lo + LANES] * scale).astype(BF16)
        q_ref[:, lo + LANES:lo + HEAD_Q] = (_rope_half(q[:, lo + LANES:lo + HEAD_Q], cos, sin) * scale).astype(BF16)
    kvn = rms(kvlat_ref[...], kvnw_ref[...])
    kt_ref[...] = _dot_nt(wkt_ref[...], kvn).astype(BF16)
    v_ref[...] = _dot(kvn, wv_ref[...]).astype(BF16)
    kpet_ref[...] = _rope_half(misc_ref[:, MISC_KPE:MISC_KPE + LANES], cos, sin).T.astype(BF16)


def _flash_kernel(tq, tk, sub, qi_ref, ki_ref, q_ref, kt_ref, kpet_ref, v_ref, o_ref, m_ref, acc_ref):
    step = pl.program_id(2)
    qi, ki = qi_ref[step], ki_ref[step]
    ratio = tq // tk

    @pl.when(ki == 0)
    def _():
        m_ref[...] = jnp.full_like(m_ref, -jnp.inf)
        acc_ref[...] = jnp.zeros_like(acc_ref)

    def update(diag):
        kcat_t = jnp.concatenate([kt_ref[...], kpet_ref[...]], axis=0)
        vcat = jnp.concatenate([v_ref[...], jnp.ones((tk, LANES), BF16)], axis=1)
        plan = []
        for r in range(tq // sub):
            row_lo, row_hi = r * sub, (r + 1) * sub - 1
            ncols = tk
            masked = False
            if diag is not None:
                col_lo = diag * tk
                if row_hi < col_lo:
                    continue
                ncols = min(tk, -(-(row_hi - col_lo + 1) // HEAD_Q) * HEAD_Q)
                masked = row_lo < col_lo + ncols - 1
            plan.append((row_lo, ncols, masked))

        def scores(row_lo, ncols, masked):
            s = _dot(q_ref[row_lo:row_lo + sub, :], kcat_t[:, :ncols])
            if masked:
                ri = row_lo + lax.broadcasted_iota(jnp.int32, s.shape, 0)
                ci = diag * tk + lax.broadcasted_iota(jnp.int32, s.shape, 1)
                s = jnp.where(ci <= ri, s, -jnp.inf)
            return s

        def absorb(row_lo, ncols, s):
            rows = slice(row_lo, row_lo + sub)
            tiles = [s[:, j * LANES:(j + 1) * LANES] for j in range(ncols // LANES)]
            fold = tiles[0]
            for tl in tiles[1:]:
                fold = jnp.maximum(fold, tl)
            m_prev = m_ref[rows, :]
            m_new = jnp.maximum(m_prev, jnp.max(fold, axis=-1, keepdims=True))
            p = jnp.concatenate([jnp.exp(tl - m_new) for tl in tiles], axis=1).astype(BF16)
            alpha = jnp.exp(m_prev - m_new)
            acc_ref[rows, :] = (acc_ref[rows, :] * jnp.concatenate([alpha, alpha], axis=1)
                                + _dot(p, vcat[:ncols]))
            m_ref[rows, :] = m_new

        s_next = scores(*plan[0])
        for idx, (row_lo, ncols, _) in enumerate(plan):
            s_cur = s_next
            if idx + 1 < len(plan):
                s_next = scores(*plan[idx + 1])
            absorb(row_lo, ncols, s_cur)

    @pl.when(ki < qi * ratio)
    def _():
        update(None)

    for d in range(ratio):
        @pl.when(ki == qi * ratio + d)
        def _(d=d):
            update(d)

    @pl.when(ki == (qi + 1) * ratio - 1)
    def _():
        o_ref[...] = acc_ref[:, 0:MLA_V] / acc_ref[:, MLA_V:]


def _mla_mixer(proj, cos, sin, q_norm_w, w_q_b, kv_norm_w, w_kv_b):
    bn, sn, _ = proj.shape
    t = bn * sn
    proj2 = proj.reshape(t, N_PROJ)
    wq = w_q_b.reshape(MLA_RANK, MLA_HEADS, MLA_QK)
    pe = wq[..., MLA_NOPE:]
    rot = jnp.concatenate([-pe[..., MLA_ROPE // 2:], pe[..., :MLA_ROPE // 2]], axis=-1)
    wq = jnp.concatenate([wq, rot], axis=-1).reshape(MLA_RANK, MLA_HEADS * HEAD_Q).astype(BF16)
    wkv = w_kv_b.reshape(MLA_RANK, MLA_HEADS, MLA_NOPE + MLA_V)
    wkt = wkv[..., :MLA_NOPE].reshape(MLA_RANK, -1).T.astype(BF16)
    wv = wkv[..., MLA_NOPE:].reshape(MLA_RANK, -1).astype(BF16)
    nq, nk, nv = MLA_HEADS * HEAD_Q, MLA_HEADS * MLA_NOPE, MLA_HEADS * MLA_V
    tm = min(MLA_TM, sn)
    nsb = sn // tm
    rowblk = lambda width, idx: pl.BlockSpec((tm, width), lambda i: (i, idx))
    const = lambda shape: pl.BlockSpec(shape, lambda i: (0, 0))
    colblk = lambda rows: pl.BlockSpec((None, rows, tm), lambda i: (i // nsb, 0, i % nsb))
    q, kt, v, kpet = pl.pallas_call(
        _mla_prep_kernel,
        grid=(t // tm,),
        in_specs=[rowblk(MLA_RANK, COL_QLAT // MLA_RANK), rowblk(MLA_RANK, COL_KVLAT // MLA_RANK),
                  rowblk(1024, COL_MISC // 1024), rowblk(LANES, 0), rowblk(LANES, 0),
                  const((1, MLA_RANK)), const((1, MLA_RANK)), const((MLA_RANK, nq)), const((nk, MLA_RANK)),
                  const((MLA_RANK, nv))],
        out_specs=[rowblk(nq, 0), colblk(nk), rowblk(nv, 0), colblk(LANES)],
        out_shape=[jax.ShapeDtypeStruct((t, nq), BF16), jax.ShapeDtypeStruct((bn, nk, sn), BF16),
                   jax.ShapeDtypeStruct((t, nv), BF16), jax.ShapeDtypeStruct((bn, LANES, sn), BF16)],
        compiler_params=_cparams(("parallel",)),
        name="mla_prep",
    )(proj2, proj2, proj2, cos, sin, q_norm_w.reshape(1, -1), kv_norm_w.reshape(1, -1), wq, wkt, wv)

    tq, tk = min(ATT_TQ, sn), min(ATT_TK, sn)
    sub = min(ATT_SUB, tq)
    ratio = tq // tk
    pairs = [(a, b) for a in range(sn // tq) for b in range((a + 1) * ratio)]
    qi_arr = jnp.asarray([p[0] for p in pairs], jnp.int32)
    ki_arr = jnp.asarray([p[1] for p in pairs], jnp.int32)
    out = pl.pallas_call(
        functools.partial(_flash_kernel, tq, tk, sub),
        grid_spec=pltpu.PrefetchScalarGridSpec(
            num_scalar_prefetch=2,
            grid=(bn, MLA_HEADS, len(pairs)),
            in_specs=[pl.BlockSpec((None, tq, HEAD_Q), lambda b, h, s, qi, ki: (b, qi[s], h)),
                      pl.BlockSpec((None, MLA_NOPE, tk), lambda b, h, s, qi, ki: (b, h, ki[s])),
                      pl.BlockSpec((None, LANES, tk), lambda b, h, s, qi, ki: (b, 0, ki[s])),
                      pl.BlockSpec((None, tk, MLA_V), lambda b, h, s, qi, ki: (b, ki[s], h))],
            out_specs=pl.BlockSpec((None, tq, MLA_V), lambda b, h, s, qi, ki: (b, qi[s], h)),
            scratch_shapes=[pltpu.VMEM((tq, LANES), F32), pltpu.VMEM((tq, 2 * MLA_V), F32)]),
        out_shape=jax.ShapeDtypeStruct((bn, sn, MLA_HEADS * MLA_V), F32),
        compiler_params=_cparams(("parallel", "parallel", "arbitrary")),
        name="mla_attention",
    )(qi_arr, ki_arr, q.reshape(bn, sn, nq), kt, kpet, v.reshape(bn, sn, nv))
    return out


MERGE_TM = 256


def _merge_kernel(g0_ref, g1_ref, g2_ref, ys_ref, yr_ref, ym_ref, x_ref, wo_ref, lnw_ref, lnb_ref, xo_ref, xb_ref):
    merged = (_sigmoid(g0_ref[...]) * ys_ref[...] + _sigmoid(g1_ref[...]) * yr_ref[...]
              + _sigmoid(g2_ref[...]) * ym_ref[...])
    h = ALPHA * x_ref[...] + _dot(merged.astype(BF16), wo_ref[...])
    y = _layernorm(h, lnw_ref[...], lnb_ref[...])
    xo_ref[...] = y
    xb_ref[...] = y.astype(BF16)


def _merge_out(proj2, y_ssd, y_rwkv, y_mla, x, w_out, ln_w, ln_b):
    t, d = x.shape
    tm = min(MERGE_TM, t)
    rowblk = lambda idx: pl.BlockSpec((tm, d), lambda i: (i, idx))
    const = lambda shape: pl.BlockSpec(shape, lambda i: (0, 0))
    return pl.pallas_call(
        _merge_kernel,
        grid=(t // tm,),
        in_specs=[rowblk(0), rowblk(1), rowblk(2), rowblk(0), rowblk(0), rowblk(0), rowblk(0),
                  const((d, d)), const((1, d)), const((1, d))],
        out_specs=[rowblk(0), rowblk(0)],
        out_shape=[jax.ShapeDtypeStruct((t, d), F32), jax.ShapeDtypeStruct((t, d), BF16)],
        compiler_params=_cparams(("parallel",)),
        name="merge_out_ln1",
    )(proj2, proj2, proj2, y_ssd, y_rwkv, y_mla, x, w_out.astype(BF16), ln_w.reshape(1, d), ln_b.reshape(1, d))


FFN_TM = 512
FFN_TF = 512


def _ffn_kernel(xb_ref, x_ref, w1_ref, w3_ref, w2_ref, lnw_ref, lnb_ref, o_ref, ob_ref, acc_ref):
    f = pl.program_id(1)

    @pl.when(f == 0)
    def _():
        acc_ref[...] = jnp.zeros_like(acc_ref)

    xb = xb_ref[...]
    h = _silu(_dot(xb, w1_ref[...])) * _dot(xb, w3_ref[...])
    acc_ref[...] += _dot(h.astype(BF16), w2_ref[...])

    @pl.when(f == pl.num_programs(1) - 1)
    def _():
        y = _layernorm(ALPHA * x_ref[...] + acc_ref[...], lnw_ref[...], lnb_ref[...])
        o_ref[...] = y
        ob_ref[...] = y.astype(BF16)


def _ffn_dense(xb, x, w1, w3, w2, ln_w, ln_b):
    t, d = x.shape
    ff = w1.shape[1]
    tm, tf = min(FFN_TM, t), FFN_TF
    tiled = lambda w: w.astype(BF16).reshape(d, ff // tf, tf).transpose(1, 0, 2)
    return pl.pallas_call(
        _ffn_kernel,
        grid=(t // tm, ff // tf),
        in_specs=[pl.BlockSpec((tm, d), lambda i, f: (i, 0)), pl.BlockSpec((tm, d), lambda i, f: (i, 0)),
                  pl.BlockSpec((None, d, tf), lambda i, f: (f, 0, 0)),
                  pl.BlockSpec((None, d, tf), lambda i, f: (f, 0, 0)),
                  pl.BlockSpec((tf, d), lambda i, f: (f, 0)),
                  pl.BlockSpec((1, d), lambda i, f: (0, 0)), pl.BlockSpec((1, d), lambda i, f: (0, 0))],
        out_specs=[pl.BlockSpec((tm, d), lambda i, f: (i, 0)), pl.BlockSpec((tm, d), lambda i, f: (i, 0))],
        out_shape=[jax.ShapeDtypeStruct((t, d), F32), jax.ShapeDtypeStruct((t, d), BF16)],
        scratch_shapes=[pltpu.VMEM((tm, d), F32)],
        compiler_params=_cparams(("parallel", "arbitrary")),
        name="ffn_dense_ln2",
    )(xb, x, tiled(w1), tiled(w3), w2.astype(BF16), ln_w.reshape(1, d), ln_b.reshape(1, d))


ROUTER_TM = 512
MOE_TM = 512
MOE_TF = 512
COMBINE_TM = 256


def _router_kernel(x_ref, wr_ref, idx_ref, wgt_ref):
    logits = _dot_hi(x_ref[...], wr_ref[...])
    lane = lax.broadcasted_iota(jnp.int32, logits.shape, 1)
    lg = jnp.where(lane < N_EXPERTS, logits, -jnp.inf)
    m1 = jnp.max(lg, axis=-1, keepdims=True)
    i1 = jnp.min(jnp.where(lg == m1, lane, LANES), axis=-1, keepdims=True)
    lg2 = jnp.where(lane == i1, -jnp.inf, lg)
    m2 = jnp.max(lg2, axis=-1, keepdims=True)
    i2 = jnp.min(jnp.where(lg2 == m2, lane, LANES), axis=-1, keepdims=True)
    e = jnp.exp(m2 - m1)
    idx_ref[...] = jnp.where(lane == 0, i1, jnp.where(lane == 1, i2, 0))
    wgt_ref[...] = jnp.where(lane == 0, 1.0 / (1.0 + e), jnp.where(lane == 1, e / (1.0 + e), 0.0))


def _moe_ffn_kernel(nf_static, be_ref, tok_ref, tok_next_ref, x_hbm, w1_ref, w3_ref, w2_ref, o_ref, xg_ref, xb_ref,
                    acc_ref, sem):
    i, f = pl.program_id(0), pl.program_id(1)
    ni, nf = pl.num_programs(0), pl.num_programs(1)
    rows = xg_ref.shape[1]
    per = -(-rows // nf_static)
    tail = rows - (nf_static - 1) * per
    slot = i % 2

    def row_copy(idx_ref, sl, r):
        return pltpu.make_async_copy(x_hbm.at[pl.ds(idx_ref[0, 0, r], 1), :], xg_ref.at[sl, pl.ds(r, 1), :],
                                     sem.at[sl])

    def wait_all(idx_ref, sl):
        for r in range(rows):
            row_copy(idx_ref, sl, r).wait()

    @pl.when(f == 0)
    def _():
        @pl.when(i == 0)
        def _():
            def body(r, c):
                row_copy(tok_ref, slot, r).start()
                return c
            lax.fori_loop(0, rows, body, 0)

        wait_all(tok_ref, slot)
        xb_ref[...] = xg_ref[slot].astype(BF16)
        acc_ref[...] = jnp.zeros_like(acc_ref)

    base = f * per
    for j in range(tail):
        row_copy(tok_next_ref, 1 - slot, base + j).start()

    @pl.when(f < nf - 1)
    def _():
        for j in range(tail, per):
            row_copy(tok_next_ref, 1 - slot, base + j).start()

    xb = xb_ref[...]
    h = _silu(_dot(xb, w1_ref[...])) * _dot(xb, w3_ref[...])
    acc_ref[...] += _dot(h.astype(BF16), w2_ref[...])

    @pl.when(f == nf - 1)
    def _():
        o_ref[...] = acc_ref[...]

        @pl.when(i == ni - 1)
        def _():
            wait_all(tok_next_ref, 1 - slot)


def _combine_kernel(dst_ref, dst_next_ref, y_hbm, wgt_ref, x_ref, lnw_ref, lnb_ref, o_ref, yg_ref, sem):
    i, n = pl.program_id(0), pl.num_programs(0)
    rows = x_ref.shape[0]
    slot = i % 2

    def row_copy(idx_ref, sl, r, k):
        return pltpu.make_async_copy(y_hbm.at[pl.ds(idx_ref[0, 0, TOP_K * r + k], 1), :],
                                     yg_ref.at[sl, k, pl.ds(r, 1), :], sem.at[sl])

    def for_all(idx_ref, sl, op):
        for r in range(rows):
            for k in range(TOP_K):
                op(row_copy(idx_ref, sl, r, k))

    @pl.when(i == 0)
    def _():
        for_all(dst_ref, slot, lambda c: c.start())

    for_all(dst_next_ref, 1 - slot, lambda c: c.start())
    for_all(dst_ref, slot, lambda c: c.wait())
    wgt = wgt_ref[...]
    f = wgt[:, 0:1] * yg_ref[slot, 0] + wgt[:, 1:2] * yg_ref[slot, 1]
    o_ref[...] = _layernorm(ALPHA * x_ref[...] + f, lnw_ref[...], lnb_ref[...])

    @pl.when(i == n - 1)
    def _():
        for_all(dst_next_ref, 1 - slot, lambda c: c.wait())


def _ffn_moe(x, router, w1, w3, w2, ln_w, ln_b):
    t, d = x.shape
    ne, _, ff = w1.shape
    tm = min(ROUTER_TM, t)
    rw = jnp.pad(router, ((0, 0), (0, LANES - ne)))
    idx, wgt = pl.pallas_call(
        _router_kernel,
        grid=(t // tm,),
        in_specs=[pl.BlockSpec((tm, d), lambda i: (i, 0)), pl.BlockSpec((d, LANES), lambda i: (0, 0))],
        out_specs=[pl.BlockSpec((tm, LANES), lambda i: (i, 0))] * 2,
        out_shape=[jax.ShapeDtypeStruct((t, LANES), jnp.int32), jax.ShapeDtypeStruct((t, LANES), F32)],
        compiler_params=_cparams(("parallel",)),
        name="moe_router",
    )(x, rw)

    blk = MOE_TM
    n_assign = t * TOP_K
    flat_e = idx[:, :TOP_K].reshape(-1)
    onehot = (flat_e[:, None] == jnp.arange(ne, dtype=jnp.int32)[None, :]).astype(jnp.int32)
    csum = jnp.cumsum(onehot, axis=0)
    rank = jnp.take_along_axis(csum, flat_e[:, None], axis=1)[:, 0] - 1
    counts = csum[-1]
    padded = (counts + blk - 1) // blk * blk
    pad_end = jnp.cumsum(padded)
    dest = (pad_end - padded)[flat_e] + rank
    n_blocks = -(-(n_assign + ne * (blk - 1)) // blk)
    n_rows = n_blocks * blk
    token_of_row = jnp.zeros((n_rows,), jnp.int32).at[dest].set(jnp.arange(n_assign, dtype=jnp.int32) // TOP_K)
    block_expert = jnp.minimum(jnp.searchsorted(pad_end, jnp.arange(n_blocks, dtype=jnp.int32) * blk, side='right'),
                               ne - 1).astype(jnp.int32)

    tf = MOE_TF
    tiled = lambda w: w.astype(BF16).reshape(ne, d, ff // tf, tf).transpose(0, 2, 1, 3)
    yb = pl.pallas_call(
        functools.partial(_moe_ffn_kernel, ff // tf),
        grid_spec=pltpu.PrefetchScalarGridSpec(
            num_scalar_prefetch=1,
            grid=(n_blocks, ff // tf),
            in_specs=[pl.BlockSpec((1, 1, blk), lambda i, f, be: (i, 0, 0), memory_space=pltpu.SMEM),
                      pl.BlockSpec((1, 1, blk), lambda i, f, be: (jnp.minimum(i + 1, n_blocks - 1), 0, 0),
                                   memory_space=pltpu.SMEM),
                      pl.BlockSpec(memory_space=pl.ANY),
                      pl.BlockSpec((None, None, d, tf), lambda i, f, be: (be[i], f, 0, 0)),
                      pl.BlockSpec((None, None, d, tf), lambda i, f, be: (be[i], f, 0, 0)),
                      pl.BlockSpec((None, tf, d), lambda i, f, be: (be[i], f, 0))],
            out_specs=pl.BlockSpec((blk, d), lambda i, f, be: (i, 0)),
            scratch_shapes=[pltpu.VMEM((2, blk, d), F32), pltpu.VMEM((blk, d), BF16), pltpu.VMEM((blk, d), F32),
                            pltpu.SemaphoreType.DMA((2,))]),
        out_shape=jax.ShapeDtypeStruct((n_rows, d), F32),
        compiler_params=_cparams(("arbitrary", "arbitrary")),
        name="moe_expert_ffn",
    )(block_expert, token_of_row.reshape(n_blocks, 1, blk), token_of_row.reshape(n_blocks, 1, blk), x,
      tiled(w1), tiled(w3), w2.astype(BF16))

    tc = min(COMBINE_TM, t)
    return pl.pallas_call(
        _combine_kernel,
        grid=(t // tc,),
        in_specs=[pl.BlockSpec((1, 1, TOP_K * tc), lambda i: (i, 0, 0), memory_space=pltpu.SMEM),
                  pl.BlockSpec((1, 1, TOP_K * tc), lambda i: (jnp.minimum(i + 1, t // tc - 1), 0, 0),
                               memory_space=pltpu.SMEM),
                  pl.BlockSpec(memory_space=pl.ANY),
                  pl.BlockSpec((tc, LANES), lambda i: (i, 0)), pl.BlockSpec((tc, d), lambda i: (i, 0)),
                  pl.BlockSpec((1, d), lambda i: (0, 0)), pl.BlockSpec((1, d), lambda i: (0, 0))],
        out_specs=pl.BlockSpec((tc, d), lambda i: (i, 0)),
        out_shape=jax.ShapeDtypeStruct((t, d), F32),
        scratch_shapes=[pltpu.VMEM((2, TOP_K, tc, d), F32), pltpu.SemaphoreType.DMA((2,))],
        compiler_params=_cparams(("arbitrary",)),
        name="moe_combine_ln2",
    )(dest.reshape(t // tc, 1, TOP_K * tc), dest.reshape(t // tc, 1, TOP_K * tc), yb, wgt, x,
      ln_w.reshape(1, d), ln_b.reshape(1, d))


def _pack_w_in(w, w_vres):
    d = w.shape[0]
    o = 0

    def take(n):
        nonlocal o
        s = w[:, o:o + n]
        o += n
        return s

    padc = lambda s, n: jnp.pad(s, ((0, 0), (0, n - s.shape[1])))
    gates = take(3 * W2K)
    z = take(W2K)
    xbc = take(W2K + 2 * SSD_BC)
    dt = take(SSD_HEADS)
    qlat = take(MLA_RANK)
    kvlat = take(MLA_RANK)
    kpe = take(MLA_ROPE)
    rkv = take(3 * W2K)
    w_lo, a_lo, g_lo = take(LORA_W), take(LORA_A), take(LORA_G)
    v_lo = jnp.zeros((d, LANES), w.dtype) if w_vres is None else padc(w_vres, LANES)
    kpe_rot = jnp.concatenate([-kpe[:, MLA_ROPE // 2:], kpe[:, :MLA_ROPE // 2]], axis=1)
    misc = jnp.concatenate([padc(w_lo, LANES), padc(a_lo, LANES), g_lo, v_lo, padc(dt, LANES), kpe, kpe_rot], axis=1)
    return jnp.concatenate([gates, z, rkv, xbc, qlat, kvlat, padc(misc, 1024)], axis=1).astype(BF16)


PROJ_TM = 1024
PROJ_TN = 1024


def kernel(x, positions, w_in, w_in_vres, w_out, ssd_conv_w, ssd_conv_b, ssd_dt_bias, ssd_a_log, ssd_d, ssd_norm_w, rwkv_mu, rwkv_mu_vres, rwkv_w0, rwkv_w2, rwkv_a0, rwkv_a2, rwkv_g2, rwkv_v0, rwkv_v2, rwkv_k_k, rwkv_k_a, rwkv_r_k, rwkv_ln_w, rwkv_ln_b, mla_q_norm_w, mla_w_q_b, mla_kv_norm_w, mla_w_kv_b, ln1_w, ln1_b, ln2_w, ln2_b, ffn_w1, ffn_w3, ffn_w2, moe_router, moe_w1, moe_w3, moe_w2):
    bn, sn, d = x.shape
    t = bn * sn
    cos, sin = _rope_tables(positions)
    xf = x.reshape(t, d)
    xb = xf.astype(BF16)
    v_first = None
    for l in range(DEPTH):
        if l == 0:
            wp, mu, v0, v2 = _pack_w_in(w_in[l], None), rwkv_mu[l], None, None
        else:
            wp = _pack_w_in(w_in[l], w_in_vres[l - 1])
            mu = jnp.concatenate([rwkv_mu[l], rwkv_mu_vres[l - 1]], axis=0)
            v0, v2 = rwkv_v0[l - 1], rwkv_v2[l - 1]
        proj2 = _matmul(xb, wp, F32, min(PROJ_TM, t), PROJ_TN)
        proj = proj2.reshape(bn, sn, N_PROJ)
        y_ssd = _ssd_mixer(proj, ssd_conv_w[l], ssd_conv_b[l], ssd_dt_bias[l], ssd_a_log[l], ssd_d[l], ssd_norm_w[l])
        y_rwkv, v_first = _rwkv_mixer(proj, v_first, mu, rwkv_w0[l], rwkv_w2[l], rwkv_a0[l], rwkv_a2[l], rwkv_g2[l],
                                      rwkv_k_k[l], rwkv_k_a[l], rwkv_r_k[l], rwkv_ln_w[l], rwkv_ln_b[l], v0, v2)
        y_mla = _mla_mixer(proj, cos, sin, mla_q_norm_w[l], mla_w_q_b[l], mla_kv_norm_w[l], mla_w_kv_b[l])
        x1, x1b = _merge_out(proj2, y_ssd.reshape(t, d), y_rwkv.reshape(t, d), y_mla.reshape(t, d), xf, w_out[l],
                             ln1_w[l], ln1_b[l])
        if l % 2 == 0:
            xf, xb = _ffn_dense(x1b, x1, ffn_w1[l // 2], ffn_w3[l // 2], ffn_w2[l // 2], ln2_w[l], ln2_b[l])
        else:
            xf = _ffn_moe(x1, moe_router[l // 2], moe_w1[l // 2], moe_w3[l // 2], moe_w2[l // 2], ln2_w[l], ln2_b[l])
            xb = xf.astype(BF16)
    return xf.reshape(bn, sn, d)
```

```python
import functools
import math

import jax
import jax.numpy as jnp
import numpy as np
from jax import lax
from jax.experimental import pallas as pl
from jax.experimental.pallas import tpu as pltpu

F32 = jnp.float32
BF16 = jnp.bfloat16

D_MODEL = 2048
DEPTH = 2
ALPHA = (2 * DEPTH) ** 0.25
LN_EPS = 1e-5
RMS_EPS = 1e-6
SSD_HEADS, SSD_HEAD_DIM, SSD_GROUPS, SSD_STATE, SSD_CONV = 32, 64, 4, 128, 4
SSD_BC = SSD_GROUPS * SSD_STATE
RWKV_HEADS, RWKV_HEAD = 32, 64
RWKV_GN_EPS = 64e-5
LORA_W, LORA_A, LORA_G, LORA_V = 96, 96, 256, 64
MLA_HEADS, MLA_NOPE, MLA_ROPE, MLA_V, MLA_RANK = 16, 128, 64, 128, 512
MLA_QK = MLA_NOPE + MLA_ROPE
ROPE_THETA = 10000.0
D_FF = 5632
N_EXPERTS = 8
TOP_K = 2

LANES = 128
SUBLANES = 8
VMEM_LIMIT = 56 * 1024 * 1024

W2K = 2048
COL_GATE = 0
COL_Z = 3 * W2K
COL_R = 4 * W2K
COL_XS = 7 * W2K
COL_BC = 8 * W2K
COL_QLAT = COL_BC + 1024
COL_KVLAT = COL_QLAT + 512
COL_MISC = COL_KVLAT + 512
MISC_W, MISC_A, MISC_G, MISC_V, MISC_DT, MISC_KPE = 0, 128, 256, 512, 640, 768
N_PROJ = COL_MISC + 1024

SSD_Q = 128
RW_T = 64
RW_BLK = 128


def _cparams(sem, vmem=VMEM_LIMIT):
    return pltpu.CompilerParams(dimension_semantics=sem, vmem_limit_bytes=vmem)


def _dot(a, b):
    return jnp.dot(a, b, preferred_element_type=F32)


def _dot_nt(a, b):
    return lax.dot_general(a, b, (((1,), (1,)), ((), ())), preferred_element_type=F32)


def _split3(a):
    hi = a.astype(BF16)
    r1 = a - hi.astype(F32)
    mid = r1.astype(BF16)
    lo = (r1 - mid.astype(F32)).astype(BF16)
    return hi, mid, lo


def _dot_exact_rhs(a_bf, b):
    hi, mid, lo = _split3(b)
    return _dot(a_bf, hi) + _dot(a_bf, mid) + _dot(a_bf, lo)


def _dot_exact_lhs(a, b_bf):
    hi, mid, lo = _split3(a)
    return _dot(hi, b_bf) + _dot(mid, b_bf) + _dot(lo, b_bf)


def _sigmoid(x):
    return 1.0 / (1.0 + jnp.exp(-x))


def _silu(x):
    return x * _sigmoid(x)


def _softplus(x):
    return jnp.maximum(x, 0.0) + jnp.log(1.0 + jnp.exp(-jnp.abs(x)))


def _layernorm(x, w, b):
    mu = jnp.mean(x, axis=-1, keepdims=True)
    xc = x - mu
    var = jnp.mean(xc * xc, axis=-1, keepdims=True)
    return xc * lax.rsqrt(var + LN_EPS) * w + b


def _mm_kernel(x_ref, w_ref, o_ref):
    o_ref[...] = _dot(x_ref[...], w_ref[...]).astype(o_ref.dtype)


def _matmul(x, w, out_dtype, tm, tn):
    m, k = x.shape
    n = w.shape[1]
    return pl.pallas_call(
        _mm_kernel,
        grid=(m // tm, n // tn),
        in_specs=[pl.BlockSpec((tm, k), lambda i, j: (i, 0)),
                  pl.BlockSpec((k, tn), lambda i, j: (0, j))],
        out_specs=pl.BlockSpec((tm, tn), lambda i, j: (i, j)),
        out_shape=jax.ShapeDtypeStruct((m, n), out_dtype),
        compiler_params=_cparams(("parallel", "arbitrary")),
        name="in_proj",
    )(x, w)


def _rope_kernel(pos_ref, freq_ref, cos_ref, sin_ref):
    ang = pos_ref[...] * freq_ref[...]
    valid = lax.broadcasted_iota(jnp.int32, ang.shape, 1) < MLA_ROPE
    cos_ref[...] = jnp.where(valid, jnp.cos(ang), 0.0)
    sin_ref[...] = jnp.where(valid, jnp.sin(ang), 0.0)


def _rope_tables(positions):
    t = positions.size
    tm = min(t, 1024)
    pos = positions.reshape(t, 1).astype(F32)
    inv_freq = ROPE_THETA ** (-jnp.arange(0, MLA_ROPE, 2, dtype=F32) / MLA_ROPE)
    freq = jnp.concatenate([inv_freq, inv_freq, jnp.zeros((LANES - MLA_ROPE,), F32)]).reshape(1, LANES)
    return pl.pallas_call(
        _rope_kernel,
        grid=(t // tm,),
        in_specs=[pl.BlockSpec((tm, 1), lambda i: (i, 0)),
                  pl.BlockSpec((1, LANES), lambda i: (0, 0))],
        out_specs=[pl.BlockSpec((tm, LANES), lambda i: (i, 0))] * 2,
        out_shape=[jax.ShapeDtypeStruct((t, LANES), F32)] * 2,
        compiler_params=_cparams(("parallel",)),
        name="rope_tables",
    )(pos, freq)


def _ssd_kernel(z_ref, xs_ref, bc_ref, misc_ref, cwx_ref, cwb_ref, cbx_ref, cbb_ref, dtb_ref, alog_ref,
                d_ref, nw_ref, e_ref, y_ref, state_ref, bufx_ref, bufb_ref):
    q = SSD_Q
    hp = SSD_HEADS // SSD_GROUPS * SSD_HEAD_DIM

    @pl.when(pl.program_id(1) == 0)
    def _():
        state_ref[...] = jnp.zeros_like(state_ref)
        bufx_ref[0:SUBLANES, :] = jnp.zeros((SUBLANES, bufx_ref.shape[1]), F32)
        bufb_ref[0:SUBLANES, :] = jnp.zeros((SUBLANES, bufb_ref.shape[1]), F32)

    bufx_ref[SUBLANES:SUBLANES + q, :] = xs_ref[...].astype(F32)
    bufb_ref[SUBLANES:SUBLANES + q, :] = bc_ref[...].astype(F32)

    def conv(buf_ref, w_ref, b_ref):
        acc = b_ref[...] + w_ref[SSD_CONV - 1:SSD_CONV, :] * buf_ref[SUBLANES:SUBLANES + q, :]
        for k in range(SSD_CONV - 1):
            off = SUBLANES - (SSD_CONV - 1) + k
            acc = acc + w_ref[k:k + 1, :] * buf_ref[off:off + q, :]
        return _silu(acc)

    xs = conv(bufx_ref, cwx_ref, cbx_ref)
    bc = conv(bufb_ref, cwb_ref, cbb_ref)
    bufx_ref[0:SUBLANES, :] = bufx_ref[q:q + SUBLANES, :]
    bufb_ref[0:SUBLANES, :] = bufb_ref[q:q + SUBLANES, :]

    dt = _softplus(misc_ref[:, MISC_DT:MISC_DT + LANES] + dtb_ref[...])
    da = dt * (-jnp.exp(alog_ref[...]))
    row = lax.broadcasted_iota(jnp.int32, (q, q), 0)
    col = lax.broadcasted_iota(jnp.int32, (q, q), 1)
    causal = row >= col
    tri = jnp.where(causal, 1.0, 0.0).astype(BF16)
    cum = _dot_exact_rhs(tri, da)
    cum_t = cum.T
    ecum = jnp.exp(cum)
    toend = jnp.exp(cum[q - 1:q, :] - cum)
    e_mat = e_ref[...]
    dt_e = _dot_exact_lhs(dt, e_mat)
    ecum_e = _dot_exact_lhs(ecum, e_mat)
    toend_e = _dot_exact_lhs(toend, e_mat)

    xdt = xs * dt_e
    xdt_b = xdt.astype(BF16)
    xw_b = (xdt * toend_e).astype(BF16)
    lane = lax.broadcasted_iota(jnp.int32, (q, LANES), 1)
    lo_half = lane < SSD_HEAD_DIM

    y_groups = []
    for g in range(SSD_GROUPS):
        b_g = bc[:, g * SSD_STATE:(g + 1) * SSD_STATE]
        c_g = bc[:, SSD_BC + g * SSD_STATE:SSD_BC + (g + 1) * SSD_STATE]
        b_gb = b_g.astype(BF16)
        c_gb = c_g.astype(BF16)
        cb = _dot_nt(c_gb, b_gb)
        st = state_ref[:, g * hp:(g + 1) * hp]
        y_inter = _dot(c_gb, st.astype(BF16)) * ecum_e[:, g * hp:(g + 1) * hp]
        parts = []
        for pr in range(hp // LANES):
            ms = []
            for e in range(2):
                h = g * (SSD_HEADS // SSD_GROUPS) + pr * 2 + e
                ci = jnp.broadcast_to(cum[:, h:h + 1], (q, q))
                cj = jnp.broadcast_to(cum_t[h:h + 1, :], (q, q))
                dec = jnp.exp(jnp.where(causal, ci - cj, -jnp.inf))
                ms.append((cb * dec).astype(BF16))
            lo = g * hp + pr * LANES
            xp = xdt_b[:, lo:lo + LANES]
            zero = jnp.zeros_like(xp)
            rhs = jnp.concatenate([jnp.where(lo_half, xp, zero), jnp.where(lo_half, zero, xp)], axis=0)
            parts.append(_dot(jnp.concatenate(ms, axis=1), rhs))
        y_groups.append(jnp.concatenate(parts, axis=1) + y_inter)
        upd = _dot(b_g.T.astype(BF16), xw_b[:, g * hp:(g + 1) * hp])
        state_ref[:, g * hp:(g + 1) * hp] = st * ecum_e[q - 1:q, g * hp:(g + 1) * hp] + upd

    y = jnp.concatenate(y_groups, axis=1) + d_ref[...] * xs
    y = y * _silu(z_ref[...].astype(F32))
    outs = []
    for g in range(SSD_GROUPS):
        yg = y[:, g * hp:(g + 1) * hp]
        outs.append(yg * lax.rsqrt(jnp.mean(yg * yg, axis=-1, keepdims=True) + RMS_EPS))
    y_ref[...] = (jnp.concatenate(outs, axis=1) * nw_ref[...]).astype(y_ref.dtype)


def _ssd_mixer(proj, misc, conv_w, conv_b, dt_bias, a_log, d_skip, norm_w):
    bn, sn, _ = proj.shape
    w = SSD_HEADS * SSD_HEAD_DIM
    pad = lambda v: jnp.pad(v, (0, LANES - v.shape[0])).reshape(1, LANES)
    a_log_p = jnp.pad(a_log, (0, LANES - SSD_HEADS), constant_values=-jnp.inf).reshape(1, LANES)
    expand = jnp.pad(jnp.repeat(jnp.eye(SSD_HEADS, dtype=BF16), SSD_HEAD_DIM, axis=1),
                     ((0, LANES - SSD_HEADS), (0, 0)))
    row = lambda v: v.reshape(1, -1)
    const = lambda shape: pl.BlockSpec(shape, lambda b, c: (0, 0))
    blk = lambda width, idx: pl.BlockSpec((None, SSD_Q, width), lambda b, c: (b, c, idx))
    return pl.pallas_call(
        _ssd_kernel,
        grid=(bn, sn // SSD_Q),
        in_specs=[blk(w, COL_Z // w), blk(w, COL_XS // w), blk(1024, COL_BC // 1024), blk(1024, 0),
                  const((SSD_CONV, w)), const((SSD_CONV, 2 * SSD_BC)), const((1, w)), const((1, 2 * SSD_BC)),
                  const((1, LANES)), const((1, LANES)), const((1, w)), const((1, w)), const((LANES, w))],
        out_specs=pl.BlockSpec((None, SSD_Q, w), lambda b, c: (b, c, 0)),
        out_shape=jax.ShapeDtypeStruct((bn, sn, w), BF16),
        scratch_shapes=[pltpu.VMEM((SSD_STATE, w), F32),
                        pltpu.VMEM((SSD_Q + SUBLANES, w), F32),
                        pltpu.VMEM((SSD_Q + SUBLANES, 2 * SSD_BC), F32)],
        compiler_params=_cparams(("parallel", "arbitrary")),
        name="ssd_mixer",
    )(proj, proj, proj, misc, conv_w[:, :w], conv_w[:, w:], row(conv_b[:w]), row(conv_b[w:]),
      pad(dt_bias), a_log_p, row(jnp.repeat(d_skip, SSD_HEAD_DIM)), row(norm_w), expand)


def _dot_hi(a, b):
    return jnp.dot(a, b, precision=lax.Precision.HIGHEST, preferred_element_type=F32)


def _split2(a):
    hi = a.astype(BF16)
    return hi, (a - hi.astype(F32)).astype(BF16)


def _dot_3pass(a, b_hi, b_lo):
    a_hi, a_lo = _split2(a)
    return _dot(a_hi, b_hi) + (_dot(a_hi, b_lo) + _dot(a_lo, b_hi))


def _head_sums(x, ones_blk):
    outs = []
    for s in range(x.shape[1] // LANES):
        outs.append(_dot(x[:, s * LANES:(s + 1) * LANES].astype(BF16), ones_blk))
    return jnp.concatenate(outs, axis=1)


def _stack_heads(x, lo_half):
    zero = jnp.zeros_like(x)
    return jnp.concatenate([jnp.where(lo_half, x, zero), jnp.where(lo_half, zero, x)], axis=0)


RW_PAIRS = RWKV_HEADS // 2
RW_GROUP = 8


def _rwkv_kernel(has_vres, *refs):
    if has_vres:
        (r_ref, k_ref, v_ref, misc_ref, vfirst_ref, mur_ref, muk_ref, muv_ref, mum_ref, w0_ref, w2h_ref, w2l_ref,
         a0_ref, a2_ref, g2_ref, kk_ref, ka_ref, rk_ref, lnw_ref, lnb_ref, v0_ref, v2_ref,
         y_ref, state_ref, carry_ref, carrym_ref, st_ref, yp_ref, pt_ref) = refs
    else:
        (r_ref, k_ref, v_ref, misc_ref, mur_ref, muk_ref, muv_ref, mum_ref, w0_ref, w2h_ref, w2l_ref, a0_ref,
         a2_ref, g2_ref, kk_ref, ka_ref, rk_ref, lnw_ref, lnb_ref,
         y_ref, vout_ref, state_ref, carry_ref, carrym_ref, st_ref, yp_ref, pt_ref) = refs
    tb, t = RW_BLK, RW_T
    w = RWKV_HEADS * RWKV_HEAD

    @pl.when(pl.program_id(1) == 0)
    def _():
        state_ref[...] = jnp.zeros_like(state_ref)
        carry_ref[...] = jnp.zeros_like(carry_ref)
        carrym_ref[...] = jnp.zeros_like(carrym_ref)

    first_row = lax.broadcasted_iota(jnp.int32, (tb, 1), 0) == 0

    def shift(p, carry_row, mu):
        prev = jnp.where(first_row, carry_row, pltpu.roll(p, 1, 0))
        return p + (prev - p) * mu

    rp, kp, vp, mp = r_ref[...].astype(F32), k_ref[...].astype(F32), v_ref[...].astype(F32), misc_ref[...]
    r = shift(rp, carry_ref[0:1, :], mur_ref[...])
    k = shift(kp, carry_ref[1:2, :], muk_ref[...])
    v = shift(vp, carry_ref[2:3, :], muv_ref[...])
    m = shift(mp, carrym_ref[0:1, :], mum_ref[...])
    carry_ref[0:1, :] = rp[tb - 1:tb, :]
    carry_ref[1:2, :] = kp[tb - 1:tb, :]
    carry_ref[2:3, :] = vp[tb - 1:tb, :]
    carrym_ref[0:1, :] = mp[tb - 1:tb, :]

    w_lo = m[:, MISC_W:MISC_W + LANES]
    a_lo = m[:, MISC_A:MISC_A + LANES]
    g_lo = m[:, MISC_G:MISC_G + LORA_G]
    log_w = -_softplus(-(w0_ref[...] + _dot_3pass(jnp.tanh(w_lo), w2h_ref[...], w2l_ref[...]))) - 0.5
    lw = -jnp.exp(log_w)
    a = _sigmoid(a0_ref[...] + _dot(a_lo.astype(BF16), a2_ref[...]))
    g = _dot(_sigmoid(g_lo).astype(BF16), g2_ref[...])
    if has_vres:
        v_lo = m[:, MISC_V:MISC_V + LANES]
        v = v + (vfirst_ref[...] - v) * _sigmoid(v0_ref[...] + _dot(v_lo.astype(BF16), v2_ref[...]))
    else:
        vout_ref[...] = v

    lane = lax.broadcasted_iota(jnp.int32, (LANES, LANES), 1)
    rowi = lax.broadcasted_iota(jnp.int32, (LANES, LANES), 0)
    ones_blk = jnp.where((lane // RWKV_HEAD) == (rowi // RWKV_HEAD), 1.0, 0.0).astype(BF16)

    kk = k * kk_ref[...]
    kk = kk / jnp.maximum(jnp.sqrt(_head_sums(kk * kk, ones_blk)), 1e-12)
    k = k * (1.0 + (a - 1.0) * ka_ref[...])
    b = kk * a

    ti = lax.broadcasted_iota(jnp.int32, (t, t), 0)
    tj = lax.broadcasted_iota(jnp.int32, (t, t), 1)
    tri = jnp.where(ti >= tj, 1.0, 0.0).astype(BF16)

    for c in range(tb // t):
        sl = slice(c * t, (c + 1) * t)
        lw_c = lw[sl]
        cl = _dot_exact_rhs(tri, lw_c)
        cl_end = cl[t - 1:t, :]
        e_neg = jnp.exp(-cl)
        e_end = jnp.exp(cl_end - cl)
        ops = (kk[sl] * jnp.exp(cl - lw_c), r[sl] * jnp.exp(cl), k[sl] * e_neg, b[sl] * e_neg,
               v[sl], k[sl] * e_end, b[sl] * e_end)
        for pi in range(RW_PAIRS):
            ls = slice(pi * LANES, (pi + 1) * LANES)
            for oi, op in enumerate(ops):
                st_ref[c, oi, pi] = op[:, ls]
            pt_ref[c, pi] = jnp.broadcast_to(jnp.exp(cl_end[:, ls]), (SUBLANES, LANES))

    lo_half = lax.broadcasted_iota(jnp.int32, (t, LANES), 1) < RWKV_HEAD
    bi = lax.broadcasted_iota(jnp.int32, (2 * t, 2 * t), 0) % t
    bj = lax.broadcasted_iota(jnp.int32, (2 * t, 2 * t), 1) % t
    strict = bi > bj
    incl = bi >= bj

    def chunk_group(c, pis, hts):
        h2 = 2 * t
        stk = [[_stack_heads(st_ref[c, oi, pi], lo_half) for oi in range(7)] for pi in pis]
        lhs2 = [jnp.concatenate([s[0], s[1]], axis=0).astype(BF16) for s in stk]
        rhs2 = [jnp.concatenate([s[2], s[3]], axis=0).astype(BF16) for s in stk]
        amat = [_dot_nt(a, b) for a, b in zip(lhs2, rhs2)]
        sh = [_dot_nt(a, h.astype(BF16)) for a, h in zip(lhs2, hts)]
        vsb = [s[4].astype(BF16) for s in stk]
        x = [s_[0:h2] + _dot(jnp.where(strict, am[0:h2, 0:h2], 0.0).astype(BF16), v_)
             for s_, am, v_ in zip(sh, amat, vsb)]
        pw = [jnp.where(strict, am[0:h2, h2:], 0.0) for am in amat]
        n, sign = 1, -1.0
        while n < t:
            if 2 * n < t:
                res = [_dot(p_.astype(BF16), jnp.concatenate([p_, x_], axis=1).astype(BF16)) for p_, x_ in zip(pw, x)]
                x = [x_ + sign * r_[:, h2:] for x_, r_ in zip(x, res)]
                pw = [r_[:, 0:h2] for r_ in res]
            else:
                x = [x_ + sign * _dot(p_.astype(BF16), x_.astype(BF16)) for p_, x_ in zip(pw, x)]
            n, sign = 2 * n, 1.0
        new_hts = []
        for i, pi in enumerate(pis):
            am, s = amat[i], stk[i]
            a_r = jnp.concatenate([jnp.where(incl, am[h2:, 0:h2], 0.0), jnp.where(incl, -am[h2:, h2:], 0.0)], axis=1)
            ys = sh[i][h2:] + _dot(a_r.astype(BF16), jnp.concatenate([s[4], x[i]], axis=0).astype(BF16))
            yp_ref[pi, c * t:(c + 1) * t, :] = ys[0:t] + ys[t:]
            lhs3 = jnp.concatenate([s[4].T, -(x[i].T)], axis=1).astype(BF16)
            rhs3 = jnp.concatenate([s[5], s[6]], axis=0).astype(BF16)
            new_hts.append(hts[i] * pt_ref[c, pi][0:1, :] + _dot(lhs3, rhs3))
        return new_hts

    for g0 in range(0, RW_PAIRS, RW_GROUP):
        pis = list(range(g0, g0 + RW_GROUP))
        hts = [state_ref[pi] for pi in pis]
        for c in range(tb // t):
            hts = chunk_group(c, pis, hts)
        for pi, ht in zip(pis, hts):
            state_ref[pi] = ht

    y = jnp.concatenate([yp_ref[pi] for pi in range(RW_PAIRS)], axis=1)
    inv_n = 1.0 / RWKV_HEAD
    mu = _head_sums(y, ones_blk) * inv_n
    yc = y - mu
    var = _head_sums(yc * yc, ones_blk) * inv_n
    y = yc * lax.rsqrt(var + RWKV_GN_EPS) * lnw_ref[...] + lnb_ref[...]
    bonus = _head_sums(r * k * rk_ref[...], ones_blk) * v
    y_ref[...] = ((y + bonus) * g).astype(y_ref.dtype)


def _rwkv_mixer(proj, misc, v_first, mu, w0, w2, a0, a2, g2, k_k, k_a, r_k, ln_w, ln_b, v0, v2):
    bn, sn, _ = proj.shape
    w = RWKV_HEADS * RWKV_HEAD
    has_vres = v_first is not None
    row = lambda x: x.reshape(1, -1)
    padrows = lambda x: jnp.pad(x, ((0, LANES - x.shape[0]), (0, 0)))
    padl = lambda x, n: jnp.pad(x, (0, n - x.shape[0]))
    mu_misc = [padl(mu[3 * w:3 * w + LORA_W], LANES), padl(mu[3 * w + LORA_W:3 * w + LORA_W + LORA_A], LANES),
               mu[3 * w + LORA_W + LORA_A:3 * w + LORA_W + LORA_A + LORA_G]]
    if has_vres:
        mu_misc.append(padl(mu[3 * w + LORA_W + LORA_A + LORA_G:], LANES))
    mu_m = padl(jnp.concatenate(mu_misc), 1024)
    const = lambda shape: pl.BlockSpec(shape, lambda b, c: (0,) * len(shape))
    blk = lambda width, idx: pl.BlockSpec((None, RW_BLK, width), lambda b, c: (b, c, idx))
    seq = pl.BlockSpec((None, RW_BLK, w), lambda b, c: (b, c, 0))
    in_specs = [blk(w, COL_R // w), blk(w, COL_R // w + 1), blk(w, COL_R // w + 2), blk(1024, 0)]
    args = [proj, proj, proj, misc]
    if has_vres:
        in_specs.append(seq)
        args.append(v_first)
    in_specs += [const((1, w))] * 3 + [const((1, 1024)), const((1, w)), const((LANES, w)), const((LANES, w)),
                                       const((1, w)), const((LANES, w)), const((LORA_G, w))] + [const((1, w))] * 5
    w2p = padrows(w2)
    w2_hi = w2p.astype(BF16)
    w2_lo = (w2p - w2_hi.astype(F32)).astype(BF16)
    args += [row(mu[:w]), row(mu[w:2 * w]), row(mu[2 * w:3 * w]), row(mu_m), row(w0), w2_hi, w2_lo, row(a0),
             padrows(a2).astype(BF16), g2.astype(BF16), row(k_k), row(k_a), row(r_k), row(ln_w), row(ln_b)]
    if has_vres:
        in_specs += [const((1, w)), const((LANES, w))]
        args += [row(v0), padrows(v2).astype(BF16)]
    out_shape = [jax.ShapeDtypeStruct((bn, sn, w), BF16)]
    out_specs = [seq]
    if not has_vres:
        out_shape.append(jax.ShapeDtypeStruct((bn, sn, w), F32))
        out_specs.append(seq)
    nc = RW_BLK // RW_T
    outs = pl.pallas_call(
        functools.partial(_rwkv_kernel, has_vres),
        grid=(bn, sn // RW_BLK),
        in_specs=in_specs,
        out_specs=out_specs,
        out_shape=out_shape,
        scratch_shapes=[pltpu.VMEM((RW_PAIRS, LANES, LANES), F32),
                        pltpu.VMEM((SUBLANES, w), F32),
                        pltpu.VMEM((SUBLANES, 1024), F32),
                        pltpu.VMEM((nc, 7, RW_PAIRS, RW_T, LANES), F32),
                        pltpu.VMEM((RW_PAIRS, RW_BLK, LANES), F32),
                        pltpu.VMEM((nc, RW_PAIRS, SUBLANES, LANES), F32)],
        compiler_params=_cparams(("parallel", "arbitrary")),
        name="rwkv7_mixer",
    )(*args)
    if has_vres:
        return outs[0], v_first
    return outs[0], outs[1]


MLA_TM = 256
ATT_TQ = 2048
ATT_TK = 2048
ATT_SUB = 512
HEAD_Q = 2 * LANES


def _rope_half(x2, cos, sin):
    return x2 * cos + pltpu.roll(x2, MLA_ROPE, 1) * sin


def _mla_prep_kernel(qlat_ref, kvlat_ref, misc_ref, cos_ref, sin_ref, qnw_ref, kvnw_ref, wq_ref, wkt_ref, wv_ref,
                     q_ref, kt_ref, v_ref, kpet_ref):
    def rms(x, w):
        x = x.astype(F32)
        return (x * lax.rsqrt(jnp.mean(x * x, axis=-1, keepdims=True) + RMS_EPS) * w).astype(BF16)

    cos, sin = cos_ref[...], sin_ref[...]
    scale = MLA_QK ** -0.5
    q = _dot(rms(qlat_ref[...], qnw_ref[...]), wq_ref[...])
    for h in range(MLA_HEADS):
        lo = h * HEAD_Q
        q_ref[:, lo:lo + LANES] = (q[:, lo:lo + LANES] * scale).astype(BF16)
        q_ref[:, lo + LANES:lo + HEAD_Q] = (_rope_half(q[:, lo + LANES:lo + HEAD_Q], cos, sin) * scale).astype(BF16)
    kvn = rms(kvlat_ref[...], kvnw_ref[...])
    kt_ref[...] = _dot_nt(wkt_ref[...], kvn).astype(BF16)
    v_ref[...] = _dot(kvn, wv_ref[...]).astype(BF16)
    kpet_ref[...] = _rope_half(misc_ref[:, MISC_KPE:MISC_KPE + LANES], cos, sin).T.astype(BF16)


def _flash_kernel(tq, tk, sub, qi_ref, ki_ref, q_ref, kt_ref, kpet_ref, v_ref, o_ref, m_ref, acc_ref):
    step = pl.program_id(2)
    qi, ki = qi_ref[step], ki_ref[step]
    ratio = tq // tk

    @pl.when(ki == 0)
    def _():
        m_ref[...] = jnp.full_like(m_ref, -jnp.inf)
        acc_ref[...] = jnp.zeros_like(acc_ref)

    def update(diag):
        kcat_t = jnp.concatenate([kt_ref[...], kpet_ref[...]], axis=0)
        vcat = jnp.concatenate([v_ref[...], jnp.ones((tk, LANES), BF16)], axis=1)
        plan = []
        for r in range(tq // sub):
            row_lo, row_hi = r * sub, (r + 1) * sub - 1
            ncols = tk
            masked = False
            if diag is not None:
                col_lo = diag * tk
                if row_hi < col_lo:
                    continue
                ncols = min(tk, -(-(row_hi - col_lo + 1) // HEAD_Q) * HEAD_Q)
                masked = row_lo < col_lo + ncols - 1
            plan.append((row_lo, ncols, masked))

        def scores(row_lo, ncols, masked):
            s = _dot(q_ref[row_lo:row_lo + sub, :], kcat_t[:, :ncols])
            if masked:
                ri = row_lo + lax.broadcasted_iota(jnp.int32, s.shape, 0)
                ci = diag * tk + lax.broadcasted_iota(jnp.int32, s.shape, 1)
                s = jnp.where(ci <= ri, s, -jnp.inf)
            return s

        def absorb(row_lo, ncols, s):
            rows = slice(row_lo, row_lo + sub)
            tiles = [s[:, j * LANES:(j + 1) * LANES] for j in range(ncols // LANES)]
            fold = tiles[0]
            for tl in tiles[1:]:
                fold = jnp.maximum(fold, tl)
            m_prev = m_ref[rows, :]
            m_new = jnp.maximum(m_prev, jnp.max(fold, axis=-1, keepdims=True))
            p = jnp.concatenate([jnp.exp(tl - m_new) for tl in tiles], axis=1).astype(BF16)
            alpha = jnp.exp(m_prev - m_new)
            acc_ref[rows, :] = (acc_ref[rows, :] * jnp.concatenate([alpha, alpha], axis=1)
                                + _dot(p, vcat[:ncols]))
            m_ref[rows, :] = m_new

        s_next = scores(*plan[0])
        for idx, (row_lo, ncols, _) in enumerate(plan):
            s_cur = s_next
            if idx + 1 < len(plan):
                s_next = scores(*plan[idx + 1])
            absorb(row_lo, ncols, s_cur)

    @pl.when(ki < qi * ratio)
    def _():
        update(None)

    for d in range(ratio):
        @pl.when(ki == qi * ratio + d)
        def _(d=d):
            update(d)

    @pl.when(ki == (qi + 1) * ratio - 1)
    def _():
        o_ref[...] = (acc_ref[:, 0:MLA_V] / acc_ref[:, MLA_V:]).astype(o_ref.dtype)


def _mla_mixer(proj, misc, cos, sin, q_norm_w, w_q_b, kv_norm_w, w_kv_b):
    bn, sn, _ = proj.shape
    t = bn * sn
    proj2 = proj.reshape(t, proj.shape[-1])
    misc2 = misc.reshape(t, misc.shape[-1])
    wq = w_q_b.reshape(MLA_RANK, MLA_HEADS, MLA_QK)
    pe = wq[..., MLA_NOPE:]
    rot = jnp.concatenate([-pe[..., MLA_ROPE // 2:], pe[..., :MLA_ROPE // 2]], axis=-1)
    wq = jnp.concatenate([wq, rot], axis=-1).reshape(MLA_RANK, MLA_HEADS * HEAD_Q).astype(BF16)
    wkv = w_kv_b.reshape(MLA_RANK, MLA_HEADS, MLA_NOPE + MLA_V)
    wkt = wkv[..., :MLA_NOPE].reshape(MLA_RANK, -1).T.astype(BF16)
    wv = wkv[..., MLA_NOPE:].reshape(MLA_RANK, -1).astype(BF16)
    nq, nk, nv = MLA_HEADS * HEAD_Q, MLA_HEADS * MLA_NOPE, MLA_HEADS * MLA_V
    tm = min(MLA_TM, sn)
    nsb = sn // tm
    rowblk = lambda width, idx: pl.BlockSpec((tm, width), lambda i: (i, idx))
    const = lambda shape: pl.BlockSpec(shape, lambda i: (0, 0))
    colblk = lambda rows: pl.BlockSpec((None, rows, tm), lambda i: (i // nsb, 0, i % nsb))
    q, kt, v, kpet = pl.pallas_call(
        _mla_prep_kernel,
        grid=(t // tm,),
        in_specs=[rowblk(MLA_RANK, COL_QLAT // MLA_RANK), rowblk(MLA_RANK, COL_KVLAT // MLA_RANK),
                  rowblk(1024, 0), rowblk(LANES, 0), rowblk(LANES, 0),
                  const((1, MLA_RANK)), const((1, MLA_RANK)), const((MLA_RANK, nq)), const((nk, MLA_RANK)),
                  const((MLA_RANK, nv))],
        out_specs=[rowblk(nq, 0), colblk(nk), rowblk(nv, 0), colblk(LANES)],
        out_shape=[jax.ShapeDtypeStruct((t, nq), BF16), jax.ShapeDtypeStruct((bn, nk, sn), BF16),
                   jax.ShapeDtypeStruct((t, nv), BF16), jax.ShapeDtypeStruct((bn, LANES, sn), BF16)],
        compiler_params=_cparams(("parallel",)),
        name="mla_prep",
    )(proj2, proj2, misc2, cos, sin, q_norm_w.reshape(1, -1), kv_norm_w.reshape(1, -1), wq, wkt, wv)

    tq, tk = min(ATT_TQ, sn), min(ATT_TK, sn)
    sub = min(ATT_SUB, tq)
    ratio = tq // tk
    pairs = [(a, b) for a in range(sn // tq) for b in range((a + 1) * ratio)]
    qi_arr = jnp.asarray([p[0] for p in pairs], jnp.int32)
    ki_arr = jnp.asarray([p[1] for p in pairs], jnp.int32)
    out = pl.pallas_call(
        functools.partial(_flash_kernel, tq, tk, sub),
        grid_spec=pltpu.PrefetchScalarGridSpec(
            num_scalar_prefetch=2,
            grid=(bn, MLA_HEADS, len(pairs)),
            in_specs=[pl.BlockSpec((None, tq, HEAD_Q), lambda b, h, s, qi, ki: (b, qi[s], h)),
                      pl.BlockSpec((None, MLA_NOPE, tk), lambda b, h, s, qi, ki: (b, h, ki[s])),
                      pl.BlockSpec((None, LANES, tk), lambda b, h, s, qi, ki: (b, 0, ki[s])),
                      pl.BlockSpec((None, tk, MLA_V), lambda b, h, s, qi, ki: (b, ki[s], h))],
            out_specs=pl.BlockSpec((None, tq, MLA_V), lambda b, h, s, qi, ki: (b, qi[s], h)),
            scratch_shapes=[pltpu.VMEM((tq, LANES), F32), pltpu.VMEM((tq, 2 * MLA_V), F32)]),
        out_shape=jax.ShapeDtypeStruct((bn, sn, MLA_HEADS * MLA_V), BF16),
        compiler_params=_cparams(("parallel", "parallel", "arbitrary")),
        name="mla_attention",
    )(qi_arr, ki_arr, q.reshape(bn, sn, nq), kt, kpet, v.reshape(bn, sn, nv))
    return out


MERGE_TM = 256


def _merge_kernel(g0_ref, g1_ref, g2_ref, ys_ref, yr_ref, ym_ref, x_ref, wo_ref, lnw_ref, lnb_ref, xo_ref, xb_ref):
    gate = lambda g_ref, y_ref: _sigmoid(g_ref[...].astype(F32)) * y_ref[...].astype(F32)
    merged = gate(g0_ref, ys_ref) + gate(g1_ref, yr_ref) + gate(g2_ref, ym_ref)
    h = ALPHA * x_ref[...] + _dot(merged.astype(BF16), wo_ref[...])
    y = _layernorm(h, lnw_ref[...], lnb_ref[...])
    xo_ref[...] = y
    xb_ref[...] = y.astype(BF16)


def _merge_out(proj2, y_ssd, y_rwkv, y_mla, x, w_out, ln_w, ln_b):
    t, d = x.shape
    tm = min(MERGE_TM, t)
    rowblk = lambda idx: pl.BlockSpec((tm, d), lambda i: (i, idx))
    const = lambda shape: pl.BlockSpec(shape, lambda i: (0, 0))
    return pl.pallas_call(
        _merge_kernel,
        grid=(t // tm,),
        in_specs=[rowblk(0), rowblk(1), rowblk(2), rowblk(0), rowblk(0), rowblk(0), rowblk(0),
                  const((d, d)), const((1, d)), const((1, d))],
        out_specs=[rowblk(0), rowblk(0)],
        out_shape=[jax.ShapeDtypeStruct((t, d), F32), jax.ShapeDtypeStruct((t, d), BF16)],
        compiler_params=_cparams(("parallel",)),
        name="merge_out_ln1",
    )(proj2, proj2, proj2, y_ssd, y_rwkv, y_mla, x, w_out.astype(BF16), ln_w.reshape(1, d), ln_b.reshape(1, d))


FFN_TM = 512
FFN_TF = 512


def _ffn_kernel(xb_ref, x_ref, w1_ref, w3_ref, w2_ref, lnw_ref, lnb_ref, o_ref, ob_ref, acc_ref):
    f = pl.program_id(1)

    @pl.when(f == 0)
    def _():
        acc_ref[...] = jnp.zeros_like(acc_ref)

    xb = xb_ref[...]
    h = _silu(_dot(xb, w1_ref[...])) * _dot(xb, w3_ref[...])
    acc_ref[...] += _dot(h.astype(BF16), w2_ref[...])

    @pl.when(f == pl.num_programs(1) - 1)
    def _():
        y = _layernorm(ALPHA * x_ref[...] + acc_ref[...], lnw_ref[...], lnb_ref[...])
        o_ref[...] = y
        ob_ref[...] = y.astype(BF16)


def _ffn_dense(xb, x, w1, w3, w2, ln_w, ln_b):
    t, d = x.shape
    ff = w1.shape[1]
    tm, tf = min(FFN_TM, t), FFN_TF
    return pl.pallas_call(
        _ffn_kernel,
        grid=(t // tm, ff // tf),
        in_specs=[pl.BlockSpec((tm, d), lambda i, f: (i, 0)), pl.BlockSpec((tm, d), lambda i, f: (i, 0)),
                  pl.BlockSpec((d, tf), lambda i, f: (0, f)), pl.BlockSpec((d, tf), lambda i, f: (0, f)),
                  pl.BlockSpec((tf, d), lambda i, f: (f, 0)),
                  pl.BlockSpec((1, d), lambda i, f: (0, 0)), pl.BlockSpec((1, d), lambda i, f: (0, 0))],
        out_specs=[pl.BlockSpec((tm, d), lambda i, f: (i, 0)), pl.BlockSpec((tm, d), lambda i, f: (i, 0))],
        out_shape=[jax.ShapeDtypeStruct((t, d), F32), jax.ShapeDtypeStruct((t, d), BF16)],
        scratch_shapes=[pltpu.VMEM((tm, d), F32)],
        compiler_params=_cparams(("parallel", "arbitrary")),
        name="ffn_dense_ln2",
    )(xb, x, w1.astype(BF16), w3.astype(BF16), w2.astype(BF16), ln_w.reshape(1, d), ln_b.reshape(1, d))


ROUTER_TM = 512
MOE_TM = 512
MOE_TF = 512
COMBINE_TM = 256


def _router_kernel(x_ref, wr_ref, idx_ref, wgt_ref):
    logits = _dot_hi(x_ref[...], wr_ref[...])
    lane = lax.broadcasted_iota(jnp.int32, logits.shape, 1)
    lg = jnp.where(lane < N_EXPERTS, logits, -jnp.inf)
    m1 = jnp.max(lg, axis=-1, keepdims=True)
    i1 = jnp.min(jnp.where(lg == m1, lane, LANES), axis=-1, keepdims=True)
    lg2 = jnp.where(lane == i1, -jnp.inf, lg)
    m2 = jnp.max(lg2, axis=-1, keepdims=True)
    i2 = jnp.min(jnp.where(lg2 == m2, lane, LANES), axis=-1, keepdims=True)
    e = jnp.exp(m2 - m1)
    idx_ref[...] = jnp.where(lane == 0, i1, jnp.where(lane == 1, i2, 0))
    wgt_ref[...] = jnp.where(lane == 0, 1.0 / (1.0 + e), jnp.where(lane == 1, e / (1.0 + e), 0.0))


def _moe_ffn_kernel(nf_static, be_ref, tok_ref, tok_next_ref, x_hbm, w1_ref, w3_ref, w2_ref, o_ref, xg_ref, xb_ref,
                    acc_ref, sem):
    i, f = pl.program_id(0), pl.program_id(1)
    ni, nf = pl.num_programs(0), pl.num_programs(1)
    rows = xg_ref.shape[1]
    per = -(-rows // nf_static)
    tail = rows - (nf_static - 1) * per
    slot = i % 2

    def row_copy(idx_ref, sl, r):
        return pltpu.make_async_copy(x_hbm.at[pl.ds(idx_ref[0, 0, r], 1), :], xg_ref.at[sl, pl.ds(r, 1), :],
                                     sem.at[sl])

    def wait_all(idx_ref, sl):
        for r in range(rows):
            row_copy(idx_ref, sl, r).wait()

    @pl.when(f == 0)
    def _():
        @pl.when(i == 0)
        def _():
            def body(r, c):
                row_copy(tok_ref, slot, r).start()
                return c
            lax.fori_loop(0, rows, body, 0)

        wait_all(tok_ref, slot)
        xb_ref[...] = xg_ref[slot].astype(BF16)
        acc_ref[...] = jnp.zeros_like(acc_ref)

    base = f * per
    for j in range(tail):
        row_copy(tok_next_ref, 1 - slot, base + j).start()

    @pl.when(f < nf - 1)
    def _():
        for j in range(tail, per):
            row_copy(tok_next_ref, 1 - slot, base + j).start()

    xb = xb_ref[...]
    h = _silu(_dot(xb, w1_ref[...])) * _dot(xb, w3_ref[...])
    acc_ref[...] += _dot(h.astype(BF16), w2_ref[...])

    @pl.when(f == nf - 1)
    def _():
        o_ref[...] = acc_ref[...]

        @pl.when(i == ni - 1)
        def _():
            wait_all(tok_next_ref, 1 - slot)


def _combine_kernel(dst_ref, dst_next_ref, y_hbm, wgt_ref, x_ref, lnw_ref, lnb_ref, o_ref, yg_ref, sem):
    i, n = pl.program_id(0), pl.num_programs(0)
    rows = x_ref.shape[0]
    slot = i % 2

    def row_copy(idx_ref, sl, r, k):
        return pltpu.make_async_copy(y_hbm.at[pl.ds(idx_ref[0, 0, TOP_K * r + k], 1), :],
                                     yg_ref.at[sl, k, pl.ds(r, 1), :], sem.at[sl])

    def for_all(idx_ref, sl, op):
        for r in range(rows):
            for k in range(TOP_K):
                op(row_copy(idx_ref, sl, r, k))

    @pl.when(i == 0)
    def _():
        for_all(dst_ref, slot, lambda c: c.start())

    for_all(dst_next_ref, 1 - slot, lambda c: c.start())
    for_all(dst_ref, slot, lambda c: c.wait())
    wgt = wgt_ref[...]
    f = wgt[:, 0:1] * yg_ref[slot, 0] + wgt[:, 1:2] * yg_ref[slot, 1]
    o_ref[...] = _layernorm(ALPHA * x_ref[...] + f, lnw_ref[...], lnb_ref[...])

    @pl.when(i == n - 1)
    def _():
        for_all(dst_next_ref, 1 - slot, lambda c: c.wait())


def _ffn_moe(x, router, w1, w3, w2, ln_w, ln_b):
    t, d = x.shape
    ne, _, ff = w1.shape
    tm = min(ROUTER_TM, t)
    rw = jnp.pad(router, ((0, 0), (0, LANES - ne)))
    idx, wgt = pl.pallas_call(
        _router_kernel,
        grid=(t // tm,),
        in_specs=[pl.BlockSpec((tm, d), lambda i: (i, 0)), pl.BlockSpec((d, LANES), lambda i: (0, 0))],
        out_specs=[pl.BlockSpec((tm, LANES), lambda i: (i, 0))] * 2,
        out_shape=[jax.ShapeDtypeStruct((t, LANES), jnp.int32), jax.ShapeDtypeStruct((t, LANES), F32)],
        compiler_params=_cparams(("parallel",)),
        name="moe_router",
    )(x, rw)

    blk = MOE_TM
    n_assign = t * TOP_K
    flat_e = idx[:, :TOP_K].reshape(-1)
    onehot = (flat_e[:, None] == jnp.arange(ne, dtype=jnp.int32)[None, :]).astype(jnp.int32)
    csum = jnp.cumsum(onehot, axis=0)
    rank = jnp.take_along_axis(csum, flat_e[:, None], axis=1)[:, 0] - 1
    counts = csum[-1]
    padded = (counts + blk - 1) // blk * blk
    pad_end = jnp.cumsum(padded)
    dest = (pad_end - padded)[flat_e] + rank
    n_blocks = -(-(n_assign + ne * (blk - 1)) // blk)
    n_rows = n_blocks * blk
    token_of_row = jnp.zeros((n_rows,), jnp.int32).at[dest].set(jnp.arange(n_assign, dtype=jnp.int32) // TOP_K)
    block_expert = jnp.minimum(jnp.searchsorted(pad_end, jnp.arange(n_blocks, dtype=jnp.int32) * blk, side='right'),
                               ne - 1).astype(jnp.int32)

    tf = MOE_TF
    yb = pl.pallas_call(
        functools.partial(_moe_ffn_kernel, ff // tf),
        grid_spec=pltpu.PrefetchScalarGridSpec(
            num_scalar_prefetch=1,
            grid=(n_blocks, ff // tf),
            in_specs=[pl.BlockSpec((1, 1, blk), lambda i, f, be: (i, 0, 0), memory_space=pltpu.SMEM),
                      pl.BlockSpec((1, 1, blk), lambda i, f, be: (jnp.minimum(i + 1, n_blocks - 1), 0, 0),
                                   memory_space=pltpu.SMEM),
                      pl.BlockSpec(memory_space=pl.ANY),
                      pl.BlockSpec((None, d, tf), lambda i, f, be: (be[i], 0, f)),
                      pl.BlockSpec((None, d, tf), lambda i, f, be: (be[i], 0, f)),
                      pl.BlockSpec((None, tf, d), lambda i, f, be: (be[i], f, 0))],
            out_specs=pl.BlockSpec((blk, d), lambda i, f, be: (i, 0)),
            scratch_shapes=[pltpu.VMEM((2, blk, d), F32), pltpu.VMEM((blk, d), BF16), pltpu.VMEM((blk, d), F32),
                            pltpu.SemaphoreType.DMA((2,))]),
        out_shape=jax.ShapeDtypeStruct((n_rows, d), F32),
        compiler_params=_cparams(("arbitrary", "arbitrary")),
        name="moe_expert_ffn",
    )(block_expert, token_of_row.reshape(n_blocks, 1, blk), token_of_row.reshape(n_blocks, 1, blk), x,
      w1.astype(BF16), w3.astype(BF16), w2.astype(BF16))

    tc = min(COMBINE_TM, t)
    return pl.pallas_call(
        _combine_kernel,
        grid=(t // tc,),
        in_specs=[pl.BlockSpec((1, 1, TOP_K * tc), lambda i: (i, 0, 0), memory_space=pltpu.SMEM),
                  pl.BlockSpec((1, 1, TOP_K * tc), lambda i: (jnp.minimum(i + 1, t // tc - 1), 0, 0),
                               memory_space=pltpu.SMEM),
                  pl.BlockSpec(memory_space=pl.ANY),
                  pl.BlockSpec((tc, LANES), lambda i: (i, 0)), pl.BlockSpec((tc, d), lambda i: (i, 0)),
                  pl.BlockSpec((1, d), lambda i: (0, 0)), pl.BlockSpec((1, d), lambda i: (0, 0))],
        out_specs=pl.BlockSpec((tc, d), lambda i: (i, 0)),
        out_shape=jax.ShapeDtypeStruct((t, d), F32),
        scratch_shapes=[pltpu.VMEM((2, TOP_K, tc, d), F32), pltpu.SemaphoreType.DMA((2,))],
        compiler_params=_cparams(("arbitrary",)),
        name="moe_combine_ln2",
    )(dest.reshape(t // tc, 1, TOP_K * tc), dest.reshape(t // tc, 1, TOP_K * tc), yb, wgt, x,
      ln_w.reshape(1, d), ln_b.reshape(1, d))


def _pack_w_in(w, w_vres):
    d = w.shape[0]
    o = 0

    def take(n):
        nonlocal o
        s = w[:, o:o + n]
        o += n
        return s

    padc = lambda s, n: jnp.pad(s, ((0, 0), (0, n - s.shape[1])))
    gates = take(3 * W2K)
    z = take(W2K)
    xbc = take(W2K + 2 * SSD_BC)
    dt = take(SSD_HEADS)
    qlat = take(MLA_RANK)
    kvlat = take(MLA_RANK)
    kpe = take(MLA_ROPE)
    rkv = take(3 * W2K)
    w_lo, a_lo, g_lo = take(LORA_W), take(LORA_A), take(LORA_G)
    v_lo = jnp.zeros((d, LANES), w.dtype) if w_vres is None else padc(w_vres, LANES)
    kpe_rot = jnp.concatenate([-kpe[:, MLA_ROPE // 2:], kpe[:, :MLA_ROPE // 2]], axis=1)
    misc = jnp.concatenate([padc(w_lo, LANES), padc(a_lo, LANES), g_lo, v_lo, padc(dt, LANES), kpe, kpe_rot], axis=1)
    return jnp.concatenate([gates, z, rkv, xbc, qlat, kvlat, padc(misc, 1024)], axis=1).astype(BF16)


PROJ_TM = 1024
PROJ_TN = 1024


def kernel(x, positions, w_in, w_in_vres, w_out, ssd_conv_w, ssd_conv_b, ssd_dt_bias, ssd_a_log, ssd_d, ssd_norm_w, rwkv_mu, rwkv_mu_vres, rwkv_w0, rwkv_w2, rwkv_a0, rwkv_a2, rwkv_g2, rwkv_v0, rwkv_v2, rwkv_k_k, rwkv_k_a, rwkv_r_k, rwkv_ln_w, rwkv_ln_b, mla_q_norm_w, mla_w_q_b, mla_kv_norm_w, mla_w_kv_b, ln1_w, ln1_b, ln2_w, ln2_b, ffn_w1, ffn_w3, ffn_w2, moe_router, moe_w1, moe_w3, moe_w2):
    bn, sn, d = x.shape
    t = bn * sn
    cos, sin = _rope_tables(positions)
    xf = x.reshape(t, d)
    xb = xf.astype(BF16)
    v_first = None
    for l in range(DEPTH):
        if l == 0:
            wp, mu, v0, v2 = _pack_w_in(w_in[l], None), rwkv_mu[l], None, None
        else:
            wp = _pack_w_in(w_in[l], w_in_vres[l - 1])
            mu = jnp.concatenate([rwkv_mu[l], rwkv_mu_vres[l - 1]], axis=0)
            v0, v2 = rwkv_v0[l - 1], rwkv_v2[l - 1]
        proj2 = _matmul(xb, wp[:, :COL_MISC], BF16, min(PROJ_TM, t), PROJ_TN)
        misc = _matmul(xb, wp[:, COL_MISC:], F32, min(PROJ_TM, t), N_PROJ - COL_MISC).reshape(bn, sn, -1)
        proj = proj2.reshape(bn, sn, COL_MISC)
        y_ssd = _ssd_mixer(proj, misc, ssd_conv_w[l], ssd_conv_b[l], ssd_dt_bias[l], ssd_a_log[l], ssd_d[l],
                           ssd_norm_w[l])
        y_rwkv, v_first = _rwkv_mixer(proj, misc, v_first, mu, rwkv_w0[l], rwkv_w2[l], rwkv_a0[l], rwkv_a2[l],
                                      rwkv_g2[l], rwkv_k_k[l], rwkv_k_a[l], rwkv_r_k[l], rwkv_ln_w[l], rwkv_ln_b[l],
                                      v0, v2)
        y_mla = _mla_mixer(proj, misc, cos, sin, mla_q_norm_w[l], mla_w_q_b[l], mla_kv_norm_w[l], mla_w_kv_b[l])
        x1, x1b = _merge_out(proj2, y_ssd.reshape(t, d), y_rwkv.reshape(t, d), y_mla.reshape(t, d), xf, w_out[l],
                             ln1_w[l], ln1_b[l])
        if l % 2 == 0:
            xf, xb = _ffn_dense(x1b, x1, ffn_w1[l // 2], ffn_w3[l // 2], ffn_w2[l // 2], ln2_w[l], ln2_b[l])
        else:
            xf = _ffn_moe(x1, moe_router[l // 2], moe_w1[l // 2], moe_w3[l // 2], moe_w2[l // 2], ln2_w[l], ln2_b[l])
            xb = xf.astype(BF16)
    return xf.reshape(bn, sn, d)
```

```python
import functools
import math

import jax
import jax.numpy as jnp
import numpy as np
from jax import lax
from jax.experimental import pallas as pl
from jax.experimental.pallas import tpu as pltpu

F32 = jnp.float32
BF16 = jnp.bfloat16

D_MODEL = 2048
DEPTH = 2
ALPHA = (2 * DEPTH) ** 0.25
LN_EPS = 1e-5
RMS_EPS = 1e-6
SSD_HEADS, SSD_HEAD_DIM, SSD_GROUPS, SSD_STATE, SSD_CONV = 32, 64, 4, 128, 4
SSD_BC = SSD_GROUPS * SSD_STATE
RWKV_HEADS, RWKV_HEAD = 32, 64
RWKV_GN_EPS = 64e-5
LORA_W, LORA_A, LORA_G, LORA_V = 96, 96, 256, 64
MLA_HEADS, MLA_NOPE, MLA_ROPE, MLA_V, MLA_RANK = 16, 128, 64, 128, 512
MLA_QK = MLA_NOPE + MLA_ROPE
ROPE_THETA = 10000.0
D_FF = 5632
N_EXPERTS = 8
TOP_K = 2

LANES = 128
SUBLANES = 8
VMEM_LIMIT = 56 * 1024 * 1024

W2K = 2048
COL_GATE = 0
COL_Z = 3 * W2K
COL_R = 4 * W2K
COL_XS = 7 * W2K
COL_BC = 8 * W2K
COL_QLAT = COL_BC + 1024
COL_KVLAT = COL_QLAT + 512
COL_MISC = COL_KVLAT + 512
MISC_W, MISC_A, MISC_G, MISC_V, MISC_DT, MISC_KPE = 0, 128, 256, 512, 640, 768
N_PROJ = COL_MISC + 1024

SSD_Q = 128
RW_T = 64
RW_BLK = 128


def _cparams(sem, vmem=VMEM_LIMIT):
    return pltpu.CompilerParams(dimension_semantics=sem, vmem_limit_bytes=vmem)


def _dot(a, b):
    return jnp.dot(a, b, preferred_element_type=F32)


def _dot_nt(a, b):
    return lax.dot_general(a, b, (((1,), (1,)), ((), ())), preferred_element_type=F32)


def _split3(a):
    hi = a.astype(BF16)
    r1 = a - hi.astype(F32)
    mid = r1.astype(BF16)
    lo = (r1 - mid.astype(F32)).astype(BF16)
    return hi, mid, lo


def _dot_exact_rhs(a_bf, b):
    hi, mid, lo = _split3(b)
    return _dot(a_bf, hi) + _dot(a_bf, mid) + _dot(a_bf, lo)


def _dot_exact_lhs(a, b_bf):
    hi, mid, lo = _split3(a)
    return _dot(hi, b_bf) + _dot(mid, b_bf) + _dot(lo, b_bf)


def _sigmoid(x):
    return 1.0 / (1.0 + jnp.exp(-x))


def _silu(x):
    return x * _sigmoid(x)


def _softplus(x):
    return jnp.maximum(x, 0.0) + jnp.log(1.0 + jnp.exp(-jnp.abs(x)))


def _layernorm(x, w, b):
    mu = jnp.mean(x, axis=-1, keepdims=True)
    xc = x - mu
    var = jnp.mean(xc * xc, axis=-1, keepdims=True)
    return xc * lax.rsqrt(var + LN_EPS) * w + b


def _mm_kernel(x_ref, w_ref, o_ref):
    o_ref[...] = _dot(x_ref[...], w_ref[...]).astype(o_ref.dtype)


def _matmul(x, w, out_dtype, tm, tn):
    m, k = x.shape
    n = w.shape[1]
    return pl.pallas_call(
        _mm_kernel,
        grid=(m // tm, n // tn),
        in_specs=[pl.BlockSpec((tm, k), lambda i, j: (i, 0)),
                  pl.BlockSpec((k, tn), lambda i, j: (0, j))],
        out_specs=pl.BlockSpec((tm, tn), lambda i, j: (i, j)),
        out_shape=jax.ShapeDtypeStruct((m, n), out_dtype),
        compiler_params=_cparams(("parallel", "arbitrary")),
        name="in_proj",
    )(x, w)


def _rope_kernel(pos_ref, freq_ref, cos_ref, sin_ref):
    ang = pos_ref[...] * freq_ref[...]
    valid = lax.broadcasted_iota(jnp.int32, ang.shape, 1) < MLA_ROPE
    cos_ref[...] = jnp.where(valid, jnp.cos(ang), 0.0)
    sin_ref[...] = jnp.where(valid, jnp.sin(ang), 0.0)


def _rope_tables(positions):
    t = positions.size
    tm = min(t, 1024)
    pos = positions.reshape(t, 1).astype(F32)
    inv_freq = ROPE_THETA ** (-jnp.arange(0, MLA_ROPE, 2, dtype=F32) / MLA_ROPE)
    freq = jnp.concatenate([inv_freq, inv_freq, jnp.zeros((LANES - MLA_ROPE,), F32)]).reshape(1, LANES)
    return pl.pallas_call(
        _rope_kernel,
        grid=(t // tm,),
        in_specs=[pl.BlockSpec((tm, 1), lambda i: (i, 0)),
                  pl.BlockSpec((1, LANES), lambda i: (0, 0))],
        out_specs=[pl.BlockSpec((tm, LANES), lambda i: (i, 0))] * 2,
        out_shape=[jax.ShapeDtypeStruct((t, LANES), F32)] * 2,
        compiler_params=_cparams(("parallel",)),
        name="rope_tables",
    )(pos, freq)


def _ssd_kernel(z_ref, xs_ref, bc_ref, misc_ref, cwx_ref, cwb_ref, cbx_ref, cbb_ref, dtb_ref, alog_ref,
                d_ref, nw_ref, e_ref, y_ref, state_ref, bufx_ref, bufb_ref):
    q = SSD_Q
    hp = SSD_HEADS // SSD_GROUPS * SSD_HEAD_DIM

    @pl.when(pl.program_id(1) == 0)
    def _():
        state_ref[...] = jnp.zeros_like(state_ref)
        bufx_ref[0:SUBLANES, :] = jnp.zeros((SUBLANES, bufx_ref.shape[1]), F32)
        bufb_ref[0:SUBLANES, :] = jnp.zeros((SUBLANES, bufb_ref.shape[1]), F32)

    bufx_ref[SUBLANES:SUBLANES + q, :] = xs_ref[...].astype(F32)
    bufb_ref[SUBLANES:SUBLANES + q, :] = bc_ref[...].astype(F32)

    def conv(buf_ref, w_ref, b_ref):
        acc = b_ref[...] + w_ref[SSD_CONV - 1:SSD_CONV, :] * buf_ref[SUBLANES:SUBLANES + q, :]
        for k in range(SSD_CONV - 1):
            off = SUBLANES - (SSD_CONV - 1) + k
            acc = acc + w_ref[k:k + 1, :] * buf_ref[off:off + q, :]
        return _silu(acc)

    xs = conv(bufx_ref, cwx_ref, cbx_ref)
    bc = conv(bufb_ref, cwb_ref, cbb_ref)
    bufx_ref[0:SUBLANES, :] = bufx_ref[q:q + SUBLANES, :]
    bufb_ref[0:SUBLANES, :] = bufb_ref[q:q + SUBLANES, :]

    dt = _softplus(misc_ref[:, MISC_DT:MISC_DT + LANES] + dtb_ref[...])
    da = dt * (-jnp.exp(alog_ref[...]))
    row = lax.broadcasted_iota(jnp.int32, (q, q), 0)
    col = lax.broadcasted_iota(jnp.int32, (q, q), 1)
    causal = row >= col
    tri = jnp.where(causal, 1.0, 0.0).astype(BF16)
    cum = _dot_exact_rhs(tri, da)
    cum_t = cum.T
    ecum = jnp.exp(cum)
    toend = jnp.exp(cum[q - 1:q, :] - cum)
    e_mat = e_ref[...]
    dt_e = _dot_exact_lhs(dt, e_mat)
    ecum_e = _dot_exact_lhs(ecum, e_mat)
    toend_e = _dot_exact_lhs(toend, e_mat)

    xdt = xs * dt_e
    xdt_b = xdt.astype(BF16)
    xw_b = (xdt * toend_e).astype(BF16)
    lane = lax.broadcasted_iota(jnp.int32, (q, LANES), 1)
    lo_half = lane < SSD_HEAD_DIM

    y_groups = []
    for g in range(SSD_GROUPS):
        b_g = bc[:, g * SSD_STATE:(g + 1) * SSD_STATE]
        c_g = bc[:, SSD_BC + g * SSD_STATE:SSD_BC + (g + 1) * SSD_STATE]
        b_gb = b_g.astype(BF16)
        c_gb = c_g.astype(BF16)
        cb = _dot_nt(c_gb, b_gb)
        st = state_ref[:, g * hp:(g + 1) * hp]
        y_inter = _dot(c_gb, st.astype(BF16)) * ecum_e[:, g * hp:(g + 1) * hp]
        parts = []
        for pr in range(hp // LANES):
            ms = []
            for e in range(2):
                h = g * (SSD_HEADS // SSD_GROUPS) + pr * 2 + e
                ci = jnp.broadcast_to(cum[:, h:h + 1], (q, q))
                cj = jnp.broadcast_to(cum_t[h:h + 1, :], (q, q))
                dec = jnp.exp(jnp.where(causal, ci - cj, -jnp.inf))
                ms.append((cb * dec).astype(BF16))
            lo = g * hp + pr * LANES
            xp = xdt_b[:, lo:lo + LANES]
            zero = jnp.zeros_like(xp)
            rhs = jnp.concatenate([jnp.where(lo_half, xp, zero), jnp.where(lo_half, zero, xp)], axis=0)
            parts.append(_dot(jnp.concatenate(ms, axis=1), rhs))
        y_groups.append(jnp.concatenate(parts, axis=1) + y_inter)
        upd = _dot(b_g.T.astype(BF16), xw_b[:, g * hp:(g + 1) * hp])
        state_ref[:, g * hp:(g + 1) * hp] = st * ecum_e[q - 1:q, g * hp:(g + 1) * hp] + upd

    y = jnp.concatenate(y_groups, axis=1) + d_ref[...] * xs
    y = y * _silu(z_ref[...].astype(F32))
    outs = []
    for g in range(SSD_GROUPS):
        yg = y[:, g * hp:(g + 1) * hp]
        outs.append(yg * lax.rsqrt(jnp.mean(yg * yg, axis=-1, keepdims=True) + RMS_EPS))
    y_ref[...] = (jnp.concatenate(outs, axis=1) * nw_ref[...]).astype(y_ref.dtype)


def _ssd_mixer(proj, misc, conv_w, conv_b, dt_bias, a_log, d_skip, norm_w):
    bn, sn, _ = proj.shape
    w = SSD_HEADS * SSD_HEAD_DIM
    pad = lambda v: jnp.pad(v, (0, LANES - v.shape[0])).reshape(1, LANES)
    a_log_p = jnp.pad(a_log, (0, LANES - SSD_HEADS), constant_values=-jnp.inf).reshape(1, LANES)
    expand = jnp.pad(jnp.repeat(jnp.eye(SSD_HEADS, dtype=BF16), SSD_HEAD_DIM, axis=1),
                     ((0, LANES - SSD_HEADS), (0, 0)))
    row = lambda v: v.reshape(1, -1)
    const = lambda shape: pl.BlockSpec(shape, lambda b, c: (0, 0))
    blk = lambda width, idx: pl.BlockSpec((None, SSD_Q, width), lambda b, c: (b, c, idx))
    return pl.pallas_call(
        _ssd_kernel,
        grid=(bn, sn // SSD_Q),
        in_specs=[blk(w, COL_Z // w), blk(w, COL_XS // w), blk(1024, COL_BC // 1024), blk(1024, 0),
                  const((SSD_CONV, w)), const((SSD_CONV, 2 * SSD_BC)), const((1, w)), const((1, 2 * SSD_BC)),
                  const((1, LANES)), const((1, LANES)), const((1, w)), const((1, w)), const((LANES, w))],
        out_specs=pl.BlockSpec((None, SSD_Q, w), lambda b, c: (b, c, 0)),
        out_shape=jax.ShapeDtypeStruct((bn, sn, w), BF16),
        scratch_shapes=[pltpu.VMEM((SSD_STATE, w), F32),
                        pltpu.VMEM((SSD_Q + SUBLANES, w), F32),
                        pltpu.VMEM((SSD_Q + SUBLANES, 2 * SSD_BC), F32)],
        compiler_params=_cparams(("parallel", "arbitrary")),
        name="ssd_mixer",
    )(proj, proj, proj, misc, conv_w[:, :w], conv_w[:, w:], row(conv_b[:w]), row(conv_b[w:]),
      pad(dt_bias), a_log_p, row(jnp.repeat(d_skip, SSD_HEAD_DIM)), row(norm_w), expand)


def _dot_hi(a, b):
    return jnp.dot(a, b, precision=lax.Precision.HIGHEST, preferred_element_type=F32)


def _split2(a):
    hi = a.astype(BF16)
    return hi, (a - hi.astype(F32)).astype(BF16)


def _dot_3pass(a, b_hi, b_lo):
    a_hi, a_lo = _split2(a)
    return _dot(a_hi, b_hi) + (_dot(a_hi, b_lo) + _dot(a_lo, b_hi))


def _head_sums(x, ones_blk):
    outs = []
    for s in range(x.shape[1] // LANES):
        outs.append(_dot(x[:, s * LANES:(s + 1) * LANES].astype(BF16), ones_blk))
    return jnp.concatenate(outs, axis=1)


def _stack_heads(x, lo_half):
    zero = jnp.zeros_like(x)
    return jnp.concatenate([jnp.where(lo_half, x, zero), jnp.where(lo_half, zero, x)], axis=0)


RW_PAIRS = RWKV_HEADS // 2
RW_GROUP = 16


def _rwkv_kernel(has_vres, *refs):
    if has_vres:
        (r_ref, k_ref, v_ref, misc_ref, vfirst_ref, mur_ref, muk_ref, muv_ref, mum_ref, w0_ref, w2h_ref, w2l_ref,
         a0_ref, a2_ref, g2_ref, kk_ref, ka_ref, rk_ref, lnw_ref, lnb_ref, v0_ref, v2_ref,
         y_ref, state_ref, carry_ref, carrym_ref, st_ref, yp_ref, pt_ref) = refs
    else:
        (r_ref, k_ref, v_ref, misc_ref, mur_ref, muk_ref, muv_ref, mum_ref, w0_ref, w2h_ref, w2l_ref, a0_ref,
         a2_ref, g2_ref, kk_ref, ka_ref, rk_ref, lnw_ref, lnb_ref,
         y_ref, vout_ref, state_ref, carry_ref, carrym_ref, st_ref, yp_ref, pt_ref) = refs
    tb, t = RW_BLK, RW_T
    w = RWKV_HEADS * RWKV_HEAD

    @pl.when(pl.program_id(1) == 0)
    def _():
        state_ref[...] = jnp.zeros_like(state_ref)
        carry_ref[...] = jnp.zeros_like(carry_ref)
        carrym_ref[...] = jnp.zeros_like(carrym_ref)

    first_row = lax.broadcasted_iota(jnp.int32, (tb, 1), 0) == 0

    def shift(p, carry_row, mu):
        prev = jnp.where(first_row, carry_row, pltpu.roll(p, 1, 0))
        return p + (prev - p) * mu

    rp, kp, vp, mp = r_ref[...].astype(F32), k_ref[...].astype(F32), v_ref[...].astype(F32), misc_ref[...]
    r = shift(rp, carry_ref[0:1, :], mur_ref[...])
    k = shift(kp, carry_ref[1:2, :], muk_ref[...])
    v = shift(vp, carry_ref[2:3, :], muv_ref[...])
    m = shift(mp, carrym_ref[0:1, :], mum_ref[...])
    carry_ref[0:1, :] = rp[tb - 1:tb, :]
    carry_ref[1:2, :] = kp[tb - 1:tb, :]
    carry_ref[2:3, :] = vp[tb - 1:tb, :]
    carrym_ref[0:1, :] = mp[tb - 1:tb, :]

    w_lo = m[:, MISC_W:MISC_W + LANES]
    a_lo = m[:, MISC_A:MISC_A + LANES]
    g_lo = m[:, MISC_G:MISC_G + LORA_G]
    log_w = -_softplus(-(w0_ref[...] + _dot_3pass(jnp.tanh(w_lo), w2h_ref[...], w2l_ref[...]))) - 0.5
    lw = -jnp.exp(log_w)
    a = _sigmoid(a0_ref[...] + _dot(a_lo.astype(BF16), a2_ref[...]))
    g = _dot(_sigmoid(g_lo).astype(BF16), g2_ref[...])
    if has_vres:
        v_lo = m[:, MISC_V:MISC_V + LANES]
        v = v + (vfirst_ref[...] - v) * _sigmoid(v0_ref[...] + _dot(v_lo.astype(BF16), v2_ref[...]))
    else:
        vout_ref[...] = v

    lane = lax.broadcasted_iota(jnp.int32, (LANES, LANES), 1)
    rowi = lax.broadcasted_iota(jnp.int32, (LANES, LANES), 0)
    ones_blk = jnp.where((lane // RWKV_HEAD) == (rowi // RWKV_HEAD), 1.0, 0.0).astype(BF16)

    kk = k * kk_ref[...]
    kk = kk / jnp.maximum(jnp.sqrt(_head_sums(kk * kk, ones_blk)), 1e-12)
    k = k * (1.0 + (a - 1.0) * ka_ref[...])
    b = kk * a

    ti = lax.broadcasted_iota(jnp.int32, (t, t), 0)
    tj = lax.broadcasted_iota(jnp.int32, (t, t), 1)
    tri = jnp.where(ti >= tj, 1.0, 0.0).astype(BF16)

    for c in range(tb // t):
        sl = slice(c * t, (c + 1) * t)
        lw_c = lw[sl]
        cl = _dot_exact_rhs(tri, lw_c)
        cl_end = cl[t - 1:t, :]
        e_neg = jnp.exp(-cl)
        e_end = jnp.exp(cl_end - cl)
        ops = (kk[sl] * jnp.exp(cl - lw_c), r[sl] * jnp.exp(cl), k[sl] * e_neg, b[sl] * e_neg,
               v[sl], k[sl] * e_end, b[sl] * e_end)
        for pi in range(RW_PAIRS):
            ls = slice(pi * LANES, (pi + 1) * LANES)
            for oi, op in enumerate(ops):
                st_ref[c, oi, pi] = op[:, ls]
            pt_ref[c, pi] = jnp.broadcast_to(jnp.exp(cl_end[:, ls]), (SUBLANES, LANES))

    lo_half = lax.broadcasted_iota(jnp.int32, (t, LANES), 1) < RWKV_HEAD
    bi = lax.broadcasted_iota(jnp.int32, (2 * t, 2 * t), 0) % t
    bj = lax.broadcasted_iota(jnp.int32, (2 * t, 2 * t), 1) % t
    strict = bi > bj
    incl = bi >= bj

    def chunk_group(c, pis, hts):
        h2 = 2 * t
        stk = [[_stack_heads(st_ref[c, oi, pi], lo_half) for oi in range(7)] for pi in pis]
        lhs2 = [jnp.concatenate([s[0], s[1]], axis=0).astype(BF16) for s in stk]
        rhs2 = [jnp.concatenate([s[2], s[3]], axis=0).astype(BF16) for s in stk]
        amat = [_dot_nt(a, b) for a, b in zip(lhs2, rhs2)]
        sh = [_dot_nt(a, h.astype(BF16)) for a, h in zip(lhs2, hts)]
        vsb = [s[4].astype(BF16) for s in stk]
        x = [s_[0:h2] + _dot(jnp.where(strict, am[0:h2, 0:h2], 0.0).astype(BF16), v_)
             for s_, am, v_ in zip(sh, amat, vsb)]
        pw = [jnp.where(strict, am[0:h2, h2:], 0.0) for am in amat]
        n, sign = 1, -1.0
        while n < t:
            if 2 * n < t:
                res = [_dot(p_.astype(BF16), jnp.concatenate([p_, x_], axis=1).astype(BF16)) for p_, x_ in zip(pw, x)]
                x = [x_ + sign * r_[:, h2:] for x_, r_ in zip(x, res)]
                pw = [r_[:, 0:h2] for r_ in res]
            else:
                x = [x_ + sign * _dot(p_.astype(BF16), x_.astype(BF16)) for p_, x_ in zip(pw, x)]
            n, sign = 2 * n, 1.0
        new_hts = []
        for i, pi in enumerate(pis):
            am, s = amat[i], stk[i]
            a_r = jnp.concatenate([jnp.where(incl, am[h2:, 0:h2], 0.0), jnp.where(incl, -am[h2:, h2:], 0.0)], axis=1)
            ys = sh[i][h2:] + _dot(a_r.astype(BF16), jnp.concatenate([s[4], x[i]], axis=0).astype(BF16))
            yp_ref[pi, c * t:(c + 1) * t, :] = ys[0:t] + ys[t:]
            lhs3 = jnp.concatenate([s[4].T, -(x[i].T)], axis=1).astype(BF16)
            rhs3 = jnp.concatenate([s[5], s[6]], axis=0).astype(BF16)
            new_hts.append(hts[i] * pt_ref[c, pi][0:1, :] + _dot(lhs3, rhs3))
        return new_hts

    for g0 in range(0, RW_PAIRS, RW_GROUP):
        pis = list(range(g0, g0 + RW_GROUP))
        hts = [state_ref[pi] for pi in pis]
        for c in range(tb // t):
            hts = chunk_group(c, pis, hts)
        for pi, ht in zip(pis, hts):
            state_ref[pi] = ht

    y = jnp.concatenate([yp_ref[pi] for pi in range(RW_PAIRS)], axis=1)
    inv_n = 1.0 / RWKV_HEAD
    mu = _head_sums(y, ones_blk) * inv_n
    yc = y - mu
    var = _head_sums(yc * yc, ones_blk) * inv_n
    y = yc * lax.rsqrt(var + RWKV_GN_EPS) * lnw_ref[...] + lnb_ref[...]
    bonus = _head_sums(r * k * rk_ref[...], ones_blk) * v
    y_ref[...] = ((y + bonus) * g).astype(y_ref.dtype)


def _rwkv_mixer(proj, misc, v_first, mu, w0, w2, a0, a2, g2, k_k, k_a, r_k, ln_w, ln_b, v0, v2):
    bn, sn, _ = proj.shape
    w = RWKV_HEADS * RWKV_HEAD
    has_vres = v_first is not None
    row = lambda x: x.reshape(1, -1)
    padrows = lambda x: jnp.pad(x, ((0, LANES - x.shape[0]), (0, 0)))
    padl = lambda x, n: jnp.pad(x, (0, n - x.shape[0]))
    mu_misc = [padl(mu[3 * w:3 * w + LORA_W], LANES), padl(mu[3 * w + LORA_W:3 * w + LORA_W + LORA_A], LANES),
               mu[3 * w + LORA_W + LORA_A:3 * w + LORA_W + LORA_A + LORA_G]]
    if has_vres:
        mu_misc.append(padl(mu[3 * w + LORA_W + LORA_A + LORA_G:], LANES))
    mu_m = padl(jnp.concatenate(mu_misc), 1024)
    const = lambda shape: pl.BlockSpec(shape, lambda b, c: (0,) * len(shape))
    blk = lambda width, idx: pl.BlockSpec((None, RW_BLK, width), lambda b, c: (b, c, idx))
    seq = pl.BlockSpec((None, RW_BLK, w), lambda b, c: (b, c, 0))
    in_specs = [blk(w, COL_R // w), blk(w, COL_R // w + 1), blk(w, COL_R // w + 2), blk(1024, 0)]
    args = [proj, proj, proj, misc]
    if has_vres:
        in_specs.append(seq)
        args.append(v_first)
    in_specs += [const((1, w))] * 3 + [const((1, 1024)), const((1, w)), const((LANES, w)), const((LANES, w)),
                                       const((1, w)), const((LANES, w)), const((LORA_G, w))] + [const((1, w))] * 5
    w2p = padrows(w2)
    w2_hi = w2p.astype(BF16)
    w2_lo = (w2p - w2_hi.astype(F32)).astype(BF16)
    args += [row(mu[:w]), row(mu[w:2 * w]), row(mu[2 * w:3 * w]), row(mu_m), row(w0), w2_hi, w2_lo, row(a0),
             padrows(a2).astype(BF16), g2.astype(BF16), row(k_k), row(k_a), row(r_k), row(ln_w), row(ln_b)]
    if has_vres:
        in_specs += [const((1, w)), const((LANES, w))]
        args += [row(v0), padrows(v2).astype(BF16)]
    out_shape = [jax.ShapeDtypeStruct((bn, sn, w), BF16)]
    out_specs = [seq]
    if not has_vres:
        out_shape.append(jax.ShapeDtypeStruct((bn, sn, w), F32))
        out_specs.append(seq)
    nc = RW_BLK // RW_T
    outs = pl.pallas_call(
        functools.partial(_rwkv_kernel, has_vres),
        grid=(bn, sn // RW_BLK),
        in_specs=in_specs,
        out_specs=out_specs,
        out_shape=out_shape,
        scratch_shapes=[pltpu.VMEM((RW_PAIRS, LANES, LANES), F32),
                        pltpu.VMEM((SUBLANES, w), F32),
                        pltpu.VMEM((SUBLANES, 1024), F32),
                        pltpu.VMEM((nc, 7, RW_PAIRS, RW_T, LANES), F32),
                        pltpu.VMEM((RW_PAIRS, RW_BLK, LANES), F32),
                        pltpu.VMEM((nc, RW_PAIRS, SUBLANES, LANES), F32)],
        compiler_params=_cparams(("parallel", "arbitrary")),
        name="rwkv7_mixer",
    )(*args)
    if has_vres:
        return outs[0], v_first
    return outs[0], outs[1]


MLA_TM = 256
ATT_TQ = 2048
ATT_TK = 2048
ATT_SUB = 512
HEAD_Q = 2 * LANES


def _rope_half(x2, cos, sin):
    return x2 * cos + pltpu.roll(x2, MLA_ROPE, 1) * sin


def _mla_prep_kernel(qlat_ref, kvlat_ref, misc_ref, cos_ref, sin_ref, qnw_ref, kvnw_ref, wq_ref, wkt_ref, wv_ref,
                     q_ref, kt_ref, v_ref, kpet_ref):
    def rms(x, w):
        x = x.astype(F32)
        return (x * lax.rsqrt(jnp.mean(x * x, axis=-1, keepdims=True) + RMS_EPS) * w).astype(BF16)

    cos, sin = cos_ref[...], sin_ref[...]
    scale = MLA_QK ** -0.5
    q = _dot(rms(qlat_ref[...], qnw_ref[...]), wq_ref[...])
    for h in range(MLA_HEADS):
        lo = h * HEAD_Q
        q_ref[:, lo:lo + LANES] = (q[:, lo:lo + LANES] * scale).astype(BF16)
        q_ref[:, lo + LANES:lo + HEAD_Q] = (_rope_half(q[:, lo + LANES:lo + HEAD_Q], cos, sin) * scale).astype(BF16)
    kvn = rms(kvlat_ref[...], kvnw_ref[...])
    kt_ref[...] = _dot_nt(wkt_ref[...], kvn).astype(BF16)
    v_ref[...] = _dot(kvn, wv_ref[...]).astype(BF16)
    kpet_ref[...] = _rope_half(misc_ref[:, MISC_KPE:MISC_KPE + LANES], cos, sin).T.astype(BF16)


def _flash_kernel(tq, tk, sub, qi_ref, ki_ref, q_ref, kt_ref, kpet_ref, v_ref, o_ref, m_ref, acc_ref):
    step = pl.program_id(2)
    qi, ki = qi_ref[step], ki_ref[step]
    ratio = tq // tk

    @pl.when(ki == 0)
    def _():
        m_ref[...] = jnp.full_like(m_ref, -jnp.inf)
        acc_ref[...] = jnp.zeros_like(acc_ref)

    def update(diag):
        kcat_t = jnp.concatenate([kt_ref[...], kpet_ref[...]], axis=0)
        vcat = jnp.concatenate([v_ref[...], jnp.ones((tk, LANES), BF16)], axis=1)
        plan = []
        for r in range(tq // sub):
            row_lo, row_hi = r * sub, (r + 1) * sub - 1
            ncols = tk
            masked = False
            if diag is not None:
                col_lo = diag * tk
                if row_hi < col_lo:
                    continue
                ncols = min(tk, -(-(row_hi - col_lo + 1) // HEAD_Q) * HEAD_Q)
                masked = row_lo < col_lo + ncols - 1
            plan.append((row_lo, ncols, masked))

        def scores(row_lo, ncols, masked):
            s = _dot(q_ref[row_lo:row_lo + sub, :], kcat_t[:, :ncols])
            if masked:
                ri = row_lo + lax.broadcasted_iota(jnp.int32, s.shape, 0)
                ci = diag * tk + lax.broadcasted_iota(jnp.int32, s.shape, 1)
                s = jnp.where(ci <= ri, s, -jnp.inf)
            return s

        def absorb(row_lo, ncols, s):
            rows = slice(row_lo, row_lo + sub)
            tiles = [s[:, j * LANES:(j + 1) * LANES] for j in range(ncols // LANES)]
            fold = tiles[0]
            for tl in tiles[1:]:
                fold = jnp.maximum(fold, tl)
            m_prev = m_ref[rows, :]
            m_new = jnp.maximum(m_prev, jnp.max(fold, axis=-1, keepdims=True))
            p = jnp.concatenate([jnp.exp(tl - m_new) for tl in tiles], axis=1).astype(BF16)
            alpha = jnp.exp(m_prev - m_new)
            acc_ref[rows, :] = (acc_ref[rows, :] * jnp.concatenate([alpha, alpha], axis=1)
                                + _dot(p, vcat[:ncols]))
            m_ref[rows, :] = m_new

        s_next = scores(*plan[0])
        for idx, (row_lo, ncols, _) in enumerate(plan):
            s_cur = s_next
            if idx + 1 < len(plan):
                s_next = scores(*plan[idx + 1])
            absorb(row_lo, ncols, s_cur)

    @pl.when(ki < qi * ratio)
    def _():
        update(None)

    for d in range(ratio):
        @pl.when(ki == qi * ratio + d)
        def _(d=d):
            update(d)

    @pl.when(ki == (qi + 1) * ratio - 1)
    def _():
        o_ref[...] = (acc_ref[:, 0:MLA_V] / acc_ref[:, MLA_V:]).astype(o_ref.dtype)


def _mla_mixer(proj, misc, cos, sin, q_norm_w, w_q_b, kv_norm_w, w_kv_b):
    bn, sn, _ = proj.shape
    t = bn * sn
    proj2 = proj.reshape(t, proj.shape[-1])
    misc2 = misc.reshape(t, misc.shape[-1])
    wq = w_q_b.reshape(MLA_RANK, MLA_HEADS, MLA_QK)
    pe = wq[..., MLA_NOPE:]
    rot = jnp.concatenate([-pe[..., MLA_ROPE // 2:], pe[..., :MLA_ROPE // 2]], axis=-1)
    wq = jnp.concatenate([wq, rot], axis=-1).reshape(MLA_RANK, MLA_HEADS * HEAD_Q).astype(BF16)
    wkv = w_kv_b.reshape(MLA_RANK, MLA_HEADS, MLA_NOPE + MLA_V)
    wkt = wkv[..., :MLA_NOPE].reshape(MLA_RANK, -1).T.astype(BF16)
    wv = wkv[..., MLA_NOPE:].reshape(MLA_RANK, -1).astype(BF16)
    nq, nk, nv = MLA_HEADS * HEAD_Q, MLA_HEADS * MLA_NOPE, MLA_HEADS * MLA_V
    tm = min(MLA_TM, sn)
    nsb = sn // tm
    rowblk = lambda width, idx: pl.BlockSpec((tm, width), lambda i: (i, idx))
    const = lambda shape: pl.BlockSpec(shape, lambda i: (0, 0))
    colblk = lambda rows: pl.BlockSpec((None, rows, tm), lambda i: (i // nsb, 0, i % nsb))
    q, kt, v, kpet = pl.pallas_call(
        _mla_prep_kernel,
        grid=(t // tm,),
        in_specs=[rowblk(MLA_RANK, COL_QLAT // MLA_RANK), rowblk(MLA_RANK, COL_KVLAT // MLA_RANK),
                  rowblk(1024, 0), rowblk(LANES, 0), rowblk(LANES, 0),
                  const((1, MLA_RANK)), const((1, MLA_RANK)), const((MLA_RANK, nq)), const((nk, MLA_RANK)),
                  const((MLA_RANK, nv))],
        out_specs=[rowblk(nq, 0), colblk(nk), rowblk(nv, 0), colblk(LANES)],
        out_shape=[jax.ShapeDtypeStruct((t, nq), BF16), jax.ShapeDtypeStruct((bn, nk, sn), BF16),
                   jax.ShapeDtypeStruct((t, nv), BF16), jax.ShapeDtypeStruct((bn, LANES, sn), BF16)],
        compiler_params=_cparams(("parallel",)),
        name="mla_prep",
    )(proj2, proj2, misc2, cos, sin, q_norm_w.reshape(1, -1), kv_norm_w.reshape(1, -1), wq, wkt, wv)

    tq, tk = min(ATT_TQ, sn), min(ATT_TK, sn)
    sub = min(ATT_SUB, tq)
    ratio = tq // tk
    pairs = [(a, b) for a in range(sn // tq) for b in range((a + 1) * ratio)]
    qi_arr = jnp.asarray([p[0] for p in pairs], jnp.int32)
    ki_arr = jnp.asarray([p[1] for p in pairs], jnp.int32)
    out = pl.pallas_call(
        functools.partial(_flash_kernel, tq, tk, sub),
        grid_spec=pltpu.PrefetchScalarGridSpec(
            num_scalar_prefetch=2,
            grid=(bn, MLA_HEADS, len(pairs)),
            in_specs=[pl.BlockSpec((None, tq, HEAD_Q), lambda b, h, s, qi, ki: (b, qi[s], h)),
                      pl.BlockSpec((None, MLA_NOPE, tk), lambda b, h, s, qi, ki: (b, h, ki[s])),
                      pl.BlockSpec((None, LANES, tk), lambda b, h, s, qi, ki: (b, 0, ki[s])),
                      pl.BlockSpec((None, tk, MLA_V), lambda b, h, s, qi, ki: (b, ki[s], h))],
            out_specs=pl.BlockSpec((None, tq, MLA_V), lambda b, h, s, qi, ki: (b, qi[s], h)),
            scratch_shapes=[pltpu.VMEM((tq, LANES), F32), pltpu.VMEM((tq, 2 * MLA_V), F32)]),
        out_shape=jax.ShapeDtypeStruct((bn, sn, MLA_HEADS * MLA_V), BF16),
        compiler_params=_cparams(("parallel", "parallel", "arbitrary")),
        name="mla_attention",
    )(qi_arr, ki_arr, q.reshape(bn, sn, nq), kt, kpet, v.reshape(bn, sn, nv))
    return out


MERGE_TM = 256


def _merge_kernel(g0_ref, g1_ref, g2_ref, ys_ref, yr_ref, ym_ref, x_ref, wo_ref, lnw_ref, lnb_ref, xo_ref, xb_ref):
    gate = lambda g_ref, y_ref: _sigmoid(g_ref[...].astype(F32)) * y_ref[...].astype(F32)
    merged = gate(g0_ref, ys_ref) + gate(g1_ref, yr_ref) + gate(g2_ref, ym_ref)
    h = ALPHA * x_ref[...] + _dot(merged.astype(BF16), wo_ref[...])
    y = _layernorm(h, lnw_ref[...], lnb_ref[...])
    xo_ref[...] = y
    xb_ref[...] = y.astype(BF16)


def _merge_out(proj2, y_ssd, y_rwkv, y_mla, x, w_out, ln_w, ln_b):
    t, d = x.shape
    tm = min(MERGE_TM, t)
    rowblk = lambda idx: pl.BlockSpec((tm, d), lambda i: (i, idx))
    const = lambda shape: pl.BlockSpec(shape, lambda i: (0, 0))
    return pl.pallas_call(
        _merge_kernel,
        grid=(t // tm,),
        in_specs=[rowblk(0), rowblk(1), rowblk(2), rowblk(0), rowblk(0), rowblk(0), rowblk(0),
                  const((d, d)), const((1, d)), const((1, d))],
        out_specs=[rowblk(0), rowblk(0)],
        out_shape=[jax.ShapeDtypeStruct((t, d), F32), jax.ShapeDtypeStruct((t, d), BF16)],
        compiler_params=_cparams(("parallel",)),
        name="merge_out_ln1",
    )(proj2, proj2, proj2, y_ssd, y_rwkv, y_mla, x, w_out.astype(BF16), ln_w.reshape(1, d), ln_b.reshape(1, d))


FFN_TM = 512
FFN_TF = 512


def _ffn_kernel(xb_ref, x_ref, w1_ref, w3_ref, w2_ref, lnw_ref, lnb_ref, o_ref, ob_ref, acc_ref):
    f = pl.program_id(1)

    @pl.when(f == 0)
    def _():
        acc_ref[...] = jnp.zeros_like(acc_ref)

    xb = xb_ref[...]
    h = _silu(_dot(xb, w1_ref[...])) * _dot(xb, w3_ref[...])
    acc_ref[...] += _dot(h.astype(BF16), w2_ref[...])

    @pl.when(f == pl.num_programs(1) - 1)
    def _():
        y = _layernorm(ALPHA * x_ref[...] + acc_ref[...], lnw_ref[...], lnb_ref[...])
        o_ref[...] = y
        ob_ref[...] = y.astype(BF16)


def _ffn_dense(xb, x, w1, w3, w2, ln_w, ln_b):
    t, d = x.shape
    ff = w1.shape[1]
    tm, tf = min(FFN_TM, t), FFN_TF
    return pl.pallas_call(
        _ffn_kernel,
        grid=(t // tm, ff // tf),
        in_specs=[pl.BlockSpec((tm, d), lambda i, f: (i, 0)), pl.BlockSpec((tm, d), lambda i, f: (i, 0)),
                  pl.BlockSpec((d, tf), lambda i, f: (0, f)), pl.BlockSpec((d, tf), lambda i, f: (0, f)),
                  pl.BlockSpec((tf, d), lambda i, f: (f, 0)),
                  pl.BlockSpec((1, d), lambda i, f: (0, 0)), pl.BlockSpec((1, d), lambda i, f: (0, 0))],
        out_specs=[pl.BlockSpec((tm, d), lambda i, f: (i, 0)), pl.BlockSpec((tm, d), lambda i, f: (i, 0))],
        out_shape=[jax.ShapeDtypeStruct((t, d), F32), jax.ShapeDtypeStruct((t, d), BF16)],
        scratch_shapes=[pltpu.VMEM((tm, d), F32)],
        compiler_params=_cparams(("parallel", "arbitrary")),
        name="ffn_dense_ln2",
    )(xb, x, w1.astype(BF16), w3.astype(BF16), w2.astype(BF16), ln_w.reshape(1, d), ln_b.reshape(1, d))


ROUTER_TM = 512
MOE_TM = 512
MOE_TF = 512
COMBINE_TM = 256


def _router_kernel(x_ref, wr_ref, idx_ref, wgt_ref):
    logits = _dot_hi(x_ref[...], wr_ref[...])
    lane = lax.broadcasted_iota(jnp.int32, logits.shape, 1)
    lg = jnp.where(lane < N_EXPERTS, logits, -jnp.inf)
    m1 = jnp.max(lg, axis=-1, keepdims=True)
    i1 = jnp.min(jnp.where(lg == m1, lane, LANES), axis=-1, keepdims=True)
    lg2 = jnp.where(lane == i1, -jnp.inf, lg)
    m2 = jnp.max(lg2, axis=-1, keepdims=True)
    i2 = jnp.min(jnp.where(lg2 == m2, lane, LANES), axis=-1, keepdims=True)
    e = jnp.exp(m2 - m1)
    idx_ref[...] = jnp.where(lane == 0, i1, jnp.where(lane == 1, i2, 0))
    wgt_ref[...] = jnp.where(lane == 0, 1.0 / (1.0 + e), jnp.where(lane == 1, e / (1.0 + e), 0.0))


def _moe_ffn_kernel(nf_static, be_ref, nused_ref, tok_ref, tok_next_ref, x_hbm, w1_ref, w3_ref, w2_ref, o_ref, xg_ref,
                    xb_ref, acc_ref, sem):
    i, f = pl.program_id(0), pl.program_id(1)
    ni, nf = pl.num_programs(0), pl.num_programs(1)
    rows = xg_ref.shape[1]
    per = -(-rows // nf_static)
    tail = rows - (nf_static - 1) * per
    slot = i % 2

    def row_copy(idx_ref, sl, r):
        return pltpu.make_async_copy(x_hbm.at[pl.ds(idx_ref[0, 0, r], 1), :], xg_ref.at[sl, pl.ds(r, 1), :],
                                     sem.at[sl])

    def wait_all(idx_ref, sl):
        for r in range(rows):
            row_copy(idx_ref, sl, r).wait()

    @pl.when(f == 0)
    def _():
        @pl.when(i == 0)
        def _():
            def body(r, c):
                row_copy(tok_ref, slot, r).start()
                return c
            lax.fori_loop(0, rows, body, 0)

        wait_all(tok_ref, slot)
        xb_ref[...] = xg_ref[slot].astype(BF16)
        acc_ref[...] = jnp.zeros_like(acc_ref)

    base = f * per

    def fetch_share():
        for j in range(tail):
            row_copy(tok_next_ref, 1 - slot, base + j).start()

    @pl.when(i < nused_ref[0])
    def _():
        fetch_share()
        xb = xb_ref[...]
        h = _silu(_dot(xb, w1_ref[...])) * _dot(xb, w3_ref[...])
        acc_ref[...] += _dot(h.astype(BF16), w2_ref[...])

    @pl.when(i >= nused_ref[0])
    def _():
        fetch_share()

    @pl.when(f < nf - 1)
    def _():
        for j in range(tail, per):
            row_copy(tok_next_ref, 1 - slot, base + j).start()

    @pl.when(f == nf - 1)
    def _():
        o_ref[...] = acc_ref[...]

        @pl.when(i == ni - 1)
        def _():
            wait_all(tok_next_ref, 1 - slot)


def _combine_kernel(dst_ref, dst_next_ref, y_hbm, wgt_ref, x_ref, lnw_ref, lnb_ref, o_ref, yg_ref, sem):
    i, n = pl.program_id(0), pl.num_programs(0)
    rows = x_ref.shape[0]
    slot = i % 2

    def row_copy(idx_ref, sl, r, k):
        return pltpu.make_async_copy(y_hbm.at[pl.ds(idx_ref[0, 0, TOP_K * r + k], 1), :],
                                     yg_ref.at[sl, k, pl.ds(r, 1), :], sem.at[sl])

    def for_all(idx_ref, sl, op):
        for r in range(rows):
            for k in range(TOP_K):
                op(row_copy(idx_ref, sl, r, k))

    @pl.when(i == 0)
    def _():
        for_all(dst_ref, slot, lambda c: c.start())

    for_all(dst_next_ref, 1 - slot, lambda c: c.start())
    for_all(dst_ref, slot, lambda c: c.wait())
    wgt = wgt_ref[...]
    f = wgt[:, 0:1] * yg_ref[slot, 0] + wgt[:, 1:2] * yg_ref[slot, 1]
    o_ref[...] = _layernorm(ALPHA * x_ref[...] + f, lnw_ref[...], lnb_ref[...])

    @pl.when(i == n - 1)
    def _():
        for_all(dst_next_ref, 1 - slot, lambda c: c.wait())


def _ffn_moe(x, router, w1, w3, w2, ln_w, ln_b):
    t, d = x.shape
    ne, _, ff = w1.shape
    tm = min(ROUTER_TM, t)
    rw = jnp.pad(router, ((0, 0), (0, LANES - ne)))
    idx, wgt = pl.pallas_call(
        _router_kernel,
        grid=(t // tm,),
        in_specs=[pl.BlockSpec((tm, d), lambda i: (i, 0)), pl.BlockSpec((d, LANES), lambda i: (0, 0))],
        out_specs=[pl.BlockSpec((tm, LANES), lambda i: (i, 0))] * 2,
        out_shape=[jax.ShapeDtypeStruct((t, LANES), jnp.int32), jax.ShapeDtypeStruct((t, LANES), F32)],
        compiler_params=_cparams(("parallel",)),
        name="moe_router",
    )(x, rw)

    blk = MOE_TM
    n_assign = t * TOP_K
    flat_e = idx[:, :TOP_K].reshape(-1)
    onehot = (flat_e[:, None] == jnp.arange(ne, dtype=jnp.int32)[None, :]).astype(jnp.int32)
    csum = jnp.cumsum(onehot, axis=0)
    rank = jnp.take_along_axis(csum, flat_e[:, None], axis=1)[:, 0] - 1
    counts = csum[-1]
    padded = (counts + blk - 1) // blk * blk
    pad_end = jnp.cumsum(padded)
    dest = (pad_end - padded)[flat_e] + rank
    n_blocks = -(-(n_assign + ne * (blk - 1)) // blk)
    n_rows = n_blocks * blk
    token_of_row = jnp.zeros((n_rows,), jnp.int32).at[dest].set(jnp.arange(n_assign, dtype=jnp.int32) // TOP_K)
    block_expert = jnp.minimum(jnp.searchsorted(pad_end, jnp.arange(n_blocks, dtype=jnp.int32) * blk, side='right'),
                               ne - 1).astype(jnp.int32)

    tf = MOE_TF
    n_used = (pad_end[-1:] // blk).astype(jnp.int32)
    wtile = lambda i, f, nu: jnp.where(i < nu[0], f, 0)
    yb = pl.pallas_call(
        functools.partial(_moe_ffn_kernel, ff // tf),
        grid_spec=pltpu.PrefetchScalarGridSpec(
            num_scalar_prefetch=2,
            grid=(n_blocks, ff // tf),
            in_specs=[pl.BlockSpec((1, 1, blk), lambda i, f, be, nu: (i, 0, 0), memory_space=pltpu.SMEM),
                      pl.BlockSpec((1, 1, blk), lambda i, f, be, nu: (jnp.minimum(i + 1, n_blocks - 1), 0, 0),
                                   memory_space=pltpu.SMEM),
                      pl.BlockSpec(memory_space=pl.ANY),
                      pl.BlockSpec((None, d, tf), lambda i, f, be, nu: (be[i], 0, wtile(i, f, nu))),
                      pl.BlockSpec((None, d, tf), lambda i, f, be, nu: (be[i], 0, wtile(i, f, nu))),
                      pl.BlockSpec((None, tf, d), lambda i, f, be, nu: (be[i], wtile(i, f, nu), 0))],
            out_specs=pl.BlockSpec((blk, d), lambda i, f, be, nu: (i, 0)),
            scratch_shapes=[pltpu.VMEM((2, blk, d), F32), pltpu.VMEM((blk, d), BF16), pltpu.VMEM((blk, d), F32),
                            pltpu.SemaphoreType.DMA((2,))]),
        out_shape=jax.ShapeDtypeStruct((n_rows, d), F32),
        compiler_params=_cparams(("arbitrary", "arbitrary")),
        name="moe_expert_ffn",
    )(block_expert, n_used, token_of_row.reshape(n_blocks, 1, blk), token_of_row.reshape(n_blocks, 1, blk), x,
      w1.astype(BF16), w3.astype(BF16), w2.astype(BF16))

    tc = min(COMBINE_TM, t)
    return pl.pallas_call(
        _combine_kernel,
        grid=(t // tc,),
        in_specs=[pl.BlockSpec((1, 1, TOP_K * tc), lambda i: (i, 0, 0), memory_space=pltpu.SMEM),
                  pl.BlockSpec((1, 1, TOP_K * tc), lambda i: (jnp.minimum(i + 1, t // tc - 1), 0, 0),
                               memory_space=pltpu.SMEM),
                  pl.BlockSpec(memory_space=pl.ANY),
                  pl.BlockSpec((tc, LANES), lambda i: (i, 0)), pl.BlockSpec((tc, d), lambda i: (i, 0)),
                  pl.BlockSpec((1, d), lambda i: (0, 0)), pl.BlockSpec((1, d), lambda i: (0, 0))],
        out_specs=pl.BlockSpec((tc, d), lambda i: (i, 0)),
        out_shape=jax.ShapeDtypeStruct((t, d), F32),
        scratch_shapes=[pltpu.VMEM((2, TOP_K, tc, d), F32), pltpu.SemaphoreType.DMA((2,))],
        compiler_params=_cparams(("arbitrary",)),
        name="moe_combine_ln2",
    )(dest.reshape(t // tc, 1, TOP_K * tc), dest.reshape(t // tc, 1, TOP_K * tc), yb, wgt, x,
      ln_w.reshape(1, d), ln_b.reshape(1, d))


def _pack_w_in(w, w_vres):
    d = w.shape[0]
    o = 0

    def take(n):
        nonlocal o
        s = w[:, o:o + n]
        o += n
        return s

    padc = lambda s, n: jnp.pad(s, ((0, 0), (0, n - s.shape[1])))
    gates = take(3 * W2K)
    z = take(W2K)
    xbc = take(W2K + 2 * SSD_BC)
    dt = take(SSD_HEADS)
    qlat = take(MLA_RANK)
    kvlat = take(MLA_RANK)
    kpe = take(MLA_ROPE)
    rkv = take(3 * W2K)
    w_lo, a_lo, g_lo = take(LORA_W), take(LORA_A), take(LORA_G)
    v_lo = jnp.zeros((d, LANES), w.dtype) if w_vres is None else padc(w_vres, LANES)
    kpe_rot = jnp.concatenate([-kpe[:, MLA_ROPE // 2:], kpe[:, :MLA_ROPE // 2]], axis=1)
    misc = jnp.concatenate([padc(w_lo, LANES), padc(a_lo, LANES), g_lo, v_lo, padc(dt, LANES), kpe, kpe_rot], axis=1)
    return jnp.concatenate([gates, z, rkv, xbc, qlat, kvlat, padc(misc, 1024)], axis=1).astype(BF16)


PROJ_TM = 2048
PROJ_TN = 1024


def kernel(x, positions, w_in, w_in_vres, w_out, ssd_conv_w, ssd_conv_b, ssd_dt_bias, ssd_a_log, ssd_d, ssd_norm_w, rwkv_mu, rwkv_mu_vres, rwkv_w0, rwkv_w2, rwkv_a0, rwkv_a2, rwkv_g2, rwkv_v0, rwkv_v2, rwkv_k_k, rwkv_k_a, rwkv_r_k, rwkv_ln_w, rwkv_ln_b, mla_q_norm_w, mla_w_q_b, mla_kv_norm_w, mla_w_kv_b, ln1_w, ln1_b, ln2_w, ln2_b, ffn_w1, ffn_w3, ffn_w2, moe_router, moe_w1, moe_w3, moe_w2):
    bn, sn, d = x.shape
    t = bn * sn
    cos, sin = _rope_tables(positions)
    xf = x.reshape(t, d)
    xb = xf.astype(BF16)
    v_first = None
    for l in range(DEPTH):
        if l == 0:
            wp, mu, v0, v2 = _pack_w_in(w_in[l], None), rwkv_mu[l], None, None
        else:
            wp = _pack_w_in(w_in[l], w_in_vres[l - 1])
            mu = jnp.concatenate([rwkv_mu[l], rwkv_mu_vres[l - 1]], axis=0)
            v0, v2 = rwkv_v0[l - 1], rwkv_v2[l - 1]
        proj2 = _matmul(xb, wp[:, :COL_MISC], BF16, min(PROJ_TM, t), PROJ_TN)
        misc = _matmul(xb, wp[:, COL_MISC:], F32, min(PROJ_TM, t), N_PROJ - COL_MISC).reshape(bn, sn, -1)
        proj = proj2.reshape(bn, sn, COL_MISC)
        y_ssd = _ssd_mixer(proj, misc, ssd_conv_w[l], ssd_conv_b[l], ssd_dt_bias[l], ssd_a_log[l], ssd_d[l],
                           ssd_norm_w[l])
        y_rwkv, v_first = _rwkv_mixer(proj, misc, v_first, mu, rwkv_w0[l], rwkv_w2[l], rwkv_a0[l], rwkv_a2[l],
                                      rwkv_g2[l], rwkv_k_k[l], rwkv_k_a[l], rwkv_r_k[l], rwkv_ln_w[l], rwkv_ln_b[l],
                                      v0, v2)
        y_mla = _mla_mixer(proj, misc, cos, sin, mla_q_norm_w[l], mla_w_q_b[l], mla_kv_norm_w[l], mla_w_kv_b[l])
        x1, x1b = _merge_out(proj2, y_ssd.reshape(t, d), y_rwkv.reshape(t, d), y_mla.reshape(t, d), xf, w_out[l],
                             ln1_w[l], ln1_b[l])
        if l % 2 == 0:
            xf, xb = _ffn_dense(x1b, x1, ffn_w1[l // 2], ffn_w3[l // 2], ffn_w2[l // 2], ln2_w[l], ln2_b[l])
        else:
            xf = _ffn_moe(x1, moe_router[l // 2], moe_w1[l // 2], moe_w3[l // 2], moe_w2[l // 2], ln2_w[l], ln2_b[l])
            xb = xf.astype(BF16)
    return xf.reshape(bn, sn, d)
```

```python
import functools
import math

import jax
import jax.numpy as jnp
import numpy as np
from jax import lax
from jax.experimental import pallas as pl
from jax.experimental.pallas import tpu as pltpu

F32 = jnp.float32
BF16 = jnp.bfloat16

D_MODEL = 2048
DEPTH = 2
ALPHA = (2 * DEPTH) ** 0.25
LN_EPS = 1e-5
RMS_EPS = 1e-6
SSD_HEADS, SSD_HEAD_DIM, SSD_GROUPS, SSD_STATE, SSD_CONV = 32, 64, 4, 128, 4
SSD_BC = SSD_GROUPS * SSD_STATE
RWKV_HEADS, RWKV_HEAD = 32, 64
RWKV_GN_EPS = 64e-5
LORA_W, LORA_A, LORA_G, LORA_V = 96, 96, 256, 64
MLA_HEADS, MLA_NOPE, MLA_ROPE, MLA_V, MLA_RANK = 16, 128, 64, 128, 512
MLA_QK = MLA_NOPE + MLA_ROPE
ROPE_THETA = 10000.0
D_FF = 5632
N_EXPERTS = 8
TOP_K = 2

LANES = 128
SUBLANES = 8
VMEM_LIMIT = 56 * 1024 * 1024

W2K = 2048
COL_GATE = 0
COL_Z = 3 * W2K
COL_R = 4 * W2K
COL_XS = 7 * W2K
COL_BC = 8 * W2K
COL_QLAT = COL_BC + 1024
COL_KVLAT = COL_QLAT + 512
COL_MISC = COL_KVLAT + 512
MISC_W, MISC_A, MISC_G, MISC_V, MISC_DT, MISC_KPE = 0, 128, 256, 512, 640, 768
N_PROJ = COL_MISC + 1024

SSD_Q = 128
RW_T = 64
RW_BLK = 128


def _cparams(sem, vmem=VMEM_LIMIT):
    return pltpu.CompilerParams(dimension_semantics=sem, vmem_limit_bytes=vmem)


def _dot(a, b):
    return jnp.dot(a, b, preferred_element_type=F32)


def _dot_nt(a, b):
    return lax.dot_general(a, b, (((1,), (1,)), ((), ())), preferred_element_type=F32)


def _split3(a):
    hi = a.astype(BF16)
    r1 = a - hi.astype(F32)
    mid = r1.astype(BF16)
    lo = (r1 - mid.astype(F32)).astype(BF16)
    return hi, mid, lo


def _dot_exact_rhs(a_bf, b):
    hi, mid, lo = _split3(b)
    return _dot(a_bf, hi) + _dot(a_bf, mid) + _dot(a_bf, lo)


def _dot_exact_lhs(a, b_bf):
    hi, mid, lo = _split3(a)
    return _dot(hi, b_bf) + _dot(mid, b_bf) + _dot(lo, b_bf)


def _sigmoid(x):
    return 1.0 / (1.0 + jnp.exp(-x))


def _silu(x):
    return x * _sigmoid(x)


def _softplus(x):
    return jnp.maximum(x, 0.0) + jnp.log(1.0 + jnp.exp(-jnp.abs(x)))


def _layernorm(x, w, b):
    mu = jnp.mean(x, axis=-1, keepdims=True)
    xc = x - mu
    var = jnp.mean(xc * xc, axis=-1, keepdims=True)
    return xc * lax.rsqrt(var + LN_EPS) * w + b


def _mm_kernel(x_ref, w_ref, o_ref):
    o_ref[...] = _dot(x_ref[...], w_ref[...]).astype(o_ref.dtype)


def _matmul(x, w, out_dtype, tm, tn):
    m, k = x.shape
    n = w.shape[1]
    return pl.pallas_call(
        _mm_kernel,
        grid=(m // tm, n // tn),
        in_specs=[pl.BlockSpec((tm, k), lambda i, j: (i, 0)),
                  pl.BlockSpec((k, tn), lambda i, j: (0, j))],
        out_specs=pl.BlockSpec((tm, tn), lambda i, j: (i, j)),
        out_shape=jax.ShapeDtypeStruct((m, n), out_dtype),
        compiler_params=_cparams(("parallel", "arbitrary")),
        name="in_proj",
    )(x, w)


def _rope_kernel(pos_ref, freq_ref, cos_ref, sin_ref):
    ang = pos_ref[...] * freq_ref[...]
    valid = lax.broadcasted_iota(jnp.int32, ang.shape, 1) < MLA_ROPE
    cos_ref[...] = jnp.where(valid, jnp.cos(ang), 0.0)
    sin_ref[...] = jnp.where(valid, jnp.sin(ang), 0.0)


def _rope_tables(positions):
    t = positions.size
    tm = min(t, 1024)
    pos = positions.reshape(t, 1).astype(F32)
    inv_freq = ROPE_THETA ** (-jnp.arange(0, MLA_ROPE, 2, dtype=F32) / MLA_ROPE)
    freq = jnp.concatenate([inv_freq, inv_freq, jnp.zeros((LANES - MLA_ROPE,), F32)]).reshape(1, LANES)
    return pl.pallas_call(
        _rope_kernel,
        grid=(t // tm,),
        in_specs=[pl.BlockSpec((tm, 1), lambda i: (i, 0)),
                  pl.BlockSpec((1, LANES), lambda i: (0, 0))],
        out_specs=[pl.BlockSpec((tm, LANES), lambda i: (i, 0))] * 2,
        out_shape=[jax.ShapeDtypeStruct((t, LANES), F32)] * 2,
        compiler_params=_cparams(("parallel",)),
        name="rope_tables",
    )(pos, freq)


def _ssd_kernel(z_ref, xs_ref, bc_ref, misc_ref, cwx_ref, cwb_ref, cbx_ref, cbb_ref, dtb_ref, alog_ref,
                d_ref, nw_ref, e_ref, y_ref, state_ref, bufx_ref, bufb_ref):
    q = SSD_Q
    hp = SSD_HEADS // SSD_GROUPS * SSD_HEAD_DIM

    @pl.when(pl.program_id(1) == 0)
    def _():
        state_ref[...] = jnp.zeros_like(state_ref)
        bufx_ref[0:SUBLANES, :] = jnp.zeros((SUBLANES, bufx_ref.shape[1]), F32)
        bufb_ref[0:SUBLANES, :] = jnp.zeros((SUBLANES, bufb_ref.shape[1]), F32)

    bufx_ref[SUBLANES:SUBLANES + q, :] = xs_ref[...].astype(F32)
    bufb_ref[SUBLANES:SUBLANES + q, :] = bc_ref[...].astype(F32)

    def conv(buf_ref, w_ref, b_ref):
        acc = b_ref[...] + w_ref[SSD_CONV - 1:SSD_CONV, :] * buf_ref[SUBLANES:SUBLANES + q, :]
        for k in range(SSD_CONV - 1):
            off = SUBLANES - (SSD_CONV - 1) + k
            acc = acc + w_ref[k:k + 1, :] * buf_ref[off:off + q, :]
        return _silu(acc)

    xs = conv(bufx_ref, cwx_ref, cbx_ref)
    bc = conv(bufb_ref, cwb_ref, cbb_ref)
    bufx_ref[0:SUBLANES, :] = bufx_ref[q:q + SUBLANES, :]
    bufb_ref[0:SUBLANES, :] = bufb_ref[q:q + SUBLANES, :]

    dt = _softplus(misc_ref[:, MISC_DT:MISC_DT + LANES] + dtb_ref[...])
    da = dt * (-jnp.exp(alog_ref[...]))
    row = lax.broadcasted_iota(jnp.int32, (q, q), 0)
    col = lax.broadcasted_iota(jnp.int32, (q, q), 1)
    causal = row >= col
    tri = jnp.where(causal, 1.0, 0.0).astype(BF16)
    cum = _dot_exact_rhs(tri, da)
    cum_t = cum.T
    ecum = jnp.exp(cum)
    toend = jnp.exp(cum[q - 1:q, :] - cum)
    e_mat = e_ref[...]
    dt_e = _dot_exact_lhs(dt, e_mat)
    ecum_e = _dot_exact_lhs(ecum, e_mat)
    toend_e = _dot_exact_lhs(toend, e_mat)

    xdt = xs * dt_e
    xdt_b = xdt.astype(BF16)
    xw_b = (xdt * toend_e).astype(BF16)
    lane = lax.broadcasted_iota(jnp.int32, (q, LANES), 1)
    lo_half = lane < SSD_HEAD_DIM

    y_groups = []
    for g in range(SSD_GROUPS):
        b_g = bc[:, g * SSD_STATE:(g + 1) * SSD_STATE]
        c_g = bc[:, SSD_BC + g * SSD_STATE:SSD_BC + (g + 1) * SSD_STATE]
        b_gb = b_g.astype(BF16)
        c_gb = c_g.astype(BF16)
        cb = _dot_nt(c_gb, b_gb)
        st = state_ref[:, g * hp:(g + 1) * hp]
        y_inter = _dot(c_gb, st.astype(BF16)) * ecum_e[:, g * hp:(g + 1) * hp]
        parts = []
        for pr in range(hp // LANES):
            ms = []
            for e in range(2):
                h = g * (SSD_HEADS // SSD_GROUPS) + pr * 2 + e
                ci = jnp.broadcast_to(cum[:, h:h + 1], (q, q))
                cj = jnp.broadcast_to(cum_t[h:h + 1, :], (q, q))
                dec = jnp.exp(jnp.where(causal, ci - cj, -jnp.inf))
                ms.append((cb * dec).astype(BF16))
            lo = g * hp + pr * LANES
            xp = xdt_b[:, lo:lo + LANES]
            zero = jnp.zeros_like(xp)
            rhs = jnp.concatenate([jnp.where(lo_half, xp, zero), jnp.where(lo_half, zero, xp)], axis=0)
            parts.append(_dot(jnp.concatenate(ms, axis=1), rhs))
        y_groups.append(jnp.concatenate(parts, axis=1) + y_inter)
        upd = _dot(b_g.T.astype(BF16), xw_b[:, g * hp:(g + 1) * hp])
        state_ref[:, g * hp:(g + 1) * hp] = st * ecum_e[q - 1:q, g * hp:(g + 1) * hp] + upd

    y = jnp.concatenate(y_groups, axis=1) + d_ref[...] * xs
    y = y * _silu(z_ref[...].astype(F32))
    outs = []
    for g in range(SSD_GROUPS):
        yg = y[:, g * hp:(g + 1) * hp]
        outs.append(yg * lax.rsqrt(jnp.mean(yg * yg, axis=-1, keepdims=True) + RMS_EPS))
    y_ref[...] = (jnp.concatenate(outs, axis=1) * nw_ref[...]).astype(y_ref.dtype)


def _ssd_mixer(proj, misc, conv_w, conv_b, dt_bias, a_log, d_skip, norm_w):
    bn, sn, _ = proj.shape
    w = SSD_HEADS * SSD_HEAD_DIM
    pad = lambda v: jnp.pad(v, (0, LANES - v.shape[0])).reshape(1, LANES)
    a_log_p = jnp.pad(a_log, (0, LANES - SSD_HEADS), constant_values=-jnp.inf).reshape(1, LANES)
    expand = jnp.pad(jnp.repeat(jnp.eye(SSD_HEADS, dtype=BF16), SSD_HEAD_DIM, axis=1),
                     ((0, LANES - SSD_HEADS), (0, 0)))
    row = lambda v: v.reshape(1, -1)
    const = lambda shape: pl.BlockSpec(shape, lambda b, c: (0, 0))
    blk = lambda width, idx: pl.BlockSpec((None, SSD_Q, width), lambda b, c: (b, c, idx))
    return pl.pallas_call(
        _ssd_kernel,
        grid=(bn, sn // SSD_Q),
        in_specs=[blk(w, COL_Z // w), blk(w, COL_XS // w), blk(1024, COL_BC // 1024), blk(1024, 0),
                  const((SSD_CONV, w)), const((SSD_CONV, 2 * SSD_BC)), const((1, w)), const((1, 2 * SSD_BC)),
                  const((1, LANES)), const((1, LANES)), const((1, w)), const((1, w)), const((LANES, w))],
        out_specs=pl.BlockSpec((None, SSD_Q, w), lambda b, c: (b, c, 0)),
        out_shape=jax.ShapeDtypeStruct((bn, sn, w), BF16),
        scratch_shapes=[pltpu.VMEM((SSD_STATE, w), F32),
                        pltpu.VMEM((SSD_Q + SUBLANES, w), F32),
                        pltpu.VMEM((SSD_Q + SUBLANES, 2 * SSD_BC), F32)],
        compiler_params=_cparams(("parallel", "arbitrary")),
        name="ssd_mixer",
    )(proj, proj, proj, misc, conv_w[:, :w], conv_w[:, w:], row(conv_b[:w]), row(conv_b[w:]),
      pad(dt_bias), a_log_p, row(jnp.repeat(d_skip, SSD_HEAD_DIM)), row(norm_w), expand)


def _dot_hi(a, b):
    return jnp.dot(a, b, precision=lax.Precision.HIGHEST, preferred_element_type=F32)


def _split2(a):
    hi = a.astype(BF16)
    return hi, (a - hi.astype(F32)).astype(BF16)


def _dot_3pass(a, b_hi, b_lo):
    a_hi, a_lo = _split2(a)
    return _dot(a_hi, b_hi) + (_dot(a_hi, b_lo) + _dot(a_lo, b_hi))


def _head_sums(x, ones_blk):
    outs = []
    for s in range(x.shape[1] // LANES):
        outs.append(_dot(x[:, s * LANES:(s + 1) * LANES].astype(BF16), ones_blk))
    return jnp.concatenate(outs, axis=1)


def _stack_heads(x, lo_half):
    zero = jnp.zeros_like(x)
    return jnp.concatenate([jnp.where(lo_half, x, zero), jnp.where(lo_half, zero, x)], axis=0)


RW_PAIRS = RWKV_HEADS // 2
RW_GROUP = 16


def _rwkv_kernel(has_vres, *refs):
    if has_vres:
        (r_ref, k_ref, v_ref, misc_ref, vfirst_ref, mur_ref, muk_ref, muv_ref, mum_ref, w0_ref, w2h_ref, w2l_ref,
         a0_ref, a2_ref, g2_ref, kk_ref, ka_ref, rk_ref, lnw_ref, lnb_ref, v0_ref, v2_ref,
         y_ref, state_ref, carry_ref, carrym_ref, st_ref, yp_ref, pt_ref) = refs
    else:
        (r_ref, k_ref, v_ref, misc_ref, mur_ref, muk_ref, muv_ref, mum_ref, w0_ref, w2h_ref, w2l_ref, a0_ref,
         a2_ref, g2_ref, kk_ref, ka_ref, rk_ref, lnw_ref, lnb_ref,
         y_ref, vout_ref, state_ref, carry_ref, carrym_ref, st_ref, yp_ref, pt_ref) = refs
    tb, t = RW_BLK, RW_T
    w = RWKV_HEADS * RWKV_HEAD

    @pl.when(pl.program_id(1) == 0)
    def _():
        state_ref[...] = jnp.zeros_like(state_ref)
        carry_ref[...] = jnp.zeros_like(carry_ref)
        carrym_ref[...] = jnp.zeros_like(carrym_ref)

    first_row = lax.broadcasted_iota(jnp.int32, (tb, 1), 0) == 0

    def shift(p, carry_row, mu):
        prev = jnp.where(first_row, carry_row, pltpu.roll(p, 1, 0))
        return p + (prev - p) * mu

    rp, kp, vp, mp = r_ref[...].astype(F32), k_ref[...].astype(F32), v_ref[...].astype(F32), misc_ref[...]
    r = shift(rp, carry_ref[0:1, :], mur_ref[...])
    k = shift(kp, carry_ref[1:2, :], muk_ref[...])
    v = shift(vp, carry_ref[2:3, :], muv_ref[...])
    m = shift(mp, carrym_ref[0:1, :], mum_ref[...])
    carry_ref[0:1, :] = rp[tb - 1:tb, :]
    carry_ref[1:2, :] = kp[tb - 1:tb, :]
    carry_ref[2:3, :] = vp[tb - 1:tb, :]
    carrym_ref[0:1, :] = mp[tb - 1:tb, :]

    w_lo = m[:, MISC_W:MISC_W + LANES]
    a_lo = m[:, MISC_A:MISC_A + LANES]
    g_lo = m[:, MISC_G:MISC_G + LORA_G]
    log_w = -_softplus(-(w0_ref[...] + _dot_3pass(jnp.tanh(w_lo), w2h_ref[...], w2l_ref[...]))) - 0.5
    lw = -jnp.exp(log_w)
    a = _sigmoid(a0_ref[...] + _dot(a_lo.astype(BF16), a2_ref[...]))
    g = _dot(_sigmoid(g_lo).astype(BF16), g2_ref[...])
    if has_vres:
        v_lo = m[:, MISC_V:MISC_V + LANES]
        v = v + (vfirst_ref[...] - v) * _sigmoid(v0_ref[...] + _dot(v_lo.astype(BF16), v2_ref[...]))
    else:
        vout_ref[...] = v

    lane = lax.broadcasted_iota(jnp.int32, (LANES, LANES), 1)
    rowi = lax.broadcasted_iota(jnp.int32, (LANES, LANES), 0)
    ones_blk = jnp.where((lane // RWKV_HEAD) == (rowi // RWKV_HEAD), 1.0, 0.0).astype(BF16)

    kk = k * kk_ref[...]
    kk = kk / jnp.maximum(jnp.sqrt(_head_sums(kk * kk, ones_blk)), 1e-12)
    k = k * (1.0 + (a - 1.0) * ka_ref[...])
    b = kk * a

    ti = lax.broadcasted_iota(jnp.int32, (t, t), 0)
    tj = lax.broadcasted_iota(jnp.int32, (t, t), 1)
    tri = jnp.where(ti >= tj, 1.0, 0.0).astype(BF16)

    for c in range(tb // t):
        sl = slice(c * t, (c + 1) * t)
        lw_c = lw[sl]
        cl = _dot_exact_rhs(tri, lw_c)
        cl_end = cl[t - 1:t, :]
        e_neg = jnp.exp(-cl)
        e_end = jnp.exp(cl_end - cl)
        ops = (kk[sl] * jnp.exp(cl - lw_c), r[sl] * jnp.exp(cl), k[sl] * e_neg, b[sl] * e_neg,
               v[sl], k[sl] * e_end, b[sl] * e_end)
        for pi in range(RW_PAIRS):
            ls = slice(pi * LANES, (pi + 1) * LANES)
            for oi, op in enumerate(ops):
                st_ref[c, oi, pi] = op[:, ls]
            pt_ref[c, pi] = jnp.broadcast_to(jnp.exp(cl_end[:, ls]), (SUBLANES, LANES))

    lo_half = lax.broadcasted_iota(jnp.int32, (t, LANES), 1) < RWKV_HEAD
    bi = lax.broadcasted_iota(jnp.int32, (2 * t, 2 * t), 0) % t
    bj = lax.broadcasted_iota(jnp.int32, (2 * t, 2 * t), 1) % t
    strict = bi > bj
    incl = bi >= bj

    def chunk_group(c, pis, hts):
        h2 = 2 * t
        stk = [[_stack_heads(st_ref[c, oi, pi], lo_half) for oi in range(7)] for pi in pis]
        lhs2 = [jnp.concatenate([s[0], s[1]], axis=0).astype(BF16) for s in stk]
        rhs2 = [jnp.concatenate([s[2], s[3]], axis=0).astype(BF16) for s in stk]
        amat = [_dot_nt(a, b) for a, b in zip(lhs2, rhs2)]
        sh = [_dot_nt(a, h.astype(BF16)) for a, h in zip(lhs2, hts)]
        vsb = [s[4].astype(BF16) for s in stk]
        x = [s_[0:h2] + _dot(jnp.where(strict, am[0:h2, 0:h2], 0.0).astype(BF16), v_)
             for s_, am, v_ in zip(sh, amat, vsb)]
        pw = [jnp.where(strict, am[0:h2, h2:], 0.0) for am in amat]
        n, sign = 1, -1.0
        while n < t:
            if 2 * n < t:
                res = [_dot(p_.astype(BF16), jnp.concatenate([p_, x_], axis=1).astype(BF16)) for p_, x_ in zip(pw, x)]
                x = [x_ + sign * r_[:, h2:] for x_, r_ in zip(x, res)]
                pw = [r_[:, 0:h2] for r_ in res]
            else:
                x = [x_ + sign * _dot(p_.astype(BF16), x_.astype(BF16)) for p_, x_ in zip(pw, x)]
            n, sign = 2 * n, 1.0
        new_hts = []
        for i, pi in enumerate(pis):
            am, s = amat[i], stk[i]
            a_r = jnp.concatenate([jnp.where(incl, am[h2:, 0:h2], 0.0), jnp.where(incl, -am[h2:, h2:], 0.0)], axis=1)
            ys = sh[i][h2:] + _dot(a_r.astype(BF16), jnp.concatenate([s[4], x[i]], axis=0).astype(BF16))
            yp_ref[pi, c * t:(c + 1) * t, :] = ys[0:t] + ys[t:]
            lhs3 = jnp.concatenate([s[4].T, -(x[i].T)], axis=1).astype(BF16)
            rhs3 = jnp.concatenate([s[5], s[6]], axis=0).astype(BF16)
            new_hts.append(hts[i] * pt_ref[c, pi][0:1, :] + _dot(lhs3, rhs3))
        return new_hts

    for g0 in range(0, RW_PAIRS, RW_GROUP):
        pis = list(range(g0, g0 + RW_GROUP))
        hts = [state_ref[pi] for pi in pis]
        for c in range(tb // t):
            hts = chunk_group(c, pis, hts)
        for pi, ht in zip(pis, hts):
            state_ref[pi] = ht

    y = jnp.concatenate([yp_ref[pi] for pi in range(RW_PAIRS)], axis=1)
    inv_n = 1.0 / RWKV_HEAD
    mu = _head_sums(y, ones_blk) * inv_n
    yc = y - mu
    var = _head_sums(yc * yc, ones_blk) * inv_n
    y = yc * lax.rsqrt(var + RWKV_GN_EPS) * lnw_ref[...] + lnb_ref[...]
    bonus = _head_sums(r * k * rk_ref[...], ones_blk) * v
    y_ref[...] = ((y + bonus) * g).astype(y_ref.dtype)


def _rwkv_mixer(proj, misc, v_first, mu, w0, w2, a0, a2, g2, k_k, k_a, r_k, ln_w, ln_b, v0, v2):
    bn, sn, _ = proj.shape
    w = RWKV_HEADS * RWKV_HEAD
    has_vres = v_first is not None
    row = lambda x: x.reshape(1, -1)
    padrows = lambda x: jnp.pad(x, ((0, LANES - x.shape[0]), (0, 0)))
    padl = lambda x, n: jnp.pad(x, (0, n - x.shape[0]))
    mu_misc = [padl(mu[3 * w:3 * w + LORA_W], LANES), padl(mu[3 * w + LORA_W:3 * w + LORA_W + LORA_A], LANES),
               mu[3 * w + LORA_W + LORA_A:3 * w + LORA_W + LORA_A + LORA_G]]
    if has_vres:
        mu_misc.append(padl(mu[3 * w + LORA_W + LORA_A + LORA_G:], LANES))
    mu_m = padl(jnp.concatenate(mu_misc), 1024)
    const = lambda shape: pl.BlockSpec(shape, lambda b, c: (0,) * len(shape))
    blk = lambda width, idx: pl.BlockSpec((None, RW_BLK, width), lambda b, c: (b, c, idx))
    seq = pl.BlockSpec((None, RW_BLK, w), lambda b, c: (b, c, 0))
    in_specs = [blk(w, COL_R // w), blk(w, COL_R // w + 1), blk(w, COL_R // w + 2), blk(1024, 0)]
    args = [proj, proj, proj, misc]
    if has_vres:
        in_specs.append(seq)
        args.append(v_first)
    in_specs += [const((1, w))] * 3 + [const((1, 1024)), const((1, w)), const((LANES, w)), const((LANES, w)),
                                       const((1, w)), const((LANES, w)), const((LORA_G, w))] + [const((1, w))] * 5
    w2p = padrows(w2)
    w2_hi = w2p.astype(BF16)
    w2_lo = (w2p - w2_hi.astype(F32)).astype(BF16)
    args += [row(mu[:w]), row(mu[w:2 * w]), row(mu[2 * w:3 * w]), row(mu_m), row(w0), w2_hi, w2_lo, row(a0),
             padrows(a2).astype(BF16), g2.astype(BF16), row(k_k), row(k_a), row(r_k), row(ln_w), row(ln_b)]
    if has_vres:
        in_specs += [const((1, w)), const((LANES, w))]
        args += [row(v0), padrows(v2).astype(BF16)]
    out_shape = [jax.ShapeDtypeStruct((bn, sn, w), BF16)]
    out_specs = [seq]
    if not has_vres:
        out_shape.append(jax.ShapeDtypeStruct((bn, sn, w), F32))
        out_specs.append(seq)
    nc = RW_BLK // RW_T
    outs = pl.pallas_call(
        functools.partial(_rwkv_kernel, has_vres),
        grid=(bn, sn // RW_BLK),
        in_specs=in_specs,
        out_specs=out_specs,
        out_shape=out_shape,
        scratch_shapes=[pltpu.VMEM((RW_PAIRS, LANES, LANES), F32),
                        pltpu.VMEM((SUBLANES, w), F32),
                        pltpu.VMEM((SUBLANES, 1024), F32),
                        pltpu.VMEM((nc, 7, RW_PAIRS, RW_T, LANES), F32),
                        pltpu.VMEM((RW_PAIRS, RW_BLK, LANES), F32),
                        pltpu.VMEM((nc, RW_PAIRS, SUBLANES, LANES), F32)],
        compiler_params=_cparams(("parallel", "arbitrary")),
        name="rwkv7_mixer",
    )(*args)
    if has_vres:
        return outs[0], v_first
    return outs[0], outs[1]


MLA_TM = 256
ATT_TQ = 4096
ATT_TK = 2048
ATT_SUB = 512
HEAD_Q = 2 * LANES


def _rope_half(x2, cos, sin):
    return x2 * cos + pltpu.roll(x2, MLA_ROPE, 1) * sin


def _mla_prep_kernel(qlat_ref, kvlat_ref, misc_ref, cos_ref, sin_ref, qnw_ref, kvnw_ref, wq_ref, wkt_ref, wv_ref,
                     q_ref, kt_ref, v_ref, kpet_ref):
    def rms(x, w):
        x = x.astype(F32)
        return (x * lax.rsqrt(jnp.mean(x * x, axis=-1, keepdims=True) + RMS_EPS) * w).astype(BF16)

    cos, sin = cos_ref[...], sin_ref[...]
    scale = MLA_QK ** -0.5
    q = _dot(rms(qlat_ref[...], qnw_ref[...]), wq_ref[...])
    for h in range(MLA_HEADS):
        lo = h * HEAD_Q
        q_ref[:, lo:lo + LANES] = (q[:, lo:lo + LANES] * scale).astype(BF16)
        q_ref[:, lo + LANES:lo + HEAD_Q] = (_rope_half(q[:, lo + LANES:lo + HEAD_Q], cos, sin) * scale).astype(BF16)
    kvn = rms(kvlat_ref[...], kvnw_ref[...])
    kt_ref[...] = _dot_nt(wkt_ref[...], kvn).astype(BF16)
    v_ref[...] = _dot(kvn, wv_ref[...]).astype(BF16)
    kpet_ref[...] = _rope_half(misc_ref[:, MISC_KPE:MISC_KPE + LANES], cos, sin).T.astype(BF16)


def _flash_kernel(tq, tk, sub, qi_ref, ki_ref, q_ref, kt_ref, kpet_ref, v_ref, o_ref, m_ref, acc_ref):
    step = pl.program_id(2)
    qi, ki = qi_ref[step], ki_ref[step]
    ratio = tq // tk

    @pl.when(ki == 0)
    def _():
        m_ref[...] = jnp.full_like(m_ref, -jnp.inf)
        acc_ref[...] = jnp.zeros_like(acc_ref)

    def update(diag):
        kcat_t = jnp.concatenate([kt_ref[...], kpet_ref[...]], axis=0)
        vcat = jnp.concatenate([v_ref[...], jnp.ones((tk, LANES), BF16)], axis=1)
        plan = []
        for r in range(tq // sub):
            row_lo, row_hi = r * sub, (r + 1) * sub - 1
            ncols = tk
            masked = False
            if diag is not None:
                col_lo = diag * tk
                if row_hi < col_lo:
                    continue
                ncols = min(tk, -(-(row_hi - col_lo + 1) // HEAD_Q) * HEAD_Q)
                masked = row_lo < col_lo + ncols - 1
            plan.append((row_lo, ncols, masked))

        def scores(row_lo, ncols, masked):
            s = _dot(q_ref[row_lo:row_lo + sub, :], kcat_t[:, :ncols])
            if masked:
                ri = row_lo + lax.broadcasted_iota(jnp.int32, s.shape, 0)
                ci = diag * tk + lax.broadcasted_iota(jnp.int32, s.shape, 1)
                s = jnp.where(ci <= ri, s, -jnp.inf)
            return s

        def absorb(row_lo, ncols, s):
            rows = slice(row_lo, row_lo + sub)
            tiles = [s[:, j * LANES:(j + 1) * LANES] for j in range(ncols // LANES)]
            fold = tiles[0]
            for tl in tiles[1:]:
                fold = jnp.maximum(fold, tl)
            m_prev = m_ref[rows, :]
            m_new = jnp.maximum(m_prev, jnp.max(fold, axis=-1, keepdims=True))
            p = jnp.concatenate([jnp.exp((tl - m_new).astype(BF16)) for tl in tiles], axis=1)
            alpha = jnp.exp(m_prev - m_new)
            acc_ref[rows, :] = (acc_ref[rows, :] * jnp.concatenate([alpha, alpha], axis=1)
                                + _dot(p, vcat[:ncols]))
            m_ref[rows, :] = m_new

        s_next = scores(*plan[0])
        for idx, (row_lo, ncols, _) in enumerate(plan):
            s_cur = s_next
            if idx + 1 < len(plan):
                s_next = scores(*plan[idx + 1])
            absorb(row_lo, ncols, s_cur)

    @pl.when(ki < qi * ratio)
    def _():
        update(None)

    for d in range(ratio):
        @pl.when(ki == qi * ratio + d)
        def _(d=d):
            update(d)

    @pl.when(ki == (qi + 1) * ratio - 1)
    def _():
        o_ref[...] = (acc_ref[:, 0:MLA_V] / acc_ref[:, MLA_V:]).astype(o_ref.dtype)


def _mla_mixer(proj, misc, cos, sin, q_norm_w, w_q_b, kv_norm_w, w_kv_b):
    bn, sn, _ = proj.shape
    t = bn * sn
    proj2 = proj.reshape(t, proj.shape[-1])
    misc2 = misc.reshape(t, misc.shape[-1])
    wq = w_q_b.reshape(MLA_RANK, MLA_HEADS, MLA_QK)
    pe = wq[..., MLA_NOPE:]
    rot = jnp.concatenate([-pe[..., MLA_ROPE // 2:], pe[..., :MLA_ROPE // 2]], axis=-1)
    wq = jnp.concatenate([wq, rot], axis=-1).reshape(MLA_RANK, MLA_HEADS * HEAD_Q).astype(BF16)
    wkv = w_kv_b.reshape(MLA_RANK, MLA_HEADS, MLA_NOPE + MLA_V)
    wkt = wkv[..., :MLA_NOPE].reshape(MLA_RANK, -1).T.astype(BF16)
    wv = wkv[..., MLA_NOPE:].reshape(MLA_RANK, -1).astype(BF16)
    nq, nk, nv = MLA_HEADS * HEAD_Q, MLA_HEADS * MLA_NOPE, MLA_HEADS * MLA_V
    tm = min(MLA_TM, sn)
    nsb = sn // tm
    rowblk = lambda width, idx: pl.BlockSpec((tm, width), lambda i: (i, idx))
    const = lambda shape: pl.BlockSpec(shape, lambda i: (0, 0))
    colblk = lambda rows: pl.BlockSpec((None, rows, tm), lambda i: (i // nsb, 0, i % nsb))
    q, kt, v, kpet = pl.pallas_call(
        _mla_prep_kernel,
        grid=(t // tm,),
        in_specs=[rowblk(MLA_RANK, COL_QLAT // MLA_RANK), rowblk(MLA_RANK, COL_KVLAT // MLA_RANK),
                  rowblk(1024, 0), rowblk(LANES, 0), rowblk(LANES, 0),
                  const((1, MLA_RANK)), const((1, MLA_RANK)), const((MLA_RANK, nq)), const((nk, MLA_RANK)),
                  const((MLA_RANK, nv))],
        out_specs=[rowblk(nq, 0), colblk(nk), rowblk(nv, 0), colblk(LANES)],
        out_shape=[jax.ShapeDtypeStruct((t, nq), BF16), jax.ShapeDtypeStruct((bn, nk, sn), BF16),
                   jax.ShapeDtypeStruct((t, nv), BF16), jax.ShapeDtypeStruct((bn, LANES, sn), BF16)],
        compiler_params=_cparams(("parallel",)),
        name="mla_prep",
    )(proj2, proj2, misc2, cos, sin, q_norm_w.reshape(1, -1), kv_norm_w.reshape(1, -1), wq, wkt, wv)

    tq, tk = min(ATT_TQ, sn), min(ATT_TK, sn)
    sub = min(ATT_SUB, tq)
    ratio = tq // tk
    pairs = [(a, b) for a in range(sn // tq) for b in range((a + 1) * ratio)]
    qi_arr = jnp.asarray([p[0] for p in pairs], jnp.int32)
    ki_arr = jnp.asarray([p[1] for p in pairs], jnp.int32)
    out = pl.pallas_call(
        functools.partial(_flash_kernel, tq, tk, sub),
        grid_spec=pltpu.PrefetchScalarGridSpec(
            num_scalar_prefetch=2,
            grid=(bn, MLA_HEADS, len(pairs)),
            in_specs=[pl.BlockSpec((None, tq, HEAD_Q), lambda b, h, s, qi, ki: (b, qi[s], h)),
                      pl.BlockSpec((None, MLA_NOPE, tk), lambda b, h, s, qi, ki: (b, h, ki[s])),
                      pl.BlockSpec((None, LANES, tk), lambda b, h, s, qi, ki: (b, 0, ki[s])),
                      pl.BlockSpec((None, tk, MLA_V), lambda b, h, s, qi, ki: (b, ki[s], h))],
            out_specs=pl.BlockSpec((None, tq, MLA_V), lambda b, h, s, qi, ki: (b, qi[s], h)),
            scratch_shapes=[pltpu.VMEM((tq, LANES), F32), pltpu.VMEM((tq, 2 * MLA_V), F32)]),
        out_shape=jax.ShapeDtypeStruct((bn, sn, MLA_HEADS * MLA_V), BF16),
        compiler_params=_cparams(("parallel", "parallel", "arbitrary")),
        name="mla_attention",
    )(qi_arr, ki_arr, q.reshape(bn, sn, nq), kt, kpet, v.reshape(bn, sn, nv))
    return out


MERGE_TM = 256


def _merge_kernel(g0_ref, g1_ref, g2_ref, ys_ref, yr_ref, ym_ref, x_ref, wo_ref, lnw_ref, lnb_ref, xo_ref, xb_ref):
    gate = lambda g_ref, y_ref: _sigmoid(g_ref[...].astype(F32)) * y_ref[...].astype(F32)
    merged = gate(g0_ref, ys_ref) + gate(g1_ref, yr_ref) + gate(g2_ref, ym_ref)
    h = ALPHA * x_ref[...] + _dot(merged.astype(BF16), wo_ref[...])
    y = _layernorm(h, lnw_ref[...], lnb_ref[...])
    xo_ref[...] = y
    xb_ref[...] = y.astype(BF16)


def _merge_out(proj2, y_ssd, y_rwkv, y_mla, x, w_out, ln_w, ln_b):
    t, d = x.shape
    tm = min(MERGE_TM, t)
    rowblk = lambda idx: pl.BlockSpec((tm, d), lambda i: (i, idx))
    const = lambda shape: pl.BlockSpec(shape, lambda i: (0, 0))
    return pl.pallas_call(
        _merge_kernel,
        grid=(t // tm,),
        in_specs=[rowblk(0), rowblk(1), rowblk(2), rowblk(0), rowblk(0), rowblk(0), rowblk(0),
                  const((d, d)), const((1, d)), const((1, d))],
        out_specs=[rowblk(0), rowblk(0)],
        out_shape=[jax.ShapeDtypeStruct((t, d), F32), jax.ShapeDtypeStruct((t, d), BF16)],
        compiler_params=_cparams(("parallel",)),
        name="merge_out_ln1",
    )(proj2, proj2, proj2, y_ssd, y_rwkv, y_mla, x, w_out.astype(BF16), ln_w.reshape(1, d), ln_b.reshape(1, d))


FFN_TM = 512
FFN_TF = 512


def _ffn_kernel(xb_ref, x_ref, w1_ref, w3_ref, w2_ref, lnw_ref, lnb_ref, o_ref, ob_ref, acc_ref):
    f = pl.program_id(1)

    @pl.when(f == 0)
    def _():
        acc_ref[...] = jnp.zeros_like(acc_ref)

    xb = xb_ref[...]
    h = _silu(_dot(xb, w1_ref[...])) * _dot(xb, w3_ref[...])
    acc_ref[...] += _dot(h.astype(BF16), w2_ref[...])

    @pl.when(f == pl.num_programs(1) - 1)
    def _():
        y = _layernorm(ALPHA * x_ref[...] + acc_ref[...], lnw_ref[...], lnb_ref[...])
        o_ref[...] = y
        ob_ref[...] = y.astype(BF16)


def _ffn_dense(xb, x, w1, w3, w2, ln_w, ln_b):
    t, d = x.shape
    ff = w1.shape[1]
    tm, tf = min(FFN_TM, t), FFN_TF
    return pl.pallas_call(
        _ffn_kernel,
        grid=(t // tm, ff // tf),
        in_specs=[pl.BlockSpec((tm, d), lambda i, f: (i, 0)), pl.BlockSpec((tm, d), lambda i, f: (i, 0)),
                  pl.BlockSpec((d, tf), lambda i, f: (0, f)), pl.BlockSpec((d, tf), lambda i, f: (0, f)),
                  pl.BlockSpec((tf, d), lambda i, f: (f, 0)),
                  pl.BlockSpec((1, d), lambda i, f: (0, 0)), pl.BlockSpec((1, d), lambda i, f: (0, 0))],
        out_specs=[pl.BlockSpec((tm, d), lambda i, f: (i, 0)), pl.BlockSpec((tm, d), lambda i, f: (i, 0))],
        out_shape=[jax.ShapeDtypeStruct((t, d), F32), jax.ShapeDtypeStruct((t, d), BF16)],
        scratch_shapes=[pltpu.VMEM((tm, d), F32)],
        compiler_params=_cparams(("parallel", "arbitrary")),
        name="ffn_dense_ln2",
    )(xb, x, w1.astype(BF16), w3.astype(BF16), w2.astype(BF16), ln_w.reshape(1, d), ln_b.reshape(1, d))


ROUTER_TM = 512
MOE_TM = 512
MOE_TF = 512
COMBINE_TM = 256


def _router_kernel(x_ref, wr_ref, idx_ref, wgt_ref, cnt_ref):
    @pl.when(pl.program_id(0) == 0)
    def _():
        cnt_ref[...] = jnp.zeros_like(cnt_ref)

    tm = x_ref.shape[0]
    logits = _dot_hi(x_ref[...], wr_ref[...])
    lane = lax.broadcasted_iota(jnp.int32, logits.shape, 1)
    lg = jnp.where(lane < N_EXPERTS, logits, -jnp.inf)
    m1 = jnp.max(lg, axis=-1, keepdims=True)
    i1 = jnp.min(jnp.where(lg == m1, lane, LANES), axis=-1, keepdims=True)
    lg2 = jnp.where(lane == i1, -jnp.inf, lg)
    m2 = jnp.max(lg2, axis=-1, keepdims=True)
    i2 = jnp.min(jnp.where(lg2 == m2, lane, LANES), axis=-1, keepdims=True)
    e = jnp.exp(m2 - m1)
    wgt_ref[...] = jnp.where(lane == 0, 1.0 / (1.0 + e), jnp.where(lane == 1, e / (1.0 + e), 0.0))

    hit1, hit2 = lane == i1, lane == i2
    both = jnp.where(hit1 | hit2, 1.0, 0.0)
    ri = lax.broadcasted_iota(jnp.int32, (tm, tm), 0)
    ci = lax.broadcasted_iota(jnp.int32, (tm, tm), 1)
    before = cnt_ref[0:1, :] + _dot(jnp.where(ri > ci, 1.0, 0.0).astype(BF16), both.astype(BF16))
    rank1 = jnp.sum(jnp.where(hit1, before, 0.0), axis=-1, keepdims=True).astype(jnp.int32)
    rank2 = jnp.sum(jnp.where(hit2, before, 0.0), axis=-1, keepdims=True).astype(jnp.int32)
    idx_ref[...] = jnp.where(lane == 0, i1, jnp.where(lane == 1, i2, jnp.where(lane == 2, rank1,
                                                                              jnp.where(lane == 3, rank2, 0))))
    cnt_ref[...] = jnp.broadcast_to(before[tm - 1:tm, :] + both[tm - 1:tm, :], cnt_ref.shape)


def _moe_ffn_kernel(nf_static, be_ref, nused_ref, tok_ref, tok_next_ref, x_hbm, w1_ref, w3_ref, w2_ref, o_ref, xg_ref,
                    xb_ref, acc_ref, sem):
    i, f = pl.program_id(0), pl.program_id(1)
    ni, nf = pl.num_programs(0), pl.num_programs(1)
    rows = xg_ref.shape[1]
    per = -(-rows // nf_static)
    tail = rows - (nf_static - 1) * per
    slot = i % 2

    def row_copy(idx_ref, sl, r):
        return pltpu.make_async_copy(x_hbm.at[pl.ds(idx_ref[0, 0, r], 1), :], xg_ref.at[sl, pl.ds(r, 1), :],
                                     sem.at[sl])

    def wait_all(idx_ref, sl):
        for r in range(rows):
            row_copy(idx_ref, sl, r).wait()

    @pl.when(f == 0)
    def _():
        @pl.when(i == 0)
        def _():
            def body(r, c):
                row_copy(tok_ref, slot, r).start()
                return c
            lax.fori_loop(0, rows, body, 0)

        wait_all(tok_ref, slot)
        xb_ref[...] = xg_ref[slot].astype(BF16)
        acc_ref[...] = jnp.zeros_like(acc_ref)

    base = f * per

    def fetch_share():
        for j in range(tail):
            row_copy(tok_next_ref, 1 - slot, base + j).start()

    @pl.when(i < nused_ref[0])
    def _():
        fetch_share()
        xb = xb_ref[...]
        h = _silu(_dot(xb, w1_ref[...])) * _dot(xb, w3_ref[...])
        acc_ref[...] += _dot(h.astype(BF16), w2_ref[...])

    @pl.when(i >= nused_ref[0])
    def _():
        fetch_share()

    @pl.when(f < nf - 1)
    def _():
        for j in range(tail, per):
            row_copy(tok_next_ref, 1 - slot, base + j).start()

    @pl.when(f == nf - 1)
    def _():
        o_ref[...] = acc_ref[...]

        @pl.when(i == ni - 1)
        def _():
            wait_all(tok_next_ref, 1 - slot)


def _combine_kernel(dst_ref, dst_next_ref, y_hbm, wgt_ref, x_ref, lnw_ref, lnb_ref, o_ref, yg_ref, sem):
    i, n = pl.program_id(0), pl.num_programs(0)
    rows = x_ref.shape[0]
    slot = i % 2

    def row_copy(idx_ref, sl, r, k):
        return pltpu.make_async_copy(y_hbm.at[pl.ds(idx_ref[0, 0, TOP_K * r + k], 1), :],
                                     yg_ref.at[sl, k, pl.ds(r, 1), :], sem.at[sl])

    def for_all(idx_ref, sl, op):
        for r in range(rows):
            for k in range(TOP_K):
                op(row_copy(idx_ref, sl, r, k))

    @pl.when(i == 0)
    def _():
        for_all(dst_ref, slot, lambda c: c.start())

    for_all(dst_next_ref, 1 - slot, lambda c: c.start())
    for_all(dst_ref, slot, lambda c: c.wait())
    wgt = wgt_ref[...]
    f = wgt[:, 0:1] * yg_ref[slot, 0] + wgt[:, 1:2] * yg_ref[slot, 1]
    o_ref[...] = _layernorm(ALPHA * x_ref[...] + f, lnw_ref[...], lnb_ref[...])

    @pl.when(i == n - 1)
    def _():
        for_all(dst_next_ref, 1 - slot, lambda c: c.wait())


def _ffn_moe(x, router, w1, w3, w2, ln_w, ln_b):
    t, d = x.shape
    ne, _, ff = w1.shape
    tm = min(ROUTER_TM, t)
    rw = jnp.pad(router, ((0, 0), (0, LANES - ne)))
    idx, wgt, cnt = pl.pallas_call(
        _router_kernel,
        grid=(t // tm,),
        in_specs=[pl.BlockSpec((tm, d), lambda i: (i, 0)), pl.BlockSpec((d, LANES), lambda i: (0, 0))],
        out_specs=[pl.BlockSpec((tm, LANES), lambda i: (i, 0))] * 2 + [pl.BlockSpec((SUBLANES, LANES), lambda i: (0, 0))],
        out_shape=[jax.ShapeDtypeStruct((t, LANES), jnp.int32), jax.ShapeDtypeStruct((t, LANES), F32),
                   jax.ShapeDtypeStruct((SUBLANES, LANES), F32)],
        compiler_params=_cparams(("arbitrary",)),
        name="moe_router",
    )(x, rw)

    blk = MOE_TM
    n_assign = t * TOP_K
    flat_e = idx[:, :TOP_K].reshape(-1)
    rank = idx[:, TOP_K:2 * TOP_K].reshape(-1)
    counts = cnt[0, :ne].astype(jnp.int32)
    padded = (counts + blk - 1) // blk * blk
    pad_end = jnp.cumsum(padded)
    dest = (pad_end - padded)[flat_e] + rank
    n_blocks = -(-(n_assign + ne * (blk - 1)) // blk)
    n_rows = n_blocks * blk
    token_of_row = jnp.zeros((n_rows,), jnp.int32).at[dest].set(jnp.arange(n_assign, dtype=jnp.int32) // TOP_K)
    block_expert = jnp.minimum(jnp.searchsorted(pad_end, jnp.arange(n_blocks, dtype=jnp.int32) * blk, side='right'),
                               ne - 1).astype(jnp.int32)

    tf = MOE_TF
    n_used = (pad_end[-1:] // blk).astype(jnp.int32)
    wtile = lambda i, f, nu: jnp.where(i < nu[0], f, 0)
    yb = pl.pallas_call(
        functools.partial(_moe_ffn_kernel, ff // tf),
        grid_spec=pltpu.PrefetchScalarGridSpec(
            num_scalar_prefetch=2,
            grid=(n_blocks, ff // tf),
            in_specs=[pl.BlockSpec((1, 1, blk), lambda i, f, be, nu: (i, 0, 0), memory_space=pltpu.SMEM),
                      pl.BlockSpec((1, 1, blk), lambda i, f, be, nu: (jnp.minimum(i + 1, n_blocks - 1), 0, 0),
                                   memory_space=pltpu.SMEM),
                      pl.BlockSpec(memory_space=pl.ANY),
                      pl.BlockSpec((None, d, tf), lambda i, f, be, nu: (be[i], 0, wtile(i, f, nu))),
                      pl.BlockSpec((None, d, tf), lambda i, f, be, nu: (be[i], 0, wtile(i, f, nu))),
                      pl.BlockSpec((None, tf, d), lambda i, f, be, nu: (be[i], wtile(i, f, nu), 0))],
            out_specs=pl.BlockSpec((blk, d), lambda i, f, be, nu: (i, 0)),
            scratch_shapes=[pltpu.VMEM((2, blk, d), F32), pltpu.VMEM((blk, d), BF16), pltpu.VMEM((blk, d), F32),
                            pltpu.SemaphoreType.DMA((2,))]),
        out_shape=jax.ShapeDtypeStruct((n_rows, d), F32),
        compiler_params=_cparams(("arbitrary", "arbitrary")),
        name="moe_expert_ffn",
    )(block_expert, n_used, token_of_row.reshape(n_blocks, 1, blk), token_of_row.reshape(n_blocks, 1, blk), x,
      w1.astype(BF16), w3.astype(BF16), w2.astype(BF16))

    tc = min(COMBINE_TM, t)
    return pl.pallas_call(
        _combine_kernel,
        grid=(t // tc,),
        in_specs=[pl.BlockSpec((1, 1, TOP_K * tc), lambda i: (i, 0, 0), memory_space=pltpu.SMEM),
                  pl.BlockSpec((1, 1, TOP_K * tc), lambda i: (jnp.minimum(i + 1, t // tc - 1), 0, 0),
                               memory_space=pltpu.SMEM),
                  pl.BlockSpec(memory_space=pl.ANY),
                  pl.BlockSpec((tc, LANES), lambda i: (i, 0)), pl.BlockSpec((tc, d), lambda i: (i, 0)),
                  pl.BlockSpec((1, d), lambda i: (0, 0)), pl.BlockSpec((1, d), lambda i: (0, 0))],
        out_specs=pl.BlockSpec((tc, d), lambda i: (i, 0)),
        out_shape=jax.ShapeDtypeStruct((t, d), F32),
        scratch_shapes=[pltpu.VMEM((2, TOP_K, tc, d), F32), pltpu.SemaphoreType.DMA((2,))],
        compiler_params=_cparams(("arbitrary",)),
        name="moe_combine_ln2",
    )(dest.reshape(t // tc, 1, TOP_K * tc), dest.reshape(t // tc, 1, TOP_K * tc), yb, wgt, x,
      ln_w.reshape(1, d), ln_b.reshape(1, d))


def _pack_w_in(w, w_vres):
    d = w.shape[0]
    o = 0

    def take(n):
        nonlocal o
        s = w[:, o:o + n]
        o += n
        return s

    padc = lambda s, n: jnp.pad(s, ((0, 0), (0, n - s.shape[1])))
    gates = take(3 * W2K)
    z = take(W2K)
    xbc = take(W2K + 2 * SSD_BC)
    dt = take(SSD_HEADS)
    qlat = take(MLA_RANK)
    kvlat = take(MLA_RANK)
    kpe = take(MLA_ROPE)
    rkv = take(3 * W2K)
    w_lo, a_lo, g_lo = take(LORA_W), take(LORA_A), take(LORA_G)
    v_lo = jnp.zeros((d, LANES), w.dtype) if w_vres is None else padc(w_vres, LANES)
    kpe_rot = jnp.concatenate([-kpe[:, MLA_ROPE // 2:], kpe[:, :MLA_ROPE // 2]], axis=1)
    misc = jnp.concatenate([padc(w_lo, LANES), padc(a_lo, LANES), g_lo, v_lo, padc(dt, LANES), kpe, kpe_rot], axis=1)
    return jnp.concatenate([gates, z, rkv, xbc, qlat, kvlat, padc(misc, 1024)], axis=1).astype(BF16)


PROJ_TM = 2048
PROJ_TN = 1024


def kernel(x, positions, w_in, w_in_vres, w_out, ssd_conv_w, ssd_conv_b, ssd_dt_bias, ssd_a_log, ssd_d, ssd_norm_w, rwkv_mu, rwkv_mu_vres, rwkv_w0, rwkv_w2, rwkv_a0, rwkv_a2, rwkv_g2, rwkv_v0, rwkv_v2, rwkv_k_k, rwkv_k_a, rwkv_r_k, rwkv_ln_w, rwkv_ln_b, mla_q_norm_w, mla_w_q_b, mla_kv_norm_w, mla_w_kv_b, ln1_w, ln1_b, ln2_w, ln2_b, ffn_w1, ffn_w3, ffn_w2, moe_router, moe_w1, moe_w3, moe_w2):
    bn, sn, d = x.shape
    t = bn * sn
    cos, sin = _rope_tables(positions)
    xf = x.reshape(t, d)
    xb = xf.astype(BF16)
    v_first = None
    for l in range(DEPTH):
        if l == 0:
            wp, mu, v0, v2 = _pack_w_in(w_in[l], None), rwkv_mu[l], None, None
        else:
            wp = _pack_w_in(w_in[l], w_in_vres[l - 1])
            mu = jnp.concatenate([rwkv_mu[l], rwkv_mu_vres[l - 1]], axis=0)
            v0, v2 = rwkv_v0[l - 1], rwkv_v2[l - 1]
        proj2 = _matmul(xb, wp[:, :COL_MISC], BF16, min(PROJ_TM, t), PROJ_TN)
        misc = _matmul(xb, wp[:, COL_MISC:], F32, min(PROJ_TM, t), N_PROJ - COL_MISC).reshape(bn, sn, -1)
        proj = proj2.reshape(bn, sn, COL_MISC)
        y_ssd = _ssd_mixer(proj, misc, ssd_conv_w[l], ssd_conv_b[l], ssd_dt_bias[l], ssd_a_log[l], ssd_d[l],
                           ssd_norm_w[l])
        y_rwkv, v_first = _rwkv_mixer(proj, misc, v_first, mu, rwkv_w0[l], rwkv_w2[l], rwkv_a0[l], rwkv_a2[l],
                                      rwkv_g2[l], rwkv_k_k[l], rwkv_k_a[l], rwkv_r_k[l], rwkv_ln_w[l], rwkv_ln_b[l],
                                      v0, v2)
        y_mla = _mla_mixer(proj, misc, cos, sin, mla_q_norm_w[l], mla_w_q_b[l], mla_kv_norm_w[l], mla_w_kv_b[l])
        x1, x1b = _merge_out(proj2, y_ssd.reshape(t, d), y_rwkv.reshape(t, d), y_mla.reshape(t, d), xf, w_out[l],
                             ln1_w[l], ln1_b[l])
        if l % 2 == 0:
            xf, xb = _ffn_dense(x1b, x1, ffn_w1[l // 2], ffn_w3[l // 2], ffn_w2[l // 2], ln2_w[l], ln2_b[l])
        else:
            xf = _ffn_moe(x1, moe_router[l // 2], moe_w1[l // 2], moe_w3[l // 2], moe_w2[l // 2], ln2_w[l], ln2_b[l])
            xb = xf.astype(BF16)
    return xf.reshape(bn, sn, d)
```

```python
import functools
import math

import jax
import jax.numpy as jnp
import numpy as np
from jax import lax
from jax.experimental import pallas as pl
from jax.experimental.pallas import tpu as pltpu

F32 = jnp.float32
BF16 = jnp.bfloat16

D_MODEL = 2048
DEPTH = 2
ALPHA = (2 * DEPTH) ** 0.25
LN_EPS = 1e-5
RMS_EPS = 1e-6
SSD_HEADS, SSD_HEAD_DIM, SSD_GROUPS, SSD_STATE, SSD_CONV = 32, 64, 4, 128, 4
SSD_BC = SSD_GROUPS * SSD_STATE
RWKV_HEADS, RWKV_HEAD = 32, 64
RWKV_GN_EPS = 64e-5
LORA_W, LORA_A, LORA_G, LORA_V = 96, 96, 256, 64
MLA_HEADS, MLA_NOPE, MLA_ROPE, MLA_V, MLA_RANK = 16, 128, 64, 128, 512
MLA_QK = MLA_NOPE + MLA_ROPE
ROPE_THETA = 10000.0
D_FF = 5632
N_EXPERTS = 8
TOP_K = 2

LANES = 128
SUBLANES = 8
VMEM_LIMIT = 56 * 1024 * 1024

W2K = 2048
COL_GATE = 0
COL_Z = 3 * W2K
COL_R = 4 * W2K
COL_XS = 7 * W2K
COL_BC = 8 * W2K
COL_QLAT = COL_BC + 1024
COL_KVLAT = COL_QLAT + 512
COL_MISC = COL_KVLAT + 512
MISC_W, MISC_A, MISC_G, MISC_V, MISC_DT, MISC_KPE = 0, 128, 256, 512, 640, 768
N_PROJ = COL_MISC + 1024

SSD_Q = 128
RW_T = 64
RW_BLK = 128


def _cparams(sem, vmem=VMEM_LIMIT):
    return pltpu.CompilerParams(dimension_semantics=sem, vmem_limit_bytes=vmem)


def _dot(a, b):
    return jnp.dot(a, b, preferred_element_type=F32)


def _dot_nt(a, b):
    return lax.dot_general(a, b, (((1,), (1,)), ((), ())), preferred_element_type=F32)


def _split3(a):
    hi = a.astype(BF16)
    r1 = a - hi.astype(F32)
    mid = r1.astype(BF16)
    lo = (r1 - mid.astype(F32)).astype(BF16)
    return hi, mid, lo


def _dot_exact_rhs(a_bf, b):
    hi, mid, lo = _split3(b)
    return _dot(a_bf, hi) + _dot(a_bf, mid) + _dot(a_bf, lo)


def _dot_exact_lhs(a, b_bf):
    hi, mid, lo = _split3(a)
    return _dot(hi, b_bf) + _dot(mid, b_bf) + _dot(lo, b_bf)


def _sigmoid(x):
    return 1.0 / (1.0 + jnp.exp(-x))


def _silu(x):
    return x * _sigmoid(x)


def _softplus(x):
    return jnp.maximum(x, 0.0) + jnp.log(1.0 + jnp.exp(-jnp.abs(x)))


def _layernorm(x, w, b):
    mu = jnp.mean(x, axis=-1, keepdims=True)
    xc = x - mu
    var = jnp.mean(xc * xc, axis=-1, keepdims=True)
    return xc * lax.rsqrt(var + LN_EPS) * w + b


def _mm_kernel(x_ref, w_ref, o_ref):
    o_ref[...] = _dot(x_ref[...].astype(BF16), w_ref[...]).astype(o_ref.dtype)


def _matmul(x, w, out_dtype, tm, tn):
    m, k = x.shape
    n = w.shape[1]
    return pl.pallas_call(
        _mm_kernel,
        grid=(m // tm, n // tn),
        in_specs=[pl.BlockSpec((tm, k), lambda i, j: (i, 0)),
                  pl.BlockSpec((k, tn), lambda i, j: (0, j))],
        out_specs=pl.BlockSpec((tm, tn), lambda i, j: (i, j)),
        out_shape=jax.ShapeDtypeStruct((m, n), out_dtype),
        compiler_params=_cparams(("parallel", "arbitrary")),
        name="in_proj",
    )(x, w)


def _rope_kernel(pos_ref, freq_ref, cos_ref, sin_ref):
    ang = pos_ref[...] * freq_ref[...]
    valid = lax.broadcasted_iota(jnp.int32, ang.shape, 1) < MLA_ROPE
    cos_ref[...] = jnp.where(valid, jnp.cos(ang), 0.0)
    sin_ref[...] = jnp.where(valid, jnp.sin(ang), 0.0)


def _rope_tables(positions):
    t = positions.size
    tm = min(t, 1024)
    pos = positions.reshape(t, 1).astype(F32)
    inv_freq = ROPE_THETA ** (-jnp.arange(0, MLA_ROPE, 2, dtype=F32) / MLA_ROPE)
    freq = jnp.concatenate([inv_freq, inv_freq, jnp.zeros((LANES - MLA_ROPE,), F32)]).reshape(1, LANES)
    return pl.pallas_call(
        _rope_kernel,
        grid=(t // tm,),
        in_specs=[pl.BlockSpec((tm, 1), lambda i: (i, 0)),
                  pl.BlockSpec((1, LANES), lambda i: (0, 0))],
        out_specs=[pl.BlockSpec((tm, LANES), lambda i: (i, 0))] * 2,
        out_shape=[jax.ShapeDtypeStruct((t, LANES), F32)] * 2,
        compiler_params=_cparams(("parallel",)),
        name="rope_tables",
    )(pos, freq)


def _ssd_kernel(z_ref, xs_ref, bc_ref, misc_ref, cwx_ref, cwb_ref, cbx_ref, cbb_ref, dtb_ref, alog_ref,
                d_ref, nw_ref, e_ref, y_ref, state_ref, bufx_ref, bufb_ref):
    q = SSD_Q
    hp = SSD_HEADS // SSD_GROUPS * SSD_HEAD_DIM

    @pl.when(pl.program_id(1) == 0)
    def _():
        state_ref[...] = jnp.zeros_like(state_ref)
        bufx_ref[0:SUBLANES, :] = jnp.zeros((SUBLANES, bufx_ref.shape[1]), F32)
        bufb_ref[0:SUBLANES, :] = jnp.zeros((SUBLANES, bufb_ref.shape[1]), F32)

    bufx_ref[SUBLANES:SUBLANES + q, :] = xs_ref[...].astype(F32)
    bufb_ref[SUBLANES:SUBLANES + q, :] = bc_ref[...].astype(F32)

    def conv(buf_ref, w_ref, b_ref):
        acc = b_ref[...] + w_ref[SSD_CONV - 1:SSD_CONV, :] * buf_ref[SUBLANES:SUBLANES + q, :]
        for k in range(SSD_CONV - 1):
            off = SUBLANES - (SSD_CONV - 1) + k
            acc = acc + w_ref[k:k + 1, :] * buf_ref[off:off + q, :]
        return _silu(acc)

    xs = conv(bufx_ref, cwx_ref, cbx_ref)
    bc = conv(bufb_ref, cwb_ref, cbb_ref)
    bufx_ref[0:SUBLANES, :] = bufx_ref[q:q + SUBLANES, :]
    bufb_ref[0:SUBLANES, :] = bufb_ref[q:q + SUBLANES, :]

    dt = _softplus(misc_ref[:, MISC_DT:MISC_DT + LANES] + dtb_ref[...])
    da = dt * (-jnp.exp(alog_ref[...]))
    row = lax.broadcasted_iota(jnp.int32, (q, q), 0)
    col = lax.broadcasted_iota(jnp.int32, (q, q), 1)
    causal = row >= col
    tri = jnp.where(causal, 1.0, 0.0).astype(BF16)
    cum = _dot_exact_rhs(tri, da)
    cum_t = cum.T
    ecum = jnp.exp(cum)
    toend = jnp.exp(cum[q - 1:q, :] - cum)
    e_mat = e_ref[...]
    def spread(a):
        hi, lo = _split2(a)
        return _dot(hi, e_mat) + _dot(lo, e_mat)

    dt_e, ecum_e, toend_e = spread(dt), spread(ecum), spread(toend)

    xdt = xs * dt_e
    xdt_b = xdt.astype(BF16)
    xw_b = (xdt * toend_e).astype(BF16)
    lane = lax.broadcasted_iota(jnp.int32, (q, LANES), 1)
    lo_half = lane < SSD_HEAD_DIM

    y_groups = []
    for g in range(SSD_GROUPS):
        b_g = bc[:, g * SSD_STATE:(g + 1) * SSD_STATE]
        c_g = bc[:, SSD_BC + g * SSD_STATE:SSD_BC + (g + 1) * SSD_STATE]
        b_gb = b_g.astype(BF16)
        c_gb = c_g.astype(BF16)
        cb = _dot_nt(c_gb, b_gb)
        st = state_ref[:, g * hp:(g + 1) * hp]
        y_inter = _dot(c_gb, st.astype(BF16)) * ecum_e[:, g * hp:(g + 1) * hp]
        parts = []
        for pr in range(hp // LANES):
            ms = []
            for e in range(2):
                h = g * (SSD_HEADS // SSD_GROUPS) + pr * 2 + e
                ci = jnp.broadcast_to(cum[:, h:h + 1], (q, q))
                cj = jnp.broadcast_to(cum_t[h:h + 1, :], (q, q))
                dec = jnp.exp(jnp.where(causal, ci - cj, -jnp.inf))
                ms.append((cb * dec).astype(BF16))
            lo = g * hp + pr * LANES
            xp = xdt_b[:, lo:lo + LANES]
            zero = jnp.zeros_like(xp)
            rhs = jnp.concatenate([jnp.where(lo_half, xp, zero), jnp.where(lo_half, zero, xp)], axis=0)
            parts.append(_dot(jnp.concatenate(ms, axis=1), rhs))
        y_groups.append(jnp.concatenate(parts, axis=1) + y_inter)
        upd = _dot(b_g.T.astype(BF16), xw_b[:, g * hp:(g + 1) * hp])
        state_ref[:, g * hp:(g + 1) * hp] = st * ecum_e[q - 1:q, g * hp:(g + 1) * hp] + upd

    y = jnp.concatenate(y_groups, axis=1) + d_ref[...] * xs
    y = y * _silu(z_ref[...].astype(F32))
    outs = []
    for g in range(SSD_GROUPS):
        yg = y[:, g * hp:(g + 1) * hp]
        outs.append(yg * lax.rsqrt(jnp.mean(yg * yg, axis=-1, keepdims=True) + RMS_EPS))
    y_ref[...] = (jnp.concatenate(outs, axis=1) * nw_ref[...]).astype(y_ref.dtype)


def _ssd_mixer(proj, misc, conv_w, conv_b, dt_bias, a_log, d_skip, norm_w):
    bn, sn, _ = proj.shape
    w = SSD_HEADS * SSD_HEAD_DIM
    pad = lambda v: jnp.pad(v, (0, LANES - v.shape[0])).reshape(1, LANES)
    a_log_p = jnp.pad(a_log, (0, LANES - SSD_HEADS), constant_values=-jnp.inf).reshape(1, LANES)
    expand = jnp.pad(jnp.repeat(jnp.eye(SSD_HEADS, dtype=BF16), SSD_HEAD_DIM, axis=1),
                     ((0, LANES - SSD_HEADS), (0, 0)))
    row = lambda v: v.reshape(1, -1)
    const = lambda shape: pl.BlockSpec(shape, lambda b, c: (0, 0))
    blk = lambda width, idx: pl.BlockSpec((None, SSD_Q, width), lambda b, c: (b, c, idx))
    return pl.pallas_call(
        _ssd_kernel,
        grid=(bn, sn // SSD_Q),
        in_specs=[blk(w, COL_Z // w), blk(w, COL_XS // w), blk(1024, COL_BC // 1024), blk(1024, 0),
                  const((SSD_CONV, w)), const((SSD_CONV, 2 * SSD_BC)), const((1, w)), const((1, 2 * SSD_BC)),
                  const((1, LANES)), const((1, LANES)), const((1, w)), const((1, w)), const((LANES, w))],
        out_specs=pl.BlockSpec((None, SSD_Q, w), lambda b, c: (b, c, 0)),
        out_shape=jax.ShapeDtypeStruct((bn, sn, w), BF16),
        scratch_shapes=[pltpu.VMEM((SSD_STATE, w), F32),
                        pltpu.VMEM((SSD_Q + SUBLANES, w), F32),
                        pltpu.VMEM((SSD_Q + SUBLANES, 2 * SSD_BC), F32)],
        compiler_params=_cparams(("parallel", "arbitrary")),
        name="ssd_mixer",
    )(proj, proj, proj, misc, conv_w[:, :w], conv_w[:, w:], row(conv_b[:w]), row(conv_b[w:]),
      pad(dt_bias), a_log_p, row(jnp.repeat(d_skip, SSD_HEAD_DIM)), row(norm_w), expand)


def _dot_hi(a, b):
    return jnp.dot(a, b, precision=lax.Precision.HIGHEST, preferred_element_type=F32)


def _split2(a):
    hi = a.astype(BF16)
    return hi, (a - hi.astype(F32)).astype(BF16)


def _dot_3pass(a, b_hi, b_lo):
    a_hi, a_lo = _split2(a)
    return _dot(a_hi, b_hi) + (_dot(a_hi, b_lo) + _dot(a_lo, b_hi))


def _head_sums(x, ones_blk):
    outs = []
    for s in range(x.shape[1] // LANES):
        outs.append(_dot(x[:, s * LANES:(s + 1) * LANES].astype(BF16), ones_blk))
    return jnp.concatenate(outs, axis=1)


def _stack_heads(x, lo_half):
    zero = jnp.zeros_like(x)
    return jnp.concatenate([jnp.where(lo_half, x, zero), jnp.where(lo_half, zero, x)], axis=0)


RW_PAIRS = RWKV_HEADS // 2
RW_GROUP = 16


def _rwkv_kernel(has_vres, *refs):
    if has_vres:
        (r_ref, k_ref, v_ref, misc_ref, vfirst_ref, mur_ref, muk_ref, muv_ref, mum_ref, w0_ref, w2h_ref, w2l_ref,
         a0_ref, a2_ref, g2_ref, kk_ref, ka_ref, rk_ref, lnw_ref, lnb_ref, v0_ref, v2_ref,
         y_ref, state_ref, carry_ref, carrym_ref, st_ref, yp_ref, pt_ref) = refs
    else:
        (r_ref, k_ref, v_ref, misc_ref, mur_ref, muk_ref, muv_ref, mum_ref, w0_ref, w2h_ref, w2l_ref, a0_ref,
         a2_ref, g2_ref, kk_ref, ka_ref, rk_ref, lnw_ref, lnb_ref,
         y_ref, vout_ref, state_ref, carry_ref, carrym_ref, st_ref, yp_ref, pt_ref) = refs
    tb, t = RW_BLK, RW_T
    w = RWKV_HEADS * RWKV_HEAD

    @pl.when(pl.program_id(1) == 0)
    def _():
        state_ref[...] = jnp.zeros_like(state_ref)
        carry_ref[...] = jnp.zeros_like(carry_ref)
        carrym_ref[...] = jnp.zeros_like(carrym_ref)

    first_row = lax.broadcasted_iota(jnp.int32, (tb, 1), 0) == 0

    def shift(p, carry_row, mu):
        prev = jnp.where(first_row, carry_row, pltpu.roll(p, 1, 0))
        return p + (prev - p) * mu

    rp, kp, vp, mp = r_ref[...].astype(F32), k_ref[...].astype(F32), v_ref[...].astype(F32), misc_ref[...]
    r = shift(rp, carry_ref[0:1, :], mur_ref[...])
    k = shift(kp, carry_ref[1:2, :], muk_ref[...])
    v = shift(vp, carry_ref[2:3, :], muv_ref[...])
    m = shift(mp, carrym_ref[0:1, :], mum_ref[...])
    carry_ref[0:1, :] = rp[tb - 1:tb, :]
    carry_ref[1:2, :] = kp[tb - 1:tb, :]
    carry_ref[2:3, :] = vp[tb - 1:tb, :]
    carrym_ref[0:1, :] = mp[tb - 1:tb, :]

    w_lo = m[:, MISC_W:MISC_W + LANES]
    a_lo = m[:, MISC_A:MISC_A + LANES]
    g_lo = m[:, MISC_G:MISC_G + LORA_G]
    log_w = -_softplus(-(w0_ref[...] + _dot_3pass(jnp.tanh(w_lo), w2h_ref[...], w2l_ref[...]))) - 0.5
    lw = -jnp.exp(log_w)
    a = _sigmoid(a0_ref[...] + _dot(a_lo.astype(BF16), a2_ref[...]))
    g = _dot(_sigmoid(g_lo).astype(BF16), g2_ref[...])
    if has_vres:
        v_lo = m[:, MISC_V:MISC_V + LANES]
        v = v + (vfirst_ref[...] - v) * _sigmoid(v0_ref[...] + _dot(v_lo.astype(BF16), v2_ref[...]))
    else:
        vout_ref[...] = v

    lane = lax.broadcasted_iota(jnp.int32, (LANES, LANES), 1)
    rowi = lax.broadcasted_iota(jnp.int32, (LANES, LANES), 0)
    ones_blk = jnp.where((lane // RWKV_HEAD) == (rowi // RWKV_HEAD), 1.0, 0.0).astype(BF16)

    kk = k * kk_ref[...]
    kk = kk / jnp.maximum(jnp.sqrt(_head_sums(kk * kk, ones_blk)), 1e-12)
    k = k * (1.0 + (a - 1.0) * ka_ref[...])
    b = kk * a

    ti = lax.broadcasted_iota(jnp.int32, (t, t), 0)
    tj = lax.broadcasted_iota(jnp.int32, (t, t), 1)
    tri = jnp.where(ti >= tj, 1.0, 0.0).astype(BF16)

    for c in range(tb // t):
        sl = slice(c * t, (c + 1) * t)
        lw_c = lw[sl]
        cl = _dot_exact_rhs(tri, lw_c)
        cl_end = cl[t - 1:t, :]
        e_neg = jnp.exp(-cl)
        e_end = jnp.exp(cl_end - cl)
        ops = (kk[sl] * jnp.exp(cl - lw_c), r[sl] * jnp.exp(cl), k[sl] * e_neg, b[sl] * e_neg,
               v[sl], k[sl] * e_end, b[sl] * e_end)
        for pi in range(RW_PAIRS):
            ls = slice(pi * LANES, (pi + 1) * LANES)
            for oi, op in enumerate(ops):
                st_ref[c, oi, pi] = op[:, ls]
            pt_ref[c, pi] = jnp.broadcast_to(jnp.exp(cl_end[:, ls]), (SUBLANES, LANES))

    lo_half = lax.broadcasted_iota(jnp.int32, (t, LANES), 1) < RWKV_HEAD
    bi = lax.broadcasted_iota(jnp.int32, (2 * t, 2 * t), 0) % t
    bj = lax.broadcasted_iota(jnp.int32, (2 * t, 2 * t), 1) % t
    strict = bi > bj
    incl = bi >= bj

    def chunk_group(c, pis, hts):
        h2 = 2 * t
        stk = [[_stack_heads(st_ref[c, oi, pi], lo_half) for oi in range(7)] for pi in pis]
        lhs2 = [jnp.concatenate([s[0], s[1]], axis=0).astype(BF16) for s in stk]
        rhs2 = [jnp.concatenate([s[2], s[3]], axis=0).astype(BF16) for s in stk]
        amat = [_dot_nt(a, b) for a, b in zip(lhs2, rhs2)]
        sh = [_dot_nt(a, h.astype(BF16)) for a, h in zip(lhs2, hts)]
        vsb = [s[4].astype(BF16) for s in stk]
        x = [s_[0:h2] + _dot(jnp.where(strict, am[0:h2, 0:h2], 0.0).astype(BF16), v_)
             for s_, am, v_ in zip(sh, amat, vsb)]
        pw = [jnp.where(strict, am[0:h2, h2:], 0.0) for am in amat]
        n, sign = 1, -1.0
        while n < t:
            if 2 * n < t:
                res = [_dot(p_.astype(BF16), jnp.concatenate([p_, x_], axis=1).astype(BF16)) for p_, x_ in zip(pw, x)]
                x = [x_ + sign * r_[:, h2:] for x_, r_ in zip(x, res)]
                pw = [r_[:, 0:h2] for r_ in res]
            else:
                x = [x_ + sign * _dot(p_.astype(BF16), x_.astype(BF16)) for p_, x_ in zip(pw, x)]
            n, sign = 2 * n, 1.0
        new_hts = []
        for i, pi in enumerate(pis):
            am, s = amat[i], stk[i]
            a_r = jnp.concatenate([jnp.where(incl, am[h2:, 0:h2], 0.0), jnp.where(incl, -am[h2:, h2:], 0.0)], axis=1)
            ys = sh[i][h2:] + _dot(a_r.astype(BF16), jnp.concatenate([s[4], x[i]], axis=0).astype(BF16))
            yp_ref[pi, c * t:(c + 1) * t, :] = ys[0:t] + ys[t:]
            lhs3 = jnp.concatenate([s[4].T, -(x[i].T)], axis=1).astype(BF16)
            rhs3 = jnp.concatenate([s[5], s[6]], axis=0).astype(BF16)
            new_hts.append(hts[i] * pt_ref[c, pi][0:1, :] + _dot(lhs3, rhs3))
        return new_hts

    for g0 in range(0, RW_PAIRS, RW_GROUP):
        pis = list(range(g0, g0 + RW_GROUP))
        hts = [state_ref[pi] for pi in pis]
        for c in range(tb // t):
            hts = chunk_group(c, pis, hts)
        for pi, ht in zip(pis, hts):
            state_ref[pi] = ht

    y = jnp.concatenate([yp_ref[pi] for pi in range(RW_PAIRS)], axis=1)
    inv_n = 1.0 / RWKV_HEAD
    mu = _head_sums(y, ones_blk) * inv_n
    yc = y - mu
    var = _head_sums(yc * yc, ones_blk) * inv_n
    y = yc * lax.rsqrt(var + RWKV_GN_EPS) * lnw_ref[...] + lnb_ref[...]
    bonus = _head_sums(r * k * rk_ref[...], ones_blk) * v
    y_ref[...] = ((y + bonus) * g).astype(y_ref.dtype)


def _rwkv_mixer(proj, misc, v_first, mu, w0, w2, a0, a2, g2, k_k, k_a, r_k, ln_w, ln_b, v0, v2):
    bn, sn, _ = proj.shape
    w = RWKV_HEADS * RWKV_HEAD
    has_vres = v_first is not None
    row = lambda x: x.reshape(1, -1)
    padrows = lambda x: jnp.pad(x, ((0, LANES - x.shape[0]), (0, 0)))
    padl = lambda x, n: jnp.pad(x, (0, n - x.shape[0]))
    mu_misc = [padl(mu[3 * w:3 * w + LORA_W], LANES), padl(mu[3 * w + LORA_W:3 * w + LORA_W + LORA_A], LANES),
               mu[3 * w + LORA_W + LORA_A:3 * w + LORA_W + LORA_A + LORA_G]]
    if has_vres:
        mu_misc.append(padl(mu[3 * w + LORA_W + LORA_A + LORA_G:], LANES))
    mu_m = padl(jnp.concatenate(mu_misc), 1024)
    const = lambda shape: pl.BlockSpec(shape, lambda b, c: (0,) * len(shape))
    blk = lambda width, idx: pl.BlockSpec((None, RW_BLK, width), lambda b, c: (b, c, idx))
    seq = pl.BlockSpec((None, RW_BLK, w), lambda b, c: (b, c, 0))
    in_specs = [blk(w, COL_R // w), blk(w, COL_R // w + 1), blk(w, COL_R // w + 2), blk(1024, 0)]
    args = [proj, proj, proj, misc]
    if has_vres:
        in_specs.append(seq)
        args.append(v_first)
    in_specs += [const((1, w))] * 3 + [const((1, 1024)), const((1, w)), const((LANES, w)), const((LANES, w)),
                                       const((1, w)), const((LANES, w)), const((LORA_G, w))] + [const((1, w))] * 5
    w2p = padrows(w2)
    w2_hi = w2p.astype(BF16)
    w2_lo = (w2p - w2_hi.astype(F32)).astype(BF16)
    args += [row(mu[:w]), row(mu[w:2 * w]), row(mu[2 * w:3 * w]), row(mu_m), row(w0), w2_hi, w2_lo, row(a0),
             padrows(a2).astype(BF16), g2.astype(BF16), row(k_k), row(k_a), row(r_k), row(ln_w), row(ln_b)]
    if has_vres:
        in_specs += [const((1, w)), const((LANES, w))]
        args += [row(v0), padrows(v2).astype(BF16)]
    out_shape = [jax.ShapeDtypeStruct((bn, sn, w), BF16)]
    out_specs = [seq]
    if not has_vres:
        out_shape.append(jax.ShapeDtypeStruct((bn, sn, w), F32))
        out_specs.append(seq)
    nc = RW_BLK // RW_T
    outs = pl.pallas_call(
        functools.partial(_rwkv_kernel, has_vres),
        grid=(bn, sn // RW_BLK),
        in_specs=in_specs,
        out_specs=out_specs,
        out_shape=out_shape,
        scratch_shapes=[pltpu.VMEM((RW_PAIRS, LANES, LANES), F32),
                        pltpu.VMEM((SUBLANES, w), F32),
                        pltpu.VMEM((SUBLANES, 1024), F32),
                        pltpu.VMEM((nc, 7, RW_PAIRS, RW_T, LANES), F32),
                        pltpu.VMEM((RW_PAIRS, RW_BLK, LANES), F32),
                        pltpu.VMEM((nc, RW_PAIRS, SUBLANES, LANES), F32)],
        compiler_params=_cparams(("parallel", "arbitrary")),
        name="rwkv7_mixer",
    )(*args)
    if has_vres:
        return outs[0], v_first
    return outs[0], outs[1]


MLA_TM = 256
ATT_TQ = 4096
ATT_TK = 2048
ATT_SUB = 512
HEAD_Q = 2 * LANES


def _rope_half(x2, cos, sin):
    return x2 * cos + pltpu.roll(x2, MLA_ROPE, 1) * sin


def _mla_prep_kernel(qlat_ref, kvlat_ref, misc_ref, cos_ref, sin_ref, qnw_ref, kvnw_ref, wq_ref, wkt_ref, wv_ref,
                     q_ref, kt_ref, v_ref, kpet_ref):
    def rms(x, w):
        x = x.astype(F32)
        return (x * lax.rsqrt(jnp.mean(x * x, axis=-1, keepdims=True) + RMS_EPS) * w).astype(BF16)

    cos, sin = cos_ref[...], sin_ref[...]
    scale = MLA_QK ** -0.5
    q = _dot(rms(qlat_ref[...], qnw_ref[...]), wq_ref[...])
    for h in range(MLA_HEADS):
        lo = h * HEAD_Q
        q_ref[:, lo:lo + LANES] = (q[:, lo:lo + LANES] * scale).astype(BF16)
        q_ref[:, lo + LANES:lo + HEAD_Q] = (_rope_half(q[:, lo + LANES:lo + HEAD_Q], cos, sin) * scale).astype(BF16)
    kvn = rms(kvlat_ref[...], kvnw_ref[...])
    kt_ref[...] = _dot_nt(wkt_ref[...], kvn).astype(BF16)
    v_ref[...] = _dot(kvn, wv_ref[...]).astype(BF16)
    kpet_ref[...] = _rope_half(misc_ref[:, MISC_KPE:MISC_KPE + LANES], cos, sin).T.astype(BF16)


def _flash_kernel(tq, tk, sub, qi_ref, ki_ref, q_ref, kt_ref, kpet_ref, v_ref, o_ref, m_ref, acc_ref):
    step = pl.program_id(2)
    qi, ki = qi_ref[step], ki_ref[step]
    ratio = tq // tk

    @pl.when(ki == 0)
    def _():
        m_ref[...] = jnp.full_like(m_ref, -jnp.inf)
        acc_ref[...] = jnp.zeros_like(acc_ref)

    def update(diag):
        kcat_t = jnp.concatenate([kt_ref[...], kpet_ref[...]], axis=0)
        vcat = jnp.concatenate([v_ref[...], jnp.ones((tk, LANES), BF16)], axis=1)
        plan = []
        for r in range(tq // sub):
            row_lo, row_hi = r * sub, (r + 1) * sub - 1
            ncols = tk
            masked = False
            if diag is not None:
                col_lo = diag * tk
                if row_hi < col_lo:
                    continue
                ncols = min(tk, -(-(row_hi - col_lo + 1) // HEAD_Q) * HEAD_Q)
                masked = row_lo < col_lo + ncols - 1
            plan.append((row_lo, ncols, masked))

        def scores(row_lo, ncols, masked):
            s = _dot(q_ref[row_lo:row_lo + sub, :], kcat_t[:, :ncols])
            if masked:
                ri = row_lo + lax.broadcasted_iota(jnp.int32, s.shape, 0)
                ci = diag * tk + lax.broadcasted_iota(jnp.int32, s.shape, 1)
                s = jnp.where(ci <= ri, s, -jnp.inf)
            return s

        def absorb(row_lo, ncols, s):
            rows = slice(row_lo, row_lo + sub)
            tiles = [s[:, j * LANES:(j + 1) * LANES] for j in range(ncols // LANES)]
            fold = tiles[0]
            for tl in tiles[1:]:
                fold = jnp.maximum(fold, tl)
            m_prev = m_ref[rows, :]
            m_new = jnp.maximum(m_prev, jnp.max(fold, axis=-1, keepdims=True))
            p = jnp.concatenate([jnp.exp((tl - m_new).astype(BF16)) for tl in tiles], axis=1)
            alpha = jnp.exp(m_prev - m_new)
            acc_ref[rows, :] = (acc_ref[rows, :] * jnp.concatenate([alpha, alpha], axis=1)
                                + _dot(p, vcat[:ncols]))
            m_ref[rows, :] = m_new

        s_next = scores(*plan[0])
        for idx, (row_lo, ncols, _) in enumerate(plan):
            s_cur = s_next
            if idx + 1 < len(plan):
                s_next = scores(*plan[idx + 1])
            absorb(row_lo, ncols, s_cur)

    @pl.when(ki < qi * ratio)
    def _():
        update(None)

    for d in range(ratio):
        @pl.when(ki == qi * ratio + d)
        def _(d=d):
            update(d)

    @pl.when(ki == (qi + 1) * ratio - 1)
    def _():
        o_ref[...] = (acc_ref[:, 0:MLA_V] / acc_ref[:, MLA_V:]).astype(o_ref.dtype)


def _mla_mixer(proj, misc, cos, sin, q_norm_w, w_q_b, kv_norm_w, w_kv_b):
    bn, sn, _ = proj.shape
    t = bn * sn
    proj2 = proj.reshape(t, proj.shape[-1])
    misc2 = misc.reshape(t, misc.shape[-1])
    wq = w_q_b.reshape(MLA_RANK, MLA_HEADS, MLA_QK)
    pe = wq[..., MLA_NOPE:]
    rot = jnp.concatenate([-pe[..., MLA_ROPE // 2:], pe[..., :MLA_ROPE // 2]], axis=-1)
    wq = jnp.concatenate([wq, rot], axis=-1).reshape(MLA_RANK, MLA_HEADS * HEAD_Q).astype(BF16)
    wkv = w_kv_b.reshape(MLA_RANK, MLA_HEADS, MLA_NOPE + MLA_V)
    wkt = wkv[..., :MLA_NOPE].reshape(MLA_RANK, -1).T.astype(BF16)
    wv = wkv[..., MLA_NOPE:].reshape(MLA_RANK, -1).astype(BF16)
    nq, nk, nv = MLA_HEADS * HEAD_Q, MLA_HEADS * MLA_NOPE, MLA_HEADS * MLA_V
    tm = min(MLA_TM, sn)
    nsb = sn // tm
    rowblk = lambda width, idx: pl.BlockSpec((tm, width), lambda i: (i, idx))
    const = lambda shape: pl.BlockSpec(shape, lambda i: (0, 0))
    colblk = lambda rows: pl.BlockSpec((None, rows, tm), lambda i: (i // nsb, 0, i % nsb))
    q, kt, v, kpet = pl.pallas_call(
        _mla_prep_kernel,
        grid=(t // tm,),
        in_specs=[rowblk(MLA_RANK, COL_QLAT // MLA_RANK), rowblk(MLA_RANK, COL_KVLAT // MLA_RANK),
                  rowblk(1024, 0), rowblk(LANES, 0), rowblk(LANES, 0),
                  const((1, MLA_RANK)), const((1, MLA_RANK)), const((MLA_RANK, nq)), const((nk, MLA_RANK)),
                  const((MLA_RANK, nv))],
        out_specs=[rowblk(nq, 0), colblk(nk), rowblk(nv, 0), colblk(LANES)],
        out_shape=[jax.ShapeDtypeStruct((t, nq), BF16), jax.ShapeDtypeStruct((bn, nk, sn), BF16),
                   jax.ShapeDtypeStruct((t, nv), BF16), jax.ShapeDtypeStruct((bn, LANES, sn), BF16)],
        compiler_params=_cparams(("parallel",)),
        name="mla_prep",
    )(proj2, proj2, misc2, cos, sin, q_norm_w.reshape(1, -1), kv_norm_w.reshape(1, -1), wq, wkt, wv)

    tq, tk = min(ATT_TQ, sn), min(ATT_TK, sn)
    sub = min(ATT_SUB, tq)
    ratio = tq // tk
    pairs = [(a, b) for a in range(sn // tq) for b in range((a + 1) * ratio)]
    qi_arr = jnp.asarray([p[0] for p in pairs], jnp.int32)
    ki_arr = jnp.asarray([p[1] for p in pairs], jnp.int32)
    out = pl.pallas_call(
        functools.partial(_flash_kernel, tq, tk, sub),
        grid_spec=pltpu.PrefetchScalarGridSpec(
            num_scalar_prefetch=2,
            grid=(bn, MLA_HEADS, len(pairs)),
            in_specs=[pl.BlockSpec((None, tq, HEAD_Q), lambda b, h, s, qi, ki: (b, qi[s], h)),
                      pl.BlockSpec((None, MLA_NOPE, tk), lambda b, h, s, qi, ki: (b, h, ki[s])),
                      pl.BlockSpec((None, LANES, tk), lambda b, h, s, qi, ki: (b, 0, ki[s])),
                      pl.BlockSpec((None, tk, MLA_V), lambda b, h, s, qi, ki: (b, ki[s], h))],
            out_specs=pl.BlockSpec((None, tq, MLA_V), lambda b, h, s, qi, ki: (b, qi[s], h)),
            scratch_shapes=[pltpu.VMEM((tq, LANES), F32), pltpu.VMEM((tq, 2 * MLA_V), F32)]),
        out_shape=jax.ShapeDtypeStruct((bn, sn, MLA_HEADS * MLA_V), BF16),
        compiler_params=_cparams(("parallel", "parallel", "arbitrary")),
        name="mla_attention",
    )(qi_arr, ki_arr, q.reshape(bn, sn, nq), kt, kpet, v.reshape(bn, sn, nv))
    return out


MERGE_TM = 256


def _merge_kernel(g0_ref, g1_ref, g2_ref, ys_ref, yr_ref, ym_ref, x_ref, wo_ref, lnw_ref, lnb_ref, xo_ref, xb_ref):
    gate = lambda g_ref, y_ref: _sigmoid(g_ref[...].astype(F32)) * y_ref[...].astype(F32)
    merged = gate(g0_ref, ys_ref) + gate(g1_ref, yr_ref) + gate(g2_ref, ym_ref)
    h = ALPHA * x_ref[...] + _dot(merged.astype(BF16), wo_ref[...])
    y = _layernorm(h, lnw_ref[...], lnb_ref[...])
    xo_ref[...] = y
    xb_ref[...] = y.astype(BF16)


def _merge_out(proj2, y_ssd, y_rwkv, y_mla, x, w_out, ln_w, ln_b):
    t, d = x.shape
    tm = min(MERGE_TM, t)
    rowblk = lambda idx: pl.BlockSpec((tm, d), lambda i: (i, idx))
    const = lambda shape: pl.BlockSpec(shape, lambda i: (0, 0))
    return pl.pallas_call(
        _merge_kernel,
        grid=(t // tm,),
        in_specs=[rowblk(0), rowblk(1), rowblk(2), rowblk(0), rowblk(0), rowblk(0), rowblk(0),
                  const((d, d)), const((1, d)), const((1, d))],
        out_specs=[rowblk(0), rowblk(0)],
        out_shape=[jax.ShapeDtypeStruct((t, d), F32), jax.ShapeDtypeStruct((t, d), BF16)],
        compiler_params=_cparams(("parallel",)),
        name="merge_out_ln1",
    )(proj2, proj2, proj2, y_ssd, y_rwkv, y_mla, x, w_out.astype(BF16), ln_w.reshape(1, d), ln_b.reshape(1, d))


FFN_TM = 512
FFN_TF = 512


def _ffn_kernel(xb_ref, x_ref, w1_ref, w3_ref, w2_ref, lnw_ref, lnb_ref, o_ref, ob_ref, acc_ref):
    f = pl.program_id(1)

    @pl.when(f == 0)
    def _():
        acc_ref[...] = jnp.zeros_like(acc_ref)

    xb = xb_ref[...]
    h = _silu(_dot(xb, w1_ref[...])) * _dot(xb, w3_ref[...])
    acc_ref[...] += _dot(h.astype(BF16), w2_ref[...])

    @pl.when(f == pl.num_programs(1) - 1)
    def _():
        y = _layernorm(ALPHA * x_ref[...] + acc_ref[...], lnw_ref[...], lnb_ref[...])
        o_ref[...] = y
        ob_ref[...] = y.astype(BF16)


def _ffn_dense(xb, x, w1, w3, w2, ln_w, ln_b):
    t, d = x.shape
    ff = w1.shape[1]
    tm, tf = min(FFN_TM, t), FFN_TF
    return pl.pallas_call(
        _ffn_kernel,
        grid=(t // tm, ff // tf),
        in_specs=[pl.BlockSpec((tm, d), lambda i, f: (i, 0)), pl.BlockSpec((tm, d), lambda i, f: (i, 0)),
                  pl.BlockSpec((d, tf), lambda i, f: (0, f)), pl.BlockSpec((d, tf), lambda i, f: (0, f)),
                  pl.BlockSpec((tf, d), lambda i, f: (f, 0)),
                  pl.BlockSpec((1, d), lambda i, f: (0, 0)), pl.BlockSpec((1, d), lambda i, f: (0, 0))],
        out_specs=[pl.BlockSpec((tm, d), lambda i, f: (i, 0)), pl.BlockSpec((tm, d), lambda i, f: (i, 0))],
        out_shape=[jax.ShapeDtypeStruct((t, d), F32), jax.ShapeDtypeStruct((t, d), BF16)],
        scratch_shapes=[pltpu.VMEM((tm, d), F32)],
        compiler_params=_cparams(("parallel", "arbitrary")),
        name="ffn_dense_ln2",
    )(xb, x, w1.astype(BF16), w3.astype(BF16), w2.astype(BF16), ln_w.reshape(1, d), ln_b.reshape(1, d))


ROUTER_TM = 512
MOE_TM = 512
MOE_TF = 512
COMBINE_TM = 256


def _router_kernel(x_ref, wr_ref, idx_ref, wgt_ref, cnt_ref):
    @pl.when(pl.program_id(0) == 0)
    def _():
        cnt_ref[...] = jnp.zeros_like(cnt_ref)

    tm = x_ref.shape[0]
    logits = _dot_hi(x_ref[...], wr_ref[...])
    lane = lax.broadcasted_iota(jnp.int32, logits.shape, 1)
    lg = jnp.where(lane < N_EXPERTS, logits, -jnp.inf)
    m1 = jnp.max(lg, axis=-1, keepdims=True)
    i1 = jnp.min(jnp.where(lg == m1, lane, LANES), axis=-1, keepdims=True)
    lg2 = jnp.where(lane == i1, -jnp.inf, lg)
    m2 = jnp.max(lg2, axis=-1, keepdims=True)
    i2 = jnp.min(jnp.where(lg2 == m2, lane, LANES), axis=-1, keepdims=True)
    e = jnp.exp(m2 - m1)
    wgt_ref[...] = jnp.where(lane == 0, 1.0 / (1.0 + e), jnp.where(lane == 1, e / (1.0 + e), 0.0))

    hit1, hit2 = lane == i1, lane == i2
    both = jnp.where(hit1 | hit2, 1.0, 0.0)
    ri = lax.broadcasted_iota(jnp.int32, (tm, tm), 0)
    ci = lax.broadcasted_iota(jnp.int32, (tm, tm), 1)
    before = cnt_ref[0:1, :] + _dot(jnp.where(ri > ci, 1.0, 0.0).astype(BF16), both.astype(BF16))
    rank1 = jnp.sum(jnp.where(hit1, before, 0.0), axis=-1, keepdims=True).astype(jnp.int32)
    rank2 = jnp.sum(jnp.where(hit2, before, 0.0), axis=-1, keepdims=True).astype(jnp.int32)
    idx_ref[...] = jnp.where(lane == 0, i1, jnp.where(lane == 1, i2, jnp.where(lane == 2, rank1,
                                                                              jnp.where(lane == 3, rank2, 0))))
    cnt_ref[...] = jnp.broadcast_to(before[tm - 1:tm, :] + both[tm - 1:tm, :], cnt_ref.shape)


def _moe_ffn_kernel(nf_static, be_ref, nused_ref, tok_ref, tok_next_ref, x_hbm, w1_ref, w3_ref, w2_ref, o_ref, xg_ref,
                    xb_ref, acc_ref, sem):
    i, f = pl.program_id(0), pl.program_id(1)
    ni, nf = pl.num_programs(0), pl.num_programs(1)
    rows = xg_ref.shape[1]
    per = -(-rows // nf_static)
    tail = rows - (nf_static - 1) * per
    slot = i % 2

    def row_copy(idx_ref, sl, r):
        return pltpu.make_async_copy(x_hbm.at[pl.ds(idx_ref[0, 0, r], 1), :], xg_ref.at[sl, pl.ds(r, 1), :],
                                     sem.at[sl])

    def wait_all(idx_ref, sl):
        for r in range(rows):
            row_copy(idx_ref, sl, r).wait()

    @pl.when(f == 0)
    def _():
        @pl.when(i == 0)
        def _():
            def body(r, c):
                row_copy(tok_ref, slot, r).start()
                return c
            lax.fori_loop(0, rows, body, 0)

        wait_all(tok_ref, slot)
        xb_ref[...] = xg_ref[slot].astype(BF16)
        acc_ref[...] = jnp.zeros_like(acc_ref)

    base = f * per

    def fetch_share():
        for j in range(tail):
            row_copy(tok_next_ref, 1 - slot, base + j).start()

    @pl.when(i < nused_ref[0])
    def _():
        fetch_share()
        xb = xb_ref[...]
        h = _silu(_dot(xb, w1_ref[...])) * _dot(xb, w3_ref[...])
        acc_ref[...] += _dot(h.astype(BF16), w2_ref[...])

    @pl.when(i >= nused_ref[0])
    def _():
        fetch_share()

    @pl.when(f < nf - 1)
    def _():
        for j in range(tail, per):
            row_copy(tok_next_ref, 1 - slot, base + j).start()

    @pl.when(f == nf - 1)
    def _():
        o_ref[...] = acc_ref[...]

        @pl.when(i == ni - 1)
        def _():
            wait_all(tok_next_ref, 1 - slot)


def _combine_kernel(dst_ref, dst_next_ref, y_hbm, wgt_ref, x_ref, lnw_ref, lnb_ref, o_ref, yg_ref, sem):
    i, n = pl.program_id(0), pl.num_programs(0)
    rows = x_ref.shape[0]
    slot = i % 2

    def row_copy(idx_ref, sl, r, k):
        return pltpu.make_async_copy(y_hbm.at[pl.ds(idx_ref[0, 0, TOP_K * r + k], 1), :],
                                     yg_ref.at[sl, k, pl.ds(r, 1), :], sem.at[sl])

    def for_all(idx_ref, sl, op):
        for r in range(rows):
            for k in range(TOP_K):
                op(row_copy(idx_ref, sl, r, k))

    @pl.when(i == 0)
    def _():
        for_all(dst_ref, slot, lambda c: c.start())

    for_all(dst_next_ref, 1 - slot, lambda c: c.start())
    for_all(dst_ref, slot, lambda c: c.wait())
    wgt = wgt_ref[...]
    f = wgt[:, 0:1] * yg_ref[slot, 0] + wgt[:, 1:2] * yg_ref[slot, 1]
    o_ref[...] = _layernorm(ALPHA * x_ref[...] + f, lnw_ref[...], lnb_ref[...])

    @pl.when(i == n - 1)
    def _():
        for_all(dst_next_ref, 1 - slot, lambda c: c.wait())


def _ffn_moe(x, router, w1, w3, w2, ln_w, ln_b):
    t, d = x.shape
    ne, _, ff = w1.shape
    tm = min(ROUTER_TM, t)
    rw = jnp.pad(router, ((0, 0), (0, LANES - ne)))
    idx, wgt, cnt = pl.pallas_call(
        _router_kernel,
        grid=(t // tm,),
        in_specs=[pl.BlockSpec((tm, d), lambda i: (i, 0)), pl.BlockSpec((d, LANES), lambda i: (0, 0))],
        out_specs=[pl.BlockSpec((tm, LANES), lambda i: (i, 0))] * 2 + [pl.BlockSpec((SUBLANES, LANES), lambda i: (0, 0))],
        out_shape=[jax.ShapeDtypeStruct((t, LANES), jnp.int32), jax.ShapeDtypeStruct((t, LANES), F32),
                   jax.ShapeDtypeStruct((SUBLANES, LANES), F32)],
        compiler_params=_cparams(("arbitrary",)),
        name="moe_router",
    )(x, rw)

    blk = MOE_TM
    n_assign = t * TOP_K
    flat_e = idx[:, :TOP_K].reshape(-1)
    rank = idx[:, TOP_K:2 * TOP_K].reshape(-1)
    counts = cnt[0, :ne].astype(jnp.int32)
    padded = (counts + blk - 1) // blk * blk
    pad_end = jnp.cumsum(padded)
    dest = (pad_end - padded)[flat_e] + rank
    n_blocks = -(-(n_assign + ne * (blk - 1)) // blk)
    n_rows = n_blocks * blk
    token_of_row = jnp.zeros((n_rows,), jnp.int32).at[dest].set(jnp.arange(n_assign, dtype=jnp.int32) // TOP_K)
    block_expert = jnp.minimum(jnp.searchsorted(pad_end, jnp.arange(n_blocks, dtype=jnp.int32) * blk, side='right'),
                               ne - 1).astype(jnp.int32)

    tf = MOE_TF
    n_used = (pad_end[-1:] // blk).astype(jnp.int32)
    wtile = lambda i, f, nu: jnp.where(i < nu[0], f, 0)
    yb = pl.pallas_call(
        functools.partial(_moe_ffn_kernel, ff // tf),
        grid_spec=pltpu.PrefetchScalarGridSpec(
            num_scalar_prefetch=2,
            grid=(n_blocks, ff // tf),
            in_specs=[pl.BlockSpec((1, 1, blk), lambda i, f, be, nu: (i, 0, 0), memory_space=pltpu.SMEM),
                      pl.BlockSpec((1, 1, blk), lambda i, f, be, nu: (jnp.minimum(i + 1, n_blocks - 1), 0, 0),
                                   memory_space=pltpu.SMEM),
                      pl.BlockSpec(memory_space=pl.ANY),
                      pl.BlockSpec((None, d, tf), lambda i, f, be, nu: (be[i], 0, wtile(i, f, nu))),
                      pl.BlockSpec((None, d, tf), lambda i, f, be, nu: (be[i], 0, wtile(i, f, nu))),
                      pl.BlockSpec((None, tf, d), lambda i, f, be, nu: (be[i], wtile(i, f, nu), 0))],
            out_specs=pl.BlockSpec((blk, d), lambda i, f, be, nu: (i, 0)),
            scratch_shapes=[pltpu.VMEM((2, blk, d), F32), pltpu.VMEM((blk, d), BF16), pltpu.VMEM((blk, d), F32),
                            pltpu.SemaphoreType.DMA((2,))]),
        out_shape=jax.ShapeDtypeStruct((n_rows, d), F32),
        compiler_params=_cparams(("arbitrary", "arbitrary")),
        name="moe_expert_ffn",
    )(block_expert, n_used, token_of_row.reshape(n_blocks, 1, blk), token_of_row.reshape(n_blocks, 1, blk), x,
      w1.astype(BF16), w3.astype(BF16), w2.astype(BF16))

    tc = min(COMBINE_TM, t)
    return pl.pallas_call(
        _combine_kernel,
        grid=(t // tc,),
        in_specs=[pl.BlockSpec((1, 1, TOP_K * tc), lambda i: (i, 0, 0), memory_space=pltpu.SMEM),
                  pl.BlockSpec((1, 1, TOP_K * tc), lambda i: (jnp.minimum(i + 1, t // tc - 1), 0, 0),
                               memory_space=pltpu.SMEM),
                  pl.BlockSpec(memory_space=pl.ANY),
                  pl.BlockSpec((tc, LANES), lambda i: (i, 0)), pl.BlockSpec((tc, d), lambda i: (i, 0)),
                  pl.BlockSpec((1, d), lambda i: (0, 0)), pl.BlockSpec((1, d), lambda i: (0, 0))],
        out_specs=pl.BlockSpec((tc, d), lambda i: (i, 0)),
        out_shape=jax.ShapeDtypeStruct((t, d), F32),
        scratch_shapes=[pltpu.VMEM((2, TOP_K, tc, d), F32), pltpu.SemaphoreType.DMA((2,))],
        compiler_params=_cparams(("arbitrary",)),
        name="moe_combine_ln2",
    )(dest.reshape(t // tc, 1, TOP_K * tc), dest.reshape(t // tc, 1, TOP_K * tc), yb, wgt, x,
      ln_w.reshape(1, d), ln_b.reshape(1, d))


def _pack_w_in(w, w_vres):
    d = w.shape[0]
    w = w.astype(BF16)
    w_vres = None if w_vres is None else w_vres.astype(BF16)
    o = 0

    def take(n):
        nonlocal o
        s = w[:, o:o + n]
        o += n
        return s

    padc = lambda s, n: jnp.pad(s, ((0, 0), (0, n - s.shape[1])))
    gates = take(3 * W2K)
    z = take(W2K)
    xbc = take(W2K + 2 * SSD_BC)
    dt = take(SSD_HEADS)
    qlat = take(MLA_RANK)
    kvlat = take(MLA_RANK)
    kpe = take(MLA_ROPE)
    rkv = take(3 * W2K)
    w_lo, a_lo, g_lo = take(LORA_W), take(LORA_A), take(LORA_G)
    v_lo = jnp.zeros((d, LANES), w.dtype) if w_vres is None else padc(w_vres, LANES)
    kpe_rot = jnp.concatenate([-kpe[:, MLA_ROPE // 2:], kpe[:, :MLA_ROPE // 2]], axis=1)
    misc = jnp.concatenate([padc(w_lo, LANES), padc(a_lo, LANES), g_lo, v_lo, padc(dt, LANES), kpe, kpe_rot], axis=1)
    return jnp.concatenate([gates, z, rkv, xbc, qlat, kvlat, padc(misc, 1024)], axis=1).astype(BF16)


PROJ_TM = 2048
PROJ_TN = 1024


def kernel(x, positions, w_in, w_in_vres, w_out, ssd_conv_w, ssd_conv_b, ssd_dt_bias, ssd_a_log, ssd_d, ssd_norm_w, rwkv_mu, rwkv_mu_vres, rwkv_w0, rwkv_w2, rwkv_a0, rwkv_a2, rwkv_g2, rwkv_v0, rwkv_v2, rwkv_k_k, rwkv_k_a, rwkv_r_k, rwkv_ln_w, rwkv_ln_b, mla_q_norm_w, mla_w_q_b, mla_kv_norm_w, mla_w_kv_b, ln1_w, ln1_b, ln2_w, ln2_b, ffn_w1, ffn_w3, ffn_w2, moe_router, moe_w1, moe_w3, moe_w2):
    bn, sn, d = x.shape
    t = bn * sn
    cos, sin = _rope_tables(positions)
    xf = x.reshape(t, d)
    xb = xf
    v_first = None
    for l in range(DEPTH):
        if l == 0:
            wp, mu, v0, v2 = _pack_w_in(w_in[l], None), rwkv_mu[l], None, None
        else:
            wp = _pack_w_in(w_in[l], w_in_vres[l - 1])
            mu = jnp.concatenate([rwkv_mu[l], rwkv_mu_vres[l - 1]], axis=0)
            v0, v2 = rwkv_v0[l - 1], rwkv_v2[l - 1]
        tm = min(PROJ_TM // 2 if xb.dtype == F32 else PROJ_TM, t)
        proj2 = _matmul(xb, wp[:, :COL_MISC], BF16, tm, PROJ_TN)
        misc = _matmul(xb, wp[:, COL_MISC:], F32, tm, N_PROJ - COL_MISC).reshape(bn, sn, -1)
        proj = proj2.reshape(bn, sn, COL_MISC)
        y_ssd = _ssd_mixer(proj, misc, ssd_conv_w[l], ssd_conv_b[l], ssd_dt_bias[l], ssd_a_log[l], ssd_d[l],
                           ssd_norm_w[l])
        y_rwkv, v_first = _rwkv_mixer(proj, misc, v_first, mu, rwkv_w0[l], rwkv_w2[l], rwkv_a0[l], rwkv_a2[l],
                                      rwkv_g2[l], rwkv_k_k[l], rwkv_k_a[l], rwkv_r_k[l], rwkv_ln_w[l], rwkv_ln_b[l],
                                      v0, v2)
        y_mla = _mla_mixer(proj, misc, cos, sin, mla_q_norm_w[l], mla_w_q_b[l], mla_kv_norm_w[l], mla_w_kv_b[l])
        x1, x1b = _merge_out(proj2, y_ssd.reshape(t, d), y_rwkv.reshape(t, d), y_mla.reshape(t, d), xf, w_out[l],
                             ln1_w[l], ln1_b[l])
        if l % 2 == 0:
            xf, xb = _ffn_dense(x1b, x1, ffn_w1[l // 2], ffn_w3[l // 2], ffn_w2[l // 2], ln2_w[l], ln2_b[l])
        else:
            xf = _ffn_moe(x1, moe_router[l // 2], moe_w1[l // 2], moe_w3[l // 2], moe_w2[l // 2], ln2_w[l], ln2_b[l])
            xb = xf
    return xf.reshape(bn, sn, d)
```

```python
import functools
import math

import jax
import jax.numpy as jnp
import numpy as np
from jax import lax
from jax.experimental import pallas as pl
from jax.experimental.pallas import tpu as pltpu

F32 = jnp.float32
BF16 = jnp.bfloat16

D_MODEL = 2048
DEPTH = 2
ALPHA = (2 * DEPTH) ** 0.25
LN_EPS = 1e-5
RMS_EPS = 1e-6
SSD_HEADS, SSD_HEAD_DIM, SSD_GROUPS, SSD_STATE, SSD_CONV = 32, 64, 4, 128, 4
SSD_BC = SSD_GROUPS * SSD_STATE
RWKV_HEADS, RWKV_HEAD = 32, 64
RWKV_GN_EPS = 64e-5
LORA_W, LORA_A, LORA_G, LORA_V = 96, 96, 256, 64
MLA_HEADS, MLA_NOPE, MLA_ROPE, MLA_V, MLA_RANK = 16, 128, 64, 128, 512
MLA_QK = MLA_NOPE + MLA_ROPE
ROPE_THETA = 10000.0
D_FF = 5632
N_EXPERTS = 8
TOP_K = 2

LANES = 128
SUBLANES = 8
VMEM_LIMIT = 56 * 1024 * 1024

W2K = 2048
COL_GATE = 0
COL_Z = 3 * W2K
COL_R = 4 * W2K
COL_XS = 7 * W2K
COL_BC = 8 * W2K
COL_QLAT = COL_BC + 1024
COL_KVLAT = COL_QLAT + 512
COL_MISC = COL_KVLAT + 512
MISC_W, MISC_A, MISC_G, MISC_V, MISC_DT, MISC_KPE = 0, 128, 256, 512, 640, 768
N_PROJ = COL_MISC + 1024

SSD_Q = 128
RW_T = 64
RW_BLK = 128


def _cparams(sem, vmem=VMEM_LIMIT):
    return pltpu.CompilerParams(dimension_semantics=sem, vmem_limit_bytes=vmem)


def _dot(a, b):
    return jnp.dot(a, b, preferred_element_type=F32)


def _dot_nt(a, b):
    return lax.dot_general(a, b, (((1,), (1,)), ((), ())), preferred_element_type=F32)


def _split3(a):
    hi = a.astype(BF16)
    r1 = a - hi.astype(F32)
    mid = r1.astype(BF16)
    lo = (r1 - mid.astype(F32)).astype(BF16)
    return hi, mid, lo


def _dot_exact_rhs(a_bf, b):
    hi, mid, lo = _split3(b)
    return _dot(a_bf, hi) + _dot(a_bf, mid) + _dot(a_bf, lo)


def _dot_exact_lhs(a, b_bf):
    hi, mid, lo = _split3(a)
    return _dot(hi, b_bf) + _dot(mid, b_bf) + _dot(lo, b_bf)


def _sigmoid(x):
    return 1.0 / (1.0 + jnp.exp(-x))


def _silu(x):
    return x * _sigmoid(x)


def _softplus(x):
    return jnp.maximum(x, 0.0) + jnp.log(1.0 + jnp.exp(-jnp.abs(x)))


def _layernorm(x, w, b):
    mu = jnp.mean(x, axis=-1, keepdims=True)
    xc = x - mu
    var = jnp.mean(xc * xc, axis=-1, keepdims=True)
    return xc * lax.rsqrt(var + LN_EPS) * w + b


def _mm_kernel(x_ref, w_ref, o_ref):
    o_ref[...] = _dot(x_ref[...].astype(BF16), w_ref[...]).astype(o_ref.dtype)


def _matmul(x, w, out_dtype, tm, tn):
    m, k = x.shape
    n = w.shape[1]
    return pl.pallas_call(
        _mm_kernel,
        grid=(m // tm, n // tn),
        in_specs=[pl.BlockSpec((tm, k), lambda i, j: (i, 0)),
                  pl.BlockSpec((k, tn), lambda i, j: (0, j))],
        out_specs=pl.BlockSpec((tm, tn), lambda i, j: (i, j)),
        out_shape=jax.ShapeDtypeStruct((m, n), out_dtype),
        compiler_params=_cparams(("parallel", "arbitrary")),
        name="in_proj",
    )(x, w)


def _rope_kernel(pos_ref, freq_ref, cos_ref, sin_ref):
    ang = pos_ref[...] * freq_ref[...]
    valid = lax.broadcasted_iota(jnp.int32, ang.shape, 1) < MLA_ROPE
    cos_ref[...] = jnp.where(valid, jnp.cos(ang), 0.0)
    sin_ref[...] = jnp.where(valid, jnp.sin(ang), 0.0)


def _rope_tables(positions):
    t = positions.size
    tm = min(t, 1024)
    pos = positions.reshape(t, 1).astype(F32)
    inv_freq = ROPE_THETA ** (-jnp.arange(0, MLA_ROPE, 2, dtype=F32) / MLA_ROPE)
    freq = jnp.concatenate([inv_freq, inv_freq, jnp.zeros((LANES - MLA_ROPE,), F32)]).reshape(1, LANES)
    return pl.pallas_call(
        _rope_kernel,
        grid=(t // tm,),
        in_specs=[pl.BlockSpec((tm, 1), lambda i: (i, 0)),
                  pl.BlockSpec((1, LANES), lambda i: (0, 0))],
        out_specs=[pl.BlockSpec((tm, LANES), lambda i: (i, 0))] * 2,
        out_shape=[jax.ShapeDtypeStruct((t, LANES), F32)] * 2,
        compiler_params=_cparams(("parallel",)),
        name="rope_tables",
    )(pos, freq)


def _ssd_kernel(z_ref, xs_ref, bc_ref, misc_ref, cwx_ref, cwb_ref, cbx_ref, cbb_ref, dtb_ref, alog_ref,
                d_ref, nw_ref, e_ref, y_ref, state_ref, bufx_ref, bufb_ref):
    q = SSD_Q
    hp = SSD_HEADS // SSD_GROUPS * SSD_HEAD_DIM

    @pl.when(pl.program_id(1) == 0)
    def _():
        state_ref[...] = jnp.zeros_like(state_ref)
        bufx_ref[0:SUBLANES, :] = jnp.zeros((SUBLANES, bufx_ref.shape[1]), F32)
        bufb_ref[0:SUBLANES, :] = jnp.zeros((SUBLANES, bufb_ref.shape[1]), F32)

    bufx_ref[SUBLANES:SUBLANES + q, :] = xs_ref[...].astype(F32)
    bufb_ref[SUBLANES:SUBLANES + q, :] = bc_ref[...].astype(F32)

    def conv(buf_ref, w_ref, b_ref):
        acc = b_ref[...] + w_ref[SSD_CONV - 1:SSD_CONV, :] * buf_ref[SUBLANES:SUBLANES + q, :]
        for k in range(SSD_CONV - 1):
            off = SUBLANES - (SSD_CONV - 1) + k
            acc = acc + w_ref[k:k + 1, :] * buf_ref[off:off + q, :]
        return _silu(acc)

    xs = conv(bufx_ref, cwx_ref, cbx_ref)
    bc = conv(bufb_ref, cwb_ref, cbb_ref)
    bufx_ref[0:SUBLANES, :] = bufx_ref[q:q + SUBLANES, :]
    bufb_ref[0:SUBLANES, :] = bufb_ref[q:q + SUBLANES, :]

    dt = _softplus(misc_ref[:, MISC_DT:MISC_DT + LANES] + dtb_ref[...])
    da = dt * (-jnp.exp(alog_ref[...]))
    row = lax.broadcasted_iota(jnp.int32, (q, q), 0)
    col = lax.broadcasted_iota(jnp.int32, (q, q), 1)
    causal = row >= col
    tri = jnp.where(causal, 1.0, 0.0).astype(BF16)
    cum = _dot_exact_rhs(tri, da)
    cum_t = cum.T
    ecum = jnp.exp(cum)
    toend = jnp.exp(cum[q - 1:q, :] - cum)
    e_mat = e_ref[...]
    def spread(a):
        hi, lo = _split2(a)
        return _dot(hi, e_mat) + _dot(lo, e_mat)

    dt_e, ecum_e, toend_e = spread(dt), spread(ecum), spread(toend)

    xdt = xs * dt_e
    xdt_b = xdt.astype(BF16)
    xw_b = (xdt * toend_e).astype(BF16)
    lane = lax.broadcasted_iota(jnp.int32, (q, LANES), 1)
    lo_half = lane < SSD_HEAD_DIM

    y_groups = []
    for g in range(SSD_GROUPS):
        b_g = bc[:, g * SSD_STATE:(g + 1) * SSD_STATE]
        c_g = bc[:, SSD_BC + g * SSD_STATE:SSD_BC + (g + 1) * SSD_STATE]
        b_gb = b_g.astype(BF16)
        c_gb = c_g.astype(BF16)
        cb = _dot_nt(c_gb, b_gb)
        st = state_ref[:, g * hp:(g + 1) * hp]
        y_inter = _dot(c_gb, st.astype(BF16)) * ecum_e[:, g * hp:(g + 1) * hp]
        parts = []
        for pr in range(hp // LANES):
            ms = []
            for e in range(2):
                h = g * (SSD_HEADS // SSD_GROUPS) + pr * 2 + e
                ci = jnp.broadcast_to(cum[:, h:h + 1], (q, q))
                cj = jnp.broadcast_to(cum_t[h:h + 1, :], (q, q))
                dec = jnp.exp(jnp.where(causal, ci - cj, -jnp.inf))
                ms.append((cb * dec).astype(BF16))
            lo = g * hp + pr * LANES
            xp = xdt_b[:, lo:lo + LANES]
            zero = jnp.zeros_like(xp)
            rhs = jnp.concatenate([jnp.where(lo_half, xp, zero), jnp.where(lo_half, zero, xp)], axis=0)
            parts.append(_dot(jnp.concatenate(ms, axis=1), rhs))
        y_groups.append(jnp.concatenate(parts, axis=1) + y_inter)
        upd = _dot(b_g.T.astype(BF16), xw_b[:, g * hp:(g + 1) * hp])
        state_ref[:, g * hp:(g + 1) * hp] = st * ecum_e[q - 1:q, g * hp:(g + 1) * hp] + upd

    y = jnp.concatenate(y_groups, axis=1) + d_ref[...] * xs
    y = y * _silu(z_ref[...].astype(F32))
    outs = []
    for g in range(SSD_GROUPS):
        yg = y[:, g * hp:(g + 1) * hp]
        outs.append(yg * lax.rsqrt(jnp.mean(yg * yg, axis=-1, keepdims=True) + RMS_EPS))
    y_ref[...] = (jnp.concatenate(outs, axis=1) * nw_ref[...]).astype(y_ref.dtype)


def _ssd_mixer(proj, misc, conv_w, conv_b, dt_bias, a_log, d_skip, norm_w):
    bn, sn, _ = proj.shape
    w = SSD_HEADS * SSD_HEAD_DIM
    pad = lambda v: jnp.pad(v, (0, LANES - v.shape[0])).reshape(1, LANES)
    a_log_p = jnp.pad(a_log, (0, LANES - SSD_HEADS), constant_values=-jnp.inf).reshape(1, LANES)
    expand = jnp.pad(jnp.repeat(jnp.eye(SSD_HEADS, dtype=BF16), SSD_HEAD_DIM, axis=1),
                     ((0, LANES - SSD_HEADS), (0, 0)))
    row = lambda v: v.reshape(1, -1)
    const = lambda shape: pl.BlockSpec(shape, lambda b, c: (0, 0))
    blk = lambda width, idx: pl.BlockSpec((None, SSD_Q, width), lambda b, c: (b, c, idx))
    return pl.pallas_call(
        _ssd_kernel,
        grid=(bn, sn // SSD_Q),
        in_specs=[blk(w, COL_Z // w), blk(w, COL_XS // w), blk(1024, COL_BC // 1024), blk(1024, 0),
                  const((SSD_CONV, w)), const((SSD_CONV, 2 * SSD_BC)), const((1, w)), const((1, 2 * SSD_BC)),
                  const((1, LANES)), const((1, LANES)), const((1, w)), const((1, w)), const((LANES, w))],
        out_specs=pl.BlockSpec((None, SSD_Q, w), lambda b, c: (b, c, 0)),
        out_shape=jax.ShapeDtypeStruct((bn, sn, w), BF16),
        scratch_shapes=[pltpu.VMEM((SSD_STATE, w), F32),
                        pltpu.VMEM((SSD_Q + SUBLANES, w), F32),
                        pltpu.VMEM((SSD_Q + SUBLANES, 2 * SSD_BC), F32)],
        compiler_params=_cparams(("parallel", "arbitrary")),
        name="ssd_mixer",
    )(proj, proj, proj, misc, conv_w[:, :w], conv_w[:, w:], row(conv_b[:w]), row(conv_b[w:]),
      pad(dt_bias), a_log_p, row(jnp.repeat(d_skip, SSD_HEAD_DIM)), row(norm_w), expand)


def _dot_hi(a, b):
    return jnp.dot(a, b, precision=lax.Precision.HIGHEST, preferred_element_type=F32)


def _split2(a):
    hi = a.astype(BF16)
    return hi, (a - hi.astype(F32)).astype(BF16)


def _dot_3pass(a, b_hi, b_lo):
    a_hi, a_lo = _split2(a)
    return _dot(a_hi, b_hi) + (_dot(a_hi, b_lo) + _dot(a_lo, b_hi))


def _head_sums(x, ones_blk):
    outs = []
    for s in range(x.shape[1] // LANES):
        outs.append(_dot(x[:, s * LANES:(s + 1) * LANES].astype(BF16), ones_blk))
    return jnp.concatenate(outs, axis=1)


def _stack_heads(x, lo_half):
    zero = jnp.zeros_like(x)
    return jnp.concatenate([jnp.where(lo_half, x, zero), jnp.where(lo_half, zero, x)], axis=0)


RW_PAIRS = RWKV_HEADS // 2
RW_GROUP = 16


def _rwkv_kernel(has_vres, *refs):
    if has_vres:
        (r_ref, k_ref, v_ref, misc_ref, vfirst_ref, mur_ref, muk_ref, muv_ref, mum_ref, w0_ref, w2h_ref, w2l_ref,
         a0_ref, a2_ref, g2_ref, kk_ref, ka_ref, rk_ref, lnw_ref, lnb_ref, v0_ref, v2_ref,
         y_ref, state_ref, carry_ref, carrym_ref, st_ref, yp_ref, pt_ref) = refs
    else:
        (r_ref, k_ref, v_ref, misc_ref, mur_ref, muk_ref, muv_ref, mum_ref, w0_ref, w2h_ref, w2l_ref, a0_ref,
         a2_ref, g2_ref, kk_ref, ka_ref, rk_ref, lnw_ref, lnb_ref,
         y_ref, vout_ref, state_ref, carry_ref, carrym_ref, st_ref, yp_ref, pt_ref) = refs
    tb, t = RW_BLK, RW_T
    w = RWKV_HEADS * RWKV_HEAD

    @pl.when(pl.program_id(1) == 0)
    def _():
        state_ref[...] = jnp.zeros_like(state_ref)
        carry_ref[...] = jnp.zeros_like(carry_ref)
        carrym_ref[...] = jnp.zeros_like(carrym_ref)

    first_row = lax.broadcasted_iota(jnp.int32, (tb, 1), 0) == 0

    def shift(p, carry_row, mu):
        prev = jnp.where(first_row, carry_row, pltpu.roll(p, 1, 0))
        return p + (prev - p) * mu

    rp, kp, vp, mp = r_ref[...].astype(F32), k_ref[...].astype(F32), v_ref[...].astype(F32), misc_ref[...]
    r = shift(rp, carry_ref[0:1, :], mur_ref[...])
    k = shift(kp, carry_ref[1:2, :], muk_ref[...])
    v = shift(vp, carry_ref[2:3, :], muv_ref[...])
    m = shift(mp, carrym_ref[0:1, :], mum_ref[...])
    carry_ref[0:1, :] = rp[tb - 1:tb, :]
    carry_ref[1:2, :] = kp[tb - 1:tb, :]
    carry_ref[2:3, :] = vp[tb - 1:tb, :]
    carrym_ref[0:1, :] = mp[tb - 1:tb, :]

    w_lo = m[:, MISC_W:MISC_W + LANES]
    a_lo = m[:, MISC_A:MISC_A + LANES]
    g_lo = m[:, MISC_G:MISC_G + LORA_G]
    log_w = -_softplus(-(w0_ref[...] + _dot_3pass(jnp.tanh(w_lo), w2h_ref[...], w2l_ref[...]))) - 0.5
    lw = -jnp.exp(log_w)
    a = _sigmoid(a0_ref[...] + _dot(a_lo.astype(BF16), a2_ref[...]))
    g = _dot(_sigmoid(g_lo).astype(BF16), g2_ref[...])
    if has_vres:
        v_lo = m[:, MISC_V:MISC_V + LANES]
        v = v + (vfirst_ref[...] - v) * _sigmoid(v0_ref[...] + _dot(v_lo.astype(BF16), v2_ref[...]))
    else:
        vout_ref[...] = v

    lane = lax.broadcasted_iota(jnp.int32, (LANES, LANES), 1)
    rowi = lax.broadcasted_iota(jnp.int32, (LANES, LANES), 0)
    ones_blk = jnp.where((lane // RWKV_HEAD) == (rowi // RWKV_HEAD), 1.0, 0.0).astype(BF16)

    kk = k * kk_ref[...]
    kk = kk / jnp.maximum(jnp.sqrt(_head_sums(kk * kk, ones_blk)), 1e-12)
    k = k * (1.0 + (a - 1.0) * ka_ref[...])
    b = kk * a

    ti = lax.broadcasted_iota(jnp.int32, (t, t), 0)
    tj = lax.broadcasted_iota(jnp.int32, (t, t), 1)
    tri = jnp.where(ti >= tj, 1.0, 0.0).astype(BF16)

    for c in range(tb // t):
        sl = slice(c * t, (c + 1) * t)
        lw_c = lw[sl]
        cl = _dot_exact_rhs(tri, lw_c)
        cl_end = cl[t - 1:t, :]
        e_neg = jnp.exp(-cl)
        e_end = jnp.exp(cl_end - cl)
        ops = (kk[sl] * jnp.exp(cl - lw_c), r[sl] * jnp.exp(cl), k[sl] * e_neg, b[sl] * e_neg,
               v[sl], k[sl] * e_end, b[sl] * e_end)
        for pi in range(RW_PAIRS):
            ls = slice(pi * LANES, (pi + 1) * LANES)
            for oi, op in enumerate(ops):
                st_ref[c, oi, pi] = op[:, ls]
            pt_ref[c, pi] = jnp.broadcast_to(jnp.exp(cl_end[:, ls]), (SUBLANES, LANES))

    lo_half = lax.broadcasted_iota(jnp.int32, (t, LANES), 1) < RWKV_HEAD
    bi = lax.broadcasted_iota(jnp.int32, (2 * t, 2 * t), 0) % t
    bj = lax.broadcasted_iota(jnp.int32, (2 * t, 2 * t), 1) % t
    strict = bi > bj
    incl = bi >= bj

    def chunk_group(c, pis, hts):
        h2 = 2 * t
        stk = [[_stack_heads(st_ref[c, oi, pi], lo_half) for oi in range(7)] for pi in pis]
        lhs2 = [jnp.concatenate([s[0], s[1]], axis=0).astype(BF16) for s in stk]
        rhs2 = [jnp.concatenate([s[2], s[3]], axis=0).astype(BF16) for s in stk]
        amat = [_dot_nt(a, b) for a, b in zip(lhs2, rhs2)]
        sh = [_dot_nt(a, h.astype(BF16)) for a, h in zip(lhs2, hts)]
        vsb = [s[4].astype(BF16) for s in stk]
        x = [s_[0:h2] + _dot(jnp.where(strict, am[0:h2, 0:h2], 0.0).astype(BF16), v_)
             for s_, am, v_ in zip(sh, amat, vsb)]
        pw = [jnp.where(strict, am[0:h2, h2:], 0.0) for am in amat]
        n, sign = 1, -1.0
        while n < t:
            if 2 * n < t:
                res = [_dot(p_.astype(BF16), jnp.concatenate([p_, x_], axis=1).astype(BF16)) for p_, x_ in zip(pw, x)]
                x = [x_ + sign * r_[:, h2:] for x_, r_ in zip(x, res)]
                pw = [r_[:, 0:h2] for r_ in res]
            else:
                x = [x_ + sign * _dot(p_.astype(BF16), x_.astype(BF16)) for p_, x_ in zip(pw, x)]
            n, sign = 2 * n, 1.0
        new_hts = []
        for i, pi in enumerate(pis):
            am, s = amat[i], stk[i]
            a_r = jnp.concatenate([jnp.where(incl, am[h2:, 0:h2], 0.0), jnp.where(incl, -am[h2:, h2:], 0.0)], axis=1)
            ys = sh[i][h2:] + _dot(a_r.astype(BF16), jnp.concatenate([s[4], x[i]], axis=0).astype(BF16))
            yp_ref[pi, c * t:(c + 1) * t, :] = ys[0:t] + ys[t:]
            lhs3 = jnp.concatenate([s[4].T, -(x[i].T)], axis=1).astype(BF16)
            rhs3 = jnp.concatenate([s[5], s[6]], axis=0).astype(BF16)
            new_hts.append(hts[i] * pt_ref[c, pi][0:1, :] + _dot(lhs3, rhs3))
        return new_hts

    for g0 in range(0, RW_PAIRS, RW_GROUP):
        pis = list(range(g0, g0 + RW_GROUP))
        hts = [state_ref[pi] for pi in pis]
        for c in range(tb // t):
            hts = chunk_group(c, pis, hts)
        for pi, ht in zip(pis, hts):
            state_ref[pi] = ht

    y = jnp.concatenate([yp_ref[pi] for pi in range(RW_PAIRS)], axis=1)
    inv_n = 1.0 / RWKV_HEAD
    mu = _head_sums(y, ones_blk) * inv_n
    yc = y - mu
    var = _head_sums(yc * yc, ones_blk) * inv_n
    y = yc * lax.rsqrt(var + RWKV_GN_EPS) * lnw_ref[...] + lnb_ref[...]
    bonus = _head_sums(r * k * rk_ref[...], ones_blk) * v
    y_ref[...] = ((y + bonus) * g).astype(y_ref.dtype)


def _rwkv_mixer(proj, misc, v_first, mu, w0, w2, a0, a2, g2, k_k, k_a, r_k, ln_w, ln_b, v0, v2):
    bn, sn, _ = proj.shape
    w = RWKV_HEADS * RWKV_HEAD
    has_vres = v_first is not None
    row = lambda x: x.reshape(1, -1)
    padrows = lambda x: jnp.pad(x, ((0, LANES - x.shape[0]), (0, 0)))
    padl = lambda x, n: jnp.pad(x, (0, n - x.shape[0]))
    mu_misc = [padl(mu[3 * w:3 * w + LORA_W], LANES), padl(mu[3 * w + LORA_W:3 * w + LORA_W + LORA_A], LANES),
               mu[3 * w + LORA_W + LORA_A:3 * w + LORA_W + LORA_A + LORA_G]]
    if has_vres:
        mu_misc.append(padl(mu[3 * w + LORA_W + LORA_A + LORA_G:], LANES))
    mu_m = padl(jnp.concatenate(mu_misc), 1024)
    const = lambda shape: pl.BlockSpec(shape, lambda b, c: (0,) * len(shape))
    blk = lambda width, idx: pl.BlockSpec((None, RW_BLK, width), lambda b, c: (b, c, idx))
    seq = pl.BlockSpec((None, RW_BLK, w), lambda b, c: (b, c, 0))
    in_specs = [blk(w, COL_R // w), blk(w, COL_R // w + 1), blk(w, COL_R // w + 2), blk(1024, 0)]
    args = [proj, proj, proj, misc]
    if has_vres:
        in_specs.append(seq)
        args.append(v_first)
    in_specs += [const((1, w))] * 3 + [const((1, 1024)), const((1, w)), const((LANES, w)), const((LANES, w)),
                                       const((1, w)), const((LANES, w)), const((LORA_G, w))] + [const((1, w))] * 5
    w2p = padrows(w2)
    w2_hi = w2p.astype(BF16)
    w2_lo = (w2p - w2_hi.astype(F32)).astype(BF16)
    args += [row(mu[:w]), row(mu[w:2 * w]), row(mu[2 * w:3 * w]), row(mu_m), row(w0), w2_hi, w2_lo, row(a0),
             padrows(a2).astype(BF16), g2.astype(BF16), row(k_k), row(k_a), row(r_k), row(ln_w), row(ln_b)]
    if has_vres:
        in_specs += [const((1, w)), const((LANES, w))]
        args += [row(v0), padrows(v2).astype(BF16)]
    out_shape = [jax.ShapeDtypeStruct((bn, sn, w), BF16)]
    out_specs = [seq]
    if not has_vres:
        out_shape.append(jax.ShapeDtypeStruct((bn, sn, w), F32))
        out_specs.append(seq)
    nc = RW_BLK // RW_T
    outs = pl.pallas_call(
        functools.partial(_rwkv_kernel, has_vres),
        grid=(bn, sn // RW_BLK),
        in_specs=in_specs,
        out_specs=out_specs,
        out_shape=out_shape,
        scratch_shapes=[pltpu.VMEM((RW_PAIRS, LANES, LANES), F32),
                        pltpu.VMEM((SUBLANES, w), F32),
                        pltpu.VMEM((SUBLANES, 1024), F32),
                        pltpu.VMEM((nc, 7, RW_PAIRS, RW_T, LANES), F32),
                        pltpu.VMEM((RW_PAIRS, RW_BLK, LANES), F32),
                        pltpu.VMEM((nc, RW_PAIRS, SUBLANES, LANES), F32)],
        compiler_params=_cparams(("parallel", "arbitrary")),
        name="rwkv7_mixer",
    )(*args)
    if has_vres:
        return outs[0], v_first
    return outs[0], outs[1]


MLA_TM = 256
ATT_TQ = 4096
ATT_TK = 2048
ATT_SUB = 512
HEAD_Q = 2 * LANES


def _rope_half(x2, cos, sin):
    return x2 * cos + pltpu.roll(x2, MLA_ROPE, 1) * sin


def _mla_prep_kernel(qlat_ref, kvlat_ref, misc_ref, cos_ref, sin_ref, qnw_ref, kvnw_ref, wq_ref, wkt_ref, wv_ref,
                     q_ref, kt_ref, v_ref, kpet_ref):
    def rms(x, w):
        x = x.astype(F32)
        return (x * lax.rsqrt(jnp.mean(x * x, axis=-1, keepdims=True) + RMS_EPS) * w).astype(BF16)

    cos, sin = cos_ref[...], sin_ref[...]
    scale = MLA_QK ** -0.5
    q = _dot(rms(qlat_ref[...], qnw_ref[...]), wq_ref[...])
    for h in range(MLA_HEADS):
        lo = h * HEAD_Q
        q_ref[:, lo:lo + LANES] = (q[:, lo:lo + LANES] * scale).astype(BF16)
        q_ref[:, lo + LANES:lo + HEAD_Q] = (_rope_half(q[:, lo + LANES:lo + HEAD_Q], cos, sin) * scale).astype(BF16)
    kvn = rms(kvlat_ref[...], kvnw_ref[...])
    kt_ref[...] = _dot_nt(wkt_ref[...], kvn).astype(BF16)
    v_ref[...] = _dot(kvn, wv_ref[...]).astype(BF16)
    kpet_ref[...] = _rope_half(misc_ref[:, MISC_KPE:MISC_KPE + LANES], cos, sin).T.astype(BF16)


def _flash_kernel(tq, tk, sub, qi_ref, ki_ref, q_ref, kt_ref, kpet_ref, v_ref, o_ref, m_ref, acc_ref):
    step = pl.program_id(2)
    qi, ki = qi_ref[step], ki_ref[step]
    ratio = tq // tk

    @pl.when(ki == 0)
    def _():
        m_ref[...] = jnp.full_like(m_ref, -jnp.inf)
        acc_ref[...] = jnp.zeros_like(acc_ref)

    def update(diag):
        kcat_t = jnp.concatenate([kt_ref[...], kpet_ref[...]], axis=0)
        vcat = jnp.concatenate([v_ref[...], jnp.ones((tk, LANES), BF16)], axis=1)
        plan = []
        for r in range(tq // sub):
            row_lo, row_hi = r * sub, (r + 1) * sub - 1
            ncols = tk
            masked = False
            if diag is not None:
                col_lo = diag * tk
                if row_hi < col_lo:
                    continue
                ncols = min(tk, -(-(row_hi - col_lo + 1) // HEAD_Q) * HEAD_Q)
                masked = row_lo < col_lo + ncols - 1
            plan.append((row_lo, ncols, masked))

        def scores(row_lo, ncols, masked):
            s = _dot(q_ref[row_lo:row_lo + sub, :], kcat_t[:, :ncols])
            if masked:
                ri = row_lo + lax.broadcasted_iota(jnp.int32, s.shape, 0)
                ci = diag * tk + lax.broadcasted_iota(jnp.int32, s.shape, 1)
                s = jnp.where(ci <= ri, s, -jnp.inf)
            return s

        def absorb(row_lo, ncols, s):
            rows = slice(row_lo, row_lo + sub)
            tiles = [s[:, j * LANES:(j + 1) * LANES] for j in range(ncols // LANES)]
            fold = tiles[0]
            for tl in tiles[1:]:
                fold = jnp.maximum(fold, tl)
            m_prev = m_ref[rows, :]
            m_new = jnp.maximum(m_prev, jnp.max(fold, axis=-1, keepdims=True))
            p = jnp.concatenate([jnp.exp((tl - m_new).astype(BF16)) for tl in tiles], axis=1)
            alpha = jnp.exp(m_prev - m_new)
            acc_ref[rows, :] = (acc_ref[rows, :] * jnp.concatenate([alpha, alpha], axis=1)
                                + _dot(p, vcat[:ncols]))
            m_ref[rows, :] = m_new

        s_next = scores(*plan[0])
        for idx, (row_lo, ncols, _) in enumerate(plan):
            s_cur = s_next
            if idx + 1 < len(plan):
                s_next = scores(*plan[idx + 1])
            absorb(row_lo, ncols, s_cur)

    @pl.when(ki < qi * ratio)
    def _():
        update(None)

    for d in range(ratio):
        @pl.when(ki == qi * ratio + d)
        def _(d=d):
            update(d)

    @pl.when(ki == (qi + 1) * ratio - 1)
    def _():
        o_ref[...] = (acc_ref[:, 0:MLA_V] / acc_ref[:, MLA_V:]).astype(o_ref.dtype)


def _mla_mixer(proj, misc, cos, sin, q_norm_w, w_q_b, kv_norm_w, w_kv_b):
    bn, sn, _ = proj.shape
    t = bn * sn
    proj2 = proj.reshape(t, proj.shape[-1])
    misc2 = misc.reshape(t, misc.shape[-1])
    wq = w_q_b.reshape(MLA_RANK, MLA_HEADS, MLA_QK)
    pe = wq[..., MLA_NOPE:]
    rot = jnp.concatenate([-pe[..., MLA_ROPE // 2:], pe[..., :MLA_ROPE // 2]], axis=-1)
    wq = jnp.concatenate([wq, rot], axis=-1).reshape(MLA_RANK, MLA_HEADS * HEAD_Q).astype(BF16)
    wkv = w_kv_b.reshape(MLA_RANK, MLA_HEADS, MLA_NOPE + MLA_V)
    wkt = wkv[..., :MLA_NOPE].reshape(MLA_RANK, -1).T.astype(BF16)
    wv = wkv[..., MLA_NOPE:].reshape(MLA_RANK, -1).astype(BF16)
    nq, nk, nv = MLA_HEADS * HEAD_Q, MLA_HEADS * MLA_NOPE, MLA_HEADS * MLA_V
    tm = min(MLA_TM, sn)
    nsb = sn // tm
    rowblk = lambda width, idx: pl.BlockSpec((tm, width), lambda i: (i, idx))
    const = lambda shape: pl.BlockSpec(shape, lambda i: (0, 0))
    colblk = lambda rows: pl.BlockSpec((None, rows, tm), lambda i: (i // nsb, 0, i % nsb))
    q, kt, v, kpet = pl.pallas_call(
        _mla_prep_kernel,
        grid=(t // tm,),
        in_specs=[rowblk(MLA_RANK, COL_QLAT // MLA_RANK), rowblk(MLA_RANK, COL_KVLAT // MLA_RANK),
                  rowblk(1024, 0), rowblk(LANES, 0), rowblk(LANES, 0),
                  const((1, MLA_RANK)), const((1, MLA_RANK)), const((MLA_RANK, nq)), const((nk, MLA_RANK)),
                  const((MLA_RANK, nv))],
        out_specs=[rowblk(nq, 0), colblk(nk), rowblk(nv, 0), colblk(LANES)],
        out_shape=[jax.ShapeDtypeStruct((t, nq), BF16), jax.ShapeDtypeStruct((bn, nk, sn), BF16),
                   jax.ShapeDtypeStruct((t, nv), BF16), jax.ShapeDtypeStruct((bn, LANES, sn), BF16)],
        compiler_params=_cparams(("parallel",)),
        name="mla_prep",
    )(proj2, proj2, misc2, cos, sin, q_norm_w.reshape(1, -1), kv_norm_w.reshape(1, -1), wq, wkt, wv)

    tq, tk = min(ATT_TQ, sn), min(ATT_TK, sn)
    sub = min(ATT_SUB, tq)
    ratio = tq // tk
    pairs = [(a, b) for a in range(sn // tq) for b in range((a + 1) * ratio)]
    qi_arr = jnp.asarray([p[0] for p in pairs], jnp.int32)
    ki_arr = jnp.asarray([p[1] for p in pairs], jnp.int32)
    out = pl.pallas_call(
        functools.partial(_flash_kernel, tq, tk, sub),
        grid_spec=pltpu.PrefetchScalarGridSpec(
            num_scalar_prefetch=2,
            grid=(bn, MLA_HEADS, len(pairs)),
            in_specs=[pl.BlockSpec((None, tq, HEAD_Q), lambda b, h, s, qi, ki: (b, qi[s], h)),
                      pl.BlockSpec((None, MLA_NOPE, tk), lambda b, h, s, qi, ki: (b, h, ki[s])),
                      pl.BlockSpec((None, LANES, tk), lambda b, h, s, qi, ki: (b, 0, ki[s])),
                      pl.BlockSpec((None, tk, MLA_V), lambda b, h, s, qi, ki: (b, ki[s], h))],
            out_specs=pl.BlockSpec((None, tq, MLA_V), lambda b, h, s, qi, ki: (b, qi[s], h)),
            scratch_shapes=[pltpu.VMEM((tq, LANES), F32), pltpu.VMEM((tq, 2 * MLA_V), F32)]),
        out_shape=jax.ShapeDtypeStruct((bn, sn, MLA_HEADS * MLA_V), BF16),
        compiler_params=_cparams(("parallel", "parallel", "arbitrary")),
        name="mla_attention",
    )(qi_arr, ki_arr, q.reshape(bn, sn, nq), kt, kpet, v.reshape(bn, sn, nv))
    return out


MERGE_TM = 256


def _merge_kernel(g0_ref, g1_ref, g2_ref, ys_ref, yr_ref, ym_ref, x_ref, wo_ref, lnw_ref, lnb_ref, xo_ref, xb_ref):
    gate = lambda g_ref, y_ref: _sigmoid(g_ref[...].astype(F32)) * y_ref[...].astype(F32)
    merged = gate(g0_ref, ys_ref) + gate(g1_ref, yr_ref) + gate(g2_ref, ym_ref)
    h = ALPHA * x_ref[...] + _dot(merged.astype(BF16), wo_ref[...])
    y = _layernorm(h, lnw_ref[...], lnb_ref[...])
    xo_ref[...] = y
    xb_ref[...] = y.astype(BF16)


def _merge_out(proj2, y_ssd, y_rwkv, y_mla, x, w_out, ln_w, ln_b):
    t, d = x.shape
    tm = min(MERGE_TM, t)
    rowblk = lambda idx: pl.BlockSpec((tm, d), lambda i: (i, idx))
    const = lambda shape: pl.BlockSpec(shape, lambda i: (0, 0))
    return pl.pallas_call(
        _merge_kernel,
        grid=(t // tm,),
        in_specs=[rowblk(0), rowblk(1), rowblk(2), rowblk(0), rowblk(0), rowblk(0), rowblk(0),
                  const((d, d)), const((1, d)), const((1, d))],
        out_specs=[rowblk(0), rowblk(0)],
        out_shape=[jax.ShapeDtypeStruct((t, d), F32), jax.ShapeDtypeStruct((t, d), BF16)],
        compiler_params=_cparams(("parallel",)),
        name="merge_out_ln1",
    )(proj2, proj2, proj2, y_ssd, y_rwkv, y_mla, x, w_out.astype(BF16), ln_w.reshape(1, d), ln_b.reshape(1, d))


FFN_TM = 512
FFN_TF = 512


def _ffn_kernel(xb_ref, x_ref, w1_ref, w3_ref, w2_ref, lnw_ref, lnb_ref, o_ref, ob_ref, acc_ref):
    f = pl.program_id(1)

    @pl.when(f == 0)
    def _():
        acc_ref[...] = jnp.zeros_like(acc_ref)

    xb = xb_ref[...]
    h = _silu(_dot(xb, w1_ref[...])) * _dot(xb, w3_ref[...])
    acc_ref[...] += _dot(h.astype(BF16), w2_ref[...])

    @pl.when(f == pl.num_programs(1) - 1)
    def _():
        y = _layernorm(ALPHA * x_ref[...] + acc_ref[...], lnw_ref[...], lnb_ref[...])
        o_ref[...] = y
        ob_ref[...] = y.astype(BF16)


def _ffn_dense(xb, x, w1, w3, w2, ln_w, ln_b):
    t, d = x.shape
    ff = w1.shape[1]
    tm, tf = min(FFN_TM, t), FFN_TF
    return pl.pallas_call(
        _ffn_kernel,
        grid=(t // tm, ff // tf),
        in_specs=[pl.BlockSpec((tm, d), lambda i, f: (i, 0)), pl.BlockSpec((tm, d), lambda i, f: (i, 0)),
                  pl.BlockSpec((d, tf), lambda i, f: (0, f)), pl.BlockSpec((d, tf), lambda i, f: (0, f)),
                  pl.BlockSpec((tf, d), lambda i, f: (f, 0)),
                  pl.BlockSpec((1, d), lambda i, f: (0, 0)), pl.BlockSpec((1, d), lambda i, f: (0, 0))],
        out_specs=[pl.BlockSpec((tm, d), lambda i, f: (i, 0)), pl.BlockSpec((tm, d), lambda i, f: (i, 0))],
        out_shape=[jax.ShapeDtypeStruct((t, d), F32), jax.ShapeDtypeStruct((t, d), BF16)],
        scratch_shapes=[pltpu.VMEM((tm, d), F32)],
        compiler_params=_cparams(("parallel", "arbitrary")),
        name="ffn_dense_ln2",
    )(xb, x, w1.astype(BF16), w3.astype(BF16), w2.astype(BF16), ln_w.reshape(1, d), ln_b.reshape(1, d))


ROUTER_TM = 512
GATHER_DMA_PRIORITY = 1
MOE_TM = 512
MOE_TF = 512
COMBINE_TM = 256


def _router_kernel(x_ref, wr_ref, idx_ref, wgt_ref, cnt_ref):
    @pl.when(pl.program_id(0) == 0)
    def _():
        cnt_ref[...] = jnp.zeros_like(cnt_ref)

    tm = x_ref.shape[0]
    logits = _dot_hi(x_ref[...], wr_ref[...])
    lane = lax.broadcasted_iota(jnp.int32, logits.shape, 1)
    lg = jnp.where(lane < N_EXPERTS, logits, -jnp.inf)
    m1 = jnp.max(lg, axis=-1, keepdims=True)
    i1 = jnp.min(jnp.where(lg == m1, lane, LANES), axis=-1, keepdims=True)
    lg2 = jnp.where(lane == i1, -jnp.inf, lg)
    m2 = jnp.max(lg2, axis=-1, keepdims=True)
    i2 = jnp.min(jnp.where(lg2 == m2, lane, LANES), axis=-1, keepdims=True)
    e = jnp.exp(m2 - m1)
    wgt_ref[...] = jnp.where(lane == 0, 1.0 / (1.0 + e), jnp.where(lane == 1, e / (1.0 + e), 0.0))

    hit1, hit2 = lane == i1, lane == i2
    both = jnp.where(hit1 | hit2, 1.0, 0.0)
    ri = lax.broadcasted_iota(jnp.int32, (tm, tm), 0)
    ci = lax.broadcasted_iota(jnp.int32, (tm, tm), 1)
    before = cnt_ref[0:1, :] + _dot(jnp.where(ri > ci, 1.0, 0.0).astype(BF16), both.astype(BF16))
    rank1 = jnp.sum(jnp.where(hit1, before, 0.0), axis=-1, keepdims=True).astype(jnp.int32)
    rank2 = jnp.sum(jnp.where(hit2, before, 0.0), axis=-1, keepdims=True).astype(jnp.int32)
    idx_ref[...] = jnp.where(lane == 0, i1, jnp.where(lane == 1, i2, jnp.where(lane == 2, rank1,
                                                                              jnp.where(lane == 3, rank2, 0))))
    cnt_ref[...] = jnp.broadcast_to(before[tm - 1:tm, :] + both[tm - 1:tm, :], cnt_ref.shape)


def _moe_ffn_kernel(nf_static, be_ref, nused_ref, tok_ref, tok_next_ref, x_hbm, w13_ref, w2_ref, o_ref, xg_ref,
                    xb_ref, acc_ref, sem):
    i, f = pl.program_id(0), pl.program_id(1)
    ni, nf = pl.num_programs(0), pl.num_programs(1)
    rows = xg_ref.shape[1]
    per = -(-rows // nf_static)
    tail = rows - (nf_static - 1) * per
    slot = i % 2

    def row_copy(idx_ref, sl, r):
        return pltpu.make_async_copy(x_hbm.at[pl.ds(idx_ref[0, 0, r], 1), :], xg_ref.at[sl, pl.ds(r, 1), :],
                                     sem.at[sl])

    def wait_all(idx_ref, sl):
        for r in range(rows):
            row_copy(idx_ref, sl, r).wait()

    @pl.when(f == 0)
    def _():
        @pl.when(i == 0)
        def _():
            def body(r, c):
                row_copy(tok_ref, slot, r).start()
                return c
            lax.fori_loop(0, rows, body, 0)

        wait_all(tok_ref, slot)
        xb_ref[...] = xg_ref[slot].astype(BF16)
        acc_ref[...] = jnp.zeros_like(acc_ref)

    base = f * per

    def fetch_share():
        for j in range(tail):
            row_copy(tok_next_ref, 1 - slot, base + j).start(priority=GATHER_DMA_PRIORITY)

    @pl.when(i < nused_ref[0])
    def _():
        fetch_share()
        tf = w2_ref.shape[0]
        h13 = _dot(xb_ref[...], w13_ref[...])
        h = _silu(h13[:, :tf]) * h13[:, tf:]
        acc_ref[...] += _dot(h.astype(BF16), w2_ref[...])

    @pl.when(i >= nused_ref[0])
    def _():
        fetch_share()

    @pl.when(f < nf - 1)
    def _():
        for j in range(tail, per):
            row_copy(tok_next_ref, 1 - slot, base + j).start(priority=GATHER_DMA_PRIORITY)

    @pl.when(f == nf - 1)
    def _():
        o_ref[...] = acc_ref[...]

        @pl.when(i == ni - 1)
        def _():
            wait_all(tok_next_ref, 1 - slot)


def _combine_kernel(dst_ref, dst_next_ref, y_hbm, wgt_ref, x_ref, lnw_ref, lnb_ref, o_ref, yg_ref, sem):
    i, n = pl.program_id(0), pl.num_programs(0)
    rows = x_ref.shape[0]
    slot = i % 2

    def row_copy(idx_ref, sl, r, k):
        return pltpu.make_async_copy(y_hbm.at[pl.ds(idx_ref[0, 0, TOP_K * r + k], 1), :],
                                     yg_ref.at[sl, k, pl.ds(r, 1), :], sem.at[sl])

    def for_all(idx_ref, sl, op):
        for r in range(rows):
            for k in range(TOP_K):
                op(row_copy(idx_ref, sl, r, k))

    @pl.when(i == 0)
    def _():
        for_all(dst_ref, slot, lambda c: c.start())

    for_all(dst_next_ref, 1 - slot, lambda c: c.start())
    for_all(dst_ref, slot, lambda c: c.wait())
    wgt = wgt_ref[...]
    f = wgt[:, 0:1] * yg_ref[slot, 0] + wgt[:, 1:2] * yg_ref[slot, 1]
    o_ref[...] = _layernorm(ALPHA * x_ref[...] + f, lnw_ref[...], lnb_ref[...])

    @pl.when(i == n - 1)
    def _():
        for_all(dst_next_ref, 1 - slot, lambda c: c.wait())


def _ffn_moe(x, router, w1, w3, w2, ln_w, ln_b):
    t, d = x.shape
    ne, _, ff = w1.shape
    tm = min(ROUTER_TM, t)
    rw = jnp.pad(router, ((0, 0), (0, LANES - ne)))
    idx, wgt, cnt = pl.pallas_call(
        _router_kernel,
        grid=(t // tm,),
        in_specs=[pl.BlockSpec((tm, d), lambda i: (i, 0)), pl.BlockSpec((d, LANES), lambda i: (0, 0))],
        out_specs=[pl.BlockSpec((tm, LANES), lambda i: (i, 0))] * 2 + [pl.BlockSpec((SUBLANES, LANES), lambda i: (0, 0))],
        out_shape=[jax.ShapeDtypeStruct((t, LANES), jnp.int32), jax.ShapeDtypeStruct((t, LANES), F32),
                   jax.ShapeDtypeStruct((SUBLANES, LANES), F32)],
        compiler_params=_cparams(("arbitrary",)),
        name="moe_router",
    )(x, rw)

    blk = MOE_TM
    n_assign = t * TOP_K
    flat_e = idx[:, :TOP_K].reshape(-1)
    rank = idx[:, TOP_K:2 * TOP_K].reshape(-1)
    counts = cnt[0, :ne].astype(jnp.int32)
    padded = (counts + blk - 1) // blk * blk
    pad_end = jnp.cumsum(padded)
    dest = (pad_end - padded)[flat_e] + rank
    n_blocks = -(-(n_assign + ne * (blk - 1)) // blk)
    n_rows = n_blocks * blk
    token_of_row = jnp.zeros((n_rows,), jnp.int32).at[dest].set(jnp.arange(n_assign, dtype=jnp.int32) // TOP_K)
    block_expert = jnp.minimum(jnp.searchsorted(pad_end, jnp.arange(n_blocks, dtype=jnp.int32) * blk, side='right'),
                               ne - 1).astype(jnp.int32)

    tf = MOE_TF
    split = lambda w: w.astype(BF16).reshape(ne, d, ff // tf, 1, tf)
    w13 = jnp.concatenate([split(w1), split(w3)], axis=3).reshape(ne, d, 2 * ff)
    n_used = (pad_end[-1:] // blk).astype(jnp.int32)
    wtile = lambda i, f, nu: jnp.where(i < nu[0], f, 0)
    yb = pl.pallas_call(
        functools.partial(_moe_ffn_kernel, ff // tf),
        grid_spec=pltpu.PrefetchScalarGridSpec(
            num_scalar_prefetch=2,
            grid=(n_blocks, ff // tf),
            in_specs=[pl.BlockSpec((1, 1, blk), lambda i, f, be, nu: (i, 0, 0), memory_space=pltpu.SMEM),
                      pl.BlockSpec((1, 1, blk), lambda i, f, be, nu: (jnp.minimum(i + 1, n_blocks - 1), 0, 0),
                                   memory_space=pltpu.SMEM),
                      pl.BlockSpec(memory_space=pl.ANY),
                      pl.BlockSpec((None, d, 2 * tf), lambda i, f, be, nu: (be[i], 0, wtile(i, f, nu))),
                      pl.BlockSpec((None, tf, d), lambda i, f, be, nu: (be[i], wtile(i, f, nu), 0))],
            out_specs=pl.BlockSpec((blk, d), lambda i, f, be, nu: (i, 0)),
            scratch_shapes=[pltpu.VMEM((2, blk, d), F32), pltpu.VMEM((blk, d), BF16), pltpu.VMEM((blk, d), F32),
                            pltpu.SemaphoreType.DMA((2,))]),
        out_shape=jax.ShapeDtypeStruct((n_rows, d), F32),
        compiler_params=_cparams(("arbitrary", "arbitrary")),
        name="moe_expert_ffn",
    )(block_expert, n_used, token_of_row.reshape(n_blocks, 1, blk), token_of_row.reshape(n_blocks, 1, blk), x,
      w13, w2.astype(BF16))

    tc = min(COMBINE_TM, t)
    return pl.pallas_call(
        _combine_kernel,
        grid=(t // tc,),
        in_specs=[pl.BlockSpec((1, 1, TOP_K * tc), lambda i: (i, 0, 0), memory_space=pltpu.SMEM),
                  pl.BlockSpec((1, 1, TOP_K * tc), lambda i: (jnp.minimum(i + 1, t // tc - 1), 0, 0),
                               memory_space=pltpu.SMEM),
                  pl.BlockSpec(memory_space=pl.ANY),
                  pl.BlockSpec((tc, LANES), lambda i: (i, 0)), pl.BlockSpec((tc, d), lambda i: (i, 0)),
                  pl.BlockSpec((1, d), lambda i: (0, 0)), pl.BlockSpec((1, d), lambda i: (0, 0))],
        out_specs=pl.BlockSpec((tc, d), lambda i: (i, 0)),
        out_shape=jax.ShapeDtypeStruct((t, d), F32),
        scratch_shapes=[pltpu.VMEM((2, TOP_K, tc, d), F32), pltpu.SemaphoreType.DMA((2,))],
        compiler_params=_cparams(("arbitrary",)),
        name="moe_combine_ln2",
    )(dest.reshape(t // tc, 1, TOP_K * tc), dest.reshape(t // tc, 1, TOP_K * tc), yb, wgt, x,
      ln_w.reshape(1, d), ln_b.reshape(1, d))


def _pack_w_in(w, w_vres):
    d = w.shape[0]
    w = w.astype(BF16)
    w_vres = None if w_vres is None else w_vres.astype(BF16)
    o = 0

    def take(n):
        nonlocal o
        s = w[:, o:o + n]
        o += n
        return s

    padc = lambda s, n: jnp.pad(s, ((0, 0), (0, n - s.shape[1])))
    gates = take(3 * W2K)
    z = take(W2K)
    xbc = take(W2K + 2 * SSD_BC)
    dt = take(SSD_HEADS)
    qlat = take(MLA_RANK)
    kvlat = take(MLA_RANK)
    kpe = take(MLA_ROPE)
    rkv = take(3 * W2K)
    w_lo, a_lo, g_lo = take(LORA_W), take(LORA_A), take(LORA_G)
    v_lo = jnp.zeros((d, LANES), w.dtype) if w_vres is None else padc(w_vres, LANES)
    kpe_rot = jnp.concatenate([-kpe[:, MLA_ROPE // 2:], kpe[:, :MLA_ROPE // 2]], axis=1)
    misc = jnp.concatenate([padc(w_lo, LANES), padc(a_lo, LANES), g_lo, v_lo, padc(dt, LANES), kpe, kpe_rot], axis=1)
    return jnp.concatenate([gates, z, rkv, xbc, qlat, kvlat, padc(misc, 1024)], axis=1).astype(BF16)


PROJ_TM = 2048
PROJ_TN = 1024


def kernel(x, positions, w_in, w_in_vres, w_out, ssd_conv_w, ssd_conv_b, ssd_dt_bias, ssd_a_log, ssd_d, ssd_norm_w, rwkv_mu, rwkv_mu_vres, rwkv_w0, rwkv_w2, rwkv_a0, rwkv_a2, rwkv_g2, rwkv_v0, rwkv_v2, rwkv_k_k, rwkv_k_a, rwkv_r_k, rwkv_ln_w, rwkv_ln_b, mla_q_norm_w, mla_w_q_b, mla_kv_norm_w, mla_w_kv_b, ln1_w, ln1_b, ln2_w, ln2_b, ffn_w1, ffn_w3, ffn_w2, moe_router, moe_w1, moe_w3, moe_w2):
    bn, sn, d = x.shape
    t = bn * sn
    cos, sin = _rope_tables(positions)
    xf = x.reshape(t, d)
    xb = xf
    v_first = None
    for l in range(DEPTH):
        if l == 0:
            wp, mu, v0, v2 = _pack_w_in(w_in[l], None), rwkv_mu[l], None, None
        else:
            wp = _pack_w_in(w_in[l], w_in_vres[l - 1])
            mu = jnp.concatenate([rwkv_mu[l], rwkv_mu_vres[l - 1]], axis=0)
            v0, v2 = rwkv_v0[l - 1], rwkv_v2[l - 1]
        tm = min(PROJ_TM // 2 if xb.dtype == F32 else PROJ_TM, t)
        proj2 = _matmul(xb, wp[:, :COL_MISC], BF16, tm, PROJ_TN)
        misc = _matmul(xb, wp[:, COL_MISC:], F32, tm, N_PROJ - COL_MISC).reshape(bn, sn, -1)
        proj = proj2.reshape(bn, sn, COL_MISC)
        y_ssd = _ssd_mixer(proj, misc, ssd_conv_w[l], ssd_conv_b[l], ssd_dt_bias[l], ssd_a_log[l], ssd_d[l],
                           ssd_norm_w[l])
        y_rwkv, v_first = _rwkv_mixer(proj, misc, v_first, mu, rwkv_w0[l], rwkv_w2[l], rwkv_a0[l], rwkv_a2[l],
                                      rwkv_g2[l], rwkv_k_k[l], rwkv_k_a[l], rwkv_r_k[l], rwkv_ln_w[l], rwkv_ln_b[l],
                                      v0, v2)
        y_mla = _mla_mixer(proj, misc, cos, sin, mla_q_norm_w[l], mla_w_q_b[l], mla_kv_norm_w[l], mla_w_kv_b[l])
        x1, x1b = _merge_out(proj2, y_ssd.reshape(t, d), y_rwkv.reshape(t, d), y_mla.reshape(t, d), xf, w_out[l],
                             ln1_w[l], ln1_b[l])
        if l % 2 == 0:
            xf, xb = _ffn_dense(x1b, x1, ffn_w1[l // 2], ffn_w3[l // 2], ffn_w2[l // 2], ln2_w[l], ln2_b[l])
        else:
            xf = _ffn_moe(x1, moe_router[l // 2], moe_w1[l // 2], moe_w3[l // 2], moe_w2[l // 2], ln2_w[l], ln2_b[l])
            xb = xf
    return xf.reshape(bn, sn, d)
```

```python
import functools

import jax
import jax.numpy as jnp
from jax import lax
from jax.experimental import pallas as pl
from jax.experimental.pallas import tpu as pltpu

F32 = jnp.float32
BF16 = jnp.bfloat16

D_MODEL = 2048
DEPTH = 2
ALPHA = (2 * DEPTH) ** 0.25
LN_EPS = 1e-5
RMS_EPS = 1e-6
SSD_HEADS, SSD_HEAD_DIM, SSD_GROUPS, SSD_STATE, SSD_CONV = 32, 64, 4, 128, 4
SSD_BC = SSD_GROUPS * SSD_STATE
RWKV_HEADS, RWKV_HEAD = 32, 64
RWKV_GN_EPS = 64e-5
LORA_W, LORA_A, LORA_G, LORA_V = 96, 96, 256, 64
MLA_HEADS, MLA_NOPE, MLA_ROPE, MLA_V, MLA_RANK = 16, 128, 64, 128, 512
MLA_QK = MLA_NOPE + MLA_ROPE
ROPE_THETA = 10000.0
D_FF = 5632
N_EXPERTS = 8
TOP_K = 2

LANES = 128
SUBLANES = 8
VMEM_LIMIT = 56 * 1024 * 1024

W2K = 2048
COL_GATE = 0
COL_Z = 3 * W2K
COL_R = 4 * W2K
COL_XS = 7 * W2K
COL_BC = 8 * W2K
COL_QLAT = COL_BC + 1024
COL_KVLAT = COL_QLAT + 512
COL_MISC = COL_KVLAT + 512
MISC_W, MISC_A, MISC_G, MISC_V, MISC_DT, MISC_KPE = 0, 128, 256, 512, 640, 768
N_PROJ = COL_MISC + 1024

SSD_Q = 128
RW_T = 64
RW_BLK = 128


def _cparams(sem, vmem=VMEM_LIMIT):
    return pltpu.CompilerParams(dimension_semantics=sem, vmem_limit_bytes=vmem)


def _dot(a, b):
    return jnp.dot(a, b, preferred_element_type=F32)


def _dot_nt(a, b):
    return lax.dot_general(a, b, (((1,), (1,)), ((), ())), preferred_element_type=F32)


def _split3(a):
    hi = a.astype(BF16)
    r1 = a - hi.astype(F32)
    mid = r1.astype(BF16)
    lo = (r1 - mid.astype(F32)).astype(BF16)
    return hi, mid, lo


def _dot_exact_rhs(a_bf, b):
    hi, mid, lo = _split3(b)
    return _dot(a_bf, hi) + _dot(a_bf, mid) + _dot(a_bf, lo)


def _sigmoid(x):
    return 1.0 / (1.0 + jnp.exp(-x))


def _silu(x):
    return x * _sigmoid(x)


def _softplus(x):
    return jnp.maximum(x, 0.0) + jnp.log(1.0 + jnp.exp(-jnp.abs(x)))


def _layernorm(x, w, b):
    mu = jnp.mean(x, axis=-1, keepdims=True)
    xc = x - mu
    var = jnp.mean(xc * xc, axis=-1, keepdims=True)
    return xc * lax.rsqrt(var + LN_EPS) * w + b


def _mm_kernel(x_ref, w_ref, o_ref):
    o_ref[...] = _dot(x_ref[...].astype(BF16), w_ref[...]).astype(o_ref.dtype)


def _matmul(x, w, out_dtype, tm, tn):
    m, k = x.shape
    n = w.shape[1]
    return pl.pallas_call(
        _mm_kernel,
        grid=(m // tm, n // tn),
        in_specs=[pl.BlockSpec((tm, k), lambda i, j: (i, 0)),
                  pl.BlockSpec((k, tn), lambda i, j: (0, j))],
        out_specs=pl.BlockSpec((tm, tn), lambda i, j: (i, j)),
        out_shape=jax.ShapeDtypeStruct((m, n), out_dtype),
        compiler_params=_cparams(("parallel", "arbitrary")),
        name="in_proj",
    )(x, w)


def _rope_kernel(pos_ref, freq_ref, cos_ref, sin_ref):
    ang = pos_ref[...] * freq_ref[...]
    valid = lax.broadcasted_iota(jnp.int32, ang.shape, 1) < MLA_ROPE
    cos_ref[...] = jnp.where(valid, jnp.cos(ang), 0.0)
    sin_ref[...] = jnp.where(valid, jnp.sin(ang), 0.0)


def _rope_tables(positions):
    t = positions.size
    tm = min(t, 1024)
    pos = positions.reshape(t, 1).astype(F32)
    inv_freq = ROPE_THETA ** (-jnp.arange(0, MLA_ROPE, 2, dtype=F32) / MLA_ROPE)
    freq = jnp.concatenate([inv_freq, inv_freq, jnp.zeros((LANES - MLA_ROPE,), F32)]).reshape(1, LANES)
    return pl.pallas_call(
        _rope_kernel,
        grid=(t // tm,),
        in_specs=[pl.BlockSpec((tm, 1), lambda i: (i, 0)),
                  pl.BlockSpec((1, LANES), lambda i: (0, 0))],
        out_specs=[pl.BlockSpec((tm, LANES), lambda i: (i, 0))] * 2,
        out_shape=[jax.ShapeDtypeStruct((t, LANES), F32)] * 2,
        compiler_params=_cparams(("parallel",)),
        name="rope_tables",
    )(pos, freq)


def _ssd_kernel(z_ref, xs_ref, bc_ref, misc_ref, cwx_ref, cwb_ref, cbx_ref, cbb_ref, dtb_ref, alog_ref,
                d_ref, nw_ref, e_ref, y_ref, state_ref, bufx_ref, bufb_ref):
    q = SSD_Q
    hp = SSD_HEADS // SSD_GROUPS * SSD_HEAD_DIM

    @pl.when(pl.program_id(1) == 0)
    def _():
        state_ref[...] = jnp.zeros_like(state_ref)
        bufx_ref[0:SUBLANES, :] = jnp.zeros((SUBLANES, bufx_ref.shape[1]), F32)
        bufb_ref[0:SUBLANES, :] = jnp.zeros((SUBLANES, bufb_ref.shape[1]), F32)

    bufx_ref[SUBLANES:SUBLANES + q, :] = xs_ref[...].astype(F32)
    bufb_ref[SUBLANES:SUBLANES + q, :] = bc_ref[...].astype(F32)

    def conv(buf_ref, w_ref, b_ref):
        acc = b_ref[...] + w_ref[SSD_CONV - 1:SSD_CONV, :] * buf_ref[SUBLANES:SUBLANES + q, :]
        for k in range(SSD_CONV - 1):
            off = SUBLANES - (SSD_CONV - 1) + k
            acc = acc + w_ref[k:k + 1, :] * buf_ref[off:off + q, :]
        return _silu(acc)

    xs = conv(bufx_ref, cwx_ref, cbx_ref)
    bc = conv(bufb_ref, cwb_ref, cbb_ref)
    bufx_ref[0:SUBLANES, :] = bufx_ref[q:q + SUBLANES, :]
    bufb_ref[0:SUBLANES, :] = bufb_ref[q:q + SUBLANES, :]

    dt = _softplus(misc_ref[:, MISC_DT:MISC_DT + LANES] + dtb_ref[...])
    da = dt * (-jnp.exp(alog_ref[...]))
    row = lax.broadcasted_iota(jnp.int32, (q, q), 0)
    col = lax.broadcasted_iota(jnp.int32, (q, q), 1)
    causal = row >= col
    tri = jnp.where(causal, 1.0, 0.0).astype(BF16)
    cum = _dot_exact_rhs(tri, da)
    cum_t = cum.T
    ecum = jnp.exp(cum)
    toend = jnp.exp(cum[q - 1:q, :] - cum)
    e_mat = e_ref[...]
    def spread(a):
        hi, lo = _split2(a)
        return _dot(hi, e_mat) + _dot(lo, e_mat)

    dt_e, ecum_e, toend_e = spread(dt), spread(ecum), spread(toend)

    xdt = xs * dt_e
    xdt_b = xdt.astype(BF16)
    xw_b = (xdt * toend_e).astype(BF16)
    lane = lax.broadcasted_iota(jnp.int32, (q, LANES), 1)
    lo_half = lane < SSD_HEAD_DIM

    y_groups = []
    for g in range(SSD_GROUPS):
        b_g = bc[:, g * SSD_STATE:(g + 1) * SSD_STATE]
        c_g = bc[:, SSD_BC + g * SSD_STATE:SSD_BC + (g + 1) * SSD_STATE]
        b_gb = b_g.astype(BF16)
        c_gb = c_g.astype(BF16)
        cb = _dot_nt(c_gb, b_gb)
        st = state_ref[:, g * hp:(g + 1) * hp]
        y_inter = _dot(c_gb, st.astype(BF16)) * ecum_e[:, g * hp:(g + 1) * hp]
        parts = []
        for pr in range(hp // LANES):
            ms = []
            for e in range(2):
                h = g * (SSD_HEADS // SSD_GROUPS) + pr * 2 + e
                ci = jnp.broadcast_to(cum[:, h:h + 1], (q, q))
                cj = jnp.broadcast_to(cum_t[h:h + 1, :], (q, q))
                dec = jnp.exp(jnp.where(causal, ci - cj, -jnp.inf))
                ms.append((cb * dec).astype(BF16))
            lo = g * hp + pr * LANES
            xp = xdt_b[:, lo:lo + LANES]
            zero = jnp.zeros_like(xp)
            rhs = jnp.concatenate([jnp.where(lo_half, xp, zero), jnp.where(lo_half, zero, xp)], axis=0)
            parts.append(_dot(jnp.concatenate(ms, axis=1), rhs))
        y_groups.append(jnp.concatenate(parts, axis=1) + y_inter)
        upd = _dot(b_g.T.astype(BF16), xw_b[:, g * hp:(g + 1) * hp])
        state_ref[:, g * hp:(g + 1) * hp] = st * ecum_e[q - 1:q, g * hp:(g + 1) * hp] + upd

    y = jnp.concatenate(y_groups, axis=1) + d_ref[...] * xs
    y = y * _silu(z_ref[...].astype(F32))
    outs = []
    for g in range(SSD_GROUPS):
        yg = y[:, g * hp:(g + 1) * hp]
        outs.append(yg * lax.rsqrt(jnp.mean(yg * yg, axis=-1, keepdims=True) + RMS_EPS))
    y_ref[...] = (jnp.concatenate(outs, axis=1) * nw_ref[...]).astype(y_ref.dtype)


def _ssd_mixer(proj, misc, conv_w, conv_b, dt_bias, a_log, d_skip, norm_w):
    bn, sn, _ = proj.shape
    w = SSD_HEADS * SSD_HEAD_DIM
    pad = lambda v: jnp.pad(v, (0, LANES - v.shape[0])).reshape(1, LANES)
    a_log_p = jnp.pad(a_log, (0, LANES - SSD_HEADS), constant_values=-jnp.inf).reshape(1, LANES)
    expand = jnp.pad(jnp.repeat(jnp.eye(SSD_HEADS, dtype=BF16), SSD_HEAD_DIM, axis=1),
                     ((0, LANES - SSD_HEADS), (0, 0)))
    row = lambda v: v.reshape(1, -1)
    const = lambda shape: pl.BlockSpec(shape, lambda b, c: (0, 0))
    blk = lambda width, idx: pl.BlockSpec((None, SSD_Q, width), lambda b, c: (b, c, idx))
    return pl.pallas_call(
        _ssd_kernel,
        grid=(bn, sn // SSD_Q),
        in_specs=[blk(w, COL_Z // w), blk(w, COL_XS // w), blk(1024, COL_BC // 1024), blk(1024, 0),
                  const((SSD_CONV, w)), const((SSD_CONV, 2 * SSD_BC)), const((1, w)), const((1, 2 * SSD_BC)),
                  const((1, LANES)), const((1, LANES)), const((1, w)), const((1, w)), const((LANES, w))],
        out_specs=pl.BlockSpec((None, SSD_Q, w), lambda b, c: (b, c, 0)),
        out_shape=jax.ShapeDtypeStruct((bn, sn, w), BF16),
        scratch_shapes=[pltpu.VMEM((SSD_STATE, w), F32),
                        pltpu.VMEM((SSD_Q + SUBLANES, w), F32),
                        pltpu.VMEM((SSD_Q + SUBLANES, 2 * SSD_BC), F32)],
        compiler_params=_cparams(("parallel", "arbitrary")),
        name="ssd_mixer",
    )(proj, proj, proj, misc, conv_w[:, :w], conv_w[:, w:], row(conv_b[:w]), row(conv_b[w:]),
      pad(dt_bias), a_log_p, row(jnp.repeat(d_skip, SSD_HEAD_DIM)), row(norm_w), expand)


def _split2(a):
    hi = a.astype(BF16)
    return hi, (a - hi.astype(F32)).astype(BF16)


def _dot_3pass(a, b_hi, b_lo):
    a_hi, a_lo = _split2(a)
    return _dot(a_hi, b_hi) + (_dot(a_hi, b_lo) + _dot(a_lo, b_hi))


def _head_sums(x, ones_blk):
    outs = []
    for s in range(x.shape[1] // LANES):
        outs.append(_dot(x[:, s * LANES:(s + 1) * LANES].astype(BF16), ones_blk))
    return jnp.concatenate(outs, axis=1)


def _stack_heads(x, lo_half):
    zero = jnp.zeros_like(x)
    return jnp.concatenate([jnp.where(lo_half, x, zero), jnp.where(lo_half, zero, x)], axis=0)


RW_PAIRS = RWKV_HEADS // 2
RW_GROUP = 16


def _rwkv_kernel(has_vres, *refs):
    if has_vres:
        (r_ref, k_ref, v_ref, misc_ref, vfirst_ref, mur_ref, muk_ref, muv_ref, mum_ref, w0_ref, w2h_ref, w2l_ref,
         a0_ref, a2_ref, g2_ref, kk_ref, ka_ref, rk_ref, lnw_ref, lnb_ref, v0_ref, v2_ref,
         y_ref, state_ref, carry_ref, carrym_ref, st_ref, yp_ref, pt_ref) = refs
    else:
        (r_ref, k_ref, v_ref, misc_ref, mur_ref, muk_ref, muv_ref, mum_ref, w0_ref, w2h_ref, w2l_ref, a0_ref,
         a2_ref, g2_ref, kk_ref, ka_ref, rk_ref, lnw_ref, lnb_ref,
         y_ref, vout_ref, state_ref, carry_ref, carrym_ref, st_ref, yp_ref, pt_ref) = refs
    tb, t = RW_BLK, RW_T
    w = RWKV_HEADS * RWKV_HEAD

    @pl.when(pl.program_id(1) == 0)
    def _():
        state_ref[...] = jnp.zeros_like(state_ref)
        carry_ref[...] = jnp.zeros_like(carry_ref)
        carrym_ref[...] = jnp.zeros_like(carrym_ref)

    first_row = lax.broadcasted_iota(jnp.int32, (tb, 1), 0) == 0

    def shift(p, carry_row, mu):
        prev = jnp.where(first_row, carry_row, pltpu.roll(p, 1, 0))
        return p + (prev - p) * mu

    rp, kp, vp, mp = r_ref[...].astype(F32), k_ref[...].astype(F32), v_ref[...].astype(F32), misc_ref[...]
    r = shift(rp, carry_ref[0:1, :], mur_ref[...])
    k = shift(kp, carry_ref[1:2, :], muk_ref[...])
    v = shift(vp, carry_ref[2:3, :], muv_ref[...])
    m = shift(mp, carrym_ref[0:1, :], mum_ref[...])
    carry_ref[0:1, :] = rp[tb - 1:tb, :]
    carry_ref[1:2, :] = kp[tb - 1:tb, :]
    carry_ref[2:3, :] = vp[tb - 1:tb, :]
    carrym_ref[0:1, :] = mp[tb - 1:tb, :]

    w_lo = m[:, MISC_W:MISC_W + LANES]
    a_lo = m[:, MISC_A:MISC_A + LANES]
    g_lo = m[:, MISC_G:MISC_G + LORA_G]
    log_w = -_softplus(-(w0_ref[...] + _dot_3pass(jnp.tanh(w_lo), w2h_ref[...], w2l_ref[...]))) - 0.5
    lw = -jnp.exp(log_w)
    a = _sigmoid(a0_ref[...] + _dot(a_lo.astype(BF16), a2_ref[...]))
    g = _dot(_sigmoid(g_lo).astype(BF16), g2_ref[...])
    if has_vres:
        v_lo = m[:, MISC_V:MISC_V + LANES]
        v = v + (vfirst_ref[...] - v) * _sigmoid(v0_ref[...] + _dot(v_lo.astype(BF16), v2_ref[...]))
    else:
        vout_ref[...] = v

    lane = lax.broadcasted_iota(jnp.int32, (LANES, LANES), 1)
    rowi = lax.broadcasted_iota(jnp.int32, (LANES, LANES), 0)
    ones_blk = jnp.where((lane // RWKV_HEAD) == (rowi // RWKV_HEAD), 1.0, 0.0).astype(BF16)

    kk = k * kk_ref[...]
    kk = kk / jnp.maximum(jnp.sqrt(_head_sums(kk * kk, ones_blk)), 1e-12)
    k = k * (1.0 + (a - 1.0) * ka_ref[...])
    b = kk * a

    ti = lax.broadcasted_iota(jnp.int32, (t, t), 0)
    tj = lax.broadcasted_iota(jnp.int32, (t, t), 1)
    tri = jnp.where(ti >= tj, 1.0, 0.0).astype(BF16)

    for c in range(tb // t):
        sl = slice(c * t, (c + 1) * t)
        lw_c = lw[sl]
        cl = _dot_exact_rhs(tri, lw_c)
        cl_end = cl[t - 1:t, :]
        e_neg = jnp.exp(-cl)
        e_end = jnp.exp(cl_end - cl)
        ops = (kk[sl] * jnp.exp(cl - lw_c), r[sl] * jnp.exp(cl), k[sl] * e_neg, b[sl] * e_neg,
               v[sl], k[sl] * e_end, b[sl] * e_end)
        for pi in range(RW_PAIRS):
            ls = slice(pi * LANES, (pi + 1) * LANES)
            for oi, op in enumerate(ops):
                st_ref[c, oi, pi] = op[:, ls]
            pt_ref[c, pi] = jnp.broadcast_to(jnp.exp(cl_end[:, ls]), (SUBLANES, LANES))

    lo_half = lax.broadcasted_iota(jnp.int32, (t, LANES), 1) < RWKV_HEAD
    bi = lax.broadcasted_iota(jnp.int32, (2 * t, 2 * t), 0) % t
    bj = lax.broadcasted_iota(jnp.int32, (2 * t, 2 * t), 1) % t
    strict = bi > bj
    incl = bi >= bj

    def chunk_group(c, pis, hts):
        h2 = 2 * t
        stk = [[_stack_heads(st_ref[c, oi, pi], lo_half) for oi in range(7)] for pi in pis]
        lhs2 = [jnp.concatenate([s[0], s[1]], axis=0).astype(BF16) for s in stk]
        rhs2 = [jnp.concatenate([s[2], s[3]], axis=0).astype(BF16) for s in stk]
        amat = [_dot_nt(a, b) for a, b in zip(lhs2, rhs2)]
        sh = [_dot_nt(a, h.astype(BF16)) for a, h in zip(lhs2, hts)]
        vsb = [s[4].astype(BF16) for s in stk]
        x = [s_[0:h2] + _dot(jnp.where(strict, am[0:h2, 0:h2], 0.0).astype(BF16), v_)
             for s_, am, v_ in zip(sh, amat, vsb)]
        pw = [jnp.where(strict, am[0:h2, h2:], 0.0) for am in amat]
        n, sign = 1, -1.0
        while n < t:
            if 2 * n < t:
                res = [_dot(p_.astype(BF16), jnp.concatenate([p_, x_], axis=1).astype(BF16)) for p_, x_ in zip(pw, x)]
                x = [x_ + sign * r_[:, h2:] for x_, r_ in zip(x, res)]
                pw = [r_[:, 0:h2] for r_ in res]
            else:
                x = [x_ + sign * _dot(p_.astype(BF16), x_.astype(BF16)) for p_, x_ in zip(pw, x)]
            n, sign = 2 * n, 1.0
        new_hts = []
        for i, pi in enumerate(pis):
            am, s = amat[i], stk[i]
            a_r = jnp.concatenate([jnp.where(incl, am[h2:, 0:h2], 0.0), jnp.where(incl, -am[h2:, h2:], 0.0)], axis=1)
            ys = sh[i][h2:] + _dot(a_r.astype(BF16), jnp.concatenate([s[4], x[i]], axis=0).astype(BF16))
            yp_ref[pi, c * t:(c + 1) * t, :] = ys[0:t] + ys[t:]
            lhs3 = jnp.concatenate([s[4].T, -(x[i].T)], axis=1).astype(BF16)
            rhs3 = jnp.concatenate([s[5], s[6]], axis=0).astype(BF16)
            new_hts.append(hts[i] * pt_ref[c, pi][0:1, :] + _dot(lhs3, rhs3))
        return new_hts

    for g0 in range(0, RW_PAIRS, RW_GROUP):
        pis = list(range(g0, g0 + RW_GROUP))
        hts = [state_ref[pi] for pi in pis]
        for c in range(tb // t):
            hts = chunk_group(c, pis, hts)
        for pi, ht in zip(pis, hts):
            state_ref[pi] = ht

    y = jnp.concatenate([yp_ref[pi] for pi in range(RW_PAIRS)], axis=1)
    inv_n = 1.0 / RWKV_HEAD
    mu = _head_sums(y, ones_blk) * inv_n
    yc = y - mu
    var = _head_sums(yc * yc, ones_blk) * inv_n
    y = yc * lax.rsqrt(var + RWKV_GN_EPS) * lnw_ref[...] + lnb_ref[...]
    bonus = _head_sums(r * k * rk_ref[...], ones_blk) * v
    y_ref[...] = ((y + bonus) * g).astype(y_ref.dtype)


def _rwkv_mixer(proj, misc, v_first, mu, w0, w2, a0, a2, g2, k_k, k_a, r_k, ln_w, ln_b, v0, v2):
    bn, sn, _ = proj.shape
    w = RWKV_HEADS * RWKV_HEAD
    has_vres = v_first is not None
    row = lambda x: x.reshape(1, -1)
    padrows = lambda x: jnp.pad(x, ((0, LANES - x.shape[0]), (0, 0)))
    padl = lambda x, n: jnp.pad(x, (0, n - x.shape[0]))
    mu_misc = [padl(mu[3 * w:3 * w + LORA_W], LANES), padl(mu[3 * w + LORA_W:3 * w + LORA_W + LORA_A], LANES),
               mu[3 * w + LORA_W + LORA_A:3 * w + LORA_W + LORA_A + LORA_G]]
    if has_vres:
        mu_misc.append(padl(mu[3 * w + LORA_W + LORA_A + LORA_G:], LANES))
    mu_m = padl(jnp.concatenate(mu_misc), 1024)
    const = lambda shape: pl.BlockSpec(shape, lambda b, c: (0,) * len(shape))
    blk = lambda width, idx: pl.BlockSpec((None, RW_BLK, width), lambda b, c: (b, c, idx))
    seq = pl.BlockSpec((None, RW_BLK, w), lambda b, c: (b, c, 0))
    in_specs = [blk(w, COL_R // w), blk(w, COL_R // w + 1), blk(w, COL_R // w + 2), blk(1024, 0)]
    args = [proj, proj, proj, misc]
    if has_vres:
        in_specs.append(seq)
        args.append(v_first)
    in_specs += [const((1, w))] * 3 + [const((1, 1024)), const((1, w)), const((LANES, w)), const((LANES, w)),
                                       const((1, w)), const((LANES, w)), const((LORA_G, w))] + [const((1, w))] * 5
    w2p = padrows(w2)
    w2_hi = w2p.astype(BF16)
    w2_lo = (w2p - w2_hi.astype(F32)).astype(BF16)
    args += [row(mu[:w]), row(mu[w:2 * w]), row(mu[2 * w:3 * w]), row(mu_m), row(w0), w2_hi, w2_lo, row(a0),
             padrows(a2).astype(BF16), g2.astype(BF16), row(k_k), row(k_a), row(r_k), row(ln_w), row(ln_b)]
    if has_vres:
        in_specs += [const((1, w)), const((LANES, w))]
        args += [row(v0), padrows(v2).astype(BF16)]
    out_shape = [jax.ShapeDtypeStruct((bn, sn, w), BF16)]
    out_specs = [seq]
    if not has_vres:
        out_shape.append(jax.ShapeDtypeStruct((bn, sn, w), F32))
        out_specs.append(seq)
    nc = RW_BLK // RW_T
    outs = pl.pallas_call(
        functools.partial(_rwkv_kernel, has_vres),
        grid=(bn, sn // RW_BLK),
        in_specs=in_specs,
        out_specs=out_specs,
        out_shape=out_shape,
        scratch_shapes=[pltpu.VMEM((RW_PAIRS, LANES, LANES), F32),
                        pltpu.VMEM((SUBLANES, w), F32),
                        pltpu.VMEM((SUBLANES, 1024), F32),
                        pltpu.VMEM((nc, 7, RW_PAIRS, RW_T, LANES), F32),
                        pltpu.VMEM((RW_PAIRS, RW_BLK, LANES), F32),
                        pltpu.VMEM((nc, RW_PAIRS, SUBLANES, LANES), F32)],
        compiler_params=_cparams(("parallel", "arbitrary")),
        name="rwkv7_mixer",
    )(*args)
    if has_vres:
        return outs[0], v_first
    return outs[0], outs[1]


MLA_TM = 256
ATT_TQ = 4096
ATT_TK = 2048
ATT_SUB = 512
HEAD_Q = 2 * LANES


def _rope_half(x2, cos, sin):
    return x2 * cos + pltpu.roll(x2, MLA_ROPE, 1) * sin


def _mla_prep_kernel(qlat_ref, kvlat_ref, misc_ref, cos_ref, sin_ref, qnw_ref, kvnw_ref, wq_ref, wkt_ref, wv_ref,
                     q_ref, kt_ref, v_ref, kpet_ref):
    def rms(x, w):
        x = x.astype(F32)
        return (x * lax.rsqrt(jnp.mean(x * x, axis=-1, keepdims=True) + RMS_EPS) * w).astype(BF16)

    cos, sin = cos_ref[...], sin_ref[...]
    scale = MLA_QK ** -0.5
    q = _dot(rms(qlat_ref[...], qnw_ref[...]), wq_ref[...])
    for h in range(MLA_HEADS):
        lo = h * HEAD_Q
        q_ref[:, lo:lo + LANES] = (q[:, lo:lo + LANES] * scale).astype(BF16)
        q_ref[:, lo + LANES:lo + HEAD_Q] = (_rope_half(q[:, lo + LANES:lo + HEAD_Q], cos, sin) * scale).astype(BF16)
    kvn = rms(kvlat_ref[...], kvnw_ref[...])
    kt_ref[...] = _dot_nt(wkt_ref[...], kvn).astype(BF16)
    v_ref[...] = _dot(kvn, wv_ref[...]).astype(BF16)
    kpet_ref[...] = _rope_half(misc_ref[:, MISC_KPE:MISC_KPE + LANES], cos, sin).T.astype(BF16)


def _flash_kernel(tq, tk, sub, qi_ref, ki_ref, q_ref, kt_ref, kpet_ref, v_ref, o_ref, m_ref, acc_ref):
    step = pl.program_id(2)
    qi, ki = qi_ref[step], ki_ref[step]
    ratio = tq // tk

    @pl.when(ki == 0)
    def _():
        m_ref[...] = jnp.full_like(m_ref, -jnp.inf)
        acc_ref[...] = jnp.zeros_like(acc_ref)

    def update(diag):
        kcat_t = jnp.concatenate([kt_ref[...], kpet_ref[...]], axis=0)
        vcat = jnp.concatenate([v_ref[...], jnp.ones((tk, LANES), BF16)], axis=1)
        plan = []
        for r in range(tq // sub):
            row_lo, row_hi = r * sub, (r + 1) * sub - 1
            ncols = tk
            masked = False
            if diag is not None:
                col_lo = diag * tk
                if row_hi < col_lo:
                    continue
                ncols = min(tk, -(-(row_hi - col_lo + 1) // HEAD_Q) * HEAD_Q)
                masked = row_lo < col_lo + ncols - 1
            plan.append((row_lo, ncols, masked))

        def scores(row_lo, ncols, masked):
            s = _dot(q_ref[row_lo:row_lo + sub, :], kcat_t[:, :ncols])
            if masked:
                ri = row_lo + lax.broadcasted_iota(jnp.int32, s.shape, 0)
                ci = diag * tk + lax.broadcasted_iota(jnp.int32, s.shape, 1)
                s = jnp.where(ci <= ri, s, -jnp.inf)
            return s

        def absorb(row_lo, ncols, s):
            rows = slice(row_lo, row_lo + sub)
            tiles = [s[:, j * LANES:(j + 1) * LANES] for j in range(ncols // LANES)]
            fold = tiles[0]
            for tl in tiles[1:]:
                fold = jnp.maximum(fold, tl)
            m_prev = m_ref[rows, :]
            m_new = jnp.maximum(m_prev, jnp.max(fold, axis=-1, keepdims=True))
            p = jnp.concatenate([jnp.exp((tl - m_new).astype(BF16)) for tl in tiles], axis=1)
            alpha = jnp.exp(m_prev - m_new)
            acc_ref[rows, :] = (acc_ref[rows, :] * jnp.concatenate([alpha, alpha], axis=1)
                                + _dot(p, vcat[:ncols]))
            m_ref[rows, :] = m_new

        s_next = scores(*plan[0])
        for idx, (row_lo, ncols, _) in enumerate(plan):
            s_cur = s_next
            if idx + 1 < len(plan):
                s_next = scores(*plan[idx + 1])
            absorb(row_lo, ncols, s_cur)

    @pl.when(ki < qi * ratio)
    def _():
        update(None)

    for d in range(ratio):
        @pl.when(ki == qi * ratio + d)
        def _(d=d):
            update(d)

    @pl.when(ki == (qi + 1) * ratio - 1)
    def _():
        o_ref[...] = (acc_ref[:, 0:MLA_V] / acc_ref[:, MLA_V:]).astype(o_ref.dtype)


def _mla_mixer(proj, misc, cos, sin, q_norm_w, w_q_b, kv_norm_w, w_kv_b):
    bn, sn, _ = proj.shape
    t = bn * sn
    proj2 = proj.reshape(t, proj.shape[-1])
    misc2 = misc.reshape(t, misc.shape[-1])
    wq = w_q_b.reshape(MLA_RANK, MLA_HEADS, MLA_QK)
    pe = wq[..., MLA_NOPE:]
    rot = jnp.concatenate([-pe[..., MLA_ROPE // 2:], pe[..., :MLA_ROPE // 2]], axis=-1)
    wq = jnp.concatenate([wq, rot], axis=-1).reshape(MLA_RANK, MLA_HEADS * HEAD_Q).astype(BF16)
    wkv = w_kv_b.reshape(MLA_RANK, MLA_HEADS, MLA_NOPE + MLA_V)
    wkt = wkv[..., :MLA_NOPE].reshape(MLA_RANK, -1).T.astype(BF16)
    wv = wkv[..., MLA_NOPE:].reshape(MLA_RANK, -1).astype(BF16)
    nq, nk, nv = MLA_HEADS * HEAD_Q, MLA_HEADS * MLA_NOPE, MLA_HEADS * MLA_V
    tm = min(MLA_TM, sn)
    nsb = sn // tm
    rowblk = lambda width, idx: pl.BlockSpec((tm, width), lambda i: (i, idx))
    const = lambda shape: pl.BlockSpec(shape, lambda i: (0, 0))
    colblk = lambda rows: pl.BlockSpec((None, rows, tm), lambda i: (i // nsb, 0, i % nsb))
    q, kt, v, kpet = pl.pallas_call(
        _mla_prep_kernel,
        grid=(t // tm,),
        in_specs=[rowblk(MLA_RANK, COL_QLAT // MLA_RANK), rowblk(MLA_RANK, COL_KVLAT // MLA_RANK),
                  rowblk(1024, 0), rowblk(LANES, 0), rowblk(LANES, 0),
                  const((1, MLA_RANK)), const((1, MLA_RANK)), const((MLA_RANK, nq)), const((nk, MLA_RANK)),
                  const((MLA_RANK, nv))],
        out_specs=[rowblk(nq, 0), colblk(nk), rowblk(nv, 0), colblk(LANES)],
        out_shape=[jax.ShapeDtypeStruct((t, nq), BF16), jax.ShapeDtypeStruct((bn, nk, sn), BF16),
                   jax.ShapeDtypeStruct((t, nv), BF16), jax.ShapeDtypeStruct((bn, LANES, sn), BF16)],
        compiler_params=_cparams(("parallel",)),
        name="mla_prep",
    )(proj2, proj2, misc2, cos, sin, q_norm_w.reshape(1, -1), kv_norm_w.reshape(1, -1), wq, wkt, wv)

    tq, tk = min(ATT_TQ, sn), min(ATT_TK, sn)
    sub = min(ATT_SUB, tq)
    ratio = tq // tk
    pairs = [(a, b) for a in range(sn // tq) for b in range((a + 1) * ratio)]
    qi_arr = jnp.asarray([p[0] for p in pairs], jnp.int32)
    ki_arr = jnp.asarray([p[1] for p in pairs], jnp.int32)
    out = pl.pallas_call(
        functools.partial(_flash_kernel, tq, tk, sub),
        grid_spec=pltpu.PrefetchScalarGridSpec(
            num_scalar_prefetch=2,
            grid=(bn, MLA_HEADS, len(pairs)),
            in_specs=[pl.BlockSpec((None, tq, HEAD_Q), lambda b, h, s, qi, ki: (b, qi[s], h)),
                      pl.BlockSpec((None, MLA_NOPE, tk), lambda b, h, s, qi, ki: (b, h, ki[s])),
                      pl.BlockSpec((None, LANES, tk), lambda b, h, s, qi, ki: (b, 0, ki[s])),
                      pl.BlockSpec((None, tk, MLA_V), lambda b, h, s, qi, ki: (b, ki[s], h))],
            out_specs=pl.BlockSpec((None, tq, MLA_V), lambda b, h, s, qi, ki: (b, qi[s], h)),
            scratch_shapes=[pltpu.VMEM((tq, LANES), F32), pltpu.VMEM((tq, 2 * MLA_V), F32)]),
        out_shape=jax.ShapeDtypeStruct((bn, sn, MLA_HEADS * MLA_V), BF16),
        compiler_params=_cparams(("parallel", "parallel", "arbitrary")),
        name="mla_attention",
    )(qi_arr, ki_arr, q.reshape(bn, sn, nq), kt, kpet, v.reshape(bn, sn, nv))
    return out


MERGE_TM = 256


def _merge_kernel(g0_ref, g1_ref, g2_ref, ys_ref, yr_ref, ym_ref, x_ref, wo_ref, lnw_ref, lnb_ref, xo_ref, xb_ref):
    gate = lambda g_ref, y_ref: _sigmoid(g_ref[...].astype(F32)) * y_ref[...].astype(F32)
    merged = gate(g0_ref, ys_ref) + gate(g1_ref, yr_ref) + gate(g2_ref, ym_ref)
    h = ALPHA * x_ref[...] + _dot(merged.astype(BF16), wo_ref[...])
    y = _layernorm(h, lnw_ref[...], lnb_ref[...])
    xo_ref[...] = y
    xb_ref[...] = y.astype(BF16)


def _merge_out(proj2, y_ssd, y_rwkv, y_mla, x, w_out, ln_w, ln_b):
    t, d = x.shape
    tm = min(MERGE_TM, t)
    rowblk = lambda idx: pl.BlockSpec((tm, d), lambda i: (i, idx))
    const = lambda shape: pl.BlockSpec(shape, lambda i: (0, 0))
    return pl.pallas_call(
        _merge_kernel,
        grid=(t // tm,),
        in_specs=[rowblk(0), rowblk(1), rowblk(2), rowblk(0), rowblk(0), rowblk(0), rowblk(0),
                  const((d, d)), const((1, d)), const((1, d))],
        out_specs=[rowblk(0), rowblk(0)],
        out_shape=[jax.ShapeDtypeStruct((t, d), F32), jax.ShapeDtypeStruct((t, d), BF16)],
        compiler_params=_cparams(("parallel",)),
        name="merge_out_ln1",
    )(proj2, proj2, proj2, y_ssd, y_rwkv, y_mla, x, w_out.astype(BF16), ln_w.reshape(1, d), ln_b.reshape(1, d))


FFN_TM = 512
FFN_TF = 512


def _ffn_kernel(xb_ref, x_ref, w1_ref, w3_ref, w2_ref, lnw_ref, lnb_ref, o_ref, ob_ref, acc_ref):
    f = pl.program_id(1)

    @pl.when(f == 0)
    def _():
        acc_ref[...] = jnp.zeros_like(acc_ref)

    xb = xb_ref[...]
    h = _silu(_dot(xb, w1_ref[...])) * _dot(xb, w3_ref[...])
    acc_ref[...] += _dot(h.astype(BF16), w2_ref[...])

    @pl.when(f == pl.num_programs(1) - 1)
    def _():
        y = _layernorm(ALPHA * x_ref[...] + acc_ref[...], lnw_ref[...], lnb_ref[...])
        o_ref[...] = y
        ob_ref[...] = y.astype(BF16)


def _ffn_dense(xb, x, w1, w3, w2, ln_w, ln_b):
    t, d = x.shape
    ff = w1.shape[1]
    tm, tf = min(FFN_TM, t), FFN_TF
    return pl.pallas_call(
        _ffn_kernel,
        grid=(t // tm, ff // tf),
        in_specs=[pl.BlockSpec((tm, d), lambda i, f: (i, 0)), pl.BlockSpec((tm, d), lambda i, f: (i, 0)),
                  pl.BlockSpec((d, tf), lambda i, f: (0, f)), pl.BlockSpec((d, tf), lambda i, f: (0, f)),
                  pl.BlockSpec((tf, d), lambda i, f: (f, 0)),
                  pl.BlockSpec((1, d), lambda i, f: (0, 0)), pl.BlockSpec((1, d), lambda i, f: (0, 0))],
        out_specs=[pl.BlockSpec((tm, d), lambda i, f: (i, 0)), pl.BlockSpec((tm, d), lambda i, f: (i, 0))],
        out_shape=[jax.ShapeDtypeStruct((t, d), F32), jax.ShapeDtypeStruct((t, d), BF16)],
        scratch_shapes=[pltpu.VMEM((tm, d), F32)],
        compiler_params=_cparams(("parallel", "arbitrary")),
        name="ffn_dense_ln2",
    )(xb, x, w1.astype(BF16), w3.astype(BF16), w2.astype(BF16), ln_w.reshape(1, d), ln_b.reshape(1, d))


ROUTER_TM = 512
MOE_TM = 512
MOE_TF = 512
COMBINE_TM = 256


def _router_kernel(x_ref, wrh_ref, wrl_ref, idx_ref, wgt_ref, cnt_ref):
    @pl.when(pl.program_id(0) == 0)
    def _():
        cnt_ref[...] = jnp.zeros_like(cnt_ref)

    tm = x_ref.shape[0]
    logits = _dot_3pass(x_ref[...], wrh_ref[...], wrl_ref[...])
    lane = lax.broadcasted_iota(jnp.int32, logits.shape, 1)
    lg = jnp.where(lane < N_EXPERTS, logits, -jnp.inf)
    m1 = jnp.max(lg, axis=-1, keepdims=True)
    i1 = jnp.min(jnp.where(lg == m1, lane, LANES), axis=-1, keepdims=True)
    lg2 = jnp.where(lane == i1, -jnp.inf, lg)
    m2 = jnp.max(lg2, axis=-1, keepdims=True)
    i2 = jnp.min(jnp.where(lg2 == m2, lane, LANES), axis=-1, keepdims=True)
    e = jnp.exp(m2 - m1)
    wgt_ref[...] = jnp.where(lane == 0, 1.0 / (1.0 + e), jnp.where(lane == 1, e / (1.0 + e), 0.0))

    hit1, hit2 = lane == i1, lane == i2
    both = jnp.where(hit1 | hit2, 1.0, 0.0)
    ri = lax.broadcasted_iota(jnp.int32, (tm, tm), 0)
    ci = lax.broadcasted_iota(jnp.int32, (tm, tm), 1)
    before = cnt_ref[0:1, :] + _dot(jnp.where(ri > ci, 1.0, 0.0).astype(BF16), both.astype(BF16))
    rank1 = jnp.sum(jnp.where(hit1, before, 0.0), axis=-1, keepdims=True).astype(jnp.int32)
    rank2 = jnp.sum(jnp.where(hit2, before, 0.0), axis=-1, keepdims=True).astype(jnp.int32)
    idx_ref[...] = jnp.where(lane == 0, i1, jnp.where(lane == 1, i2, jnp.where(lane == 2, rank1,
                                                                              jnp.where(lane == 3, rank2, 0))))
    cnt_ref[...] = jnp.broadcast_to(before[tm - 1:tm, :] + both[tm - 1:tm, :], cnt_ref.shape)


def _moe_ffn_kernel(nf_static, be_ref, nused_ref, tok_ref, tok_next_ref, x_hbm, w1_ref, w3_ref, w2_ref, o_ref, xg_ref,
                    xb_ref, acc_ref, sem):
    i, f = pl.program_id(0), pl.program_id(1)
    ni, nf = pl.num_programs(0), pl.num_programs(1)
    rows = xg_ref.shape[1]
    per = -(-rows // nf_static)
    tail = rows - (nf_static - 1) * per
    slot = i % 2

    def row_copy(idx_ref, sl, r):
        return pltpu.make_async_copy(x_hbm.at[pl.ds(idx_ref[0, 0, r], 1), :], xg_ref.at[sl, pl.ds(r, 1), :],
                                     sem.at[sl])

    def wait_all(idx_ref, sl):
        for r in range(rows):
            row_copy(idx_ref, sl, r).wait()

    @pl.when(f == 0)
    def _():
        @pl.when(i == 0)
        def _():
            def body(r, c):
                row_copy(tok_ref, slot, r).start()
                return c
            lax.fori_loop(0, rows, body, 0)

        wait_all(tok_ref, slot)
        xb_ref[...] = xg_ref[slot].astype(BF16)
        acc_ref[...] = jnp.zeros_like(acc_ref)

    base = f * per

    def fetch_share():
        for j in range(tail):
            row_copy(tok_next_ref, 1 - slot, base + j).start()

    @pl.when(i < nused_ref[0])
    def _():
        fetch_share()
        xb = xb_ref[...]
        h = _silu(_dot(xb, w1_ref[...])) * _dot(xb, w3_ref[...])
        acc_ref[...] += _dot(h.astype(BF16), w2_ref[...])

    @pl.when(i >= nused_ref[0])
    def _():
        fetch_share()

    @pl.when(f < nf - 1)
    def _():
        for j in range(tail, per):
            row_copy(tok_next_ref, 1 - slot, base + j).start()

    @pl.when(f == nf - 1)
    def _():
        o_ref[...] = acc_ref[...]

        @pl.when(i == ni - 1)
        def _():
            wait_all(tok_next_ref, 1 - slot)


def _combine_kernel(dst_ref, dst_next_ref, y_hbm, wgt_ref, x_ref, lnw_ref, lnb_ref, o_ref, yg_ref, sem):
    i, n = pl.program_id(0), pl.num_programs(0)
    rows = x_ref.shape[0]
    slot = i % 2

    def row_copy(idx_ref, sl, r, k):
        return pltpu.make_async_copy(y_hbm.at[pl.ds(idx_ref[0, 0, TOP_K * r + k], 1), :],
                                     yg_ref.at[sl, k, pl.ds(r, 1), :], sem.at[sl])

    def for_all(idx_ref, sl, op):
        for r in range(rows):
            for k in range(TOP_K):
                op(row_copy(idx_ref, sl, r, k))

    @pl.when(i == 0)
    def _():
        for_all(dst_ref, slot, lambda c: c.start())

    for_all(dst_next_ref, 1 - slot, lambda c: c.start())
    for_all(dst_ref, slot, lambda c: c.wait())
    wgt = wgt_ref[...]
    f = wgt[:, 0:1] * yg_ref[slot, 0] + wgt[:, 1:2] * yg_ref[slot, 1]
    o_ref[...] = _layernorm(ALPHA * x_ref[...] + f, lnw_ref[...], lnb_ref[...])

    @pl.when(i == n - 1)
    def _():
        for_all(dst_next_ref, 1 - slot, lambda c: c.wait())


def _ffn_moe(x, router, w1, w3, w2, ln_w, ln_b):
    t, d = x.shape
    ne, _, ff = w1.shape
    tm = min(ROUTER_TM, t)
    rw = jnp.pad(router, ((0, 0), (0, LANES - ne)))
    rw_hi = rw.astype(BF16)
    rw_lo = (rw - rw_hi.astype(F32)).astype(BF16)
    idx, wgt, cnt = pl.pallas_call(
        _router_kernel,
        grid=(t // tm,),
        in_specs=[pl.BlockSpec((tm, d), lambda i: (i, 0)), pl.BlockSpec((d, LANES), lambda i: (0, 0)),
                  pl.BlockSpec((d, LANES), lambda i: (0, 0))],
        out_specs=[pl.BlockSpec((tm, LANES), lambda i: (i, 0))] * 2 + [pl.BlockSpec((SUBLANES, LANES), lambda i: (0, 0))],
        out_shape=[jax.ShapeDtypeStruct((t, LANES), jnp.int32), jax.ShapeDtypeStruct((t, LANES), F32),
                   jax.ShapeDtypeStruct((SUBLANES, LANES), F32)],
        compiler_params=_cparams(("arbitrary",)),
        name="moe_router",
    )(x, rw_hi, rw_lo)

    blk = MOE_TM
    n_assign = t * TOP_K
    flat_e = idx[:, :TOP_K].reshape(-1)
    rank = idx[:, TOP_K:2 * TOP_K].reshape(-1)
    counts = cnt[0, :ne].astype(jnp.int32)
    padded = (counts + blk - 1) // blk * blk
    pad_end = jnp.cumsum(padded)
    dest = (pad_end - padded)[flat_e] + rank
    n_blocks = -(-(n_assign + ne * (blk - 1)) // blk)
    n_rows = n_blocks * blk
    token_of_row = jnp.zeros((n_rows,), jnp.int32).at[dest].set(jnp.arange(n_assign, dtype=jnp.int32) // TOP_K,
                                                                unique_indices=True)
    block_expert = jnp.minimum(jnp.searchsorted(pad_end, jnp.arange(n_blocks, dtype=jnp.int32) * blk, side='right'),
                               ne - 1).astype(jnp.int32)

    tf = MOE_TF
    n_used = (pad_end[-1:] // blk).astype(jnp.int32)
    wtile = lambda i, f, nu: jnp.where(i < nu[0], f, 0)
    yb = pl.pallas_call(
        functools.partial(_moe_ffn_kernel, ff // tf),
        grid_spec=pltpu.PrefetchScalarGridSpec(
            num_scalar_prefetch=2,
            grid=(n_blocks, ff // tf),
            in_specs=[pl.BlockSpec((1, 1, blk), lambda i, f, be, nu: (i, 0, 0), memory_space=pltpu.SMEM),
                      pl.BlockSpec((1, 1, blk), lambda i, f, be, nu: (jnp.minimum(i + 1, n_blocks - 1), 0, 0),
                                   memory_space=pltpu.SMEM),
                      pl.BlockSpec(memory_space=pl.ANY),
                      pl.BlockSpec((None, d, tf), lambda i, f, be, nu: (be[i], 0, wtile(i, f, nu))),
                      pl.BlockSpec((None, d, tf), lambda i, f, be, nu: (be[i], 0, wtile(i, f, nu))),
                      pl.BlockSpec((None, tf, d), lambda i, f, be, nu: (be[i], wtile(i, f, nu), 0))],
            out_specs=pl.BlockSpec((blk, d), lambda i, f, be, nu: (i, 0)),
            scratch_shapes=[pltpu.VMEM((2, blk, d), F32), pltpu.VMEM((blk, d), BF16), pltpu.VMEM((blk, d), F32),
                            pltpu.SemaphoreType.DMA((2,))]),
        out_shape=jax.ShapeDtypeStruct((n_rows, d), F32),
        compiler_params=_cparams(("arbitrary", "arbitrary")),
        name="moe_expert_ffn",
    )(block_expert, n_used, token_of_row.reshape(n_blocks, 1, blk), token_of_row.reshape(n_blocks, 1, blk), x,
      w1.astype(BF16), w3.astype(BF16), w2.astype(BF16))

    tc = min(COMBINE_TM, t)
    return pl.pallas_call(
        _combine_kernel,
        grid=(t // tc,),
        in_specs=[pl.BlockSpec((1, 1, TOP_K * tc), lambda i: (i, 0, 0), memory_space=pltpu.SMEM),
                  pl.BlockSpec((1, 1, TOP_K * tc), lambda i: (jnp.minimum(i + 1, t // tc - 1), 0, 0),
                               memory_space=pltpu.SMEM),
                  pl.BlockSpec(memory_space=pl.ANY),
                  pl.BlockSpec((tc, LANES), lambda i: (i, 0)), pl.BlockSpec((tc, d), lambda i: (i, 0)),
                  pl.BlockSpec((1, d), lambda i: (0, 0)), pl.BlockSpec((1, d), lambda i: (0, 0))],
        out_specs=pl.BlockSpec((tc, d), lambda i: (i, 0)),
        out_shape=jax.ShapeDtypeStruct((t, d), F32),
        scratch_shapes=[pltpu.VMEM((2, TOP_K, tc, d), F32), pltpu.SemaphoreType.DMA((2,))],
        compiler_params=_cparams(("arbitrary",)),
        name="moe_combine_ln2",
    )(dest.reshape(t // tc, 1, TOP_K * tc), dest.reshape(t // tc, 1, TOP_K * tc), yb, wgt, x,
      ln_w.reshape(1, d), ln_b.reshape(1, d))


def _pack_w_in(w, w_vres):
    d = w.shape[0]
    w = w.astype(BF16)
    w_vres = None if w_vres is None else w_vres.astype(BF16)
    o = 0

    def take(n):
        nonlocal o
        s = w[:, o:o + n]
        o += n
        return s

    padc = lambda s, n: jnp.pad(s, ((0, 0), (0, n - s.shape[1])))
    gates = take(3 * W2K)
    z = take(W2K)
    xbc = take(W2K + 2 * SSD_BC)
    dt = take(SSD_HEADS)
    qlat = take(MLA_RANK)
    kvlat = take(MLA_RANK)
    kpe = take(MLA_ROPE)
    rkv = take(3 * W2K)
    w_lo, a_lo, g_lo = take(LORA_W), take(LORA_A), take(LORA_G)
    v_lo = jnp.zeros((d, LANES), w.dtype) if w_vres is None else padc(w_vres, LANES)
    kpe_rot = jnp.concatenate([-kpe[:, MLA_ROPE // 2:], kpe[:, :MLA_ROPE // 2]], axis=1)
    misc = jnp.concatenate([padc(w_lo, LANES), padc(a_lo, LANES), g_lo, v_lo, padc(dt, LANES), kpe, kpe_rot], axis=1)
    return jnp.concatenate([gates, z, rkv, xbc, qlat, kvlat, padc(misc, 1024)], axis=1).astype(BF16)


PROJ_TM = 2048
PROJ_TN = 1024


def kernel(x, positions, w_in, w_in_vres, w_out, ssd_conv_w, ssd_conv_b, ssd_dt_bias, ssd_a_log, ssd_d, ssd_norm_w, rwkv_mu, rwkv_mu_vres, rwkv_w0, rwkv_w2, rwkv_a0, rwkv_a2, rwkv_g2, rwkv_v0, rwkv_v2, rwkv_k_k, rwkv_k_a, rwkv_r_k, rwkv_ln_w, rwkv_ln_b, mla_q_norm_w, mla_w_q_b, mla_kv_norm_w, mla_w_kv_b, ln1_w, ln1_b, ln2_w, ln2_b, ffn_w1, ffn_w3, ffn_w2, moe_router, moe_w1, moe_w3, moe_w2):
    bn, sn, d = x.shape
    t = bn * sn
    cos, sin = _rope_tables(positions)
    xf = x.reshape(t, d)
    xb = xf
    v_first = None
    for l in range(DEPTH):
        if l == 0:
            wp, mu, v0, v2 = _pack_w_in(w_in[l], None), rwkv_mu[l], None, None
        else:
            wp = _pack_w_in(w_in[l], w_in_vres[l - 1])
            mu = jnp.concatenate([rwkv_mu[l], rwkv_mu_vres[l - 1]], axis=0)
            v0, v2 = rwkv_v0[l - 1], rwkv_v2[l - 1]
        tm = min(PROJ_TM // 2 if xb.dtype == F32 else PROJ_TM, t)
        proj2 = _matmul(xb, wp[:, :COL_MISC], BF16, tm, PROJ_TN)
        misc = _matmul(xb, wp[:, COL_MISC:], F32, tm, N_PROJ - COL_MISC).reshape(bn, sn, -1)
        proj = proj2.reshape(bn, sn, COL_MISC)
        y_ssd = _ssd_mixer(proj, misc, ssd_conv_w[l], ssd_conv_b[l], ssd_dt_bias[l], ssd_a_log[l], ssd_d[l],
                           ssd_norm_w[l])
        y_rwkv, v_first = _rwkv_mixer(proj, misc, v_first, mu, rwkv_w0[l], rwkv_w2[l], rwkv_a0[l], rwkv_a2[l],
                                      rwkv_g2[l], rwkv_k_k[l], rwkv_k_a[l], rwkv_r_k[l], rwkv_ln_w[l], rwkv_ln_b[l],
                                      v0, v2)
        y_mla = _mla_mixer(proj, misc, cos, sin, mla_q_norm_w[l], mla_w_q_b[l], mla_kv_norm_w[l], mla_w_kv_b[l])
        x1, x1b = _merge_out(proj2, y_ssd.reshape(t, d), y_rwkv.reshape(t, d), y_mla.reshape(t, d), xf, w_out[l],
                             ln1_w[l], ln1_b[l])
        if l % 2 == 0:
            xf, xb = _ffn_dense(x1b, x1, ffn_w1[l // 2], ffn_w3[l // 2], ffn_w2[l // 2], ln2_w[l], ln2_b[l])
        else:
            xf = _ffn_moe(x1, moe_router[l // 2], moe_w1[l // 2], moe_w3[l // 2], moe_w2[l // 2], ln2_w[l], ln2_b[l])
            xb = xf
    return xf.reshape(bn, sn, d)
```

```python
import functools

import jax
import jax.numpy as jnp
from jax import lax
from jax.experimental import pallas as pl
from jax.experimental.pallas import tpu as pltpu

F32 = jnp.float32
BF16 = jnp.bfloat16

D_MODEL = 2048
DEPTH = 2
ALPHA = (2 * DEPTH) ** 0.25
LN_EPS = 1e-5
RMS_EPS = 1e-6
SSD_HEADS, SSD_HEAD_DIM, SSD_GROUPS, SSD_STATE, SSD_CONV = 32, 64, 4, 128, 4
SSD_BC = SSD_GROUPS * SSD_STATE
RWKV_HEADS, RWKV_HEAD = 32, 64
RWKV_GN_EPS = 64e-5
LORA_W, LORA_A, LORA_G, LORA_V = 96, 96, 256, 64
MLA_HEADS, MLA_NOPE, MLA_ROPE, MLA_V, MLA_RANK = 16, 128, 64, 128, 512
MLA_QK = MLA_NOPE + MLA_ROPE
ROPE_THETA = 10000.0
D_FF = 5632
N_EXPERTS = 8
TOP_K = 2

LANES = 128
SUBLANES = 8
VMEM_LIMIT = 56 * 1024 * 1024

W2K = 2048
COL_GATE = 0
COL_Z = 3 * W2K
COL_R = 4 * W2K
COL_XS = 7 * W2K
COL_BC = 8 * W2K
COL_QLAT = COL_BC + 1024
COL_KVLAT = COL_QLAT + 512
COL_MISC = COL_KVLAT + 512
MISC_W, MISC_A, MISC_G, MISC_V, MISC_DT, MISC_KPE = 0, 128, 256, 512, 640, 768
N_PROJ = COL_MISC + 1024

SSD_Q = 128
RW_T = 64
RW_BLK = 128


def _cparams(sem, vmem=VMEM_LIMIT):
    return pltpu.CompilerParams(dimension_semantics=sem, vmem_limit_bytes=vmem)


def _dot(a, b):
    return jnp.dot(a, b, preferred_element_type=F32)


def _dot_nt(a, b):
    return lax.dot_general(a, b, (((1,), (1,)), ((), ())), preferred_element_type=F32)


def _split3(a):
    hi = a.astype(BF16)
    r1 = a - hi.astype(F32)
    mid = r1.astype(BF16)
    lo = (r1 - mid.astype(F32)).astype(BF16)
    return hi, mid, lo


def _dot_exact_rhs(a_bf, b):
    hi, mid, lo = _split3(b)
    return _dot(a_bf, hi) + _dot(a_bf, mid) + _dot(a_bf, lo)


def _sigmoid(x):
    return 1.0 / (1.0 + jnp.exp(-x))


def _silu(x):
    return x * _sigmoid(x)


def _softplus(x):
    return jnp.maximum(x, 0.0) + jnp.log(1.0 + jnp.exp(-jnp.abs(x)))


def _layernorm(x, w, b):
    mu = jnp.mean(x, axis=-1, keepdims=True)
    xc = x - mu
    var = jnp.mean(xc * xc, axis=-1, keepdims=True)
    return xc * lax.rsqrt(var + LN_EPS) * w + b


def _mm_kernel(x_ref, w_ref, o_ref):
    o_ref[...] = _dot(x_ref[...].astype(BF16), w_ref[...]).astype(o_ref.dtype)


def _matmul(x, w, out_dtype, tm, tn):
    m, k = x.shape
    n = w.shape[1]
    return pl.pallas_call(
        _mm_kernel,
        grid=(m // tm, n // tn),
        in_specs=[pl.BlockSpec((tm, k), lambda i, j: (i, 0)),
                  pl.BlockSpec((k, tn), lambda i, j: (0, j))],
        out_specs=pl.BlockSpec((tm, tn), lambda i, j: (i, j)),
        out_shape=jax.ShapeDtypeStruct((m, n), out_dtype),
        compiler_params=_cparams(("parallel", "arbitrary")),
        name="in_proj",
    )(x, w)


def _rope_kernel(pos_ref, freq_ref, cos_ref, sin_ref):
    ang = pos_ref[...] * freq_ref[...]
    valid = lax.broadcasted_iota(jnp.int32, ang.shape, 1) < MLA_ROPE
    cos_ref[...] = jnp.where(valid, jnp.cos(ang), 0.0)
    sin_ref[...] = jnp.where(valid, jnp.sin(ang), 0.0)


def _rope_tables(positions):
    t = positions.size
    tm = min(t, 1024)
    pos = positions.reshape(t, 1).astype(F32)
    inv_freq = ROPE_THETA ** (-jnp.arange(0, MLA_ROPE, 2, dtype=F32) / MLA_ROPE)
    freq = jnp.concatenate([inv_freq, inv_freq, jnp.zeros((LANES - MLA_ROPE,), F32)]).reshape(1, LANES)
    return pl.pallas_call(
        _rope_kernel,
        grid=(t // tm,),
        in_specs=[pl.BlockSpec((tm, 1), lambda i: (i, 0)),
                  pl.BlockSpec((1, LANES), lambda i: (0, 0))],
        out_specs=[pl.BlockSpec((tm, LANES), lambda i: (i, 0))] * 2,
        out_shape=[jax.ShapeDtypeStruct((t, LANES), F32)] * 2,
        compiler_params=_cparams(("parallel",)),
        name="rope_tables",
    )(pos, freq)


def _ssd_kernel(z_ref, xs_ref, bc_ref, misc_ref, cwx_ref, cwb_ref, cbx_ref, cbb_ref, dtb_ref, alog_ref,
                d_ref, nw_ref, e_ref, y_ref, state_ref, bufx_ref, bufb_ref):
    q = SSD_Q
    hp = SSD_HEADS // SSD_GROUPS * SSD_HEAD_DIM

    @pl.when(pl.program_id(1) == 0)
    def _():
        state_ref[...] = jnp.zeros_like(state_ref)
        bufx_ref[0:SUBLANES, :] = jnp.zeros((SUBLANES, bufx_ref.shape[1]), F32)
        bufb_ref[0:SUBLANES, :] = jnp.zeros((SUBLANES, bufb_ref.shape[1]), F32)

    bufx_ref[SUBLANES:SUBLANES + q, :] = xs_ref[...].astype(F32)
    bufb_ref[SUBLANES:SUBLANES + q, :] = bc_ref[...].astype(F32)

    def conv(buf_ref, w_ref, b_ref):
        acc = b_ref[...] + w_ref[SSD_CONV - 1:SSD_CONV, :] * buf_ref[SUBLANES:SUBLANES + q, :]
        for k in range(SSD_CONV - 1):
            off = SUBLANES - (SSD_CONV - 1) + k
            acc = acc + w_ref[k:k + 1, :] * buf_ref[off:off + q, :]
        return _silu(acc)

    xs = conv(bufx_ref, cwx_ref, cbx_ref)
    bc = conv(bufb_ref, cwb_ref, cbb_ref)
    bufx_ref[0:SUBLANES, :] = bufx_ref[q:q + SUBLANES, :]
    bufb_ref[0:SUBLANES, :] = bufb_ref[q:q + SUBLANES, :]

    dt = _softplus(misc_ref[:, MISC_DT:MISC_DT + LANES] + dtb_ref[...])
    da = dt * (-jnp.exp(alog_ref[...]))
    row = lax.broadcasted_iota(jnp.int32, (q, q), 0)
    col = lax.broadcasted_iota(jnp.int32, (q, q), 1)
    causal = row >= col
    tri = jnp.where(causal, 1.0, 0.0).astype(BF16)
    cum = _dot_exact_rhs(tri, da)
    cum_t = cum.T
    ecum = jnp.exp(cum)
    toend = jnp.exp(cum[q - 1:q, :] - cum)
    e_mat = e_ref[...]
    def spread(a):
        hi, lo = _split2(a)
        return _dot(hi, e_mat) + _dot(lo, e_mat)

    dt_e, ecum_e, toend_e = spread(dt), spread(ecum), spread(toend)

    xdt = xs * dt_e
    xdt_b = xdt.astype(BF16)
    xw_b = (xdt * toend_e).astype(BF16)
    lane = lax.broadcasted_iota(jnp.int32, (q, LANES), 1)
    lo_half = lane < SSD_HEAD_DIM

    y_groups = []
    for g in range(SSD_GROUPS):
        b_g = bc[:, g * SSD_STATE:(g + 1) * SSD_STATE]
        c_g = bc[:, SSD_BC + g * SSD_STATE:SSD_BC + (g + 1) * SSD_STATE]
        b_gb = b_g.astype(BF16)
        c_gb = c_g.astype(BF16)
        cb = _dot_nt(c_gb, b_gb)
        st = state_ref[:, g * hp:(g + 1) * hp]
        y_inter = _dot(c_gb, st.astype(BF16)) * ecum_e[:, g * hp:(g + 1) * hp]
        parts = []
        for pr in range(hp // LANES):
            ms = []
            for e in range(2):
                h = g * (SSD_HEADS // SSD_GROUPS) + pr * 2 + e
                ci = jnp.broadcast_to(cum[:, h:h + 1], (q, q))
                cj = jnp.broadcast_to(cum_t[h:h + 1, :], (q, q))
                dec = jnp.exp(jnp.where(causal, ci - cj, -jnp.inf))
                ms.append((cb * dec).astype(BF16))
            lo = g * hp + pr * LANES
            xp = xdt_b[:, lo:lo + LANES]
            zero = jnp.zeros_like(xp)
            rhs = jnp.concatenate([jnp.where(lo_half, xp, zero), jnp.where(lo_half, zero, xp)], axis=0)
            parts.append(_dot(jnp.concatenate(ms, axis=1), rhs))
        y_groups.append(jnp.concatenate(parts, axis=1) + y_inter)
        upd = _dot(b_g.T.astype(BF16), xw_b[:, g * hp:(g + 1) * hp])
        state_ref[:, g * hp:(g + 1) * hp] = st * ecum_e[q - 1:q, g * hp:(g + 1) * hp] + upd

    y = jnp.concatenate(y_groups, axis=1) + d_ref[...] * xs
    y = y * _silu(z_ref[...].astype(F32))
    outs = []
    for g in range(SSD_GROUPS):
        yg = y[:, g * hp:(g + 1) * hp]
        outs.append(yg * lax.rsqrt(jnp.mean(yg * yg, axis=-1, keepdims=True) + RMS_EPS))
    y_ref[...] = (jnp.concatenate(outs, axis=1) * nw_ref[...]).astype(y_ref.dtype)


def _ssd_mixer(proj, misc, conv_w, conv_b, dt_bias, a_log, d_skip, norm_w):
    bn, sn, _ = proj.shape
    w = SSD_HEADS * SSD_HEAD_DIM
    pad = lambda v: jnp.pad(v, (0, LANES - v.shape[0])).reshape(1, LANES)
    a_log_p = jnp.pad(a_log, (0, LANES - SSD_HEADS), constant_values=-jnp.inf).reshape(1, LANES)
    expand = jnp.pad(jnp.repeat(jnp.eye(SSD_HEADS, dtype=BF16), SSD_HEAD_DIM, axis=1),
                     ((0, LANES - SSD_HEADS), (0, 0)))
    row = lambda v: v.reshape(1, -1)
    const = lambda shape: pl.BlockSpec(shape, lambda b, c: (0, 0))
    blk = lambda width, idx: pl.BlockSpec((None, SSD_Q, width), lambda b, c: (b, c, idx))
    return pl.pallas_call(
        _ssd_kernel,
        grid=(bn, sn // SSD_Q),
        in_specs=[blk(w, COL_Z // w), blk(w, COL_XS // w), blk(1024, COL_BC // 1024), blk(1024, 0),
                  const((SSD_CONV, w)), const((SSD_CONV, 2 * SSD_BC)), const((1, w)), const((1, 2 * SSD_BC)),
                  const((1, LANES)), const((1, LANES)), const((1, w)), const((1, w)), const((LANES, w))],
        out_specs=pl.BlockSpec((None, SSD_Q, w), lambda b, c: (b, c, 0)),
        out_shape=jax.ShapeDtypeStruct((bn, sn, w), BF16),
        scratch_shapes=[pltpu.VMEM((SSD_STATE, w), F32),
                        pltpu.VMEM((SSD_Q + SUBLANES, w), F32),
                        pltpu.VMEM((SSD_Q + SUBLANES, 2 * SSD_BC), F32)],
        compiler_params=_cparams(("parallel", "arbitrary")),
        name="ssd_mixer",
    )(proj, proj, proj, misc, conv_w[:, :w], conv_w[:, w:], row(conv_b[:w]), row(conv_b[w:]),
      pad(dt_bias), a_log_p, row(jnp.repeat(d_skip, SSD_HEAD_DIM)), row(norm_w), expand)


def _split2(a):
    hi = a.astype(BF16)
    return hi, (a - hi.astype(F32)).astype(BF16)


def _dot_3pass(a, b_hi, b_lo):
    a_hi, a_lo = _split2(a)
    return _dot(a_hi, b_hi) + (_dot(a_hi, b_lo) + _dot(a_lo, b_hi))


def _head_sums(x, ones_blk):
    outs = []
    for s in range(x.shape[1] // LANES):
        outs.append(_dot(x[:, s * LANES:(s + 1) * LANES].astype(BF16), ones_blk))
    return jnp.concatenate(outs, axis=1)


def _stack_heads(x, lo_half):
    zero = jnp.zeros_like(x)
    return jnp.concatenate([jnp.where(lo_half, x, zero), jnp.where(lo_half, zero, x)], axis=0)


RW_PAIRS = RWKV_HEADS // 2
RW_GROUP = 16


def _rwkv_kernel(has_vres, *refs):
    if has_vres:
        (r_ref, k_ref, v_ref, misc_ref, vfirst_ref, mur_ref, muk_ref, muv_ref, mum_ref, w0_ref, w2h_ref, w2l_ref,
         a0_ref, a2_ref, g2_ref, kk_ref, ka_ref, rk_ref, lnw_ref, lnb_ref, v0_ref, v2_ref,
         y_ref, state_ref, carry_ref, carrym_ref, st_ref, yp_ref, pt_ref) = refs
    else:
        (r_ref, k_ref, v_ref, misc_ref, mur_ref, muk_ref, muv_ref, mum_ref, w0_ref, w2h_ref, w2l_ref, a0_ref,
         a2_ref, g2_ref, kk_ref, ka_ref, rk_ref, lnw_ref, lnb_ref,
         y_ref, vout_ref, state_ref, carry_ref, carrym_ref, st_ref, yp_ref, pt_ref) = refs
    tb, t = RW_BLK, RW_T
    w = RWKV_HEADS * RWKV_HEAD

    @pl.when(pl.program_id(1) == 0)
    def _():
        state_ref[...] = jnp.zeros_like(state_ref)
        carry_ref[...] = jnp.zeros_like(carry_ref)
        carrym_ref[...] = jnp.zeros_like(carrym_ref)

    first_row = lax.broadcasted_iota(jnp.int32, (tb, 1), 0) == 0

    def shift(p, carry_row, mu):
        prev = jnp.where(first_row, carry_row, pltpu.roll(p, 1, 0))
        return p + (prev - p) * mu

    rp, kp, vp, mp = r_ref[...].astype(F32), k_ref[...].astype(F32), v_ref[...].astype(F32), misc_ref[...]
    r = shift(rp, carry_ref[0:1, :], mur_ref[...])
    k = shift(kp, carry_ref[1:2, :], muk_ref[...])
    v = shift(vp, carry_ref[2:3, :], muv_ref[...])
    m = shift(mp, carrym_ref[0:1, :], mum_ref[...])
    carry_ref[0:1, :] = rp[tb - 1:tb, :]
    carry_ref[1:2, :] = kp[tb - 1:tb, :]
    carry_ref[2:3, :] = vp[tb - 1:tb, :]
    carrym_ref[0:1, :] = mp[tb - 1:tb, :]

    w_lo = m[:, MISC_W:MISC_W + LANES]
    a_lo = m[:, MISC_A:MISC_A + LANES]
    g_lo = m[:, MISC_G:MISC_G + LORA_G]
    log_w = -_softplus(-(w0_ref[...] + _dot_3pass(jnp.tanh(w_lo), w2h_ref[...], w2l_ref[...]))) - 0.5
    lw = -jnp.exp(log_w)
    a = _sigmoid(a0_ref[...] + _dot(a_lo.astype(BF16), a2_ref[...]))
    g = _dot(_sigmoid(g_lo).astype(BF16), g2_ref[...])
    if has_vres:
        v_lo = m[:, MISC_V:MISC_V + LANES]
        v = v + (vfirst_ref[...] - v) * _sigmoid(v0_ref[...] + _dot(v_lo.astype(BF16), v2_ref[...]))
    else:
        vout_ref[...] = v

    lane = lax.broadcasted_iota(jnp.int32, (LANES, LANES), 1)
    rowi = lax.broadcasted_iota(jnp.int32, (LANES, LANES), 0)
    ones_blk = jnp.where((lane // RWKV_HEAD) == (rowi // RWKV_HEAD), 1.0, 0.0).astype(BF16)

    kk = k * kk_ref[...]
    kk = kk / jnp.maximum(jnp.sqrt(_head_sums(kk * kk, ones_blk)), 1e-12)
    k = k * (1.0 + (a - 1.0) * ka_ref[...])
    b = kk * a

    ti = lax.broadcasted_iota(jnp.int32, (t, t), 0)
    tj = lax.broadcasted_iota(jnp.int32, (t, t), 1)
    tri = jnp.where(ti >= tj, 1.0, 0.0).astype(BF16)

    for c in range(tb // t):
        sl = slice(c * t, (c + 1) * t)
        lw_c = lw[sl]
        lw_hi, lw_lo = _split2(lw_c)
        cl = _dot(tri, lw_hi) + _dot(tri, lw_lo)
        cl_end = cl[t - 1:t, :]
        e_neg = jnp.exp(-cl)
        e_end = jnp.exp(cl_end - cl)
        ops = (kk[sl] * jnp.exp(cl - lw_c), r[sl] * jnp.exp(cl), k[sl] * e_neg, b[sl] * e_neg,
               v[sl], k[sl] * e_end, b[sl] * e_end)
        for pi in range(RW_PAIRS):
            ls = slice(pi * LANES, (pi + 1) * LANES)
            for oi, op in enumerate(ops):
                st_ref[c, oi, pi] = op[:, ls]
            pt_ref[c, pi] = jnp.broadcast_to(jnp.exp(cl_end[:, ls]), (SUBLANES, LANES))

    lo_half = lax.broadcasted_iota(jnp.int32, (t, LANES), 1) < RWKV_HEAD
    bi = lax.broadcasted_iota(jnp.int32, (2 * t, 2 * t), 0) % t
    bj = lax.broadcasted_iota(jnp.int32, (2 * t, 2 * t), 1) % t
    strict = bi > bj
    incl = bi >= bj

    def chunk_group(c, pis, hts):
        h2 = 2 * t
        stk = [[_stack_heads(st_ref[c, oi, pi], lo_half) for oi in range(7)] for pi in pis]
        lhs2 = [jnp.concatenate([s[0], s[1]], axis=0).astype(BF16) for s in stk]
        rhs2 = [jnp.concatenate([s[2], s[3]], axis=0).astype(BF16) for s in stk]
        amat = [_dot_nt(a, b) for a, b in zip(lhs2, rhs2)]
        sh = [_dot_nt(a, h.astype(BF16)) for a, h in zip(lhs2, hts)]
        vsb = [s[4].astype(BF16) for s in stk]
        x = [s_[0:h2] + _dot(jnp.where(strict, am[0:h2, 0:h2], 0.0).astype(BF16), v_)
             for s_, am, v_ in zip(sh, amat, vsb)]
        pw = [jnp.where(strict, am[0:h2, h2:], 0.0) for am in amat]
        n, sign = 1, -1.0
        while n < t:
            if 2 * n < t:
                res = [_dot(p_.astype(BF16), jnp.concatenate([p_, x_], axis=1).astype(BF16)) for p_, x_ in zip(pw, x)]
                x = [x_ + sign * r_[:, h2:] for x_, r_ in zip(x, res)]
                pw = [r_[:, 0:h2] for r_ in res]
            else:
                x = [x_ + sign * _dot(p_.astype(BF16), x_.astype(BF16)) for p_, x_ in zip(pw, x)]
            n, sign = 2 * n, 1.0
        new_hts = []
        for i, pi in enumerate(pis):
            am, s = amat[i], stk[i]
            a_r = jnp.concatenate([jnp.where(incl, am[h2:, 0:h2], 0.0), jnp.where(incl, -am[h2:, h2:], 0.0)], axis=1)
            ys = sh[i][h2:] + _dot(a_r.astype(BF16), jnp.concatenate([s[4], x[i]], axis=0).astype(BF16))
            yp_ref[pi, c * t:(c + 1) * t, :] = ys[0:t] + ys[t:]
            lhs3 = jnp.concatenate([s[4].T, -(x[i].T)], axis=1).astype(BF16)
            rhs3 = jnp.concatenate([s[5], s[6]], axis=0).astype(BF16)
            new_hts.append(hts[i] * pt_ref[c, pi][0:1, :] + _dot(lhs3, rhs3))
        return new_hts

    for g0 in range(0, RW_PAIRS, RW_GROUP):
        pis = list(range(g0, g0 + RW_GROUP))
        hts = [state_ref[pi] for pi in pis]
        for c in range(tb // t):
            hts = chunk_group(c, pis, hts)
        for pi, ht in zip(pis, hts):
            state_ref[pi] = ht

    y = jnp.concatenate([yp_ref[pi] for pi in range(RW_PAIRS)], axis=1)
    inv_n = 1.0 / RWKV_HEAD
    mu = _head_sums(y, ones_blk) * inv_n
    yc = y - mu
    var = _head_sums(yc * yc, ones_blk) * inv_n
    y = yc * lax.rsqrt(var + RWKV_GN_EPS) * lnw_ref[...] + lnb_ref[...]
    bonus = _head_sums(r * k * rk_ref[...], ones_blk) * v
    y_ref[...] = ((y + bonus) * g).astype(y_ref.dtype)


def _rwkv_mixer(proj, misc, v_first, mu, w0, w2, a0, a2, g2, k_k, k_a, r_k, ln_w, ln_b, v0, v2):
    bn, sn, _ = proj.shape
    w = RWKV_HEADS * RWKV_HEAD
    has_vres = v_first is not None
    row = lambda x: x.reshape(1, -1)
    padrows = lambda x: jnp.pad(x, ((0, LANES - x.shape[0]), (0, 0)))
    padl = lambda x, n: jnp.pad(x, (0, n - x.shape[0]))
    mu_misc = [padl(mu[3 * w:3 * w + LORA_W], LANES), padl(mu[3 * w + LORA_W:3 * w + LORA_W + LORA_A], LANES),
               mu[3 * w + LORA_W + LORA_A:3 * w + LORA_W + LORA_A + LORA_G]]
    if has_vres:
        mu_misc.append(padl(mu[3 * w + LORA_W + LORA_A + LORA_G:], LANES))
    mu_m = padl(jnp.concatenate(mu_misc), 1024)
    const = lambda shape: pl.BlockSpec(shape, lambda b, c: (0,) * len(shape))
    blk = lambda width, idx: pl.BlockSpec((None, RW_BLK, width), lambda b, c: (b, c, idx))
    seq = pl.BlockSpec((None, RW_BLK, w), lambda b, c: (b, c, 0))
    in_specs = [blk(w, COL_R // w), blk(w, COL_R // w + 1), blk(w, COL_R // w + 2), blk(1024, 0)]
    args = [proj, proj, proj, misc]
    if has_vres:
        in_specs.append(seq)
        args.append(v_first)
    in_specs += [const((1, w))] * 3 + [const((1, 1024)), const((1, w)), const((LANES, w)), const((LANES, w)),
                                       const((1, w)), const((LANES, w)), const((LORA_G, w))] + [const((1, w))] * 5
    w2p = padrows(w2)
    w2_hi = w2p.astype(BF16)
    w2_lo = (w2p - w2_hi.astype(F32)).astype(BF16)
    args += [row(mu[:w]), row(mu[w:2 * w]), row(mu[2 * w:3 * w]), row(mu_m), row(w0), w2_hi, w2_lo, row(a0),
             padrows(a2).astype(BF16), g2.astype(BF16), row(k_k), row(k_a), row(r_k), row(ln_w), row(ln_b)]
    if has_vres:
        in_specs += [const((1, w)), const((LANES, w))]
        args += [row(v0), padrows(v2).astype(BF16)]
    out_shape = [jax.ShapeDtypeStruct((bn, sn, w), BF16)]
    out_specs = [seq]
    if not has_vres:
        out_shape.append(jax.ShapeDtypeStruct((bn, sn, w), F32))
        out_specs.append(seq)
    nc = RW_BLK // RW_T
    outs = pl.pallas_call(
        functools.partial(_rwkv_kernel, has_vres),
        grid=(bn, sn // RW_BLK),
        in_specs=in_specs,
        out_specs=out_specs,
        out_shape=out_shape,
        scratch_shapes=[pltpu.VMEM((RW_PAIRS, LANES, LANES), F32),
                        pltpu.VMEM((SUBLANES, w), F32),
                        pltpu.VMEM((SUBLANES, 1024), F32),
                        pltpu.VMEM((nc, 7, RW_PAIRS, RW_T, LANES), F32),
                        pltpu.VMEM((RW_PAIRS, RW_BLK, LANES), F32),
                        pltpu.VMEM((nc, RW_PAIRS, SUBLANES, LANES), F32)],
        compiler_params=_cparams(("parallel", "arbitrary")),
        name="rwkv7_mixer",
    )(*args)
    if has_vres:
        return outs[0], v_first
    return outs[0], outs[1]


MLA_TM = 256
ATT_TQ = 4096
ATT_TK = 2048
ATT_SUB = 512
ATT_SUB_FULL = 1024
HEAD_Q = 2 * LANES


def _rope_half(x2, cos, sin):
    return x2 * cos + pltpu.roll(x2, MLA_ROPE, 1) * sin


def _mla_prep_kernel(qlat_ref, kvlat_ref, misc_ref, cos_ref, sin_ref, qnw_ref, kvnw_ref, wq_ref, wkt_ref, wv_ref,
                     q_ref, kt_ref, v_ref, kpet_ref):
    def rms(x, w):
        x = x.astype(F32)
        return (x * lax.rsqrt(jnp.mean(x * x, axis=-1, keepdims=True) + RMS_EPS) * w).astype(BF16)

    cos, sin = cos_ref[...], sin_ref[...]
    scale = MLA_QK ** -0.5
    q = _dot(rms(qlat_ref[...], qnw_ref[...]), wq_ref[...])
    for h in range(MLA_HEADS):
        lo = h * HEAD_Q
        q_ref[:, lo:lo + LANES] = (q[:, lo:lo + LANES] * scale).astype(BF16)
        q_ref[:, lo + LANES:lo + HEAD_Q] = (_rope_half(q[:, lo + LANES:lo + HEAD_Q], cos, sin) * scale).astype(BF16)
    kvn = rms(kvlat_ref[...], kvnw_ref[...])
    kt_ref[...] = _dot_nt(wkt_ref[...], kvn).astype(BF16)
    v_ref[...] = _dot(kvn, wv_ref[...]).astype(BF16)
    kpet_ref[...] = _rope_half(misc_ref[:, MISC_KPE:MISC_KPE + LANES], cos, sin).T.astype(BF16)


def _flash_kernel(tq, tk, sub, qi_ref, ki_ref, q_ref, kt_ref, kpet_ref, v_ref, o_ref, m_ref, acc_ref):
    step = pl.program_id(2)
    qi, ki = qi_ref[step], ki_ref[step]
    ratio = tq // tk

    @pl.when(ki == 0)
    def _():
        m_ref[...] = jnp.full_like(m_ref, -jnp.inf)
        acc_ref[...] = jnp.zeros_like(acc_ref)

    def update(diag):
        kcat_t = jnp.concatenate([kt_ref[...], kpet_ref[...]], axis=0)
        vcat = jnp.concatenate([v_ref[...], jnp.ones((tk, LANES), BF16)], axis=1)
        sb = sub if diag is not None else min(tq, ATT_SUB_FULL)
        plan = []
        for r in range(tq // sb):
            row_lo, row_hi = r * sb, (r + 1) * sb - 1
            ncols = tk
            masked = False
            if diag is not None:
                col_lo = diag * tk
                if row_hi < col_lo:
                    continue
                ncols = min(tk, -(-(row_hi - col_lo + 1) // HEAD_Q) * HEAD_Q)
                masked = row_lo < col_lo + ncols - 1
            plan.append((row_lo, ncols, masked))

        def scores(row_lo, ncols, masked):
            s = _dot(q_ref[row_lo:row_lo + sb, :], kcat_t[:, :ncols])
            if masked:
                ri = row_lo + lax.broadcasted_iota(jnp.int32, s.shape, 0)
                ci = diag * tk + lax.broadcasted_iota(jnp.int32, s.shape, 1)
                s = jnp.where(ci <= ri, s, -jnp.inf)
            return s

        def absorb(row_lo, ncols, s):
            rows = slice(row_lo, row_lo + sb)
            tiles = [s[:, j * LANES:(j + 1) * LANES] for j in range(ncols // LANES)]
            fold = tiles[0]
            for tl in tiles[1:]:
                fold = jnp.maximum(fold, tl)
            m_prev = m_ref[rows, :]
            m_new = jnp.maximum(m_prev, jnp.max(fold, axis=-1, keepdims=True))
            p = jnp.concatenate([jnp.exp((tl - m_new).astype(BF16)) for tl in tiles], axis=1)
            alpha = jnp.exp(m_prev - m_new)
            acc_ref[rows, :] = (acc_ref[rows, :] * jnp.concatenate([alpha, alpha], axis=1)
                                + _dot(p, vcat[:ncols]))
            m_ref[rows, :] = m_new

        s_next = scores(*plan[0])
        for idx, (row_lo, ncols, _) in enumerate(plan):
            s_cur = s_next
            if idx + 1 < len(plan):
                s_next = scores(*plan[idx + 1])
            absorb(row_lo, ncols, s_cur)

    @pl.when(ki < qi * ratio)
    def _():
        update(None)

    for d in range(ratio):
        @pl.when(ki == qi * ratio + d)
        def _(d=d):
            update(d)

    @pl.when(ki == (qi + 1) * ratio - 1)
    def _():
        o_ref[...] = (acc_ref[:, 0:MLA_V] / acc_ref[:, MLA_V:]).astype(o_ref.dtype)


def _mla_mixer(proj, misc, cos, sin, q_norm_w, w_q_b, kv_norm_w, w_kv_b):
    bn, sn, _ = proj.shape
    t = bn * sn
    proj2 = proj.reshape(t, proj.shape[-1])
    misc2 = misc.reshape(t, misc.shape[-1])
    wq = w_q_b.reshape(MLA_RANK, MLA_HEADS, MLA_QK)
    pe = wq[..., MLA_NOPE:]
    rot = jnp.concatenate([-pe[..., MLA_ROPE // 2:], pe[..., :MLA_ROPE // 2]], axis=-1)
    wq = jnp.concatenate([wq, rot], axis=-1).reshape(MLA_RANK, MLA_HEADS * HEAD_Q).astype(BF16)
    wkv = w_kv_b.reshape(MLA_RANK, MLA_HEADS, MLA_NOPE + MLA_V)
    wkt = wkv[..., :MLA_NOPE].reshape(MLA_RANK, -1).T.astype(BF16)
    wv = wkv[..., MLA_NOPE:].reshape(MLA_RANK, -1).astype(BF16)
    nq, nk, nv = MLA_HEADS * HEAD_Q, MLA_HEADS * MLA_NOPE, MLA_HEADS * MLA_V
    tm = min(MLA_TM, sn)
    nsb = sn // tm
    rowblk = lambda width, idx: pl.BlockSpec((tm, width), lambda i: (i, idx))
    const = lambda shape: pl.BlockSpec(shape, lambda i: (0, 0))
    colblk = lambda rows: pl.BlockSpec((None, rows, tm), lambda i: (i // nsb, 0, i % nsb))
    q, kt, v, kpet = pl.pallas_call(
        _mla_prep_kernel,
        grid=(t // tm,),
        in_specs=[rowblk(MLA_RANK, COL_QLAT // MLA_RANK), rowblk(MLA_RANK, COL_KVLAT // MLA_RANK),
                  rowblk(1024, 0), rowblk(LANES, 0), rowblk(LANES, 0),
                  const((1, MLA_RANK)), const((1, MLA_RANK)), const((MLA_RANK, nq)), const((nk, MLA_RANK)),
                  const((MLA_RANK, nv))],
        out_specs=[rowblk(nq, 0), colblk(nk), rowblk(nv, 0), colblk(LANES)],
        out_shape=[jax.ShapeDtypeStruct((t, nq), BF16), jax.ShapeDtypeStruct((bn, nk, sn), BF16),
                   jax.ShapeDtypeStruct((t, nv), BF16), jax.ShapeDtypeStruct((bn, LANES, sn), BF16)],
        compiler_params=_cparams(("parallel",)),
        name="mla_prep",
    )(proj2, proj2, misc2, cos, sin, q_norm_w.reshape(1, -1), kv_norm_w.reshape(1, -1), wq, wkt, wv)

    tq, tk = min(ATT_TQ, sn), min(ATT_TK, sn)
    sub = min(ATT_SUB, tq)
    ratio = tq // tk
    pairs = [(a, b) for a in range(sn // tq) for b in range((a + 1) * ratio)]
    qi_arr = jnp.asarray([p[0] for p in pairs], jnp.int32)
    ki_arr = jnp.asarray([p[1] for p in pairs], jnp.int32)
    out = pl.pallas_call(
        functools.partial(_flash_kernel, tq, tk, sub),
        grid_spec=pltpu.PrefetchScalarGridSpec(
            num_scalar_prefetch=2,
            grid=(bn, MLA_HEADS, len(pairs)),
            in_specs=[pl.BlockSpec((None, tq, HEAD_Q), lambda b, h, s, qi, ki: (b, qi[s], h)),
                      pl.BlockSpec((None, MLA_NOPE, tk), lambda b, h, s, qi, ki: (b, h, ki[s])),
                      pl.BlockSpec((None, LANES, tk), lambda b, h, s, qi, ki: (b, 0, ki[s])),
                      pl.BlockSpec((None, tk, MLA_V), lambda b, h, s, qi, ki: (b, ki[s], h))],
            out_specs=pl.BlockSpec((None, tq, MLA_V), lambda b, h, s, qi, ki: (b, qi[s], h)),
            scratch_shapes=[pltpu.VMEM((tq, LANES), F32), pltpu.VMEM((tq, 2 * MLA_V), F32)]),
        out_shape=jax.ShapeDtypeStruct((bn, sn, MLA_HEADS * MLA_V), BF16),
        compiler_params=_cparams(("parallel", "parallel", "arbitrary")),
        name="mla_attention",
    )(qi_arr, ki_arr, q.reshape(bn, sn, nq), kt, kpet, v.reshape(bn, sn, nv))
    return out


MERGE_TM = 256


def _merge_kernel(g0_ref, g1_ref, g2_ref, ys_ref, yr_ref, ym_ref, x_ref, wo_ref, lnw_ref, lnb_ref, xo_ref, xb_ref):
    gate = lambda g_ref, y_ref: _sigmoid(g_ref[...].astype(F32)) * y_ref[...].astype(F32)
    merged = gate(g0_ref, ys_ref) + gate(g1_ref, yr_ref) + gate(g2_ref, ym_ref)
    h = ALPHA * x_ref[...] + _dot(merged.astype(BF16), wo_ref[...])
    y = _layernorm(h, lnw_ref[...], lnb_ref[...])
    xo_ref[...] = y
    xb_ref[...] = y.astype(BF16)


def _merge_out(proj2, y_ssd, y_rwkv, y_mla, x, w_out, ln_w, ln_b):
    t, d = x.shape
    tm = min(MERGE_TM, t)
    rowblk = lambda idx: pl.BlockSpec((tm, d), lambda i: (i, idx))
    const = lambda shape: pl.BlockSpec(shape, lambda i: (0, 0))
    return pl.pallas_call(
        _merge_kernel,
        grid=(t // tm,),
        in_specs=[rowblk(0), rowblk(1), rowblk(2), rowblk(0), rowblk(0), rowblk(0), rowblk(0),
                  const((d, d)), const((1, d)), const((1, d))],
        out_specs=[rowblk(0), rowblk(0)],
        out_shape=[jax.ShapeDtypeStruct((t, d), F32), jax.ShapeDtypeStruct((t, d), BF16)],
        compiler_params=_cparams(("parallel",)),
        name="merge_out_ln1",
    )(proj2, proj2, proj2, y_ssd, y_rwkv, y_mla, x, w_out.astype(BF16), ln_w.reshape(1, d), ln_b.reshape(1, d))


FFN_TM = 512
FFN_TF = 512


def _ffn_kernel(xb_ref, x_ref, w1_ref, w3_ref, w2_ref, lnw_ref, lnb_ref, o_ref, ob_ref, acc_ref):
    f = pl.program_id(1)

    @pl.when(f == 0)
    def _():
        acc_ref[...] = jnp.zeros_like(acc_ref)

    xb = xb_ref[...]
    h = _silu(_dot(xb, w1_ref[...])) * _dot(xb, w3_ref[...])
    acc_ref[...] += _dot(h.astype(BF16), w2_ref[...])

    @pl.when(f == pl.num_programs(1) - 1)
    def _():
        y = _layernorm(ALPHA * x_ref[...] + acc_ref[...], lnw_ref[...], lnb_ref[...])
        o_ref[...] = y
        ob_ref[...] = y.astype(BF16)


def _ffn_dense(xb, x, w1, w3, w2, ln_w, ln_b):
    t, d = x.shape
    ff = w1.shape[1]
    tm, tf = min(FFN_TM, t), FFN_TF
    return pl.pallas_call(
        _ffn_kernel,
        grid=(t // tm, ff // tf),
        in_specs=[pl.BlockSpec((tm, d), lambda i, f: (i, 0)), pl.BlockSpec((tm, d), lambda i, f: (i, 0)),
                  pl.BlockSpec((d, tf), lambda i, f: (0, f)), pl.BlockSpec((d, tf), lambda i, f: (0, f)),
                  pl.BlockSpec((tf, d), lambda i, f: (f, 0)),
                  pl.BlockSpec((1, d), lambda i, f: (0, 0)), pl.BlockSpec((1, d), lambda i, f: (0, 0))],
        out_specs=[pl.BlockSpec((tm, d), lambda i, f: (i, 0)), pl.BlockSpec((tm, d), lambda i, f: (i, 0))],
        out_shape=[jax.ShapeDtypeStruct((t, d), F32), jax.ShapeDtypeStruct((t, d), BF16)],
        scratch_shapes=[pltpu.VMEM((tm, d), F32)],
        compiler_params=_cparams(("parallel", "arbitrary")),
        name="ffn_dense_ln2",
    )(xb, x, w1.astype(BF16), w3.astype(BF16), w2.astype(BF16), ln_w.reshape(1, d), ln_b.reshape(1, d))


ROUTER_TM = 512
MOE_TM = 512
MOE_TF = 512
COMBINE_TM = 256


def _router_kernel(x_ref, wrh_ref, wrl_ref, idx_ref, wgt_ref, cnt_ref):
    @pl.when(pl.program_id(0) == 0)
    def _():
        cnt_ref[...] = jnp.zeros_like(cnt_ref)

    tm = x_ref.shape[0]
    logits = _dot_3pass(x_ref[...], wrh_ref[...], wrl_ref[...])
    lane = lax.broadcasted_iota(jnp.int32, logits.shape, 1)
    lg = jnp.where(lane < N_EXPERTS, logits, -jnp.inf)
    m1 = jnp.max(lg, axis=-1, keepdims=True)
    i1 = jnp.min(jnp.where(lg == m1, lane, LANES), axis=-1, keepdims=True)
    lg2 = jnp.where(lane == i1, -jnp.inf, lg)
    m2 = jnp.max(lg2, axis=-1, keepdims=True)
    i2 = jnp.min(jnp.where(lg2 == m2, lane, LANES), axis=-1, keepdims=True)
    e = jnp.exp(m2 - m1)
    wgt_ref[...] = jnp.where(lane == 0, 1.0 / (1.0 + e), jnp.where(lane == 1, e / (1.0 + e), 0.0))

    hit1, hit2 = lane == i1, lane == i2
    both = jnp.where(hit1 | hit2, 1.0, 0.0)
    ri = lax.broadcasted_iota(jnp.int32, (tm, tm), 0)
    ci = lax.broadcasted_iota(jnp.int32, (tm, tm), 1)
    before = cnt_ref[0:1, :] + _dot(jnp.where(ri > ci, 1.0, 0.0).astype(BF16), both.astype(BF16))
    rank1 = jnp.sum(jnp.where(hit1, before, 0.0), axis=-1, keepdims=True).astype(jnp.int32)
    rank2 = jnp.sum(jnp.where(hit2, before, 0.0), axis=-1, keepdims=True).astype(jnp.int32)
    idx_ref[...] = jnp.where(lane == 0, i1, jnp.where(lane == 1, i2, jnp.where(lane == 2, rank1,
                                                                              jnp.where(lane == 3, rank2, 0))))
    cnt_ref[...] = jnp.broadcast_to(before[tm - 1:tm, :] + both[tm - 1:tm, :], cnt_ref.shape)


def _moe_ffn_kernel(nf_static, be_ref, nused_ref, tok_ref, tok_next_ref, x_hbm, w1_ref, w3_ref, w2_ref, o_ref, xg_ref,
                    xb_ref, acc_ref, sem):
    i, f = pl.program_id(0), pl.program_id(1)
    ni, nf = pl.num_programs(0), pl.num_programs(1)
    rows = xg_ref.shape[1]
    per = -(-rows // nf_static)
    tail = rows - (nf_static - 1) * per
    slot = i % 2

    def row_copy(idx_ref, sl, r):
        return pltpu.make_async_copy(x_hbm.at[pl.ds(idx_ref[0, 0, r], 1), :], xg_ref.at[sl, pl.ds(r, 1), :],
                                     sem.at[sl])

    def wait_all(idx_ref, sl):
        for r in range(rows):
            row_copy(idx_ref, sl, r).wait()

    @pl.when(f == 0)
    def _():
        @pl.when(i == 0)
        def _():
            def body(r, c):
                row_copy(tok_ref, slot, r).start()
                return c
            lax.fori_loop(0, rows, body, 0)

        wait_all(tok_ref, slot)
        xb_ref[...] = xg_ref[slot].astype(BF16)
        acc_ref[...] = jnp.zeros_like(acc_ref)

    base = f * per

    def fetch_share():
        for j in range(tail):
            row_copy(tok_next_ref, 1 - slot, base + j).start()

    @pl.when(i < nused_ref[0])
    def _():
        fetch_share()
        xb = xb_ref[...]
        h = _silu(_dot(xb, w1_ref[...])) * _dot(xb, w3_ref[...])
        acc_ref[...] += _dot(h.astype(BF16), w2_ref[...])

    @pl.when(i >= nused_ref[0])
    def _():
        fetch_share()

    @pl.when(f < nf - 1)
    def _():
        for j in range(tail, per):
            row_copy(tok_next_ref, 1 - slot, base + j).start()

    @pl.when(f == nf - 1)
    def _():
        o_ref[...] = acc_ref[...]

        @pl.when(i == ni - 1)
        def _():
            wait_all(tok_next_ref, 1 - slot)


def _combine_kernel(dst_ref, dst_next_ref, y_hbm, wgt_ref, x_ref, lnw_ref, lnb_ref, o_ref, yg_ref, sem):
    i, n = pl.program_id(0), pl.num_programs(0)
    rows = x_ref.shape[0]
    slot = i % 2

    def row_copy(idx_ref, sl, r, k):
        return pltpu.make_async_copy(y_hbm.at[pl.ds(idx_ref[0, 0, TOP_K * r + k], 1), :],
                                     yg_ref.at[sl, k, pl.ds(r, 1), :], sem.at[sl])

    def for_all(idx_ref, sl, op):
        for r in range(rows):
            for k in range(TOP_K):
                op(row_copy(idx_ref, sl, r, k))

    @pl.when(i == 0)
    def _():
        for_all(dst_ref, slot, lambda c: c.start())

    for_all(dst_next_ref, 1 - slot, lambda c: c.start())
    for_all(dst_ref, slot, lambda c: c.wait())
    wgt = wgt_ref[...]
    f = wgt[:, 0:1] * yg_ref[slot, 0] + wgt[:, 1:2] * yg_ref[slot, 1]
    o_ref[...] = _layernorm(ALPHA * x_ref[...] + f, lnw_ref[...], lnb_ref[...])

    @pl.when(i == n - 1)
    def _():
        for_all(dst_next_ref, 1 - slot, lambda c: c.wait())


def _ffn_moe(x, router, w1, w3, w2, ln_w, ln_b):
    t, d = x.shape
    ne, _, ff = w1.shape
    tm = min(ROUTER_TM, t)
    rw = jnp.pad(router, ((0, 0), (0, LANES - ne)))
    rw_hi = rw.astype(BF16)
    rw_lo = (rw - rw_hi.astype(F32)).astype(BF16)
    idx, wgt, cnt = pl.pallas_call(
        _router_kernel,
        grid=(t // tm,),
        in_specs=[pl.BlockSpec((tm, d), lambda i: (i, 0)), pl.BlockSpec((d, LANES), lambda i: (0, 0)),
                  pl.BlockSpec((d, LANES), lambda i: (0, 0))],
        out_specs=[pl.BlockSpec((tm, LANES), lambda i: (i, 0))] * 2 + [pl.BlockSpec((SUBLANES, LANES), lambda i: (0, 0))],
        out_shape=[jax.ShapeDtypeStruct((t, LANES), jnp.int32), jax.ShapeDtypeStruct((t, LANES), F32),
                   jax.ShapeDtypeStruct((SUBLANES, LANES), F32)],
        compiler_params=_cparams(("arbitrary",)),
        name="moe_router",
    )(x, rw_hi, rw_lo)

    blk = MOE_TM
    n_assign = t * TOP_K
    flat_e = idx[:, :TOP_K].reshape(-1)
    rank = idx[:, TOP_K:2 * TOP_K].reshape(-1)
    counts = cnt[0, :ne].astype(jnp.int32)
    padded = (counts + blk - 1) // blk * blk
    pad_end = jnp.cumsum(padded)
    dest = (pad_end - padded)[flat_e] + rank
    n_blocks = -(-(n_assign + ne * (blk - 1)) // blk)
    n_rows = n_blocks * blk
    token_of_row = jnp.zeros((n_rows,), jnp.int32).at[dest].set(jnp.arange(n_assign, dtype=jnp.int32) // TOP_K,
                                                                unique_indices=True)
    block_expert = jnp.minimum(jnp.searchsorted(pad_end, jnp.arange(n_blocks, dtype=jnp.int32) * blk, side='right'),
                               ne - 1).astype(jnp.int32)

    tf = MOE_TF
    n_used = (pad_end[-1:] // blk).astype(jnp.int32)
    wtile = lambda i, f, nu: jnp.where(i < nu[0], f, 0)
    yb = pl.pallas_call(
        functools.partial(_moe_ffn_kernel, ff // tf),
        grid_spec=pltpu.PrefetchScalarGridSpec(
            num_scalar_prefetch=2,
            grid=(n_blocks, ff // tf),
            in_specs=[pl.BlockSpec((1, 1, blk), lambda i, f, be, nu: (i, 0, 0), memory_space=pltpu.SMEM),
                      pl.BlockSpec((1, 1, blk), lambda i, f, be, nu: (jnp.minimum(i + 1, n_blocks - 1), 0, 0),
                                   memory_space=pltpu.SMEM),
                      pl.BlockSpec(memory_space=pl.ANY),
                      pl.BlockSpec((None, d, tf), lambda i, f, be, nu: (be[i], 0, wtile(i, f, nu))),
                      pl.BlockSpec((None, d, tf), lambda i, f, be, nu: (be[i], 0, wtile(i, f, nu))),
                      pl.BlockSpec((None, tf, d), lambda i, f, be, nu: (be[i], wtile(i, f, nu), 0))],
            out_specs=pl.BlockSpec((blk, d), lambda i, f, be, nu: (i, 0)),
            scratch_shapes=[pltpu.VMEM((2, blk, d), F32), pltpu.VMEM((blk, d), BF16), pltpu.VMEM((blk, d), F32),
                            pltpu.SemaphoreType.DMA((2,))]),
        out_shape=jax.ShapeDtypeStruct((n_rows, d), F32),
        compiler_params=_cparams(("arbitrary", "arbitrary")),
        name="moe_expert_ffn",
    )(block_expert, n_used, token_of_row.reshape(n_blocks, 1, blk), token_of_row.reshape(n_blocks, 1, blk), x,
      w1.astype(BF16), w3.astype(BF16), w2.astype(BF16))

    tc = min(COMBINE_TM, t)
    return pl.pallas_call(
        _combine_kernel,
        grid=(t // tc,),
        in_specs=[pl.BlockSpec((1, 1, TOP_K * tc), lambda i: (i, 0, 0), memory_space=pltpu.SMEM),
                  pl.BlockSpec((1, 1, TOP_K * tc), lambda i: (jnp.minimum(i + 1, t // tc - 1), 0, 0),
                               memory_space=pltpu.SMEM),
                  pl.BlockSpec(memory_space=pl.ANY),
                  pl.BlockSpec((tc, LANES), lambda i: (i, 0)), pl.BlockSpec((tc, d), lambda i: (i, 0)),
                  pl.BlockSpec((1, d), lambda i: (0, 0)), pl.BlockSpec((1, d), lambda i: (0, 0))],
        out_specs=pl.BlockSpec((tc, d), lambda i: (i, 0)),
        out_shape=jax.ShapeDtypeStruct((t, d), F32),
        scratch_shapes=[pltpu.VMEM((2, TOP_K, tc, d), F32), pltpu.SemaphoreType.DMA((2,))],
        compiler_params=_cparams(("arbitrary",)),
        name="moe_combine_ln2",
    )(dest.reshape(t // tc, 1, TOP_K * tc), dest.reshape(t // tc, 1, TOP_K * tc), yb, wgt, x,
      ln_w.reshape(1, d), ln_b.reshape(1, d))


def _pack_w_in(w, w_vres):
    d = w.shape[0]
    w = w.astype(BF16)
    w_vres = None if w_vres is None else w_vres.astype(BF16)
    o = 0

    def take(n):
        nonlocal o
        s = w[:, o:o + n]
        o += n
        return s

    padc = lambda s, n: jnp.pad(s, ((0, 0), (0, n - s.shape[1])))
    gates = take(3 * W2K)
    z = take(W2K)
    xbc = take(W2K + 2 * SSD_BC)
    dt = take(SSD_HEADS)
    qlat = take(MLA_RANK)
    kvlat = take(MLA_RANK)
    kpe = take(MLA_ROPE)
    rkv = take(3 * W2K)
    w_lo, a_lo, g_lo = take(LORA_W), take(LORA_A), take(LORA_G)
    v_lo = jnp.zeros((d, LANES), w.dtype) if w_vres is None else padc(w_vres, LANES)
    kpe_rot = jnp.concatenate([-kpe[:, MLA_ROPE // 2:], kpe[:, :MLA_ROPE // 2]], axis=1)
    misc = jnp.concatenate([padc(w_lo, LANES), padc(a_lo, LANES), g_lo, v_lo, padc(dt, LANES), kpe, kpe_rot], axis=1)
    return jnp.concatenate([gates, z, rkv, xbc, qlat, kvlat, padc(misc, 1024)], axis=1).astype(BF16)


PROJ_TM = 2048
PROJ_TN = 1024


def kernel(x, positions, w_in, w_in_vres, w_out, ssd_conv_w, ssd_conv_b, ssd_dt_bias, ssd_a_log, ssd_d, ssd_norm_w, rwkv_mu, rwkv_mu_vres, rwkv_w0, rwkv_w2, rwkv_a0, rwkv_a2, rwkv_g2, rwkv_v0, rwkv_v2, rwkv_k_k, rwkv_k_a, rwkv_r_k, rwkv_ln_w, rwkv_ln_b, mla_q_norm_w, mla_w_q_b, mla_kv_norm_w, mla_w_kv_b, ln1_w, ln1_b, ln2_w, ln2_b, ffn_w1, ffn_w3, ffn_w2, moe_router, moe_w1, moe_w3, moe_w2):
    bn, sn, d = x.shape
    t = bn * sn
    cos, sin = _rope_tables(positions)
    xf = x.reshape(t, d)
    xb = xf
    v_first = None
    for l in range(DEPTH):
        if l == 0:
            wp, mu, v0, v2 = _pack_w_in(w_in[l], None), rwkv_mu[l], None, None
        else:
            wp = _pack_w_in(w_in[l], w_in_vres[l - 1])
            mu = jnp.concatenate([rwkv_mu[l], rwkv_mu_vres[l - 1]], axis=0)
            v0, v2 = rwkv_v0[l - 1], rwkv_v2[l - 1]
        tm = min(PROJ_TM // 2 if xb.dtype == F32 else PROJ_TM, t)
        proj2 = _matmul(xb, wp[:, :COL_MISC], BF16, tm, PROJ_TN)
        misc = _matmul(xb, wp[:, COL_MISC:], F32, tm, N_PROJ - COL_MISC).reshape(bn, sn, -1)
        proj = proj2.reshape(bn, sn, COL_MISC)
        y_ssd = _ssd_mixer(proj, misc, ssd_conv_w[l], ssd_conv_b[l], ssd_dt_bias[l], ssd_a_log[l], ssd_d[l],
                           ssd_norm_w[l])
        y_rwkv, v_first = _rwkv_mixer(proj, misc, v_first, mu, rwkv_w0[l], rwkv_w2[l], rwkv_a0[l], rwkv_a2[l],
                                      rwkv_g2[l], rwkv_k_k[l], rwkv_k_a[l], rwkv_r_k[l], rwkv_ln_w[l], rwkv_ln_b[l],
                                      v0, v2)
        y_mla = _mla_mixer(proj, misc, cos, sin, mla_q_norm_w[l], mla_w_q_b[l], mla_kv_norm_w[l], mla_w_kv_b[l])
        x1, x1b = _merge_out(proj2, y_ssd.reshape(t, d), y_rwkv.reshape(t, d), y_mla.reshape(t, d), xf, w_out[l],
                             ln1_w[l], ln1_b[l])
        if l % 2 == 0:
            xf, xb = _ffn_dense(x1b, x1, ffn_w1[l // 2], ffn_w3[l // 2], ffn_w2[l // 2], ln2_w[l], ln2_b[l])
        else:
            xf = _ffn_moe(x1, moe_router[l // 2], moe_w1[l // 2], moe_w3[l // 2], moe_w2[l // 2], ln2_w[l], ln2_b[l])
            xb = xf
    return xf.reshape(bn, sn, d)
```

```python
import functools

import jax
import jax.numpy as jnp
from jax import lax
from jax.experimental import pallas as pl
from jax.experimental.pallas import tpu as pltpu

F32 = jnp.float32
BF16 = jnp.bfloat16

D_MODEL = 2048
DEPTH = 2
ALPHA = (2 * DEPTH) ** 0.25
LN_EPS = 1e-5
RMS_EPS = 1e-6
SSD_HEADS, SSD_HEAD_DIM, SSD_GROUPS, SSD_STATE, SSD_CONV = 32, 64, 4, 128, 4
SSD_BC = SSD_GROUPS * SSD_STATE
RWKV_HEADS, RWKV_HEAD = 32, 64
RWKV_GN_EPS = 64e-5
LORA_W, LORA_A, LORA_G, LORA_V = 96, 96, 256, 64
MLA_HEADS, MLA_NOPE, MLA_ROPE, MLA_V, MLA_RANK = 16, 128, 64, 128, 512
MLA_QK = MLA_NOPE + MLA_ROPE
ROPE_THETA = 10000.0
D_FF = 5632
N_EXPERTS = 8
TOP_K = 2

LANES = 128
SUBLANES = 8
VMEM_LIMIT = 56 * 1024 * 1024

W2K = 2048
COL_GATE = 0
COL_Z = 3 * W2K
COL_R = 4 * W2K
COL_XS = 7 * W2K
COL_BC = 8 * W2K
COL_QLAT = COL_BC + 1024
COL_KVLAT = COL_QLAT + 512
COL_MISC = COL_KVLAT + 512
MISC_W, MISC_A, MISC_G, MISC_V, MISC_DT, MISC_KPE = 0, 128, 256, 512, 640, 768
N_PROJ = COL_MISC + 1024

SSD_Q = 128
RW_T = 64
RW_BLK = 128


def _cparams(sem, vmem=VMEM_LIMIT):
    return pltpu.CompilerParams(dimension_semantics=sem, vmem_limit_bytes=vmem)


def _dot(a, b):
    return jnp.dot(a, b, preferred_element_type=F32)


def _dot_nt(a, b):
    return lax.dot_general(a, b, (((1,), (1,)), ((), ())), preferred_element_type=F32)


def _split3(a):
    hi = a.astype(BF16)
    r1 = a - hi.astype(F32)
    mid = r1.astype(BF16)
    lo = (r1 - mid.astype(F32)).astype(BF16)
    return hi, mid, lo


def _dot_exact_rhs(a_bf, b):
    hi, mid, lo = _split3(b)
    return _dot(a_bf, hi) + _dot(a_bf, mid) + _dot(a_bf, lo)


def _sigmoid(x):
    return 1.0 / (1.0 + jnp.exp(-x))


def _silu(x):
    return x * _sigmoid(x)


def _softplus(x):
    return jnp.maximum(x, 0.0) + jnp.log(1.0 + jnp.exp(-jnp.abs(x)))


def _layernorm(x, w, b):
    mu = jnp.mean(x, axis=-1, keepdims=True)
    xc = x - mu
    var = jnp.mean(xc * xc, axis=-1, keepdims=True)
    return xc * lax.rsqrt(var + LN_EPS) * w + b


def _mm_kernel(x_ref, w_ref, o_ref):
    o_ref[...] = _dot(x_ref[...].astype(BF16), w_ref[...]).astype(o_ref.dtype)


def _matmul(x, w, out_dtype, tm, tn):
    m, k = x.shape
    n = w.shape[1]
    return pl.pallas_call(
        _mm_kernel,
        grid=(m // tm, n // tn),
        in_specs=[pl.BlockSpec((tm, k), lambda i, j: (i, 0)),
                  pl.BlockSpec((k, tn), lambda i, j: (0, j))],
        out_specs=pl.BlockSpec((tm, tn), lambda i, j: (i, j)),
        out_shape=jax.ShapeDtypeStruct((m, n), out_dtype),
        compiler_params=_cparams(("parallel", "arbitrary")),
        name="in_proj",
    )(x, w)


def _rope_kernel(pos_ref, freq_ref, cos_ref, sin_ref):
    ang = pos_ref[...] * freq_ref[...]
    valid = lax.broadcasted_iota(jnp.int32, ang.shape, 1) < MLA_ROPE
    cos_ref[...] = jnp.where(valid, jnp.cos(ang), 0.0)
    sin_ref[...] = jnp.where(valid, jnp.sin(ang), 0.0)


def _rope_tables(positions):
    t = positions.size
    tm = min(t, 1024)
    pos = positions.reshape(t, 1).astype(F32)
    inv_freq = ROPE_THETA ** (-jnp.arange(0, MLA_ROPE, 2, dtype=F32) / MLA_ROPE)
    freq = jnp.concatenate([inv_freq, inv_freq, jnp.zeros((LANES - MLA_ROPE,), F32)]).reshape(1, LANES)
    return pl.pallas_call(
        _rope_kernel,
        grid=(t // tm,),
        in_specs=[pl.BlockSpec((tm, 1), lambda i: (i, 0)),
                  pl.BlockSpec((1, LANES), lambda i: (0, 0))],
        out_specs=[pl.BlockSpec((tm, LANES), lambda i: (i, 0))] * 2,
        out_shape=[jax.ShapeDtypeStruct((t, LANES), F32)] * 2,
        compiler_params=_cparams(("parallel",)),
        name="rope_tables",
    )(pos, freq)


def _ssd_kernel(z_ref, xs_ref, bc_ref, misc_ref, cwx_ref, cwb_ref, cbx_ref, cbb_ref, dtb_ref, alog_ref,
                d_ref, nw_ref, e_ref, y_ref, state_ref, bufx_ref, bufb_ref):
    q = SSD_Q
    hp = SSD_HEADS // SSD_GROUPS * SSD_HEAD_DIM

    @pl.when(pl.program_id(1) == 0)
    def _():
        state_ref[...] = jnp.zeros_like(state_ref)
        bufx_ref[0:SUBLANES, :] = jnp.zeros((SUBLANES, bufx_ref.shape[1]), F32)
        bufb_ref[0:SUBLANES, :] = jnp.zeros((SUBLANES, bufb_ref.shape[1]), F32)

    bufx_ref[SUBLANES:SUBLANES + q, :] = xs_ref[...].astype(F32)
    bufb_ref[SUBLANES:SUBLANES + q, :] = bc_ref[...].astype(F32)

    def conv(buf_ref, w_ref, b_ref):
        acc = b_ref[...] + w_ref[SSD_CONV - 1:SSD_CONV, :] * buf_ref[SUBLANES:SUBLANES + q, :]
        for k in range(SSD_CONV - 1):
            off = SUBLANES - (SSD_CONV - 1) + k
            acc = acc + w_ref[k:k + 1, :] * buf_ref[off:off + q, :]
        return _silu(acc)

    xs = conv(bufx_ref, cwx_ref, cbx_ref)
    bc = conv(bufb_ref, cwb_ref, cbb_ref)
    bufx_ref[0:SUBLANES, :] = bufx_ref[q:q + SUBLANES, :]
    bufb_ref[0:SUBLANES, :] = bufb_ref[q:q + SUBLANES, :]

    dt = _softplus(misc_ref[:, MISC_DT:MISC_DT + LANES] + dtb_ref[...])
    da = dt * (-jnp.exp(alog_ref[...]))
    row = lax.broadcasted_iota(jnp.int32, (q, q), 0)
    col = lax.broadcasted_iota(jnp.int32, (q, q), 1)
    causal = row >= col
    tri = jnp.where(causal, 1.0, 0.0).astype(BF16)
    cum = _dot_exact_rhs(tri, da)
    cum_t = cum.T
    ecum = jnp.exp(cum)
    toend = jnp.exp(cum[q - 1:q, :] - cum)
    e_mat = e_ref[...]
    def spread(a):
        hi, lo = _split2(a)
        return _dot(hi, e_mat) + _dot(lo, e_mat)

    dt_e, ecum_e, toend_e = spread(dt), spread(ecum), spread(toend)

    xdt = xs * dt_e
    xdt_b = xdt.astype(BF16)
    xw_b = (xdt * toend_e).astype(BF16)
    lane = lax.broadcasted_iota(jnp.int32, (q, LANES), 1)
    lo_half = lane < SSD_HEAD_DIM

    y_groups = []
    for g in range(SSD_GROUPS):
        b_g = bc[:, g * SSD_STATE:(g + 1) * SSD_STATE]
        c_g = bc[:, SSD_BC + g * SSD_STATE:SSD_BC + (g + 1) * SSD_STATE]
        b_gb = b_g.astype(BF16)
        c_gb = c_g.astype(BF16)
        cb = _dot_nt(c_gb, b_gb)
        st = state_ref[:, g * hp:(g + 1) * hp]
        y_inter = _dot(c_gb, st.astype(BF16)) * ecum_e[:, g * hp:(g + 1) * hp]
        parts = []
        for pr in range(hp // LANES):
            ms = []
            for e in range(2):
                h = g * (SSD_HEADS // SSD_GROUPS) + pr * 2 + e
                ci = jnp.broadcast_to(cum[:, h:h + 1], (q, q))
                cj = jnp.broadcast_to(cum_t[h:h + 1, :], (q, q))
                dec = jnp.exp(jnp.where(causal, ci - cj, -jnp.inf))
                ms.append((cb * dec).astype(BF16))
            lo = g * hp + pr * LANES
            xp = xdt_b[:, lo:lo + LANES]
            zero = jnp.zeros_like(xp)
            rhs = jnp.concatenate([jnp.where(lo_half, xp, zero), jnp.where(lo_half, zero, xp)], axis=0)
            parts.append(_dot(jnp.concatenate(ms, axis=1), rhs))
        y_groups.append(jnp.concatenate(parts, axis=1) + y_inter)
        upd = _dot(b_g.T.astype(BF16), xw_b[:, g * hp:(g + 1) * hp])
        state_ref[:, g * hp:(g + 1) * hp] = st * ecum_e[q - 1:q, g * hp:(g + 1) * hp] + upd

    y = jnp.concatenate(y_groups, axis=1) + d_ref[...] * xs
    y = y * _silu(z_ref[...].astype(F32))
    outs = []
    for g in range(SSD_GROUPS):
        yg = y[:, g * hp:(g + 1) * hp]
        outs.append(yg * lax.rsqrt(jnp.mean(yg * yg, axis=-1, keepdims=True) + RMS_EPS))
    y_ref[...] = (jnp.concatenate(outs, axis=1) * nw_ref[...]).astype(y_ref.dtype)


def _ssd_mixer(proj, misc, conv_w, conv_b, dt_bias, a_log, d_skip, norm_w):
    bn, sn, _ = proj.shape
    w = SSD_HEADS * SSD_HEAD_DIM
    pad = lambda v: jnp.pad(v, (0, LANES - v.shape[0])).reshape(1, LANES)
    a_log_p = jnp.pad(a_log, (0, LANES - SSD_HEADS), constant_values=-jnp.inf).reshape(1, LANES)
    expand = jnp.pad(jnp.repeat(jnp.eye(SSD_HEADS, dtype=BF16), SSD_HEAD_DIM, axis=1),
                     ((0, LANES - SSD_HEADS), (0, 0)))
    row = lambda v: v.reshape(1, -1)
    const = lambda shape: pl.BlockSpec(shape, lambda b, c: (0, 0))
    blk = lambda width, idx: pl.BlockSpec((None, SSD_Q, width), lambda b, c: (b, c, idx))
    return pl.pallas_call(
        _ssd_kernel,
        grid=(bn, sn // SSD_Q),
        in_specs=[blk(w, COL_Z // w), blk(w, COL_XS // w), blk(1024, COL_BC // 1024), blk(1024, 0),
                  const((SSD_CONV, w)), const((SSD_CONV, 2 * SSD_BC)), const((1, w)), const((1, 2 * SSD_BC)),
                  const((1, LANES)), const((1, LANES)), const((1, w)), const((1, w)), const((LANES, w))],
        out_specs=pl.BlockSpec((None, SSD_Q, w), lambda b, c: (b, c, 0)),
        out_shape=jax.ShapeDtypeStruct((bn, sn, w), BF16),
        scratch_shapes=[pltpu.VMEM((SSD_STATE, w), F32),
                        pltpu.VMEM((SSD_Q + SUBLANES, w), F32),
                        pltpu.VMEM((SSD_Q + SUBLANES, 2 * SSD_BC), F32)],
        compiler_params=_cparams(("parallel", "arbitrary")),
        name="ssd_mixer",
    )(proj, proj, proj, misc, conv_w[:, :w], conv_w[:, w:], row(conv_b[:w]), row(conv_b[w:]),
      pad(dt_bias), a_log_p, row(jnp.repeat(d_skip, SSD_HEAD_DIM)), row(norm_w), expand)


def _split2(a):
    hi = a.astype(BF16)
    return hi, (a - hi.astype(F32)).astype(BF16)


def _dot_3pass(a, b_hi, b_lo):
    a_hi, a_lo = _split2(a)
    return _dot(a_hi, b_hi) + (_dot(a_hi, b_lo) + _dot(a_lo, b_hi))


def _head_sums(x, ones_blk):
    outs = []
    for s in range(x.shape[1] // LANES):
        outs.append(_dot(x[:, s * LANES:(s + 1) * LANES].astype(BF16), ones_blk))
    return jnp.concatenate(outs, axis=1)


def _stack_heads(x, lo_half):
    zero = jnp.zeros_like(x)
    return jnp.concatenate([jnp.where(lo_half, x, zero), jnp.where(lo_half, zero, x)], axis=0)


RW_PAIRS = RWKV_HEADS // 2
RW_GROUP = 16


def _rwkv_kernel(has_vres, *refs):
    if has_vres:
        (r_ref, k_ref, v_ref, misc_ref, vfirst_ref, mur_ref, muk_ref, muv_ref, mum_ref, w0_ref, w2h_ref, w2l_ref,
         a0_ref, a2_ref, g2_ref, kk_ref, ka_ref, rk_ref, lnw_ref, lnb_ref, v0_ref, v2_ref,
         y_ref, state_ref, carry_ref, carrym_ref, st_ref, yp_ref, pt_ref) = refs
    else:
        (r_ref, k_ref, v_ref, misc_ref, mur_ref, muk_ref, muv_ref, mum_ref, w0_ref, w2h_ref, w2l_ref, a0_ref,
         a2_ref, g2_ref, kk_ref, ka_ref, rk_ref, lnw_ref, lnb_ref,
         y_ref, vout_ref, state_ref, carry_ref, carrym_ref, st_ref, yp_ref, pt_ref) = refs
    tb, t = RW_BLK, RW_T
    w = RWKV_HEADS * RWKV_HEAD

    @pl.when(pl.program_id(1) == 0)
    def _():
        state_ref[...] = jnp.zeros_like(state_ref)
        carry_ref[...] = jnp.zeros_like(carry_ref)
        carrym_ref[...] = jnp.zeros_like(carrym_ref)

    first_row = lax.broadcasted_iota(jnp.int32, (tb, 1), 0) == 0

    def shift(p, carry_row, mu):
        prev = jnp.where(first_row, carry_row, pltpu.roll(p, 1, 0))
        return p + (prev - p) * mu

    rp, kp, vp, mp = r_ref[...].astype(F32), k_ref[...].astype(F32), v_ref[...].astype(F32), misc_ref[...]
    r = shift(rp, carry_ref[0:1, :], mur_ref[...])
    k = shift(kp, carry_ref[1:2, :], muk_ref[...])
    v = shift(vp, carry_ref[2:3, :], muv_ref[...])
    m = shift(mp, carrym_ref[0:1, :], mum_ref[...])
    carry_ref[0:1, :] = rp[tb - 1:tb, :]
    carry_ref[1:2, :] = kp[tb - 1:tb, :]
    carry_ref[2:3, :] = vp[tb - 1:tb, :]
    carrym_ref[0:1, :] = mp[tb - 1:tb, :]

    w_lo = m[:, MISC_W:MISC_W + LANES]
    a_lo = m[:, MISC_A:MISC_A + LANES]
    g_lo = m[:, MISC_G:MISC_G + LORA_G]
    log_w = -_softplus(-(w0_ref[...] + _dot_3pass(jnp.tanh(w_lo), w2h_ref[...], w2l_ref[...]))) - 0.5
    lw = -jnp.exp(log_w)
    a = _sigmoid(a0_ref[...] + _dot(a_lo.astype(BF16), a2_ref[...]))
    g = _dot(_sigmoid(g_lo).astype(BF16), g2_ref[...])
    if has_vres:
        v_lo = m[:, MISC_V:MISC_V + LANES]
        v = v + (vfirst_ref[...] - v) * _sigmoid(v0_ref[...] + _dot(v_lo.astype(BF16), v2_ref[...]))
    else:
        vout_ref[...] = v

    lane = lax.broadcasted_iota(jnp.int32, (LANES, LANES), 1)
    rowi = lax.broadcasted_iota(jnp.int32, (LANES, LANES), 0)
    ones_blk = jnp.where((lane // RWKV_HEAD) == (rowi // RWKV_HEAD), 1.0, 0.0).astype(BF16)

    kk = k * kk_ref[...]
    kk = kk / jnp.maximum(jnp.sqrt(_head_sums(kk * kk, ones_blk)), 1e-12)
    k = k * (1.0 + (a - 1.0) * ka_ref[...])
    b = kk * a

    ti = lax.broadcasted_iota(jnp.int32, (t, t), 0)
    tj = lax.broadcasted_iota(jnp.int32, (t, t), 1)
    tri = jnp.where(ti >= tj, 1.0, 0.0).astype(BF16)

    for c in range(tb // t):
        sl = slice(c * t, (c + 1) * t)
        lw_c = lw[sl]
        lw_hi, lw_lo = _split2(lw_c)
        cl = _dot(tri, lw_hi) + _dot(tri, lw_lo)
        cl_end = cl[t - 1:t, :]
        e_neg = jnp.exp(-cl)
        e_end = jnp.exp(cl_end - cl)
        ops = (kk[sl] * jnp.exp(cl - lw_c), r[sl] * jnp.exp(cl), k[sl] * e_neg, b[sl] * e_neg,
               v[sl], k[sl] * e_end, b[sl] * e_end)
        for pi in range(RW_PAIRS):
            ls = slice(pi * LANES, (pi + 1) * LANES)
            for oi, op in enumerate(ops):
                st_ref[c, oi, pi] = op[:, ls]
            pt_ref[c, pi] = jnp.broadcast_to(jnp.exp(cl_end[:, ls]), (SUBLANES, LANES))

    lo_half = lax.broadcasted_iota(jnp.int32, (t, LANES), 1) < RWKV_HEAD
    bi = lax.broadcasted_iota(jnp.int32, (2 * t, 2 * t), 0) % t
    bj = lax.broadcasted_iota(jnp.int32, (2 * t, 2 * t), 1) % t
    strict = bi > bj
    incl = bi >= bj

    def chunk_group(c, pis, hts):
        h2 = 2 * t
        stk = [[_stack_heads(st_ref[c, oi, pi], lo_half) for oi in range(7)] for pi in pis]
        lhs2 = [jnp.concatenate([s[0], s[1]], axis=0).astype(BF16) for s in stk]
        rhs2 = [jnp.concatenate([s[2], s[3]], axis=0).astype(BF16) for s in stk]
        amat = [_dot_nt(a, b) for a, b in zip(lhs2, rhs2)]
        sh = [_dot_nt(a, h.astype(BF16)) for a, h in zip(lhs2, hts)]
        vsb = [s[4].astype(BF16) for s in stk]
        x = [s_[0:h2] + _dot(jnp.where(strict, am[0:h2, 0:h2], 0.0).astype(BF16), v_)
             for s_, am, v_ in zip(sh, amat, vsb)]
        pw = [jnp.where(strict, am[0:h2, h2:], 0.0) for am in amat]
        n, sign = 1, -1.0
        while n < t:
            if 2 * n < t:
                res = [_dot(p_.astype(BF16), jnp.concatenate([p_, x_], axis=1).astype(BF16)) for p_, x_ in zip(pw, x)]
                x = [x_ + sign * r_[:, h2:] for x_, r_ in zip(x, res)]
                pw = [r_[:, 0:h2] for r_ in res]
            else:
                x = [x_ + sign * _dot(p_.astype(BF16), x_.astype(BF16)) for p_, x_ in zip(pw, x)]
            n, sign = 2 * n, 1.0
        new_hts = []
        for i, pi in enumerate(pis):
            am, s = amat[i], stk[i]
            a_r = jnp.concatenate([jnp.where(incl, am[h2:, 0:h2], 0.0), jnp.where(incl, -am[h2:, h2:], 0.0)], axis=1)
            ys = sh[i][h2:] + _dot(a_r.astype(BF16), jnp.concatenate([s[4], x[i]], axis=0).astype(BF16))
            yp_ref[pi, c * t:(c + 1) * t, :] = ys[0:t] + ys[t:]
            lhs3 = jnp.concatenate([s[4].T, -(x[i].T)], axis=1).astype(BF16)
            rhs3 = jnp.concatenate([s[5], s[6]], axis=0).astype(BF16)
            new_hts.append(hts[i] * pt_ref[c, pi][0:1, :] + _dot(lhs3, rhs3))
        return new_hts

    for g0 in range(0, RW_PAIRS, RW_GROUP):
        pis = list(range(g0, g0 + RW_GROUP))
        hts = [state_ref[pi] for pi in pis]
        for c in range(tb // t):
            hts = chunk_group(c, pis, hts)
        for pi, ht in zip(pis, hts):
            state_ref[pi] = ht

    y = jnp.concatenate([yp_ref[pi] for pi in range(RW_PAIRS)], axis=1)
    inv_n = 1.0 / RWKV_HEAD
    mu = _head_sums(y, ones_blk) * inv_n
    yc = y - mu
    var = _head_sums(yc * yc, ones_blk) * inv_n
    y = yc * lax.rsqrt(var + RWKV_GN_EPS) * lnw_ref[...] + lnb_ref[...]
    bonus = _head_sums(r * k * rk_ref[...], ones_blk) * v
    y_ref[...] = ((y + bonus) * g).astype(y_ref.dtype)


def _rwkv_mixer(proj, misc, v_first, mu, w0, w2, a0, a2, g2, k_k, k_a, r_k, ln_w, ln_b, v0, v2):
    bn, sn, _ = proj.shape
    w = RWKV_HEADS * RWKV_HEAD
    has_vres = v_first is not None
    row = lambda x: x.reshape(1, -1)
    padrows = lambda x: jnp.pad(x, ((0, LANES - x.shape[0]), (0, 0)))
    padl = lambda x, n: jnp.pad(x, (0, n - x.shape[0]))
    mu_misc = [padl(mu[3 * w:3 * w + LORA_W], LANES), padl(mu[3 * w + LORA_W:3 * w + LORA_W + LORA_A], LANES),
               mu[3 * w + LORA_W + LORA_A:3 * w + LORA_W + LORA_A + LORA_G]]
    if has_vres:
        mu_misc.append(padl(mu[3 * w + LORA_W + LORA_A + LORA_G:], LANES))
    mu_m = padl(jnp.concatenate(mu_misc), 1024)
    const = lambda shape: pl.BlockSpec(shape, lambda b, c: (0,) * len(shape))
    blk = lambda width, idx: pl.BlockSpec((None, RW_BLK, width), lambda b, c: (b, c, idx))
    seq = pl.BlockSpec((None, RW_BLK, w), lambda b, c: (b, c, 0))
    in_specs = [blk(w, COL_R // w), blk(w, COL_R // w + 1), blk(w, COL_R // w + 2), blk(1024, 0)]
    args = [proj, proj, proj, misc]
    if has_vres:
        in_specs.append(seq)
        args.append(v_first)
    in_specs += [const((1, w))] * 3 + [const((1, 1024)), const((1, w)), const((LANES, w)), const((LANES, w)),
                                       const((1, w)), const((LANES, w)), const((LORA_G, w))] + [const((1, w))] * 5
    w2p = padrows(w2)
    w2_hi = w2p.astype(BF16)
    w2_lo = (w2p - w2_hi.astype(F32)).astype(BF16)
    args += [row(mu[:w]), row(mu[w:2 * w]), row(mu[2 * w:3 * w]), row(mu_m), row(w0), w2_hi, w2_lo, row(a0),
             padrows(a2).astype(BF16), g2.astype(BF16), row(k_k), row(k_a), row(r_k), row(ln_w), row(ln_b)]
    if has_vres:
        in_specs += [const((1, w)), const((LANES, w))]
        args += [row(v0), padrows(v2).astype(BF16)]
    out_shape = [jax.ShapeDtypeStruct((bn, sn, w), BF16)]
    out_specs = [seq]
    if not has_vres:
        out_shape.append(jax.ShapeDtypeStruct((bn, sn, w), F32))
        out_specs.append(seq)
    nc = RW_BLK // RW_T
    outs = pl.pallas_call(
        functools.partial(_rwkv_kernel, has_vres),
        grid=(bn, sn // RW_BLK),
        in_specs=in_specs,
        out_specs=out_specs,
        out_shape=out_shape,
        scratch_shapes=[pltpu.VMEM((RW_PAIRS, LANES, LANES), F32),
                        pltpu.VMEM((SUBLANES, w), F32),
                        pltpu.VMEM((SUBLANES, 1024), F32),
                        pltpu.VMEM((nc, 7, RW_PAIRS, RW_T, LANES), F32),
                        pltpu.VMEM((RW_PAIRS, RW_BLK, LANES), F32),
                        pltpu.VMEM((nc, RW_PAIRS, SUBLANES, LANES), F32)],
        compiler_params=_cparams(("parallel", "arbitrary")),
        name="rwkv7_mixer",
    )(*args)
    if has_vres:
        return outs[0], v_first
    return outs[0], outs[1]


MLA_TM = 256
ATT_TQ = 4096
ATT_TK = 2048
ATT_SUB = 512
ATT_SUB_FULL = 1024
HEAD_Q = 2 * LANES


def _rope_half(x2, cos, sin):
    return x2 * cos + pltpu.roll(x2, MLA_ROPE, 1) * sin


def _mla_prep_kernel(qlat_ref, kvlat_ref, misc_ref, cos_ref, sin_ref, qnw_ref, kvnw_ref, wq_ref, wkt_ref, wv_ref,
                     q_ref, kt_ref, v_ref, kpet_ref):
    def rms(x, w):
        x = x.astype(F32)
        return (x * lax.rsqrt(jnp.mean(x * x, axis=-1, keepdims=True) + RMS_EPS) * w).astype(BF16)

    cos, sin = cos_ref[...], sin_ref[...]
    scale = MLA_QK ** -0.5
    q = _dot(rms(qlat_ref[...], qnw_ref[...]), wq_ref[...])
    for h in range(MLA_HEADS):
        lo = h * HEAD_Q
        q_ref[:, lo:lo + LANES] = (q[:, lo:lo + LANES] * scale).astype(BF16)
        q_ref[:, lo + LANES:lo + HEAD_Q] = (_rope_half(q[:, lo + LANES:lo + HEAD_Q], cos, sin) * scale).astype(BF16)
    kvn = rms(kvlat_ref[...], kvnw_ref[...])
    kt_ref[...] = _dot_nt(wkt_ref[...], kvn).astype(BF16)
    v_ref[...] = _dot(kvn, wv_ref[...]).astype(BF16)
    kpet_ref[...] = _rope_half(misc_ref[:, MISC_KPE:MISC_KPE + LANES], cos, sin).T.astype(BF16)


def _flash_kernel(tq, tk, sub, qi_ref, ki_ref, q_ref, kt_ref, kpet_ref, v_ref, o_ref, m_ref, acc_ref):
    step = pl.program_id(2)
    qi, ki = qi_ref[step], ki_ref[step]
    ratio = tq // tk

    @pl.when(ki == 0)
    def _():
        m_ref[...] = jnp.full_like(m_ref, -jnp.inf)
        acc_ref[...] = jnp.zeros_like(acc_ref)

    def update(diag):
        kcat_t = jnp.concatenate([kt_ref[...], kpet_ref[...]], axis=0)
        vcat = jnp.concatenate([v_ref[...], jnp.ones((tk, LANES), BF16)], axis=1)
        sb = sub if diag is not None else min(tq, ATT_SUB_FULL)
        plan = []
        for r in range(tq // sb):
            row_lo, row_hi = r * sb, (r + 1) * sb - 1
            ncols = tk
            masked = False
            if diag is not None:
                col_lo = diag * tk
                if row_hi < col_lo:
                    continue
                ncols = min(tk, -(-(row_hi - col_lo + 1) // HEAD_Q) * HEAD_Q)
                masked = row_lo < col_lo + ncols - 1
            plan.append((row_lo, ncols, masked))

        def scores(row_lo, ncols, masked):
            s = _dot(q_ref[row_lo:row_lo + sb, :], kcat_t[:, :ncols])
            if masked:
                ri = row_lo + lax.broadcasted_iota(jnp.int32, s.shape, 0)
                ci = diag * tk + lax.broadcasted_iota(jnp.int32, s.shape, 1)
                s = jnp.where(ci <= ri, s, -jnp.inf)
            return s

        def absorb(row_lo, ncols, s):
            rows = slice(row_lo, row_lo + sb)
            tiles = [s[:, j * LANES:(j + 1) * LANES] for j in range(ncols // LANES)]
            fold = tiles[0]
            for tl in tiles[1:]:
                fold = jnp.maximum(fold, tl)
            m_prev = m_ref[rows, :]
            m_new = jnp.maximum(m_prev, jnp.max(fold, axis=-1, keepdims=True))
            p = jnp.concatenate([jnp.exp((tl - m_new).astype(BF16)) for tl in tiles], axis=1)
            alpha = jnp.exp(m_prev - m_new)
            acc_ref[rows, :] = (acc_ref[rows, :] * jnp.concatenate([alpha, alpha], axis=1)
                                + _dot(p, vcat[:ncols]))
            m_ref[rows, :] = m_new

        s_next = scores(*plan[0])
        for idx, (row_lo, ncols, _) in enumerate(plan):
            s_cur = s_next
            if idx + 1 < len(plan):
                s_next = scores(*plan[idx + 1])
            absorb(row_lo, ncols, s_cur)

    @pl.when(ki < qi * ratio)
    def _():
        update(None)

    for d in range(ratio):
        @pl.when(ki == qi * ratio + d)
        def _(d=d):
            update(d)

    @pl.when(ki == (qi + 1) * ratio - 1)
    def _():
        o_ref[...] = (acc_ref[:, 0:MLA_V] / acc_ref[:, MLA_V:]).astype(o_ref.dtype)


def _mla_mixer(proj, misc, cos, sin, q_norm_w, w_q_b, kv_norm_w, w_kv_b):
    bn, sn, _ = proj.shape
    t = bn * sn
    proj2 = proj.reshape(t, proj.shape[-1])
    misc2 = misc.reshape(t, misc.shape[-1])
    wq = w_q_b.reshape(MLA_RANK, MLA_HEADS, MLA_QK)
    pe = wq[..., MLA_NOPE:]
    rot = jnp.concatenate([-pe[..., MLA_ROPE // 2:], pe[..., :MLA_ROPE // 2]], axis=-1)
    wq = jnp.concatenate([wq, rot], axis=-1).reshape(MLA_RANK, MLA_HEADS * HEAD_Q).astype(BF16)
    wkv = w_kv_b.reshape(MLA_RANK, MLA_HEADS, MLA_NOPE + MLA_V)
    wkt = wkv[..., :MLA_NOPE].reshape(MLA_RANK, -1).T.astype(BF16)
    wv = wkv[..., MLA_NOPE:].reshape(MLA_RANK, -1).astype(BF16)
    nq, nk, nv = MLA_HEADS * HEAD_Q, MLA_HEADS * MLA_NOPE, MLA_HEADS * MLA_V
    tm = min(MLA_TM, sn)
    nsb = sn // tm
    rowblk = lambda width, idx: pl.BlockSpec((tm, width), lambda i: (i, idx))
    const = lambda shape: pl.BlockSpec(shape, lambda i: (0, 0))
    colblk = lambda rows: pl.BlockSpec((None, rows, tm), lambda i: (i // nsb, 0, i % nsb))
    q, kt, v, kpet = pl.pallas_call(
        _mla_prep_kernel,
        grid=(t // tm,),
        in_specs=[rowblk(MLA_RANK, COL_QLAT // MLA_RANK), rowblk(MLA_RANK, COL_KVLAT // MLA_RANK),
                  rowblk(1024, 0), rowblk(LANES, 0), rowblk(LANES, 0),
                  const((1, MLA_RANK)), const((1, MLA_RANK)), const((MLA_RANK, nq)), const((nk, MLA_RANK)),
                  const((MLA_RANK, nv))],
        out_specs=[rowblk(nq, 0), colblk(nk), rowblk(nv, 0), colblk(LANES)],
        out_shape=[jax.ShapeDtypeStruct((t, nq), BF16), jax.ShapeDtypeStruct((bn, nk, sn), BF16),
                   jax.ShapeDtypeStruct((t, nv), BF16), jax.ShapeDtypeStruct((bn, LANES, sn), BF16)],
        compiler_params=_cparams(("parallel",)),
        name="mla_prep",
    )(proj2, proj2, misc2, cos, sin, q_norm_w.reshape(1, -1), kv_norm_w.reshape(1, -1), wq, wkt, wv)

    tq, tk = min(ATT_TQ, sn), min(ATT_TK, sn)
    sub = min(ATT_SUB, tq)
    ratio = tq // tk
    pairs = [(a, b) for a in range(sn // tq) for b in range((a + 1) * ratio)]
    qi_arr = jnp.asarray([p[0] for p in pairs], jnp.int32)
    ki_arr = jnp.asarray([p[1] for p in pairs], jnp.int32)
    out = pl.pallas_call(
        functools.partial(_flash_kernel, tq, tk, sub),
        grid_spec=pltpu.PrefetchScalarGridSpec(
            num_scalar_prefetch=2,
            grid=(bn, MLA_HEADS, len(pairs)),
            in_specs=[pl.BlockSpec((None, tq, HEAD_Q), lambda b, h, s, qi, ki: (b, qi[s], h)),
                      pl.BlockSpec((None, MLA_NOPE, tk), lambda b, h, s, qi, ki: (b, h, ki[s])),
                      pl.BlockSpec((None, LANES, tk), lambda b, h, s, qi, ki: (b, 0, ki[s])),
                      pl.BlockSpec((None, tk, MLA_V), lambda b, h, s, qi, ki: (b, ki[s], h))],
            out_specs=pl.BlockSpec((None, tq, MLA_V), lambda b, h, s, qi, ki: (b, qi[s], h)),
            scratch_shapes=[pltpu.VMEM((tq, LANES), F32), pltpu.VMEM((tq, 2 * MLA_V), F32)]),
        out_shape=jax.ShapeDtypeStruct((bn, sn, MLA_HEADS * MLA_V), BF16),
        compiler_params=_cparams(("parallel", "parallel", "arbitrary")),
        name="mla_attention",
    )(qi_arr, ki_arr, q.reshape(bn, sn, nq), kt, kpet, v.reshape(bn, sn, nv))
    return out


MERGE_TM = 256


def _merge_kernel(g0_ref, g1_ref, g2_ref, ys_ref, yr_ref, ym_ref, x_ref, wo_ref, lnw_ref, lnb_ref, xo_ref, xb_ref):
    gate = lambda g_ref, y_ref: _sigmoid(g_ref[...].astype(F32)) * y_ref[...].astype(F32)
    merged = gate(g0_ref, ys_ref) + gate(g1_ref, yr_ref) + gate(g2_ref, ym_ref)
    h = ALPHA * x_ref[...] + _dot(merged.astype(BF16), wo_ref[...])
    y = _layernorm(h, lnw_ref[...], lnb_ref[...])
    xo_ref[...] = y
    xb_ref[...] = y.astype(BF16)


def _merge_out(proj2, y_ssd, y_rwkv, y_mla, x, w_out, ln_w, ln_b):
    t, d = x.shape
    tm = min(MERGE_TM, t)
    rowblk = lambda idx: pl.BlockSpec((tm, d), lambda i: (i, idx))
    const = lambda shape: pl.BlockSpec(shape, lambda i: (0, 0))
    return pl.pallas_call(
        _merge_kernel,
        grid=(t // tm,),
        in_specs=[rowblk(0), rowblk(1), rowblk(2), rowblk(0), rowblk(0), rowblk(0), rowblk(0),
                  const((d, d)), const((1, d)), const((1, d))],
        out_specs=[rowblk(0), rowblk(0)],
        out_shape=[jax.ShapeDtypeStruct((t, d), F32), jax.ShapeDtypeStruct((t, d), BF16)],
        compiler_params=_cparams(("parallel",)),
        name="merge_out_ln1",
    )(proj2, proj2, proj2, y_ssd, y_rwkv, y_mla, x, w_out.astype(BF16), ln_w.reshape(1, d), ln_b.reshape(1, d))


FFN_TM = 512
FFN_TF = 512


FFN_HALF = 256


def _swiglu_tile(xb, w1_ref, w3_ref, w2_ref):
    tf = w2_ref.shape[0]
    hs = [(_silu(_dot(xb, w1_ref[:, c:c + FFN_HALF])) * _dot(xb, w3_ref[:, c:c + FFN_HALF])).astype(BF16)
          for c in range(0, tf, FFN_HALF)]
    return _dot(jnp.concatenate(hs, axis=1), w2_ref[...])


def _ffn_kernel(xb_ref, x_ref, w1_ref, w3_ref, w2_ref, lnw_ref, lnb_ref, o_ref, ob_ref, acc_ref):
    f = pl.program_id(1)

    @pl.when(f == 0)
    def _():
        acc_ref[...] = jnp.zeros_like(acc_ref)

    acc_ref[...] += _swiglu_tile(xb_ref[...], w1_ref, w3_ref, w2_ref)

    @pl.when(f == pl.num_programs(1) - 1)
    def _():
        y = _layernorm(ALPHA * x_ref[...] + acc_ref[...], lnw_ref[...], lnb_ref[...])
        o_ref[...] = y
        ob_ref[...] = y.astype(BF16)


def _ffn_dense(xb, x, w1, w3, w2, ln_w, ln_b):
    t, d = x.shape
    ff = w1.shape[1]
    tm, tf = min(FFN_TM, t), FFN_TF
    return pl.pallas_call(
        _ffn_kernel,
        grid=(t // tm, ff // tf),
        in_specs=[pl.BlockSpec((tm, d), lambda i, f: (i, 0)), pl.BlockSpec((tm, d), lambda i, f: (i, 0)),
                  pl.BlockSpec((d, tf), lambda i, f: (0, f)), pl.BlockSpec((d, tf), lambda i, f: (0, f)),
                  pl.BlockSpec((tf, d), lambda i, f: (f, 0)),
                  pl.BlockSpec((1, d), lambda i, f: (0, 0)), pl.BlockSpec((1, d), lambda i, f: (0, 0))],
        out_specs=[pl.BlockSpec((tm, d), lambda i, f: (i, 0)), pl.BlockSpec((tm, d), lambda i, f: (i, 0))],
        out_shape=[jax.ShapeDtypeStruct((t, d), F32), jax.ShapeDtypeStruct((t, d), BF16)],
        scratch_shapes=[pltpu.VMEM((tm, d), F32)],
        compiler_params=_cparams(("parallel", "arbitrary")),
        name="ffn_dense_ln2",
    )(xb, x, w1.astype(BF16), w3.astype(BF16), w2.astype(BF16), ln_w.reshape(1, d), ln_b.reshape(1, d))


ROUTER_TM = 512
MOE_TM = 512
MOE_TF = 512
COMBINE_TM = 256


def _router_kernel(x_ref, wrh_ref, wrl_ref, idx_ref, wgt_ref, cnt_ref):
    @pl.when(pl.program_id(0) == 0)
    def _():
        cnt_ref[...] = jnp.zeros_like(cnt_ref)

    tm = x_ref.shape[0]
    logits = _dot_3pass(x_ref[...], wrh_ref[...], wrl_ref[...])
    lane = lax.broadcasted_iota(jnp.int32, logits.shape, 1)
    lg = jnp.where(lane < N_EXPERTS, logits, -jnp.inf)
    m1 = jnp.max(lg, axis=-1, keepdims=True)
    i1 = jnp.min(jnp.where(lg == m1, lane, LANES), axis=-1, keepdims=True)
    lg2 = jnp.where(lane == i1, -jnp.inf, lg)
    m2 = jnp.max(lg2, axis=-1, keepdims=True)
    i2 = jnp.min(jnp.where(lg2 == m2, lane, LANES), axis=-1, keepdims=True)
    e = jnp.exp(m2 - m1)
    wgt_ref[...] = jnp.where(lane == 0, 1.0 / (1.0 + e), jnp.where(lane == 1, e / (1.0 + e), 0.0))

    hit1, hit2 = lane == i1, lane == i2
    both = jnp.where(hit1 | hit2, 1.0, 0.0)
    ri = lax.broadcasted_iota(jnp.int32, (tm, tm), 0)
    ci = lax.broadcasted_iota(jnp.int32, (tm, tm), 1)
    before = cnt_ref[0:1, :] + _dot(jnp.where(ri > ci, 1.0, 0.0).astype(BF16), both.astype(BF16))
    rank1 = jnp.sum(jnp.where(hit1, before, 0.0), axis=-1, keepdims=True).astype(jnp.int32)
    rank2 = jnp.sum(jnp.where(hit2, before, 0.0), axis=-1, keepdims=True).astype(jnp.int32)
    idx_ref[...] = jnp.where(lane == 0, i1, jnp.where(lane == 1, i2, jnp.where(lane == 2, rank1,
                                                                              jnp.where(lane == 3, rank2, 0))))
    cnt_ref[...] = jnp.broadcast_to(before[tm - 1:tm, :] + both[tm - 1:tm, :], cnt_ref.shape)


def _moe_ffn_kernel(nf_static, be_ref, nused_ref, tok_ref, tok_next_ref, x_hbm, w1_ref, w3_ref, w2_ref, o_ref, xg_ref,
                    xb_ref, acc_ref, sem):
    i, f = pl.program_id(0), pl.program_id(1)
    ni, nf = pl.num_programs(0), pl.num_programs(1)
    rows = xg_ref.shape[1]
    per = -(-rows // nf_static)
    tail = rows - (nf_static - 1) * per
    slot = i % 2

    def row_copy(idx_ref, sl, r):
        return pltpu.make_async_copy(x_hbm.at[pl.ds(idx_ref[0, 0, r], 1), :], xg_ref.at[sl, pl.ds(r, 1), :],
                                     sem.at[sl])

    def wait_all(idx_ref, sl):
        for r in range(rows):
            row_copy(idx_ref, sl, r).wait()

    @pl.when(f == 0)
    def _():
        @pl.when(i == 0)
        def _():
            def body(r, c):
                row_copy(tok_ref, slot, r).start()
                return c
            lax.fori_loop(0, rows, body, 0)

        wait_all(tok_ref, slot)
        xb_ref[...] = xg_ref[slot].astype(BF16)
        acc_ref[...] = jnp.zeros_like(acc_ref)

    base = f * per

    def fetch_share():
        for j in range(tail):
            row_copy(tok_next_ref, 1 - slot, base + j).start()

    @pl.when(i < nused_ref[0])
    def _():
        fetch_share()
        acc_ref[...] += _swiglu_tile(xb_ref[...], w1_ref, w3_ref, w2_ref)

    @pl.when(i >= nused_ref[0])
    def _():
        fetch_share()

    @pl.when(f < nf - 1)
    def _():
        for j in range(tail, per):
            row_copy(tok_next_ref, 1 - slot, base + j).start()

    @pl.when(f == nf - 1)
    def _():
        o_ref[...] = acc_ref[...]

        @pl.when(i == ni - 1)
        def _():
            wait_all(tok_next_ref, 1 - slot)


def _combine_kernel(dst_ref, dst_next_ref, y_hbm, wgt_ref, x_ref, lnw_ref, lnb_ref, o_ref, yg_ref, sem):
    i, n = pl.program_id(0), pl.num_programs(0)
    rows = x_ref.shape[0]
    slot = i % 2

    def row_copy(idx_ref, sl, r, k):
        return pltpu.make_async_copy(y_hbm.at[pl.ds(idx_ref[0, 0, TOP_K * r + k], 1), :],
                                     yg_ref.at[sl, k, pl.ds(r, 1), :], sem.at[sl])

    def for_all(idx_ref, sl, op):
        for r in range(rows):
            for k in range(TOP_K):
                op(row_copy(idx_ref, sl, r, k))

    @pl.when(i == 0)
    def _():
        for_all(dst_ref, slot, lambda c: c.start())

    for_all(dst_next_ref, 1 - slot, lambda c: c.start())
    for_all(dst_ref, slot, lambda c: c.wait())
    wgt = wgt_ref[...]
    f = wgt[:, 0:1] * yg_ref[slot, 0] + wgt[:, 1:2] * yg_ref[slot, 1]
    o_ref[...] = _layernorm(ALPHA * x_ref[...] + f, lnw_ref[...], lnb_ref[...])

    @pl.when(i == n - 1)
    def _():
        for_all(dst_next_ref, 1 - slot, lambda c: c.wait())


def _ffn_moe(x, router, w1, w3, w2, ln_w, ln_b):
    t, d = x.shape
    ne, _, ff = w1.shape
    tm = min(ROUTER_TM, t)
    rw = jnp.pad(router, ((0, 0), (0, LANES - ne)))
    rw_hi = rw.astype(BF16)
    rw_lo = (rw - rw_hi.astype(F32)).astype(BF16)
    idx, wgt, cnt = pl.pallas_call(
        _router_kernel,
        grid=(t // tm,),
        in_specs=[pl.BlockSpec((tm, d), lambda i: (i, 0)), pl.BlockSpec((d, LANES), lambda i: (0, 0)),
                  pl.BlockSpec((d, LANES), lambda i: (0, 0))],
        out_specs=[pl.BlockSpec((tm, LANES), lambda i: (i, 0))] * 2 + [pl.BlockSpec((SUBLANES, LANES), lambda i: (0, 0))],
        out_shape=[jax.ShapeDtypeStruct((t, LANES), jnp.int32), jax.ShapeDtypeStruct((t, LANES), F32),
                   jax.ShapeDtypeStruct((SUBLANES, LANES), F32)],
        compiler_params=_cparams(("arbitrary",)),
        name="moe_router",
    )(x, rw_hi, rw_lo)

    blk = MOE_TM
    n_assign = t * TOP_K
    flat_e = idx[:, :TOP_K].reshape(-1)
    rank = idx[:, TOP_K:2 * TOP_K].reshape(-1)
    counts = cnt[0, :ne].astype(jnp.int32)
    padded = (counts + blk - 1) // blk * blk
    pad_end = jnp.cumsum(padded)
    dest = (pad_end - padded)[flat_e] + rank
    n_blocks = -(-(n_assign + ne * (blk - 1)) // blk)
    n_rows = n_blocks * blk
    token_of_row = jnp.zeros((n_rows,), jnp.int32).at[dest].set(jnp.arange(n_assign, dtype=jnp.int32) // TOP_K,
                                                                unique_indices=True)
    block_expert = jnp.minimum(jnp.searchsorted(pad_end, jnp.arange(n_blocks, dtype=jnp.int32) * blk, side='right'),
                               ne - 1).astype(jnp.int32)

    tf = MOE_TF
    n_used = (pad_end[-1:] // blk).astype(jnp.int32)
    wtile = lambda i, f, nu: jnp.where(i < nu[0], f, 0)
    yb = pl.pallas_call(
        functools.partial(_moe_ffn_kernel, ff // tf),
        grid_spec=pltpu.PrefetchScalarGridSpec(
            num_scalar_prefetch=2,
            grid=(n_blocks, ff // tf),
            in_specs=[pl.BlockSpec((1, 1, blk), lambda i, f, be, nu: (i, 0, 0), memory_space=pltpu.SMEM),
                      pl.BlockSpec((1, 1, blk), lambda i, f, be, nu: (jnp.minimum(i + 1, n_blocks - 1), 0, 0),
                                   memory_space=pltpu.SMEM),
                      pl.BlockSpec(memory_space=pl.ANY),
                      pl.BlockSpec((None, d, tf), lambda i, f, be, nu: (be[i], 0, wtile(i, f, nu))),
                      pl.BlockSpec((None, d, tf), lambda i, f, be, nu: (be[i], 0, wtile(i, f, nu))),
                      pl.BlockSpec((None, tf, d), lambda i, f, be, nu: (be[i], wtile(i, f, nu), 0))],
            out_specs=pl.BlockSpec((blk, d), lambda i, f, be, nu: (i, 0)),
            scratch_shapes=[pltpu.VMEM((2, blk, d), F32), pltpu.VMEM((blk, d), BF16), pltpu.VMEM((blk, d), F32),
                            pltpu.SemaphoreType.DMA((2,))]),
        out_shape=jax.ShapeDtypeStruct((n_rows, d), F32),
        compiler_params=_cparams(("arbitrary", "arbitrary")),
        name="moe_expert_ffn",
    )(block_expert, n_used, token_of_row.reshape(n_blocks, 1, blk), token_of_row.reshape(n_blocks, 1, blk), x,
      w1.astype(BF16), w3.astype(BF16), w2.astype(BF16))

    tc = min(COMBINE_TM, t)
    return pl.pallas_call(
        _combine_kernel,
        grid=(t // tc,),
        in_specs=[pl.BlockSpec((1, 1, TOP_K * tc), lambda i: (i, 0, 0), memory_space=pltpu.SMEM),
                  pl.BlockSpec((1, 1, TOP_K * tc), lambda i: (jnp.minimum(i + 1, t // tc - 1), 0, 0),
                               memory_space=pltpu.SMEM),
                  pl.BlockSpec(memory_space=pl.ANY),
                  pl.BlockSpec((tc, LANES), lambda i: (i, 0)), pl.BlockSpec((tc, d), lambda i: (i, 0)),
                  pl.BlockSpec((1, d), lambda i: (0, 0)), pl.BlockSpec((1, d), lambda i: (0, 0))],
        out_specs=pl.BlockSpec((tc, d), lambda i: (i, 0)),
        out_shape=jax.ShapeDtypeStruct((t, d), F32),
        scratch_shapes=[pltpu.VMEM((2, TOP_K, tc, d), F32), pltpu.SemaphoreType.DMA((2,))],
        compiler_params=_cparams(("arbitrary",)),
        name="moe_combine_ln2",
    )(dest.reshape(t // tc, 1, TOP_K * tc), dest.reshape(t // tc, 1, TOP_K * tc), yb, wgt, x,
      ln_w.reshape(1, d), ln_b.reshape(1, d))


def _pack_w_in(w, w_vres):
    d = w.shape[0]
    w = w.astype(BF16)
    w_vres = None if w_vres is None else w_vres.astype(BF16)
    o = 0

    def take(n):
        nonlocal o
        s = w[:, o:o + n]
        o += n
        return s

    padc = lambda s, n: jnp.pad(s, ((0, 0), (0, n - s.shape[1])))
    gates = take(3 * W2K)
    z = take(W2K)
    xbc = take(W2K + 2 * SSD_BC)
    dt = take(SSD_HEADS)
    qlat = take(MLA_RANK)
    kvlat = take(MLA_RANK)
    kpe = take(MLA_ROPE)
    rkv = take(3 * W2K)
    w_lo, a_lo, g_lo = take(LORA_W), take(LORA_A), take(LORA_G)
    v_lo = jnp.zeros((d, LANES), w.dtype) if w_vres is None else padc(w_vres, LANES)
    kpe_rot = jnp.concatenate([-kpe[:, MLA_ROPE // 2:], kpe[:, :MLA_ROPE // 2]], axis=1)
    misc = jnp.concatenate([padc(w_lo, LANES), padc(a_lo, LANES), g_lo, v_lo, padc(dt, LANES), kpe, kpe_rot], axis=1)
    return jnp.concatenate([gates, z, rkv, xbc, qlat, kvlat, padc(misc, 1024)], axis=1).astype(BF16)


PROJ_TM = 2048
PROJ_TN = 1024


def kernel(x, positions, w_in, w_in_vres, w_out, ssd_conv_w, ssd_conv_b, ssd_dt_bias, ssd_a_log, ssd_d, ssd_norm_w, rwkv_mu, rwkv_mu_vres, rwkv_w0, rwkv_w2, rwkv_a0, rwkv_a2, rwkv_g2, rwkv_v0, rwkv_v2, rwkv_k_k, rwkv_k_a, rwkv_r_k, rwkv_ln_w, rwkv_ln_b, mla_q_norm_w, mla_w_q_b, mla_kv_norm_w, mla_w_kv_b, ln1_w, ln1_b, ln2_w, ln2_b, ffn_w1, ffn_w3, ffn_w2, moe_router, moe_w1, moe_w3, moe_w2):
    bn, sn, d = x.shape
    t = bn * sn
    cos, sin = _rope_tables(positions)
    xf = x.reshape(t, d)
    xb = xf
    v_first = None
    for l in range(DEPTH):
        if l == 0:
            wp, mu, v0, v2 = _pack_w_in(w_in[l], None), rwkv_mu[l], None, None
        else:
            wp = _pack_w_in(w_in[l], w_in_vres[l - 1])
            mu = jnp.concatenate([rwkv_mu[l], rwkv_mu_vres[l - 1]], axis=0)
            v0, v2 = rwkv_v0[l - 1], rwkv_v2[l - 1]
        tm = min(PROJ_TM // 2 if xb.dtype == F32 else PROJ_TM, t)
        proj2 = _matmul(xb, wp[:, :COL_MISC], BF16, tm, PROJ_TN)
        misc = _matmul(xb, wp[:, COL_MISC:], F32, tm, N_PROJ - COL_MISC).reshape(bn, sn, -1)
        proj = proj2.reshape(bn, sn, COL_MISC)
        y_ssd = _ssd_mixer(proj, misc, ssd_conv_w[l], ssd_conv_b[l], ssd_dt_bias[l], ssd_a_log[l], ssd_d[l],
                           ssd_norm_w[l])
        y_rwkv, v_first = _rwkv_mixer(proj, misc, v_first, mu, rwkv_w0[l], rwkv_w2[l], rwkv_a0[l], rwkv_a2[l],
                                      rwkv_g2[l], rwkv_k_k[l], rwkv_k_a[l], rwkv_r_k[l], rwkv_ln_w[l], rwkv_ln_b[l],
                                      v0, v2)
        y_mla = _mla_mixer(proj, misc, cos, sin, mla_q_norm_w[l], mla_w_q_b[l], mla_kv_norm_w[l], mla_w_kv_b[l])
        x1, x1b = _merge_out(proj2, y_ssd.reshape(t, d), y_rwkv.reshape(t, d), y_mla.reshape(t, d), xf, w_out[l],
                             ln1_w[l], ln1_b[l])
        if l % 2 == 0:
            xf, xb = _ffn_dense(x1b, x1, ffn_w1[l // 2], ffn_w3[l // 2], ffn_w2[l // 2], ln2_w[l], ln2_b[l])
        else:
            xf = _ffn_moe(x1, moe_router[l // 2], moe_w1[l // 2], moe_w3[l // 2], moe_w2[l // 2], ln2_w[l], ln2_b[l])
            xb = xf
    return xf.reshape(bn, sn, d)
```
